```python
import jax, jax.numpy as jnp
from jax import lax
import numpy as np

D_MODEL = 1024
BATCH = 32
SEQ = 256
DEPTH = 1
DEC_BATCH = 2
DEC_SEQ = 2048
PAST_LEN = 512

GRID_W = 64
H_RET = 4
HD_RET = 128
W_RET = H_RET * HD_RET
H_ML = 4
HD_ML = 128
W_ML = H_ML * HD_ML
MIX_W = W_RET + W_ML
N_DIR = 2
IN_COLS = 4 * W_RET + 4 * W_ML + 2 * N_DIR * H_ML
D_FF = 4 * D_MODEL
CHUNK = 128
CONV_W = 3
EPS = 1e-6
ROPE_BASE = 10000.0
F32 = jnp.float32

kernel_name = 'bidir_retention_mlstm_hybrid_dit_step'


def rms_norm(x, g):
    xf = x.astype(F32)
    y = xf * lax.rsqrt(jnp.mean(xf * xf, -1, keepdims=True) + EPS)
    return (y * g.astype(F32)).astype(x.dtype)


def group_norm(h, g):
    hf = h.astype(F32)
    mu = jnp.mean(hf, -1, keepdims=True)
    var = jnp.mean(jnp.square(hf - mu), -1, keepdims=True)
    y = (hf - mu) * lax.rsqrt(var + EPS)
    B, T, H, D = h.shape
    return y.reshape(B, T, H * D) * g.astype(F32)


def axial_rope(x):
    T, D = x.shape[1], x.shape[-1]
    rows = T // GRID_W
    row = jnp.repeat(jnp.arange(rows), GRID_W).astype(F32)
    col = jnp.tile(jnp.arange(GRID_W), rows).astype(F32)
    nf = D // 4
    inv = ROPE_BASE ** (-jnp.arange(nf, dtype=F32) / nf)
    ang = jnp.concatenate([row[:, None] * inv, col[:, None] * inv], -1)
    cos = jnp.cos(ang)[None, :, None, :]
    sin = jnp.sin(ang)[None, :, None, :]
    xf = x.astype(F32)
    x1, x2 = xf[..., :D // 2], xf[..., D // 2:]
    return jnp.concatenate([x1 * cos - x2 * sin, x1 * sin + x2 * cos], -1)


def centred_conv(x, w):
    pad = CONV_W // 2
    T = x.shape[1]
    xp = jnp.pad(x, ((0, 0), (pad, pad), (0, 0)))
    out = xp[:, 0:T] * w[0]
    for j in range(1, CONV_W):
        out = out + xp[:, j:j + T] * w[j]
    return out


def to_chunks(a):
    B, H, T = a.shape[:3]
    n = T // CHUNK
    a = a.reshape((B, H, n, CHUNK) + a.shape[3:])
    return jnp.moveaxis(a, 2, 0)


def from_chunks(a):
    n, B, H, L, D = a.shape
    return jnp.moveaxis(a, 0, 2).reshape(B, H, n * L, D)


def retention_scan(q, k, v, log_gamma, s0):
    idx = jnp.arange(CHUNK, dtype=F32)
    lg = log_gamma.astype(F32)[:, None]
    diff = idx[:, None] - idx[None, :]
    mask = diff >= 0
    intra = jnp.where(mask, jnp.exp(lg[:, :, None] * jnp.where(mask, diff, 0.0)), 0.0)
    inter = jnp.exp(lg * (idx + 1.0))
    to_end = jnp.exp(lg * (CHUNK - 1.0 - idx))
    chunk_decay = jnp.exp(lg[:, 0] * CHUNK)

    def step(s, xs):
        qi, ki, vi = xs
        sc = jnp.einsum('bhtd,bhsd->bhts', qi, ki) * intra
        o = jnp.einsum('bhts,bhse->bhte', sc, vi) + jnp.einsum('bhtd,bhde->bhte', qi, s) * inter[..., None]
        s_new = s * chunk_decay[:, None, None] + jnp.einsum('bhsd,bhse->bhde', ki * to_end[..., None], vi)
        return s_new, o

    s_fin, oc = lax.scan(step, s0.astype(F32), (to_chunks(q), to_chunks(k), to_chunks(v)))
    return from_chunks(oc), s_fin


def mlstm_scan(q, k, v, log_i, log_f, C0, n0, m0):
    causal = jnp.tril(jnp.ones((CHUNK, CHUNK), bool))

    def step(carry, xs):
        C, nv, m = carry
        qi, ki, vi, li, lf = xs
        b = jnp.cumsum(lf, -1)
        log_inter = b + m[..., None]
        log_intra = jnp.where(causal, b[..., :, None] - b[..., None, :] + li[..., None, :], -jnp.inf)
        m_t = jnp.maximum(log_inter, jnp.max(log_intra, -1))
        w_inter = jnp.exp(log_inter - m_t)
        w_intra = jnp.exp(log_intra - m_t[..., None])
        sc = jnp.einsum('bhtd,bhsd->bhts', qi, ki) * w_intra
        num = jnp.einsum('bhts,bhse->bhte', sc, vi) + jnp.einsum('bhtd,bhde->bhte', qi, C) * w_inter[..., None]
        den = jnp.sum(sc, -1) + jnp.einsum('bhtd,bhd->bht', qi, nv) * w_inter
        h = num / jnp.maximum(jnp.abs(den), jnp.exp(-m_t))[..., None]
        b_end = b[..., -1]
        log_w_end = b_end[..., None] - b + li
        m_new = jnp.maximum(b_end + m, jnp.max(log_w_end, -1))
        decay = jnp.exp(b_end + m - m_new)
        w_end = jnp.exp(log_w_end - m_new[..., None])
        C_new = C * decay[..., None, None] + jnp.einsum('bhs,bhsd,bhse->bhde', w_end, ki, vi)
        n_new = nv * decay[..., None] + jnp.einsum('bhs,bhsd->bhd', w_end, ki)
        return (C_new, n_new, m_new), h

    xs = (to_chunks(q), to_chunks(k), to_chunks(v), to_chunks(log_i), to_chunks(log_f))
    (C_f, n_f, m_f), hc = lax.scan(step, (C0.astype(F32), n0.astype(F32), m0.astype(F32)), xs)
    return from_chunks(hc), C_f, n_f, m_f


def flip_t(a):
    return jnp.flip(a, axis=2)


def mixer(h, init, latent, w_in, conv_w, ret_decay_logit, mlstm_gate_bias, ret_gn_g, mlstm_gn_g, w_out):
    B, T, _ = h.shape
    s_ret0, C0, n0, m0 = init
    proj = h @ w_in
    cuts = [W_RET, 2 * W_RET, 3 * W_RET, 4 * W_RET, 4 * W_RET + 2 * W_ML, 4 * W_RET + 3 * W_ML, 4 * W_RET + 4 * W_ML]
    rq, rk, rv, rg, mqk, mv, mo, mgate = jnp.split(proj, cuts, axis=-1)

    rq = rq.reshape(B, T, H_RET, HD_RET)
    rk = rk.reshape(B, T, H_RET, HD_RET) * (HD_RET ** -0.5)
    if latent:
        rq, rk = axial_rope(rq), axial_rope(rk)
    q = jnp.transpose(rq.astype(F32), (0, 2, 1, 3))
    k = jnp.transpose(rk.astype(F32), (0, 2, 1, 3))
    v = jnp.transpose(rv.reshape(B, T, H_RET, HD_RET).astype(F32), (0, 2, 1, 3))
    log_gamma = jax.nn.log_sigmoid(ret_decay_logit.astype(F32))
    o_f, s_f = retention_scan(q, k, v, log_gamma[0], s_ret0[:, 0])
    o_b, s_b = retention_scan(flip_t(q), flip_t(k), flip_t(v), log_gamma[1], s_ret0[:, 1])
    o_ret = o_f + flip_t(o_b)
    ret_y = group_norm(jnp.transpose(o_ret, (0, 2, 1, 3)), ret_gn_g) * jax.nn.silu(rg.astype(F32))

    mqk = jax.nn.silu(centred_conv(mqk, conv_w))
    mq, mk = jnp.split(mqk, 2, axis=-1)
    q = jnp.transpose(mq.reshape(B, T, H_ML, HD_ML).astype(F32), (0, 2, 1, 3))
    k = jnp.transpose(mk.reshape(B, T, H_ML, HD_ML).astype(F32), (0, 2, 1, 3)) * (HD_ML ** -0.5)
    v = jnp.transpose(mv.reshape(B, T, H_ML, HD_ML).astype(F32), (0, 2, 1, 3))
    gates = mgate.astype(F32).reshape(B, T, N_DIR, 2, H_ML) + mlstm_gate_bias.astype(F32)
    gates = jnp.transpose(gates, (2, 3, 0, 4, 1))
    log_i = gates[:, 0]
    log_f = jax.nn.log_sigmoid(gates[:, 1])
    h_f, C_f, n_f, m_f = mlstm_scan(q, k, v, log_i[0], log_f[0], C0[:, 0], n0[:, 0], m0[:, 0])
    h_b, C_b, n_b, m_b = mlstm_scan(flip_t(q), flip_t(k), flip_t(v), flip_t(log_i[1]), flip_t(log_f[1]),
                                    C0[:, 1], n0[:, 1], m0[:, 1])
    h_ml = h_f + flip_t(h_b)
    ml_y = group_norm(jnp.transpose(h_ml, (0, 2, 1, 3)), mlstm_gn_g) * jax.nn.sigmoid(mo.astype(F32))

    out = jnp.concatenate([ret_y, ml_y], -1).astype(h.dtype) @ w_out
    new_state = (jnp.stack([s_f, s_b], 1), jnp.stack([C_f, C_b], 1),
                 jnp.stack([n_f, n_b], 1), jnp.stack([m_f, m_b], 1))
    return out, new_state


def layer(x, cond, init, latent, w_ada, b_ada, norm1_g, norm2_g, w_in, conv_w, ret_decay_logit,
          mlstm_gate_bias, ret_gn_g, mlstm_gn_g, w_out, w_ff1, w_ff2):
    mod = (jax.nn.silu(cond) @ w_ada + b_ada)[:, None, :]
    sh1, sc1, g1, sh2, sc2, g2 = jnp.split(mod, 6, axis=-1)
    h = rms_norm(x, norm1_g) * (1 + sc1) + sh1
    a, st = mixer(h, init, latent, w_in, conv_w, ret_decay_logit, mlstm_gate_bias, ret_gn_g, mlstm_gn_g, w_out)
    x = x + g1 * a
    h = rms_norm(x, norm2_g) * (1 + sc2) + sh2
    f = jnp.square(jax.nn.relu(h @ w_ff1)) @ w_ff2
    x = x + g2 * f
    return x, st


def setup_inputs(seed: int = 0) -> dict:
    key = jax.random.key(seed)
    ks = jax.random.split(key, 26)

    def nrm(k, shape, s):
        return jax.random.normal(k, shape, F32) * s

    hidx = jnp.arange(H_RET, dtype=F32)
    gam = 1.0 - 2.0 ** (-5.0 - hidx)
    base_logit = jnp.log(gam) - jnp.log1p(-gam)
    f_bias = jnp.linspace(3.0, 6.0, H_ML, dtype=F32)
    i_b = nrm(ks[16], (DEPTH, N_DIR, 1, H_ML), 0.1)
    f_b = f_bias + nrm(ks[17], (DEPTH, N_DIR, 1, H_ML), 0.1)
    return {
        'x_prompt': nrm(ks[0], (BATCH, SEQ, D_MODEL), 1.0),
        'x_sample': nrm(ks[1], (DEC_BATCH, DEC_SEQ, D_MODEL), 1.0),
        'state_ret': nrm(ks[2], (DEC_BATCH, DEPTH, N_DIR, H_RET, HD_RET, HD_RET), 0.5),
        'state_mlstm_C': nrm(ks[3], (DEC_BATCH, DEPTH, N_DIR, H_ML, HD_ML, HD_ML), 0.5),
        'state_mlstm_n': nrm(ks[4], (DEC_BATCH, DEPTH, N_DIR, H_ML, HD_ML), 0.5),
        'state_mlstm_m': nrm(ks[5], (DEC_BATCH, DEPTH, N_DIR, H_ML), 0.5),
        'c': nrm(ks[6], (DEC_BATCH, D_MODEL), 1.0),
        'c_ctx': nrm(ks[7], (D_MODEL,), 1.0),
        'w_ada': nrm(ks[8], (DEPTH, D_MODEL, 6 * D_MODEL), 0.5 * D_MODEL ** -0.5),
        'b_ada': nrm(ks[9], (DEPTH, 6 * D_MODEL), 0.1),
        'norm1_g': 1.0 + nrm(ks[10], (DEPTH, D_MODEL), 0.05),
        'norm2_g': 1.0 + nrm(ks[11], (DEPTH, D_MODEL), 0.05),
        'w_in': nrm(ks[12], (DEPTH, D_MODEL, IN_COLS), D_MODEL ** -0.5),
        'conv_w': nrm(ks[13], (DEPTH, CONV_W, 2 * W_ML), CONV_W ** -0.5),
        'ret_decay_logit': base_logit + nrm(ks[14], (DEPTH, N_DIR, H_RET), 0.1),
        'mlstm_gate_bias': jnp.concatenate([i_b, f_b], axis=2),
        'ret_gn_g': 1.0 + nrm(ks[18], (DEPTH, W_RET), 0.05),
        'mlstm_gn_g': 1.0 + nrm(ks[19], (DEPTH, W_ML), 0.05),
        'w_out': nrm(ks[20], (DEPTH, MIX_W, D_MODEL), MIX_W ** -0.5),
        'w_ff1': nrm(ks[21], (DEPTH, D_MODEL, D_FF), D_MODEL ** -0.5),
        'w_ff2': nrm(ks[22], (DEPTH, D_FF, D_MODEL), D_FF ** -0.5),
        'final_g': 1.0 + nrm(ks[23], (D_MODEL,), 0.05),
    }


def reference(x_prompt, x_sample, state_ret, state_mlstm_C, state_mlstm_n, state_mlstm_m, c, c_ctx,
              w_ada, b_ada, norm1_g, norm2_g, w_in, conv_w, ret_decay_logit, mlstm_gate_bias,
              ret_gn_g, mlstm_gn_g, w_out, w_ff1, w_ff2, final_g):
    Bp = x_prompt.shape[0]
    ctx_cond = jnp.broadcast_to(c_ctx, (Bp, D_MODEL))
    zero_init = (jnp.zeros((Bp, N_DIR, H_RET, HD_RET, HD_RET), F32),
                 jnp.zeros((Bp, N_DIR, H_ML, HD_ML, HD_ML), F32),
                 jnp.zeros((Bp, N_DIR, H_ML, HD_ML), F32),
                 jnp.zeros((Bp, N_DIR, H_ML), F32))
    xp, xs = x_prompt, x_sample
    new_ret, new_C, new_n, new_m = [], [], [], []
    for l in range(DEPTH):
        lw = (w_ada[l], b_ada[l], norm1_g[l], norm2_g[l], w_in[l], conv_w[l], ret_decay_logit[l],
              mlstm_gate_bias[l], ret_gn_g[l], mlstm_gn_g[l], w_out[l], w_ff1[l], w_ff2[l])
        xp, st = layer(xp, ctx_cond, zero_init, False, *lw)
        new_ret.append(st[0]); new_C.append(st[1]); new_n.append(st[2]); new_m.append(st[3])
        cache = (state_ret[:, l], state_mlstm_C[:, l], state_mlstm_n[:, l], state_mlstm_m[:, l])
        xs, _ = layer(xs, c, cache, True, *lw)
    y_prompt = rms_norm(xp, final_g)
    y_sample = rms_norm(xs, final_g)
    return (y_prompt, y_sample, jnp.stack(new_ret, 1), jnp.stack(new_C, 1), jnp.stack(new_n, 1), jnp.stack(new_m, 1))
```

```python
import functools

import numpy as np
import jax
import jax.numpy as jnp
from jax import lax
from jax.experimental import pallas as pl
from jax.experimental.pallas import tpu as pltpu

F32 = jnp.float32
BF16 = jnp.bfloat16

D_MODEL = 1024
N_HEADS = 4
HEAD_DIM = 128
CHUNK = 128
GRID_W = 64
D_FF = 4 * D_MODEL
EPS = 1e-6
ROPE_BASE = 10000.0
PAIR_COLS = 9 * HEAD_DIM
ROW_TILE = 256
FFN_ROWS = 512
VMEM_LIMIT = 60 * 1024 * 1024


def _dot(a, b):
    return jnp.dot(a, b, preferred_element_type=F32)


def _dot_nt(a, b):
    return lax.dot_general(a, b, (((1,), (1,)), ((), ())), preferred_element_type=F32)


def _dot_tn(a, b):
    return lax.dot_general(a, b, (((0,), (0,)), ((), ())), preferred_element_type=F32)


def _rms(x, g):
    return x * lax.rsqrt(jnp.mean(x * x, axis=-1, keepdims=True) + EPS) * g


def _group_norm(o, g):
    mu = jnp.mean(o, axis=-1, keepdims=True)
    c = o - mu
    var = jnp.mean(c * c, axis=-1, keepdims=True)
    return c * lax.rsqrt(var + EPS) * g


def _log_sigmoid(x):
    return jnp.minimum(x, 0.0) - jnp.log1p(jnp.exp(-jnp.abs(x)))


def _sigmoid(x):
    return 1.0 / (1.0 + jnp.exp(-x))


def _split3(x):
    hi = x.astype(BF16)
    r = x - hi.astype(F32)
    mid = r.astype(BF16)
    lo = (r - mid.astype(F32)).astype(BF16)
    return hi, mid, lo


def _mod_kernel(cond_ref, w_ref, b_ref, out_ref):
    c = cond_ref[...]
    s = (c * _sigmoid(c)).astype(BF16)
    out_ref[...] = _dot(s, w_ref[...].astype(BF16)) + b_ref[...]


def _modulation(cond, w_ada, b_ada):
    n = w_ada.shape[1]
    tn = 1024
    return pl.pallas_call(
        _mod_kernel,
        out_shape=jax.ShapeDtypeStruct((cond.shape[0], n), F32),
        grid=(n // tn,),
        in_specs=[pl.BlockSpec(cond.shape, lambda j: (0, 0)),
                  pl.BlockSpec((D_MODEL, tn), lambda j: (0, j)),
                  pl.BlockSpec((1, tn), lambda j: (0, j))],
        out_specs=pl.BlockSpec((cond.shape[0], tn), lambda j: (0, j)),
        compiler_params=pltpu.CompilerParams(dimension_semantics=("arbitrary",)),
        name="adaln_mod",
    )(cond, w_ada, b_ada)


def _mixer_kernel(*refs, latent, seq):
    n_chunks = seq // CHUNK
    L = CHUNK
    if latent:
        (x_ref, mod_ref, g1_ref, w_ref, hp_ref, cos_ref, sin_ref, sr_in, sc_in, sn_in, sm_in,
         y_ref,
         hn_ref, p_ref, rq_ref, rk_ref, rv_ref, mq_ref, mk_ref, mvx_ref, gbc_ref,
         orf_ref, orb_ref, hmf_ref, hmb_ref, rc_ref, s_ref, c_ref, m_ref) = refs
    else:
        (x_ref, mod_ref, g1_ref, w_ref, hp_ref,
         y_ref, so_ref, co_ref, no_ref, mo_ref,
         hn_ref, p_ref, rq_ref, rk_ref, rv_ref, mq_ref, mk_ref, mvx_ref, gbc_ref,
         orf_ref, orb_ref, hmf_ref, hmb_ref, rc_ref, s_ref, c_ref, m_ref) = refs

    head = pl.program_id(1)

    @pl.when(head == 0)
    def _():
        sh1 = mod_ref[0:1, :]
        sc1 = mod_ref[1:2, :]
        g1 = g1_ref[...]

        def body(i, carry):
            rows = pl.ds(pl.multiple_of(i * ROW_TILE, ROW_TILE), ROW_TILE)
            hn_ref[rows, :] = (_rms(x_ref[rows, :], g1) * (1.0 + sc1) + sh1).astype(BF16)
            return carry

        lax.fori_loop(0, seq // ROW_TILE, body, 0)

    def proj_body(i, carry):
        rows = pl.ds(pl.multiple_of(i * ROW_TILE, ROW_TILE), ROW_TILE)
        p_ref[rows, :] = _dot(hn_ref[rows, :], w_ref[...])
        return carry

    lax.fori_loop(0, seq // ROW_TILE, proj_body, 0)

    hp = hp_ref[...]
    scale = HEAD_DIM ** -0.5

    q = p_ref[:, 0:128]
    k = p_ref[:, 128:256] * scale
    if latent:
        cos2 = cos_ref[...]
        sin2 = sin_ref[...]
        q = q * cos2 + pltpu.roll(q, HEAD_DIM // 2, axis=1) * sin2
        k = k * cos2 + pltpu.roll(k, HEAD_DIM // 2, axis=1) * sin2
    rq_ref[...] = q
    rk_ref[...] = k
    rv_ref[...] = p_ref[:, 256:384]

    row_t = lax.broadcasted_iota(jnp.int32, (seq, HEAD_DIM), 0)

    def conv_silu(xc, w0, w1, w2):
        prev = jnp.where(row_t == 0, 0.0, pltpu.roll(xc, 1, axis=0))
        nxt = jnp.where(row_t == seq - 1, 0.0, pltpu.roll(xc, seq - 1, axis=0))
        out = prev * w0 + xc * w1 + nxt * w2
        return out * _sigmoid(out)

    mq_ref[...] = conv_silu(p_ref[:, 512:640], hp[0:1], hp[1:2], hp[2:3])
    mk_ref[...] = conv_silu(p_ref[:, 640:768], hp[3:4], hp[4:5], hp[5:6]) * scale
    lane_t = lax.broadcasted_iota(jnp.int32, (seq, HEAD_DIM), 1)
    mvx_ref[:, 0:128] = p_ref[:, 768:896].astype(BF16)
    mvx_ref[:, 128:256] = jnp.where(lane_t == 0, 1.0, 0.0).astype(BF16)

    gates = p_ref[:, 1024:1152] + hp[10:11]
    for j in range(4):
        col = jnp.broadcast_to(gates[:, j:j + 1], (seq, HEAD_DIM))
        if j % 2 == 1:
            col = _log_sigmoid(col)
        gbc_ref[j] = col

    r_i = lax.broadcasted_iota(jnp.int32, (L, L), 0)
    s_i = lax.broadcasted_iota(jnp.int32, (L, L), 1)
    r_f = r_i.astype(F32)
    s_f = s_i.astype(F32)
    lg_f = _log_sigmoid(hp[8:9])
    lg_b = _log_sigmoid(hp[9:10])
    rc_ref[0] = jnp.where(r_i >= s_i, jnp.exp(lg_f * jnp.where(r_i >= s_i, r_f - s_f, 0.0)), 0.0)
    rc_ref[1] = jnp.where(s_i >= r_i, jnp.exp(lg_b * jnp.where(s_i >= r_i, s_f - r_f, 0.0)), 0.0)
    rc_ref[2] = jnp.exp(lg_f * (r_f + 1.0))
    rc_ref[3] = jnp.exp(lg_b * (L - r_f))
    rc_ref[4] = jnp.exp(lg_f * (L - 1.0 - r_f))
    rc_ref[5] = jnp.exp(lg_b * r_f)
    chunk_decay = (jnp.exp(lg_f * float(L)), jnp.exp(lg_b * float(L)))

    if latent:
        lane_c = lax.broadcasted_iota(jnp.int32, (HEAD_DIM, HEAD_DIM), 1)
        for d in range(2):
            s_ref[d] = sr_in[d]
            c_ref[d, :, 0:128] = sc_in[d]
            n_col = jnp.broadcast_to(sn_in[d], (HEAD_DIM, HEAD_DIM)).T
            c_ref[d, :, 128:256] = jnp.where(lane_c == 0, n_col, 0.0)
            m_ref[d] = jnp.broadcast_to(sm_in[d:d + 1, :], (8, HEAD_DIM))
    else:
        s_ref[...] = jnp.zeros_like(s_ref)
        c_ref[...] = jnp.zeros_like(c_ref)
        m_ref[...] = jnp.zeros_like(m_ref)

    tri = (jnp.where(r_i >= s_i, 1.0, 0.0).astype(BF16),
           jnp.where(s_i >= r_i, 1.0, 0.0).astype(BF16))

    def ret_step(d, c):
        rows = pl.ds(pl.multiple_of(c * L, L), L)
        qb = rq_ref[rows, :].astype(BF16)
        kf = rk_ref[rows, :]
        vb = rv_ref[rows, :].astype(BF16)
        sc = (_dot_nt(qb, kf.astype(BF16)) * rc_ref[d]).astype(BF16)
        s_old = s_ref[d]
        o = _dot(sc, vb) + _dot(qb, s_old.astype(BF16)) * rc_ref[2 + d]
        s_ref[d] = s_old * chunk_decay[d] + _dot_tn((kf * rc_ref[4 + d]).astype(BF16), vb)
        return o

    def ml_step(d, c):
        rows = pl.ds(pl.multiple_of(c * L, L), L)
        qb = mq_ref[rows, :].astype(BF16)
        kf = mk_ref[rows, :]
        vx = mvx_ref[rows, :]
        li = gbc_ref[2 * d, rows, :]
        lf = gbc_ref[2 * d + 1, rows, :]
        strict = (r_i > s_i) if d == 0 else (r_i < s_i)
        x_mat = jnp.where(strict, lf, 0.0) + jnp.where(r_i == s_i, li, 0.0)
        hi, mid, lo = _split3(jnp.concatenate([x_mat, lf], axis=1))
        d_ext = _dot(tri[d], hi) + _dot(tri[d], mid) + _dot(tri[d], lo)
        bcum = d_ext[:, 128:256]
        causal = (s_i <= r_i) if d == 0 else (s_i >= r_i)
        log_intra = jnp.where(causal, d_ext[:, 0:128], -jnp.inf)
        m_prev = m_ref[d][0:1, :]
        log_inter = bcum + m_prev
        m_t = jnp.maximum(log_inter, jnp.max(log_intra, axis=1, keepdims=True))
        w_inter = jnp.exp(log_inter - m_t)
        w_intra = jnp.exp(log_intra - m_t)
        sc = (_dot_nt(qb, kf.astype(BF16)) * w_intra).astype(BF16)
        c_old = c_ref[d]
        r1 = _dot(sc, vx)
        r2 = _dot(qb, c_old.astype(BF16))
        num = r1[:, 0:128] + r2[:, 0:128] * w_inter
        den = r1[:, 128:256] + r2[:, 128:256] * w_inter
        den = jnp.broadcast_to(den[:, 0:1], (L, HEAD_DIM))
        h_out = num / jnp.maximum(jnp.abs(den), jnp.exp(-m_t))
        b_end = bcum[L - 1:L, :] if d == 0 else bcum[0:1, :]
        log_w_end = b_end - bcum + li
        m_new = jnp.maximum(b_end + m_prev, jnp.max(log_w_end, axis=0, keepdims=True))
        decay = jnp.exp(b_end + m_prev - m_new)
        w_end = jnp.exp(log_w_end - m_new)
        kv = _dot_tn((kf * w_end).astype(BF16), vx)
        c_ref[d] = c_old * jnp.concatenate([decay, decay], axis=1) + kv
        m_ref[d] = jnp.broadcast_to(m_new, (8, HEAD_DIM))
        return h_out

    def scan_body(j, carry):
        cf = j
        cb = n_chunks - 1 - j
        rows_f = pl.ds(pl.multiple_of(cf * L, L), L)
        rows_b = pl.ds(pl.multiple_of(cb * L, L), L)
        orf_ref[rows_f, :] = ret_step(0, cf)
        orb_ref[rows_b, :] = ret_step(1, cb)
        hmf_ref[rows_f, :] = ml_step(0, cf)
        hmb_ref[rows_b, :] = ml_step(1, cb)
        return carry

    lax.fori_loop(0, n_chunks, scan_body, 0)

    rg = p_ref[:, 384:512]
    ret_y = _group_norm(orf_ref[...] + orb_ref[...], hp[6:7]) * (rg * _sigmoid(rg))
    y_ref[:, 0:128] = ret_y.astype(BF16)
    ml_y = _group_norm(hmf_ref[...] + hmb_ref[...], hp[7:8]) * _sigmoid(p_ref[:, 896:1024])
    y_ref[:, 128:256] = ml_y.astype(BF16)

    if not latent:
        for d in range(2):
            so_ref[d] = s_ref[d]
            co_ref[d] = c_ref[d, :, 0:128]
            no_ref[d:d + 1, :] = c_ref[d, :, 128:256].T[0:1, :]
            mo_ref[d:d + 1, :] = m_ref[d][0:1, :]


def _mixer(x, mod3, mod_row, norm1_g, w_pair, head_params, *, latent, rope=None, states=None):
    bsz, seq, _ = x.shape
    kern = functools.partial(_mixer_kernel, latent=latent, seq=seq)
    in_specs = [
        pl.BlockSpec((None, seq, D_MODEL), lambda b, h: (b, 0, 0)),
        pl.BlockSpec((None, 6, D_MODEL), lambda b, h: (mod_row(b), 0, 0)),
        pl.BlockSpec((1, D_MODEL), lambda b, h: (0, 0)),
        pl.BlockSpec((D_MODEL, PAIR_COLS), lambda b, h: (0, h)),
        pl.BlockSpec((None, 16, HEAD_DIM), lambda b, h: (h, 0, 0)),
    ]
    args = [x, mod3, norm1_g, w_pair, head_params]
    y_shape = jax.ShapeDtypeStruct((bsz, seq, D_MODEL), BF16)
    y_spec = pl.BlockSpec((None, seq, 2 * HEAD_DIM), lambda b, h: (b, 0, h))
    if latent:
        cos2, sin2 = rope
        s_ret, s_c, s_n, s_m = states
        in_specs += [
            pl.BlockSpec((seq, HEAD_DIM), lambda b, h: (0, 0)),
            pl.BlockSpec((seq, HEAD_DIM), lambda b, h: (0, 0)),
            pl.BlockSpec((None, None, 2, None, HEAD_DIM, HEAD_DIM), lambda b, h: (b, 0, 0, h, 0, 0)),
            pl.BlockSpec((None, None, 2, None, HEAD_DIM, HEAD_DIM), lambda b, h: (b, 0, 0, h, 0, 0)),
            pl.BlockSpec((None, None, 2, 1, HEAD_DIM), lambda b, h: (b, h, 0, 0, 0)),
            pl.BlockSpec((None, None, 2, HEAD_DIM), lambda b, h: (b, h, 0, 0)),
        ]
        args += [cos2, sin2, s_ret, s_c, s_n, s_m]
        out_shape = y_shape
        out_specs = y_spec
    else:
        st = jax.ShapeDtypeStruct((bsz, 1, 2, N_HEADS, HEAD_DIM, HEAD_DIM), F32)
        vec = jax.ShapeDtypeStruct((bsz, N_HEADS, 2, HEAD_DIM), F32)
        st_spec = pl.BlockSpec((None, None, 2, None, HEAD_DIM, HEAD_DIM), lambda b, h: (b, 0, 0, h, 0, 0))
        vec_spec = pl.BlockSpec((None, None, 2, HEAD_DIM), lambda b, h: (b, h, 0, 0))
        out_shape = (y_shape, st, st, vec, vec)
        out_specs = (y_spec, st_spec, st_spec, vec_spec, vec_spec)
    scratch = [
        pltpu.VMEM((seq, D_MODEL), BF16),
        pltpu.VMEM((seq, PAIR_COLS), F32),
        pltpu.VMEM((seq, HEAD_DIM), F32),
        pltpu.VMEM((seq, HEAD_DIM), F32),
        pltpu.VMEM((seq, HEAD_DIM), F32),
        pltpu.VMEM((seq, HEAD_DIM), F32),
        pltpu.VMEM((seq, HEAD_DIM), F32),
        pltpu.VMEM((seq, 2 * HEAD_DIM), BF16),
        pltpu.VMEM((4, seq, HEAD_DIM), F32),
        pltpu.VMEM((seq, HEAD_DIM), F32),
        pltpu.VMEM((seq, HEAD_DIM), F32),
        pltpu.VMEM((seq, HEAD_DIM), F32),
        pltpu.VMEM((seq, HEAD_DIM), F32),
        pltpu.VMEM((6, CHUNK, CHUNK), F32),
        pltpu.VMEM((2, HEAD_DIM, HEAD_DIM), F32),
        pltpu.VMEM((2, HEAD_DIM, 2 * HEAD_DIM), F32),
        pltpu.VMEM((2, 8, HEAD_DIM), F32),
    ]
    return pl.pallas_call(
        kern,
        out_shape=out_shape,
        grid=(bsz, N_HEADS),
        in_specs=in_specs,
        out_specs=out_specs,
        scratch_shapes=scratch,
        compiler_params=pltpu.CompilerParams(
            dimension_semantics=("arbitrary", "arbitrary"), vmem_limit_bytes=VMEM_LIMIT),
        name="mixer_latent" if latent else "mixer_context",
    )(*args)


def _ffn_kernel(x_ref, y_ref, mod_ref, wo_ref, g2_ref, w1_ref, w2_ref, gf_ref, out_ref):
    g1 = mod_ref[2:3, :]
    sh2 = mod_ref[3:4, :]
    sc2 = mod_ref[4:5, :]
    g2 = mod_ref[5:6, :]
    x1 = x_ref[...] + g1 * _dot(y_ref[...], wo_ref[...])
    h2 = (_rms(x1, g2_ref[...]) * (1.0 + sc2) + sh2).astype(BF16)
    f = jnp.zeros_like(x1)
    for j in range(D_FF // D_MODEL):
        cols = slice(j * D_MODEL, (j + 1) * D_MODEL)
        hid = jnp.maximum(_dot(h2, w1_ref[:, cols]), 0.0)
        f = f + _dot((hid * hid).astype(BF16), w2_ref[cols, :])
    out_ref[...] = _rms(x1 + g2 * f, gf_ref[...])


def _ffn(x2d, y2d, mod3, mod_row, w_out, norm2_g, w_ff1, w_ff2, final_g):
    n_tok = x2d.shape[0]
    const = lambda i: (0, 0)
    return pl.pallas_call(
        _ffn_kernel,
        out_shape=jax.ShapeDtypeStruct((n_tok, D_MODEL), F32),
        grid=(n_tok // FFN_ROWS,),
        in_specs=[
            pl.BlockSpec((FFN_ROWS, D_MODEL), lambda i: (i, 0)),
            pl.BlockSpec((FFN_ROWS, D_MODEL), lambda i: (i, 0)),
            pl.BlockSpec((None, 6, D_MODEL), lambda i: (mod_row(i), 0, 0)),
            pl.BlockSpec((D_MODEL, D_MODEL), const, pipeline_mode=pl.Buffered(1)),
            pl.BlockSpec((1, D_MODEL), const),
            pl.BlockSpec((D_MODEL, D_FF), const, pipeline_mode=pl.Buffered(1)),
            pl.BlockSpec((D_FF, D_MODEL), const, pipeline_mode=pl.Buffered(1)),
            pl.BlockSpec((1, D_MODEL), const),
        ],
        out_specs=pl.BlockSpec((FFN_ROWS, D_MODEL), lambda i: (i, 0)),
        compiler_params=pltpu.CompilerParams(
            dimension_semantics=("arbitrary",), vmem_limit_bytes=VMEM_LIMIT),
        name="outproj_mlp",
    )(x2d, y2d, mod3, w_out, norm2_g, w_ff1, w_ff2, final_g)


def _pair_columns():
    cols = []
    for h in range(N_HEADS):
        for grp in range(8):
            base = grp * N_HEADS * HEAD_DIM + h * HEAD_DIM
            cols.extend(range(base, base + HEAD_DIM))
        gate0 = 8 * N_HEADS * HEAD_DIM
        g = [gate0 + h, gate0 + 4 + h, gate0 + 8 + h, gate0 + 12 + h]
        cols.extend(g + [g[0]] * (HEAD_DIM - 4))
    return np.asarray(cols, dtype=np.int32)


def _pair_rows():
    rows = []
    for h in range(N_HEADS):
        rows.extend(range(h * HEAD_DIM, (h + 1) * HEAD_DIM))
        rows.extend(range(N_HEADS * HEAD_DIM + h * HEAD_DIM, N_HEADS * HEAD_DIM + (h + 1) * HEAD_DIM))
    return np.asarray(rows, dtype=np.int32)


def _rope_tables(seq):
    pos = jnp.arange(seq)
    row = (pos // GRID_W).astype(F32)
    col = (pos % GRID_W).astype(F32)
    nf = HEAD_DIM // 4
    inv = ROPE_BASE ** (-jnp.arange(nf, dtype=F32) / nf)
    ang = jnp.concatenate([row[:, None] * inv, col[:, None] * inv], -1)
    cos = jnp.cos(ang)
    sin = jnp.sin(ang)
    return jnp.concatenate([cos, cos], -1), jnp.concatenate([-sin, sin], -1)


def kernel(x_prompt, x_sample, state_ret, state_mlstm_C, state_mlstm_n, state_mlstm_m, c, c_ctx,
           w_ada, b_ada, norm1_g, norm2_g, w_in, conv_w, ret_decay_logit, mlstm_gate_bias,
           ret_gn_g, mlstm_gn_g, w_out, w_ff1, w_ff2, final_g):
    assert w_ada.shape[0] == 1, "single-layer kernel"
    bp, tp, _ = x_prompt.shape
    bs, ts, _ = x_sample.shape

    cond = jnp.concatenate([c_ctx[None, :], c, jnp.zeros((8 - 1 - bs, D_MODEL), F32)], 0)
    mod = _modulation(cond, w_ada[0], b_ada)
    mod3 = mod[:1 + bs].reshape(1 + bs, 6, D_MODEL)

    w_pair = jnp.take(w_in[0], _pair_columns(), axis=1).astype(BF16)
    w_out_p = jnp.take(w_out[0], _pair_rows(), axis=0).astype(BF16)
    w1 = w_ff1[0].astype(BF16)
    w2 = w_ff2[0].astype(BF16)

    cw = conv_w[0]
    hp_rows = [cw[j, :512].reshape(N_HEADS, HEAD_DIM) for j in range(3)]
    hp_rows += [cw[j, 512:].reshape(N_HEADS, HEAD_DIM) for j in range(3)]
    hp_rows += [ret_gn_g[0].reshape(N_HEADS, HEAD_DIM), mlstm_gn_g[0].reshape(N_HEADS, HEAD_DIM)]
    hp_rows += [jnp.broadcast_to(ret_decay_logit[0, d][:, None], (N_HEADS, HEAD_DIM)) for d in range(2)]
    gate_bias = mlstm_gate_bias[0].reshape(4, N_HEADS).T
    hp_rows += [jnp.pad(gate_bias, ((0, 0), (0, HEAD_DIM - 4)))]
    hp_rows += [jnp.zeros((N_HEADS, HEAD_DIM), F32)] * (16 - len(hp_rows))
    head_params = jnp.stack(hp_rows, axis=1).astype(F32)

    g1 = norm1_g[0][None, :]
    g2 = norm2_g[0][None, :]
    gf = final_g[None, :]

    y_p, new_ret, new_c, new_n, new_m = _mixer(
        x_prompt, mod3, lambda b: 0, g1, w_pair, head_params, latent=False)
    out_p = _ffn(x_prompt.reshape(bp * tp, D_MODEL), y_p.reshape(bp * tp, D_MODEL), mod3,
                 lambda i: 0, w_out_p, g2, w1, w2, gf).reshape(bp, tp, D_MODEL)

    s_n = jnp.transpose(state_mlstm_n[:, 0], (0, 2, 1, 3))[:, :, :, None, :]
    s_m = jnp.broadcast_to(jnp.transpose(state_mlstm_m[:, 0], (0, 2, 1))[..., None],
                           (bs, N_HEADS, 2, HEAD_DIM))
    y_s = _mixer(x_sample, mod3, lambda b: 1 + b, g1, w_pair, head_params, latent=True,
                 rope=_rope_tables(ts), states=(state_ret, state_mlstm_C, s_n, s_m))
    tiles_per_seq = ts // FFN_ROWS
    out_s = _ffn(x_sample.reshape(bs * ts, D_MODEL), y_s.reshape(bs * ts, D_MODEL), mod3,
                 lambda i: 1 + i // tiles_per_seq, w_out_p, g2, w1, w2, gf).reshape(bs, ts, D_MODEL)

    new_n = jnp.transpose(new_n, (0, 2, 1, 3))[:, None]
    new_m = jnp.transpose(new_m[..., 0], (0, 2, 1))[:, None]
    return out_p, out_s, new_ret, new_c, new_n, new_m
```

```python
import functools

import numpy as np
import jax
import jax.numpy as jnp
from jax import lax
from jax.experimental import pallas as pl
from jax.experimental.pallas import tpu as pltpu

F32 = jnp.float32
BF16 = jnp.bfloat16

D_MODEL = 1024
N_HEADS = 4
HEAD_DIM = 128
CHUNK = 128
GRID_W = 64
D_FF = 4 * D_MODEL
EPS = 1e-6
ROPE_BASE = 10000.0
PAIR_COLS = 9 * HEAD_DIM
ROW_TILE = 256
FFN_ROWS = 512
CTX_SEQS_PER_STEP = 4
VMEM_LIMIT = 60 * 1024 * 1024


def _dot(a, b):
    return jnp.dot(a, b, preferred_element_type=F32)


def _dot_nt(a, b):
    return lax.dot_general(a, b, (((1,), (1,)), ((), ())), preferred_element_type=F32)


def _dot_tn(a, b):
    return lax.dot_general(a, b, (((0,), (0,)), ((), ())), preferred_element_type=F32)


def _rms(x, g):
    return x * lax.rsqrt(jnp.mean(x * x, axis=-1, keepdims=True) + EPS) * g


def _group_norm(o, g):
    mu = jnp.mean(o, axis=-1, keepdims=True)
    c = o - mu
    var = jnp.mean(c * c, axis=-1, keepdims=True)
    return c * lax.rsqrt(var + EPS) * g


def _log_sigmoid(x):
    return jnp.minimum(x, 0.0) - jnp.log1p(jnp.exp(-jnp.abs(x)))


def _sigmoid(x):
    return 1.0 / (1.0 + jnp.exp(-x))


def _split3(x):
    hi = x.astype(BF16)
    r = x - hi.astype(F32)
    mid = r.astype(BF16)
    lo = (r - mid.astype(F32)).astype(BF16)
    return hi, mid, lo


def _mod_kernel(cond_ref, w_ref, b_ref, out_ref):
    c = cond_ref[...]
    s = (c * _sigmoid(c)).astype(BF16)
    out_ref[...] = _dot(s, w_ref[...].astype(BF16)) + b_ref[...]


def _modulation(cond, w_ada, b_ada):
    n = w_ada.shape[1]
    tn = 1024
    return pl.pallas_call(
        _mod_kernel,
        out_shape=jax.ShapeDtypeStruct((cond.shape[0], n), F32),
        grid=(n // tn,),
        in_specs=[pl.BlockSpec(cond.shape, lambda j: (0, 0)),
                  pl.BlockSpec((D_MODEL, tn), lambda j: (0, j)),
                  pl.BlockSpec((1, tn), lambda j: (0, j))],
        out_specs=pl.BlockSpec((cond.shape[0], tn), lambda j: (0, j)),
        compiler_params=pltpu.CompilerParams(dimension_semantics=("arbitrary",)),
        name="adaln_mod",
    )(cond, w_ada, b_ada)


_B_END, _MAX_LWE, _M_PREV, _M_NEW, _DECAY = 0, 2, 4, 6, 8


def _mixer_kernel(*refs, latent, n_seq, cps):
    L = CHUNK
    n_chunks = n_seq * cps
    rows_total = n_chunks * L
    seq_len = cps * L
    assert seq_len & (seq_len - 1) == 0
    if latent:
        (x_ref, mod_ref, g1_ref, w_ref, hp_ref, cos_ref, sin_ref, sr_in, sc_in, sn_in, sm_in,
         y_ref, *scratch) = refs
    else:
        (x_ref, mod_ref, g1_ref, w_ref, hp_ref,
         y_ref, so_ref, co_ref, no_ref, mo_ref, *scratch) = refs
    (hn_ref, rq_ref, rk_ref, rv_ref, rg_ref, mq_ref, mk_ref, mvx_ref, mog_ref, gt_ref,
     ar_ref, am_ref, logi_ref, bc_ref, skv_ref, ckv_ref, cs_ref, rc_ref) = scratch

    head = pl.program_id(1)

    @pl.when(head == 0)
    def _():
        sh1 = mod_ref[0:1, :]
        sc1 = mod_ref[1:2, :]
        g1 = g1_ref[...]

        def body(i, carry):
            r = pl.ds(pl.multiple_of(i * ROW_TILE, ROW_TILE), ROW_TILE)
            hn_ref[r, :] = (_rms(x_ref[r, :], g1) * (1.0 + sc1) + sh1).astype(BF16)
            return carry

        lax.fori_loop(0, rows_total // ROW_TILE, body, 0)

    hp = hp_ref[...]
    scale = HEAD_DIM ** -0.5
    hn = hn_ref[...]

    pr = _dot(hn, w_ref[:, 0:256])
    q = pr[:, 0:128]
    k = pr[:, 128:256] * scale
    if latent:
        cos2 = cos_ref[...]
        sin2 = sin_ref[...]
        q = q * cos2 + pltpu.roll(q, HEAD_DIM // 2, axis=1) * sin2
        k = k * cos2 + pltpu.roll(k, HEAD_DIM // 2, axis=1) * sin2
    rq_ref[...] = q
    rk_ref[...] = k

    pv = _dot(hn, w_ref[:, 256:512])
    rv_ref[...] = pv[:, 0:128].astype(BF16)
    rg_ref[...] = pv[:, 128:256]

    row_t = lax.broadcasted_iota(jnp.int32, (rows_total, HEAD_DIM), 0)
    pos = jnp.bitwise_and(row_t, seq_len - 1)
    lane_t = lax.broadcasted_iota(jnp.int32, (rows_total, HEAD_DIM), 1)

    def conv_silu(xc, w0, w1, w2):
        prev = jnp.where(pos == 0, 0.0, pltpu.roll(xc, 1, axis=0))
        nxt = jnp.where(pos == seq_len - 1, 0.0, pltpu.roll(xc, rows_total - 1, axis=0))
        out = prev * w0 + xc * w1 + nxt * w2
        return out * _sigmoid(out)

    pm = _dot(hn, w_ref[:, 512:768])
    mq_ref[...] = conv_silu(pm[:, 0:128], hp[0:1], hp[1:2], hp[2:3])
    mk_ref[...] = conv_silu(pm[:, 128:256], hp[3:4], hp[4:5], hp[5:6]) * scale

    po = _dot(hn, w_ref[:, 768:1024])
    mvx_ref[:, 0:128] = po[:, 0:128].astype(BF16)
    mvx_ref[:, 128:256] = jnp.where(lane_t == 0, 1.0, 0.0).astype(BF16)
    mog_ref[...] = po[:, 128:256]

    gates = _dot(hn, w_ref[:, 1024:1152]) + hp[10:11]
    gt_ref[...] = jnp.where(jnp.bitwise_and(lane_t, 1) == 1, _log_sigmoid(gates), gates)

    r_i = lax.broadcasted_iota(jnp.int32, (L, L), 0)
    s_i = lax.broadcasted_iota(jnp.int32, (L, L), 1)
    r_f = r_i.astype(F32)
    s_f = s_i.astype(F32)
    lg_f = _log_sigmoid(hp[8:9])
    lg_b = _log_sigmoid(hp[9:10])
    rc_ref[0] = (jnp.where(r_i >= s_i, jnp.exp(lg_f * jnp.where(r_i >= s_i, r_f - s_f, 0.0)), 0.0)
                 + jnp.where(s_i >= r_i, jnp.exp(lg_b * jnp.where(s_i >= r_i, s_f - r_f, 0.0)), 0.0))
    rc_ref[1] = jnp.exp(lg_f * (r_f + 1.0))
    rc_ref[2] = jnp.exp(lg_b * (L - r_f))
    rc_ref[3] = jnp.exp(lg_f * (L - 1.0 - r_f))
    rc_ref[4] = jnp.exp(lg_b * r_f)
    chunk_decay = (jnp.exp(lg_f * float(L)), jnp.exp(lg_b * float(L)))

    tri = (jnp.where(r_i >= s_i, 1.0, 0.0).astype(BF16),
           jnp.where(s_i >= r_i, 1.0, 0.0).astype(BF16))

    def rows(c):
        return slice(c * L, (c + 1) * L)

    def gate_col(c, lane):
        g = gt_ref[rows(c), :]
        return jnp.broadcast_to(g[:, lane:lane + 1], (L, HEAD_DIM))

    for c in range(n_chunks):
        ar_ref[rows(c), :] = (_dot_nt(rq_ref[rows(c), :].astype(BF16), rk_ref[rows(c), :].astype(BF16))
                              * rc_ref[0]).astype(BF16)
        am_ref[rows(c), :] = _dot_nt(mq_ref[rows(c), :].astype(BF16), mk_ref[rows(c), :].astype(BF16))
    for c in range(n_chunks):
        for d in range(2):
            li = gate_col(c, 2 * d)
            lf = gate_col(c, 2 * d + 1)
            strict = (r_i > s_i) if d == 0 else (r_i < s_i)
            x_mat = jnp.where(strict, lf, jnp.where(r_i == s_i, li, 0.0))
            hi, mid, lo = _split3(jnp.concatenate([x_mat, lf], axis=1))
            d_ext = _dot(tri[d], hi) + _dot(tri[d], mid) + _dot(tri[d], lo)
            bcum = d_ext[:, 128:256]
            causal = (s_i <= r_i) if d == 0 else (s_i >= r_i)
            logi_ref[d, rows(c), :] = jnp.where(causal, d_ext[:, 0:128], -jnp.inf)
            bc_ref[d, rows(c), :] = bcum
            b_end = bcum[L - 1:L, :] if d == 0 else bcum[0:1, :]
            cs_ref[_B_END + d, c:c + 1, :] = b_end
            cs_ref[_MAX_LWE + d, c:c + 1, :] = jnp.max(b_end - bcum + li, axis=0, keepdims=True)

    m_final = {}
    for s in range(n_seq):
        for d in range(2):
            m = sm_in[d:d + 1, :] if latent else jnp.zeros((1, HEAD_DIM), F32)
            order = range(cps) if d == 0 else range(cps - 1, -1, -1)
            for j in order:
                c = s * cps + j
                cs_ref[_M_PREV + d, c:c + 1, :] = m
                b_end = cs_ref[_B_END + d, c:c + 1, :]
                m_new = jnp.maximum(b_end + m, cs_ref[_MAX_LWE + d, c:c + 1, :])
                cs_ref[_M_NEW + d, c:c + 1, :] = m_new
                cs_ref[_DECAY + d, c:c + 1, :] = jnp.exp(b_end + m - m_new)
                m = m_new
            m_final[(s, d)] = m

    for c in range(n_chunks):
        kf = rk_ref[rows(c), :]
        vb = rv_ref[rows(c), :]
        for d in range(2):
            skv_ref[d, c] = _dot_tn((kf * rc_ref[3 + d]).astype(BF16), vb)
    for c in range(n_chunks):
        kf = mk_ref[rows(c), :]
        vx = mvx_ref[rows(c), :]
        for d in range(2):
            log_w_end = cs_ref[_B_END + d, c:c + 1, :] - bc_ref[d, rows(c), :] + gate_col(c, 2 * d)
            w_end = jnp.exp(log_w_end - cs_ref[_M_NEW + d, c:c + 1, :])
            ckv_ref[d, c] = _dot_tn((kf * w_end).astype(BF16), vx)

    if latent:
        lane_c = lax.broadcasted_iota(jnp.int32, (HEAD_DIM, HEAD_DIM), 1)
    for s in range(n_seq):
        for d in range(2):
            if latent:
                s_state = sr_in[d]
                n_col = jnp.broadcast_to(sn_in[d], (HEAD_DIM, HEAD_DIM)).T
                c_state = jnp.concatenate([sc_in[d], jnp.where(lane_c == 0, n_col, 0.0)], axis=1)
            else:
                s_state = jnp.zeros((HEAD_DIM, HEAD_DIM), F32)
                c_state = jnp.zeros((HEAD_DIM, 2 * HEAD_DIM), F32)
            order = range(cps) if d == 0 else range(cps - 1, -1, -1)
            for j in order:
                c = s * cps + j
                inc = skv_ref[d, c]
                skv_ref[d, c] = s_state
                s_state = s_state * chunk_decay[d] + inc
                inc = ckv_ref[d, c]
                ckv_ref[d, c] = c_state
                decay = cs_ref[_DECAY + d, c:c + 1, :]
                c_state = c_state * jnp.concatenate([decay, decay], axis=1) + inc
            if not latent:
                so_ref[s, d] = s_state
                co_ref[s, d] = c_state[:, 0:128]
                no_ref[s, d:d + 1, :] = c_state[:, 128:256].T[0:1, :]
                mo_ref[s, d:d + 1, :] = m_final[(s, d)]

    for c in range(n_chunks):
        qf = rq_ref[rows(c), :]
        lhs = jnp.concatenate([ar_ref[rows(c), :], (qf * rc_ref[1]).astype(BF16),
                               (qf * rc_ref[2]).astype(BF16)], axis=1)
        rhs = jnp.concatenate([rv_ref[rows(c), :], skv_ref[0, c].astype(BF16),
                               skv_ref[1, c].astype(BF16)], axis=0)
        rg = rg_ref[rows(c), :]
        ret_y = _group_norm(_dot(lhs, rhs), hp[6:7]) * (rg * _sigmoid(rg))
        y_ref[rows(c), 0:128] = ret_y.astype(BF16)
    for c in range(n_chunks):
        qf = mq_ref[rows(c), :]
        a_mat = am_ref[rows(c), :]
        vx = mvx_ref[rows(c), :]
        h_sum = None
        for d in range(2):
            log_intra = logi_ref[d, rows(c), :]
            log_inter = bc_ref[d, rows(c), :] + cs_ref[_M_PREV + d, c:c + 1, :]
            m_t = jnp.maximum(log_inter, jnp.max(log_intra, axis=1, keepdims=True))
            w_inter = jnp.exp(log_inter - m_t)
            w_intra = jnp.exp(log_intra - m_t)
            lhs = jnp.concatenate([(a_mat * w_intra).astype(BF16), (qf * w_inter).astype(BF16)], axis=1)
            rhs = jnp.concatenate([vx, ckv_ref[d, c].astype(BF16)], axis=0)
            res = _dot(lhs, rhs)
            den = jnp.broadcast_to(res[:, 128:129], (L, HEAD_DIM))
            h_dir = res[:, 0:128] / jnp.maximum(jnp.abs(den), jnp.exp(-m_t))
            h_sum = h_dir if h_sum is None else h_sum + h_dir
        ml_y = _group_norm(h_sum, hp[7:8]) * _sigmoid(mog_ref[rows(c), :])
        y_ref[rows(c), 128:256] = ml_y.astype(BF16)


def _mixer(x2d, mod3, mod_row, norm1_g, w_pair, head_params, *, latent, n_seq, cps,
           rope=None, states=None):
    n_tok = x2d.shape[0]
    rows_blk = n_seq * cps * CHUNK
    n_blk = n_tok // rows_blk
    n_chunks = n_seq * cps
    kern = functools.partial(_mixer_kernel, latent=latent, n_seq=n_seq, cps=cps)
    once = pl.Buffered(1)
    in_specs = [
        pl.BlockSpec((rows_blk, D_MODEL), lambda b, h: (b, 0), pipeline_mode=once if latent else None),
        pl.BlockSpec((None, 6, D_MODEL), lambda b, h: (mod_row(b), 0, 0)),
        pl.BlockSpec((1, D_MODEL), lambda b, h: (0, 0)),
        pl.BlockSpec((D_MODEL, PAIR_COLS), lambda b, h: (0, h)),
        pl.BlockSpec((None, 16, HEAD_DIM), lambda b, h: (h, 0, 0)),
    ]
    args = [x2d, mod3, norm1_g, w_pair, head_params]
    y_shape = jax.ShapeDtypeStruct((n_tok, D_MODEL), BF16)
    y_spec = pl.BlockSpec((rows_blk, 2 * HEAD_DIM), lambda b, h: (b, h))
    if latent:
        assert n_seq == 1
        cos2, sin2 = rope
        s_ret, s_c, s_n, s_m = states
        in_specs += [
            pl.BlockSpec((rows_blk, HEAD_DIM), lambda b, h: (0, 0), pipeline_mode=once),
            pl.BlockSpec((rows_blk, HEAD_DIM), lambda b, h: (0, 0), pipeline_mode=once),
            pl.BlockSpec((None, None, 2, None, HEAD_DIM, HEAD_DIM), lambda b, h: (b, 0, 0, h, 0, 0)),
            pl.BlockSpec((None, None, 2, None, HEAD_DIM, HEAD_DIM), lambda b, h: (b, 0, 0, h, 0, 0)),
            pl.BlockSpec((None, None, 2, 1, HEAD_DIM), lambda b, h: (b, h, 0, 0, 0)),
            pl.BlockSpec((None, None, 2, HEAD_DIM), lambda b, h: (b, h, 0, 0)),
        ]
        args += [cos2, sin2, s_ret, s_c, s_n, s_m]
        out_shape = y_shape
        out_specs = y_spec
    else:
        bsz = n_blk * n_seq
        st = jax.ShapeDtypeStruct((bsz, 1, 2, N_HEADS, HEAD_DIM, HEAD_DIM), F32)
        vec = jax.ShapeDtypeStruct((bsz, N_HEADS, 2, HEAD_DIM), F32)
        st_spec = pl.BlockSpec((n_seq, None, 2, None, HEAD_DIM, HEAD_DIM), lambda b, h: (b, 0, 0, h, 0, 0))
        vec_spec = pl.BlockSpec((n_seq, None, 2, HEAD_DIM), lambda b, h: (b, h, 0, 0))
        out_shape = (y_shape, st, st, vec, vec)
        out_specs = (y_spec, st_spec, st_spec, vec_spec, vec_spec)
    col = lambda dt: pltpu.VMEM((rows_blk, HEAD_DIM), dt)
    scratch = [
        pltpu.VMEM((rows_blk, D_MODEL), BF16),
        col(F32), col(F32), col(BF16), col(F32),
        col(F32), col(F32),
        pltpu.VMEM((rows_blk, 2 * HEAD_DIM), BF16),
        col(F32), col(F32),
        col(BF16), col(F32),
        pltpu.VMEM((2, rows_blk, HEAD_DIM), F32),
        pltpu.VMEM((2, rows_blk, HEAD_DIM), F32),
        pltpu.VMEM((2, n_chunks, HEAD_DIM, HEAD_DIM), F32),
        pltpu.VMEM((2, n_chunks, HEAD_DIM, 2 * HEAD_DIM), F32),
        pltpu.VMEM((10, max(n_chunks, 8), HEAD_DIM), F32),
        pltpu.VMEM((5, CHUNK, CHUNK), F32),
    ]
    return pl.pallas_call(
        kern,
        out_shape=out_shape,
        grid=(n_blk, N_HEADS),
        in_specs=in_specs,
        out_specs=out_specs,
        scratch_shapes=scratch,
        compiler_params=pltpu.CompilerParams(
            dimension_semantics=("arbitrary", "arbitrary"), vmem_limit_bytes=VMEM_LIMIT),
        name="mixer_latent" if latent else "mixer_context",
    )(*args)


def _ffn_kernel(x_ref, y_ref, mod_ref, wo_ref, g2_ref, w1_ref, w2_ref, gf_ref, out_ref):
    g1 = mod_ref[2:3, :]
    sh2 = mod_ref[3:4, :]
    sc2 = mod_ref[4:5, :]
    g2 = mod_ref[5:6, :]
    x1 = x_ref[...] + g1 * _dot(y_ref[...], wo_ref[...])
    h2 = (_rms(x1, g2_ref[...]) * (1.0 + sc2) + sh2).astype(BF16)
    f = jnp.zeros_like(x1)
    for j in range(D_FF // D_MODEL):
        cols = slice(j * D_MODEL, (j + 1) * D_MODEL)
        hid = jnp.maximum(_dot(h2, w1_ref[:, cols]), 0.0)
        f = f + _dot((hid * hid).astype(BF16), w2_ref[cols, :])
    out_ref[...] = _rms(x1 + g2 * f, gf_ref[...])


def _ffn(x2d, y2d, mod3, mod_row, w_out, norm2_g, w_ff1, w_ff2, final_g):
    n_tok = x2d.shape[0]
    const = lambda i: (0, 0)
    return pl.pallas_call(
        _ffn_kernel,
        out_shape=jax.ShapeDtypeStruct((n_tok, D_MODEL), F32),
        grid=(n_tok // FFN_ROWS,),
        in_specs=[
            pl.BlockSpec((FFN_ROWS, D_MODEL), lambda i: (i, 0)),
            pl.BlockSpec((FFN_ROWS, D_MODEL), lambda i: (i, 0)),
            pl.BlockSpec((None, 6, D_MODEL), lambda i: (mod_row(i), 0, 0)),
            pl.BlockSpec((D_MODEL, D_MODEL), const, pipeline_mode=pl.Buffered(1)),
            pl.BlockSpec((1, D_MODEL), const),
            pl.BlockSpec((D_MODEL, D_FF), const, pipeline_mode=pl.Buffered(1)),
            pl.BlockSpec((D_FF, D_MODEL), const, pipeline_mode=pl.Buffered(1)),
            pl.BlockSpec((1, D_MODEL), const),
        ],
        out_specs=pl.BlockSpec((FFN_ROWS, D_MODEL), lambda i: (i, 0)),
        compiler_params=pltpu.CompilerParams(
            dimension_semantics=("arbitrary",), vmem_limit_bytes=VMEM_LIMIT),
        name="outproj_mlp",
    )(x2d, y2d, mod3, w_out, norm2_g, w_ff1, w_ff2, final_g)


def _pair_columns():
    cols = []
    for h in range(N_HEADS):
        for grp in range(8):
            base = grp * N_HEADS * HEAD_DIM + h * HEAD_DIM
            cols.extend(range(base, base + HEAD_DIM))
        gate0 = 8 * N_HEADS * HEAD_DIM
        g = [gate0 + h, gate0 + 4 + h, gate0 + 8 + h, gate0 + 12 + h]
        cols.extend(g + [g[0]] * (HEAD_DIM - 4))
    return np.asarray(cols, dtype=np.int32)


def _pair_rows():
    rows = []
    for h in range(N_HEADS):
        rows.extend(range(h * HEAD_DIM, (h + 1) * HEAD_DIM))
        rows.extend(range(N_HEADS * HEAD_DIM + h * HEAD_DIM, N_HEADS * HEAD_DIM + (h + 1) * HEAD_DIM))
    return np.asarray(rows, dtype=np.int32)


def _rope_tables(seq):
    pos = jnp.arange(seq)
    row = (pos // GRID_W).astype(F32)
    col = (pos % GRID_W).astype(F32)
    nf = HEAD_DIM // 4
    inv = ROPE_BASE ** (-jnp.arange(nf, dtype=F32) / nf)
    ang = jnp.concatenate([row[:, None] * inv, col[:, None] * inv], -1)
    cos = jnp.cos(ang)
    sin = jnp.sin(ang)
    return jnp.concatenate([cos, cos], -1), jnp.concatenate([-sin, sin], -1)


def kernel(x_prompt, x_sample, state_ret, state_mlstm_C, state_mlstm_n, state_mlstm_m, c, c_ctx,
           w_ada, b_ada, norm1_g, norm2_g, w_in, conv_w, ret_decay_logit, mlstm_gate_bias,
           ret_gn_g, mlstm_gn_g, w_out, w_ff1, w_ff2, final_g):
    assert w_ada.shape[0] == 1, "single-layer kernel"
    bp, tp, _ = x_prompt.shape
    bs, ts, _ = x_sample.shape
    assert tp % CHUNK == 0 and ts % CHUNK == 0 and bp % CTX_SEQS_PER_STEP == 0

    cond = jnp.concatenate([c_ctx[None, :], c, jnp.zeros((8 - 1 - bs, D_MODEL), F32)], 0)
    mod = _modulation(cond, w_ada[0], b_ada)
    mod3 = mod[:1 + bs].reshape(1 + bs, 6, D_MODEL)

    w_pair = jnp.take(w_in[0], _pair_columns(), axis=1).astype(BF16)
    w_out_p = jnp.take(w_out[0], _pair_rows(), axis=0).astype(BF16)
    w1 = w_ff1[0].astype(BF16)
    w2 = w_ff2[0].astype(BF16)

    cw = conv_w[0]
    hp_rows = [cw[j, :512].reshape(N_HEADS, HEAD_DIM) for j in range(3)]
    hp_rows += [cw[j, 512:].reshape(N_HEADS, HEAD_DIM) for j in range(3)]
    hp_rows += [ret_gn_g[0].reshape(N_HEADS, HEAD_DIM), mlstm_gn_g[0].reshape(N_HEADS, HEAD_DIM)]
    hp_rows += [jnp.broadcast_to(ret_decay_logit[0, d][:, None], (N_HEADS, HEAD_DIM)) for d in range(2)]
    gate_bias = mlstm_gate_bias[0].reshape(4, N_HEADS).T
    hp_rows += [jnp.pad(gate_bias, ((0, 0), (0, HEAD_DIM - 4)))]
    hp_rows += [jnp.zeros((N_HEADS, HEAD_DIM), F32)] * (16 - len(hp_rows))
    head_params = jnp.stack(hp_rows, axis=1).astype(F32)

    g1 = norm1_g[0][None, :]
    g2 = norm2_g[0][None, :]
    gf = final_g[None, :]
    xp2d = x_prompt.reshape(bp * tp, D_MODEL)
    xs2d = x_sample.reshape(bs * ts, D_MODEL)

    y_p, new_ret, new_c, new_n, new_m = _mixer(
        xp2d, mod3, lambda b: 0, g1, w_pair, head_params,
        latent=False, n_seq=CTX_SEQS_PER_STEP, cps=tp // CHUNK)
    out_p = _ffn(xp2d, y_p, mod3, lambda i: 0, w_out_p, g2, w1, w2, gf).reshape(bp, tp, D_MODEL)

    s_n = jnp.transpose(state_mlstm_n[:, 0], (0, 2, 1, 3))[:, :, :, None, :]
    s_m = jnp.broadcast_to(jnp.transpose(state_mlstm_m[:, 0], (0, 2, 1))[..., None],
                           (bs, N_HEADS, 2, HEAD_DIM))
    y_s = _mixer(xs2d, mod3, lambda b: 1 + b, g1, w_pair, head_params,
                 latent=True, n_seq=1, cps=ts // CHUNK,
                 rope=_rope_tables(ts), states=(state_ret, state_mlstm_C, s_n, s_m))
    tiles_per_seq = ts // FFN_ROWS
    out_s = _ffn(xs2d, y_s, mod3, lambda i: 1 + i // tiles_per_seq,
                 w_out_p, g2, w1, w2, gf).reshape(bs, ts, D_MODEL)

    new_n = jnp.transpose(new_n, (0, 2, 1, 3))[:, None]
    new_m = jnp.transpose(new_m[..., 0], (0, 2, 1))[:, None]
    return out_p, out_s, new_ret, new_c, new_n, new_m
```

```python
import functools

import jax
import jax.numpy as jnp
from jax import lax
from jax.experimental import pallas as pl
from jax.experimental.pallas import tpu as pltpu

F32 = jnp.float32
BF16 = jnp.bfloat16

D_MODEL = 1024
N_HEADS = 4
HEAD_DIM = 128
CHUNK = 128
GRID_W = 64
D_FF = 4 * D_MODEL
EPS = 1e-6
ROPE_BASE = 10000.0
PAIR_COLS = 9 * HEAD_DIM
ROW_TILE = 256
FFN_ROWS = 512
CTX_SEQS_PER_STEP = 4
VMEM_LIMIT = 60 * 1024 * 1024


def _dot(a, b):
    return jnp.dot(a, b, preferred_element_type=F32)


def _dot_nt(a, b):
    return lax.dot_general(a, b, (((1,), (1,)), ((), ())), preferred_element_type=F32)


def _dot_tn(a, b):
    return lax.dot_general(a, b, (((0,), (0,)), ((), ())), preferred_element_type=F32)


def _rms(x, g):
    return x * lax.rsqrt(jnp.mean(x * x, axis=-1, keepdims=True) + EPS) * g


def _group_norm(o, g):
    mu = jnp.mean(o, axis=-1, keepdims=True)
    c = o - mu
    var = jnp.mean(c * c, axis=-1, keepdims=True)
    return c * lax.rsqrt(var + EPS) * g


def _log_sigmoid(x):
    return jnp.minimum(x, 0.0) - jnp.log1p(jnp.exp(-jnp.abs(x)))


def _sigmoid(x):
    return 1.0 / (1.0 + jnp.exp(-x))


def _split3(x):
    hi = x.astype(BF16)
    r = x - hi.astype(F32)
    mid = r.astype(BF16)
    lo = (r - mid.astype(F32)).astype(BF16)
    return hi, mid, lo


def _mod_kernel(cond_ref, w_ref, b_ref, out_ref):
    c = cond_ref[...]
    s = (c * _sigmoid(c)).astype(BF16)
    out_ref[...] = _dot(s, w_ref[...].astype(BF16)) + b_ref[...]


def _modulation(cond, w_ada, b_ada):
    n = w_ada.shape[1]
    tn = 1024
    return pl.pallas_call(
        _mod_kernel,
        out_shape=jax.ShapeDtypeStruct((cond.shape[0], n), F32),
        grid=(n // tn,),
        in_specs=[pl.BlockSpec(cond.shape, lambda j: (0, 0)),
                  pl.BlockSpec((D_MODEL, tn), lambda j: (0, j)),
                  pl.BlockSpec((1, tn), lambda j: (0, j))],
        out_specs=pl.BlockSpec((cond.shape[0], tn), lambda j: (0, j)),
        compiler_params=pltpu.CompilerParams(dimension_semantics=("arbitrary",)),
        name="adaln_mod",
    )(cond, w_ada, b_ada)


_B_END, _MAX_LWE, _M_PREV, _M_NEW, _DECAY = 0, 2, 4, 6, 8


def _mixer_kernel(*refs, latent, n_seq, cps):
    L = CHUNK
    n_chunks = n_seq * cps
    rows_total = n_chunks * L
    seq_len = cps * L
    assert seq_len & (seq_len - 1) == 0
    if latent:
        (x_ref, mod_ref, g1_ref, w_ref, hp_ref, cos_ref, sin_ref, sr_in, sc_in, sn_in, sm_in,
         y_ref, *scratch) = refs
    else:
        (x_ref, mod_ref, g1_ref, w_ref, hp_ref,
         y_ref, so_ref, co_ref, no_ref, mo_ref, *scratch) = refs
    (hn_ref, rq_ref, rk_ref, rv_ref, rg_ref, mq_ref, mk_ref, mvx_ref, mog_ref, gt_ref,
     ar_ref, am_ref, logi_ref, bc_ref, skv_ref, ckv_ref, cs_ref, rc_ref) = scratch

    head = pl.program_id(1)

    @pl.when(head == 0)
    def _():
        sh1 = mod_ref[0:1, :]
        sc1 = mod_ref[1:2, :]
        g1 = g1_ref[...]

        def body(i, carry):
            r = pl.ds(pl.multiple_of(i * ROW_TILE, ROW_TILE), ROW_TILE)
            hn_ref[r, :] = (_rms(x_ref[r, :], g1) * (1.0 + sc1) + sh1).astype(BF16)
            return carry

        lax.fori_loop(0, rows_total // ROW_TILE, body, 0)

    hp = hp_ref[...]
    scale = HEAD_DIM ** -0.5
    hn = hn_ref[...]

    pr = _dot(hn, w_ref[:, 0:256])
    q = pr[:, 0:128]
    k = pr[:, 128:256] * scale
    if latent:
        cos2 = cos_ref[...]
        sin2 = sin_ref[...]
        q = q * cos2 + pltpu.roll(q, HEAD_DIM // 2, axis=1) * sin2
        k = k * cos2 + pltpu.roll(k, HEAD_DIM // 2, axis=1) * sin2
    rq_ref[...] = q
    rk_ref[...] = k

    pv = _dot(hn, w_ref[:, 256:512])
    rv_ref[...] = pv[:, 0:128].astype(BF16)
    rg_ref[...] = pv[:, 128:256]

    row_t = lax.broadcasted_iota(jnp.int32, (rows_total, HEAD_DIM), 0)
    pos = jnp.bitwise_and(row_t, seq_len - 1)
    lane_t = lax.broadcasted_iota(jnp.int32, (rows_total, HEAD_DIM), 1)

    def conv_silu(xc, w0, w1, w2):
        prev = jnp.where(pos == 0, 0.0, pltpu.roll(xc, 1, axis=0))
        nxt = jnp.where(pos == seq_len - 1, 0.0, pltpu.roll(xc, rows_total - 1, axis=0))
        out = prev * w0 + xc * w1 + nxt * w2
        return out * _sigmoid(out)

    pm = _dot(hn, w_ref[:, 512:768])
    mq_ref[...] = conv_silu(pm[:, 0:128], hp[0:1], hp[1:2], hp[2:3])
    mk_ref[...] = conv_silu(pm[:, 128:256], hp[3:4], hp[4:5], hp[5:6]) * scale

    po = _dot(hn, w_ref[:, 768:1024])
    mvx_ref[:, 0:128] = po[:, 0:128].astype(BF16)
    mvx_ref[:, 128:256] = jnp.ones((rows_total, HEAD_DIM), BF16)
    mog_ref[...] = po[:, 128:256]

    gates = _dot(hn, w_ref[:, 1024:1152]) + hp[10:11]
    gt_ref[...] = jnp.where(jnp.bitwise_and(lane_t, 1) == 1, _log_sigmoid(gates), gates)

    r_i = lax.broadcasted_iota(jnp.int32, (L, L), 0)
    s_i = lax.broadcasted_iota(jnp.int32, (L, L), 1)
    r_f = r_i.astype(F32)
    s_f = s_i.astype(F32)
    lg_f = _log_sigmoid(hp[8:9])
    lg_b = _log_sigmoid(hp[9:10])
    rc_ref[0] = (jnp.where(r_i >= s_i, jnp.exp(lg_f * jnp.where(r_i >= s_i, r_f - s_f, 0.0)), 0.0)
                 + jnp.where(s_i >= r_i, jnp.exp(lg_b * jnp.where(s_i >= r_i, s_f - r_f, 0.0)), 0.0))
    rc_ref[1] = jnp.exp(lg_f * (r_f + 1.0))
    rc_ref[2] = jnp.exp(lg_b * (L - r_f))
    rc_ref[3] = jnp.exp(lg_f * (L - 1.0 - r_f))
    rc_ref[4] = jnp.exp(lg_b * r_f)
    chunk_decay = (jnp.exp(lg_f * float(L)), jnp.exp(lg_b * float(L)))

    tri = (jnp.where(r_i >= s_i, 1.0, 0.0).astype(BF16),
           jnp.where(s_i >= r_i, 1.0, 0.0).astype(BF16))

    def rows(c):
        return slice(c * L, (c + 1) * L)

    def gate_col(c, lane):
        g = gt_ref[rows(c), :]
        return jnp.broadcast_to(g[:, lane:lane + 1], (L, HEAD_DIM))

    for c in range(n_chunks):
        ar_ref[rows(c), :] = (_dot_nt(rq_ref[rows(c), :].astype(BF16), rk_ref[rows(c), :].astype(BF16))
                              * rc_ref[0]).astype(BF16)
        am_ref[rows(c), :] = _dot_nt(mq_ref[rows(c), :].astype(BF16), mk_ref[rows(c), :].astype(BF16))
    for c in range(n_chunks):
        for d in range(2):
            li = gate_col(c, 2 * d)
            lf = gate_col(c, 2 * d + 1)
            strict = (r_i > s_i) if d == 0 else (r_i < s_i)
            x_mat = jnp.where(strict, lf, jnp.where(r_i == s_i, li, 0.0))
            hi, mid, lo = _split3(jnp.concatenate([x_mat, lf], axis=1))
            d_ext = _dot(tri[d], hi) + _dot(tri[d], mid) + _dot(tri[d], lo)
            bcum = d_ext[:, 128:256]
            causal = (s_i <= r_i) if d == 0 else (s_i >= r_i)
            logi_ref[d, rows(c), :] = jnp.where(causal, d_ext[:, 0:128], -jnp.inf)
            bc_ref[d, rows(c), :] = bcum
            b_end = bcum[L - 1:L, :] if d == 0 else bcum[0:1, :]
            cs_ref[_B_END + d, c:c + 1, :] = b_end
            cs_ref[_MAX_LWE + d, c:c + 1, :] = jnp.max(b_end - bcum + li, axis=0, keepdims=True)

    m_final = {}
    for s in range(n_seq):
        for d in range(2):
            m = sm_in[d:d + 1, :] if latent else jnp.zeros((1, HEAD_DIM), F32)
            order = range(cps) if d == 0 else range(cps - 1, -1, -1)
            for j in order:
                c = s * cps + j
                cs_ref[_M_PREV + d, c:c + 1, :] = m
                b_end = cs_ref[_B_END + d, c:c + 1, :]
                m_new = jnp.maximum(b_end + m, cs_ref[_MAX_LWE + d, c:c + 1, :])
                cs_ref[_M_NEW + d, c:c + 1, :] = m_new
                cs_ref[_DECAY + d, c:c + 1, :] = jnp.exp(b_end + m - m_new)
                m = m_new
            m_final[(s, d)] = m

    for c in range(n_chunks):
        kf = rk_ref[rows(c), :]
        vb = rv_ref[rows(c), :]
        for d in range(2):
            skv_ref[d, c] = _dot_tn((kf * rc_ref[3 + d]).astype(BF16), vb)
    for c in range(n_chunks):
        kf = mk_ref[rows(c), :]
        vx = mvx_ref[rows(c), :]
        for d in range(2):
            log_w_end = cs_ref[_B_END + d, c:c + 1, :] - bc_ref[d, rows(c), :] + gate_col(c, 2 * d)
            w_end = jnp.exp(log_w_end - cs_ref[_M_NEW + d, c:c + 1, :])
            ckv_ref[d, c] = _dot_tn((kf * w_end).astype(BF16), vx)

    for s in range(n_seq):
        for d in range(2):
            if latent:
                s_state = sr_in[d]
                n_rep = jnp.broadcast_to(sn_in[d], (HEAD_DIM, HEAD_DIM)).T
                c_state = jnp.concatenate([sc_in[d], n_rep], axis=1)
            else:
                s_state = jnp.zeros((HEAD_DIM, HEAD_DIM), F32)
                c_state = jnp.zeros((HEAD_DIM, 2 * HEAD_DIM), F32)
            order = range(cps) if d == 0 else range(cps - 1, -1, -1)
            for j in order:
                c = s * cps + j
                inc = skv_ref[d, c]
                skv_ref[d, c] = s_state
                s_state = s_state * chunk_decay[d] + inc
                inc = ckv_ref[d, c]
                ckv_ref[d, c] = c_state
                decay = cs_ref[_DECAY + d, c:c + 1, :]
                c_state = c_state * jnp.concatenate([decay, decay], axis=1) + inc
            if not latent:
                so_ref[s, d] = s_state
                co_ref[s, d] = c_state[:, 0:128]
                no_ref[s, d:d + 1, :] = c_state[:, 128:256].T[0:1, :]
                mo_ref[s, d:d + 1, :] = m_final[(s, d)]

    for c in range(n_chunks):
        qf = rq_ref[rows(c), :]
        lhs = jnp.concatenate([ar_ref[rows(c), :], (qf * rc_ref[1]).astype(BF16),
                               (qf * rc_ref[2]).astype(BF16)], axis=1)
        rhs = jnp.concatenate([rv_ref[rows(c), :], skv_ref[0, c].astype(BF16),
                               skv_ref[1, c].astype(BF16)], axis=0)
        rq_ref[rows(c), :] = _dot(lhs, rhs)
    for c in range(n_chunks):
        qf = mq_ref[rows(c), :]
        a_mat = am_ref[rows(c), :]
        vx = mvx_ref[rows(c), :]
        h_sum = None
        for d in range(2):
            log_intra = logi_ref[d, rows(c), :]
            log_inter = bc_ref[d, rows(c), :] + cs_ref[_M_PREV + d, c:c + 1, :]
            m_t = jnp.maximum(log_inter, jnp.max(log_intra, axis=1, keepdims=True))
            w_inter = jnp.exp(log_inter - m_t)
            w_intra = jnp.exp(log_intra - m_t)
            lhs = jnp.concatenate([(a_mat * w_intra).astype(BF16), (qf * w_inter).astype(BF16)], axis=1)
            rhs = jnp.concatenate([vx, ckv_ref[d, c].astype(BF16)], axis=0)
            res = _dot(lhs, rhs)
            h_dir = res[:, 0:128] / jnp.maximum(jnp.abs(res[:, 128:256]), jnp.exp(-m_t))
            h_sum = h_dir if h_sum is None else h_sum + h_dir
        am_ref[rows(c), :] = h_sum

    def finish(i, carry):
        r = pl.ds(pl.multiple_of(i * ROW_TILE, ROW_TILE), ROW_TILE)
        rg = rg_ref[r, :]
        ret_y = _group_norm(rq_ref[r, :], hp[6:7]) * (rg * _sigmoid(rg))
        y_ref[r, 0:128] = ret_y.astype(BF16)
        ml_y = _group_norm(am_ref[r, :], hp[7:8]) * _sigmoid(mog_ref[r, :])
        y_ref[r, 128:256] = ml_y.astype(BF16)
        return carry

    lax.fori_loop(0, rows_total // ROW_TILE, finish, 0)


def _mixer(x2d, mod3, mod_row, norm1_g, w_pair, head_params, *, latent, n_seq, cps,
           rope=None, states=None):
    n_tok = x2d.shape[0]
    rows_blk = n_seq * cps * CHUNK
    n_blk = n_tok // rows_blk
    n_chunks = n_seq * cps
    kern = functools.partial(_mixer_kernel, latent=latent, n_seq=n_seq, cps=cps)
    once = pl.Buffered(1)
    in_specs = [
        pl.BlockSpec((rows_blk, D_MODEL), lambda b, h: (b, 0), pipeline_mode=once if latent else None),
        pl.BlockSpec((None, 6, D_MODEL), lambda b, h: (mod_row(b), 0, 0)),
        pl.BlockSpec((1, D_MODEL), lambda b, h: (0, 0)),
        pl.BlockSpec((D_MODEL, PAIR_COLS), lambda b, h: (0, h)),
        pl.BlockSpec((None, 16, HEAD_DIM), lambda b, h: (h, 0, 0)),
    ]
    args = [x2d, mod3, norm1_g, w_pair, head_params]
    y_shape = jax.ShapeDtypeStruct((n_tok, D_MODEL), BF16)
    y_spec = pl.BlockSpec((rows_blk, 2 * HEAD_DIM), lambda b, h: (b, h))
    if latent:
        assert n_seq == 1
        cos2, sin2 = rope
        s_ret, s_c, s_n, s_m = states
        in_specs += [
            pl.BlockSpec((rows_blk, HEAD_DIM), lambda b, h: (0, 0), pipeline_mode=once),
            pl.BlockSpec((rows_blk, HEAD_DIM), lambda b, h: (0, 0), pipeline_mode=once),
            pl.BlockSpec((None, None, 2, None, HEAD_DIM, HEAD_DIM), lambda b, h: (b, 0, 0, h, 0, 0)),
            pl.BlockSpec((None, None, 2, None, HEAD_DIM, HEAD_DIM), lambda b, h: (b, 0, 0, h, 0, 0)),
            pl.BlockSpec((None, None, 2, 1, HEAD_DIM), lambda b, h: (b, h, 0, 0, 0)),
            pl.BlockSpec((None, None, 2, HEAD_DIM), lambda b, h: (b, h, 0, 0)),
        ]
        args += [cos2, sin2, s_ret, s_c, s_n, s_m]
        out_shape = y_shape
        out_specs = y_spec
    else:
        bsz = n_blk * n_seq
        st = jax.ShapeDtypeStruct((bsz, 1, 2, N_HEADS, HEAD_DIM, HEAD_DIM), F32)
        vec = jax.ShapeDtypeStruct((bsz, N_HEADS, 2, HEAD_DIM), F32)
        st_spec = pl.BlockSpec((n_seq, None, 2, None, HEAD_DIM, HEAD_DIM), lambda b, h: (b, 0, 0, h, 0, 0))
        vec_spec = pl.BlockSpec((n_seq, None, 2, HEAD_DIM), lambda b, h: (b, h, 0, 0))
        out_shape = (y_shape, st, st, vec, vec)
        out_specs = (y_spec, st_spec, st_spec, vec_spec, vec_spec)
    col = lambda dt: pltpu.VMEM((rows_blk, HEAD_DIM), dt)
    scratch = [
        pltpu.VMEM((rows_blk, D_MODEL), BF16),
        col(F32), col(F32), col(BF16), col(F32),
        col(F32), col(F32),
        pltpu.VMEM((rows_blk, 2 * HEAD_DIM), BF16),
        col(F32), col(F32),
        col(BF16), col(F32),
        pltpu.VMEM((2, rows_blk, HEAD_DIM), F32),
        pltpu.VMEM((2, rows_blk, HEAD_DIM), F32),
        pltpu.VMEM((2, n_chunks, HEAD_DIM, HEAD_DIM), F32),
        pltpu.VMEM((2, n_chunks, HEAD_DIM, 2 * HEAD_DIM), F32),
        pltpu.VMEM((10, max(n_chunks, 8), HEAD_DIM), F32),
        pltpu.VMEM((5, CHUNK, CHUNK), F32),
    ]
    return pl.pallas_call(
        kern,
        out_shape=out_shape,
        grid=(n_blk, N_HEADS),
        in_specs=in_specs,
        out_specs=out_specs,
        scratch_shapes=scratch,
        compiler_params=pltpu.CompilerParams(
            dimension_semantics=("arbitrary", "arbitrary"), vmem_limit_bytes=VMEM_LIMIT),
        name="mixer_latent" if latent else "mixer_context",
    )(*args)


def _ffn_kernel(x_ref, y_ref, mod_ref, wo_ref, g2_ref, w1_ref, w2_ref, gf_ref, out_ref):
    g1 = mod_ref[2:3, :]
    sh2 = mod_ref[3:4, :]
    sc2 = mod_ref[4:5, :]
    g2 = mod_ref[5:6, :]
    x1 = x_ref[...] + g1 * _dot(y_ref[...], wo_ref[...])
    h2 = (_rms(x1, g2_ref[...]) * (1.0 + sc2) + sh2).astype(BF16)
    f = jnp.zeros_like(x1)
    for j in range(D_FF // D_MODEL):
        cols = slice(j * D_MODEL, (j + 1) * D_MODEL)
        hid = jnp.maximum(_dot(h2, w1_ref[:, cols]), 0.0)
        f = f + _dot((hid * hid).astype(BF16), w2_ref[cols, :])
    out_ref[...] = _rms(x1 + g2 * f, gf_ref[...])


def _ffn(x2d, y2d, mod3, mod_row, w_out, norm2_g, w_ff1, w_ff2, final_g):
    n_tok = x2d.shape[0]
    const = lambda i: (0, 0)
    return pl.pallas_call(
        _ffn_kernel,
        out_shape=jax.ShapeDtypeStruct((n_tok, D_MODEL), F32),
        grid=(n_tok // FFN_ROWS,),
        in_specs=[
            pl.BlockSpec((FFN_ROWS, D_MODEL), lambda i: (i, 0)),
            pl.BlockSpec((FFN_ROWS, D_MODEL), lambda i: (i, 0)),
            pl.BlockSpec((None, 6, D_MODEL), lambda i: (mod_row(i), 0, 0)),
            pl.BlockSpec((D_MODEL, D_MODEL), const, pipeline_mode=pl.Buffered(1)),
            pl.BlockSpec((1, D_MODEL), const),
            pl.BlockSpec((D_MODEL, D_FF), const, pipeline_mode=pl.Buffered(1)),
            pl.BlockSpec((D_FF, D_MODEL), const, pipeline_mode=pl.Buffered(1)),
            pl.BlockSpec((1, D_MODEL), const),
        ],
        out_specs=pl.BlockSpec((FFN_ROWS, D_MODEL), lambda i: (i, 0)),
        compiler_params=pltpu.CompilerParams(
            dimension_semantics=("arbitrary",), vmem_limit_bytes=VMEM_LIMIT),
        name="outproj_mlp",
    )(x2d, y2d, mod3, w_out, norm2_g, w_ff1, w_ff2, final_g)


def _rope_tables(seq):
    pos = jnp.arange(seq)
    row = (pos // GRID_W).astype(F32)
    col = (pos % GRID_W).astype(F32)
    nf = HEAD_DIM // 4
    inv = ROPE_BASE ** (-jnp.arange(nf, dtype=F32) / nf)
    ang = jnp.concatenate([row[:, None] * inv, col[:, None] * inv], -1)
    cos = jnp.cos(ang)
    sin = jnp.sin(ang)
    return jnp.concatenate([cos, cos], -1), jnp.concatenate([-sin, sin], -1)


def kernel(x_prompt, x_sample, state_ret, state_mlstm_C, state_mlstm_n, state_mlstm_m, c, c_ctx,
           w_ada, b_ada, norm1_g, norm2_g, w_in, conv_w, ret_decay_logit, mlstm_gate_bias,
           ret_gn_g, mlstm_gn_g, w_out, w_ff1, w_ff2, final_g):
    assert w_ada.shape[0] == 1, "single-layer kernel"
    bp, tp, _ = x_prompt.shape
    bs, ts, _ = x_sample.shape
    assert tp % CHUNK == 0 and ts % CHUNK == 0 and bp % CTX_SEQS_PER_STEP == 0

    cond = jnp.concatenate([c_ctx[None, :], c, jnp.zeros((8 - 1 - bs, D_MODEL), F32)], 0)
    mod = _modulation(cond, w_ada[0], b_ada)
    mod3 = mod[:1 + bs].reshape(1 + bs, 6, D_MODEL)

    n_op = 8 * N_HEADS * HEAD_DIM
    w_ops = w_in[0][:, :n_op].reshape(D_MODEL, 8, N_HEADS, HEAD_DIM).transpose(0, 2, 1, 3)
    w_gates = w_in[0][:, n_op:].reshape(D_MODEL, 4, N_HEADS).transpose(0, 2, 1)
    w_gates = jnp.pad(w_gates, ((0, 0), (0, 0), (0, HEAD_DIM - 4)))
    w_pair = jnp.concatenate([w_ops.reshape(D_MODEL, N_HEADS, 8 * HEAD_DIM), w_gates], axis=2)
    w_pair = w_pair.reshape(D_MODEL, N_HEADS * PAIR_COLS).astype(BF16)
    w_out_p = (w_out[0].reshape(2, N_HEADS, HEAD_DIM, D_MODEL).transpose(1, 0, 2, 3)
               .reshape(D_MODEL, D_MODEL).astype(BF16))
    w1 = w_ff1[0].astype(BF16)
    w2 = w_ff2[0].astype(BF16)

    cw = conv_w[0]
    hp_rows = [cw[j, :512].reshape(N_HEADS, HEAD_DIM) for j in range(3)]
    hp_rows += [cw[j, 512:].reshape(N_HEADS, HEAD_DIM) for j in range(3)]
    hp_rows += [ret_gn_g[0].reshape(N_HEADS, HEAD_DIM), mlstm_gn_g[0].reshape(N_HEADS, HEAD_DIM)]
    hp_rows += [jnp.broadcast_to(ret_decay_logit[0, d][:, None], (N_HEADS, HEAD_DIM)) for d in range(2)]
    gate_bias = mlstm_gate_bias[0].reshape(4, N_HEADS).T
    hp_rows += [jnp.pad(gate_bias, ((0, 0), (0, HEAD_DIM - 4)))]
    hp_rows += [jnp.zeros((N_HEADS, HEAD_DIM), F32)] * (16 - len(hp_rows))
    head_params = jnp.stack(hp_rows, axis=1).astype(F32)

    g1 = norm1_g[0][None, :]
    g2 = norm2_g[0][None, :]
    gf = final_g[None, :]
    xp2d = x_prompt.reshape(bp * tp, D_MODEL)
    xs2d = x_sample.reshape(bs * ts, D_MODEL)

    y_p, new_ret, new_c, new_n, new_m = _mixer(
        xp2d, mod3, lambda b: 0, g1, w_pair, head_params,
        latent=False, n_seq=CTX_SEQS_PER_STEP, cps=tp // CHUNK)
    out_p = _ffn(xp2d, y_p, mod3, lambda i: 0, w_out_p, g2, w1, w2, gf).reshape(bp, tp, D_MODEL)

    s_n = jnp.transpose(state_mlstm_n[:, 0], (0, 2, 1, 3))[:, :, :, None, :]
    s_m = jnp.broadcast_to(jnp.transpose(state_mlstm_m[:, 0], (0, 2, 1))[..., None],
                           (bs, N_HEADS, 2, HEAD_DIM))
    y_s = _mixer(xs2d, mod3, lambda b: 1 + b, g1, w_pair, head_params,
                 latent=True, n_seq=1, cps=ts // CHUNK,
                 rope=_rope_tables(ts), states=(state_ret, state_mlstm_C, s_n, s_m))
    tiles_per_seq = ts // FFN_ROWS
    out_s = _ffn(xs2d, y_s, mod3, lambda i: 1 + i // tiles_per_seq,
                 w_out_p, g2, w1, w2, gf).reshape(bs, ts, D_MODEL)

    new_n = jnp.transpose(new_n, (0, 2, 1, 3))[:, None]
    new_m = jnp.transpose(new_m[..., 0], (0, 2, 1))[:, None]
    return out_p, out_s, new_ret, new_c, new_n, new_m
```

```python
import functools

import jax
import jax.numpy as jnp
from jax import lax
from jax.experimental import pallas as pl
from jax.experimental.pallas import tpu as pltpu

F32 = jnp.float32
BF16 = jnp.bfloat16

D_MODEL = 1024
N_HEADS = 4
HEAD_DIM = 128
CHUNK = 128
GRID_W = 64
D_FF = 4 * D_MODEL
EPS = 1e-6
ROPE_BASE = 10000.0
PAIR_COLS = 8 * HEAD_DIM
ROW_TILE = 256
FFN_ROWS = 512
CTX_SEQS_PER_STEP = 4
VMEM_LIMIT = 60 * 1024 * 1024


def _dot(a, b):
    return jnp.dot(a, b, preferred_element_type=F32)


def _dot_nt(a, b):
    return lax.dot_general(a, b, (((1,), (1,)), ((), ())), preferred_element_type=F32)


def _dot_tn(a, b):
    return lax.dot_general(a, b, (((0,), (0,)), ((), ())), preferred_element_type=F32)


def _rms(x, g):
    return x * lax.rsqrt(jnp.mean(x * x, axis=-1, keepdims=True) + EPS) * g


def _group_norm(o, g):
    mu = jnp.mean(o, axis=-1, keepdims=True)
    c = o - mu
    var = jnp.mean(c * c, axis=-1, keepdims=True)
    return c * lax.rsqrt(var + EPS) * g


def _log_sigmoid(x):
    return jnp.minimum(x, 0.0) - jnp.log1p(jnp.exp(-jnp.abs(x)))


def _sigmoid(x):
    return 1.0 / (1.0 + jnp.exp(-x))


def _split2(x):
    hi = x.astype(BF16)
    lo = (x - hi.astype(F32)).astype(BF16)
    return hi, lo


def _mod_kernel(cond_ref, w_ref, b_ref, out_ref):
    c = cond_ref[...]
    s = (c * _sigmoid(c)).astype(BF16)
    out_ref[...] = _dot(s, w_ref[...].astype(BF16)) + b_ref[...]


def _modulation(cond, w_ada, b_ada):
    n = w_ada.shape[1]
    tn = 1024
    return pl.pallas_call(
        _mod_kernel,
        out_shape=jax.ShapeDtypeStruct((cond.shape[0], n), F32),
        grid=(n // tn,),
        in_specs=[pl.BlockSpec(cond.shape, lambda j: (0, 0)),
                  pl.BlockSpec((D_MODEL, tn), lambda j: (0, j)),
                  pl.BlockSpec((1, tn), lambda j: (0, j))],
        out_specs=pl.BlockSpec((cond.shape[0], tn), lambda j: (0, j)),
        compiler_params=pltpu.CompilerParams(dimension_semantics=("arbitrary",)),
        name="adaln_mod",
    )(cond, w_ada, b_ada)


def _regroup_kernel(*refs, axis):
    *in_refs, out_ref = refs
    width = in_refs[0].shape[axis]
    for g, ref in enumerate(in_refs):
        if axis == 1:
            out_ref[:, g * width:(g + 1) * width] = ref[...].astype(BF16)
        else:
            out_ref[g * width:(g + 1) * width, :] = ref[...].astype(BF16)


def _regroup_in_proj(w_in2d):
    return pl.pallas_call(
        functools.partial(_regroup_kernel, axis=1),
        out_shape=jax.ShapeDtypeStruct((D_MODEL, N_HEADS * PAIR_COLS), BF16),
        grid=(N_HEADS,),
        in_specs=[pl.BlockSpec((D_MODEL, HEAD_DIM), lambda h, g=g: (0, g * N_HEADS + h)) for g in range(8)],
        out_specs=pl.BlockSpec((D_MODEL, PAIR_COLS), lambda h: (0, h)),
        compiler_params=pltpu.CompilerParams(dimension_semantics=("arbitrary",)),
        name="regroup_w_in",
    )(*([w_in2d] * 8))


def _regroup_out_proj(w_out2d):
    return pl.pallas_call(
        functools.partial(_regroup_kernel, axis=0),
        out_shape=jax.ShapeDtypeStruct((D_MODEL, D_MODEL), BF16),
        grid=(N_HEADS,),
        in_specs=[pl.BlockSpec((HEAD_DIM, D_MODEL), lambda h, g=g: (g * N_HEADS + h, 0)) for g in range(2)],
        out_specs=pl.BlockSpec((2 * HEAD_DIM, D_MODEL), lambda h: (h, 0)),
        compiler_params=pltpu.CompilerParams(dimension_semantics=("arbitrary",)),
        name="regroup_w_out",
    )(w_out2d, w_out2d)


_B_END, _MAX_LWE, _M_PREV, _M_NEW, _DECAY = 0, 2, 4, 6, 8


def _mixer_kernel(*refs, latent, n_seq, cps):
    L = CHUNK
    n_chunks = n_seq * cps
    rows_total = n_chunks * L
    seq_len = cps * L
    assert seq_len & (seq_len - 1) == 0
    if latent:
        (x_ref, mod_ref, g1_ref, w_ref, wg_ref, hp_ref, cos_ref, sin_ref, sr_in, sc_in, sn_in, sm_in,
         y_ref, *scratch) = refs
    else:
        (x_ref, mod_ref, g1_ref, w_ref, wg_ref, hp_ref,
         y_ref, so_ref, co_ref, no_ref, mo_ref, *scratch) = refs
    (hn_ref, rq_ref, rk_ref, rv_ref, rg_ref, mq_ref, mk_ref, mvx_ref, mog_ref, gt_ref,
     ar_ref, am_ref, logi_ref, bc_ref, skv_ref, ckv_ref, cs_ref, rc_ref) = scratch

    head = pl.program_id(1)

    @pl.when(head == 0)
    def _():
        sh1 = mod_ref[0:1, :]
        sc1 = mod_ref[1:2, :]
        g1 = g1_ref[...]

        def body(i, carry):
            r = pl.ds(pl.multiple_of(i * ROW_TILE, ROW_TILE), ROW_TILE)
            hn_ref[r, :] = (_rms(x_ref[r, :], g1) * (1.0 + sc1) + sh1).astype(BF16)
            return carry

        lax.fori_loop(0, rows_total // ROW_TILE, body, 0)

    hp = hp_ref[...]
    scale = HEAD_DIM ** -0.5
    hn = hn_ref[...]

    pr = _dot(hn, w_ref[:, 0:256])
    q = pr[:, 0:128]
    k = pr[:, 128:256] * scale
    if latent:
        cos2 = cos_ref[...]
        sin2 = sin_ref[...]
        q = q * cos2 + pltpu.roll(q, HEAD_DIM // 2, axis=1) * sin2
        k = k * cos2 + pltpu.roll(k, HEAD_DIM // 2, axis=1) * sin2
    rq_ref[...] = q
    rk_ref[...] = k

    pv = _dot(hn, w_ref[:, 256:512])
    rv_ref[...] = pv[:, 0:128].astype(BF16)
    rg_ref[...] = pv[:, 128:256]

    row_t = lax.broadcasted_iota(jnp.int32, (rows_total, HEAD_DIM), 0)
    pos = jnp.bitwise_and(row_t, seq_len - 1)
    lane_t = lax.broadcasted_iota(jnp.int32, (rows_total, HEAD_DIM), 1)

    def conv_silu(xc, w0, w1, w2):
        prev = jnp.where(pos == 0, 0.0, pltpu.roll(xc, 1, axis=0))
        nxt = jnp.where(pos == seq_len - 1, 0.0, pltpu.roll(xc, rows_total - 1, axis=0))
        out = prev * w0 + xc * w1 + nxt * w2
        return out * _sigmoid(out)

    pm = _dot(hn, w_ref[:, 512:768])
    mq_ref[...] = conv_silu(pm[:, 0:128], hp[0:1], hp[1:2], hp[2:3])
    mk_ref[...] = conv_silu(pm[:, 128:256], hp[3:4], hp[4:5], hp[5:6]) * scale

    po = _dot(hn, w_ref[:, 768:1024])
    mvx_ref[:, 0:128] = po[:, 0:128].astype(BF16)
    mvx_ref[:, 128:256] = jnp.ones((rows_total, HEAD_DIM), BF16)
    mog_ref[...] = po[:, 128:256]

    gates = _dot(hn, wg_ref[...]) + hp[10:11]
    gt_ref[...] = jnp.where(jnp.bitwise_and(lane_t, 1) == 1, _log_sigmoid(gates), gates)

    r_i = lax.broadcasted_iota(jnp.int32, (L, L), 0)
    s_i = lax.broadcasted_iota(jnp.int32, (L, L), 1)
    r_f = r_i.astype(F32)
    s_f = s_i.astype(F32)
    lg_f = _log_sigmoid(hp[8:9])
    lg_b = _log_sigmoid(hp[9:10])
    rc_ref[0] = (jnp.where(r_i >= s_i, jnp.exp(lg_f * jnp.where(r_i >= s_i, r_f - s_f, 0.0)), 0.0)
                 + jnp.where(s_i >= r_i, jnp.exp(lg_b * jnp.where(s_i >= r_i, s_f - r_f, 0.0)), 0.0))
    rc_ref[1] = jnp.exp(lg_f * (r_f + 1.0))
    rc_ref[2] = jnp.exp(lg_b * (L - r_f))
    rc_ref[3] = jnp.exp(lg_f * (L - 1.0 - r_f))
    rc_ref[4] = jnp.exp(lg_b * r_f)
    chunk_decay = (jnp.exp(lg_f * float(L)), jnp.exp(lg_b * float(L)))

    tri = (jnp.where(r_i >= s_i, 1.0, 0.0).astype(BF16),
           jnp.where(s_i >= r_i, 1.0, 0.0).astype(BF16))

    def rows(c):
        return slice(c * L, (c + 1) * L)

    def gate_col(c, lane):
        g = gt_ref[rows(c), :]
        return jnp.broadcast_to(g[:, lane:lane + 1], (L, HEAD_DIM))

    for c in range(n_chunks):
        ar_ref[rows(c), :] = (_dot_nt(rq_ref[rows(c), :].astype(BF16), rk_ref[rows(c), :].astype(BF16))
                              * rc_ref[0]).astype(BF16)
        am_ref[rows(c), :] = _dot_nt(mq_ref[rows(c), :].astype(BF16), mk_ref[rows(c), :].astype(BF16))
    for c in range(n_chunks):
        for d in range(2):
            li = gate_col(c, 2 * d)
            lf = gate_col(c, 2 * d + 1)
            strict = (r_i > s_i) if d == 0 else (r_i < s_i)
            x_mat = jnp.where(strict, lf, jnp.where(r_i == s_i, li, 0.0))
            hi, lo = _split2(jnp.concatenate([x_mat, lf], axis=1))
            d_ext = _dot(tri[d], hi) + _dot(tri[d], lo)
            bcum = d_ext[:, 128:256]
            causal = (s_i <= r_i) if d == 0 else (s_i >= r_i)
            logi_ref[d, rows(c), :] = jnp.where(causal, d_ext[:, 0:128], -jnp.inf)
            bc_ref[d, rows(c), :] = bcum
            b_end = bcum[L - 1:L, :] if d == 0 else bcum[0:1, :]
            cs_ref[_B_END + d, c:c + 1, :] = b_end
            cs_ref[_MAX_LWE + d, c:c + 1, :] = jnp.max(b_end - bcum + li, axis=0, keepdims=True)

    m_final = {}
    for s in range(n_seq):
        for d in range(2):
            m = sm_in[d:d + 1, :] if latent else jnp.zeros((1, HEAD_DIM), F32)
            order = range(cps) if d == 0 else range(cps - 1, -1, -1)
            for j in order:
                c = s * cps + j
                cs_ref[_M_PREV + d, c:c + 1, :] = m
                b_end = cs_ref[_B_END + d, c:c + 1, :]
                m_new = jnp.maximum(b_end + m, cs_ref[_MAX_LWE + d, c:c + 1, :])
                cs_ref[_M_NEW + d, c:c + 1, :] = m_new
                cs_ref[_DECAY + d, c:c + 1, :] = jnp.exp(b_end + m - m_new)
                m = m_new
            m_final[(s, d)] = m

    for c in range(n_chunks):
        kf = rk_ref[rows(c), :]
        vb = rv_ref[rows(c), :]
        for d in range(2):
            skv_ref[d, c] = _dot_tn((kf * rc_ref[3 + d]).astype(BF16), vb)
    for c in range(n_chunks):
        kf = mk_ref[rows(c), :]
        vx = mvx_ref[rows(c), :]
        for d in range(2):
            log_w_end = cs_ref[_B_END + d, c:c + 1, :] - bc_ref[d, rows(c), :] + gate_col(c, 2 * d)
            w_end = jnp.exp(log_w_end - cs_ref[_M_NEW + d, c:c + 1, :])
            ckv_ref[d, c] = _dot_tn((kf * w_end).astype(BF16), vx)

    for s in range(n_seq):
        for d in range(2):
            if latent:
                s_state = sr_in[d]
                n_rep = jnp.broadcast_to(sn_in[d], (HEAD_DIM, HEAD_DIM)).T
                c_state = jnp.concatenate([sc_in[d], n_rep], axis=1)
            else:
                s_state = jnp.zeros((HEAD_DIM, HEAD_DIM), F32)
                c_state = jnp.zeros((HEAD_DIM, 2 * HEAD_DIM), F32)
            order = range(cps) if d == 0 else range(cps - 1, -1, -1)
            for j in order:
                c = s * cps + j
                inc = skv_ref[d, c]
                skv_ref[d, c] = s_state
                s_state = s_state * chunk_decay[d] + inc
                inc = ckv_ref[d, c]
                ckv_ref[d, c] = c_state
                decay = cs_ref[_DECAY + d, c:c + 1, :]
                c_state = c_state * jnp.concatenate([decay, decay], axis=1) + inc
            if not latent:
                so_ref[s, d] = s_state
                co_ref[s, d] = c_state[:, 0:128]
                no_ref[s, d:d + 1, :] = c_state[:, 128:256].T[0:1, :]
                mo_ref[s, d:d + 1, :] = m_final[(s, d)]

    for c in range(n_chunks):
        qf = rq_ref[rows(c), :]
        lhs = jnp.concatenate([ar_ref[rows(c), :], (qf * rc_ref[1]).astype(BF16),
                               (qf * rc_ref[2]).astype(BF16)], axis=1)
        rhs = jnp.concatenate([rv_ref[rows(c), :], skv_ref[0, c].astype(BF16),
                               skv_ref[1, c].astype(BF16)], axis=0)
        rq_ref[rows(c), :] = _dot(lhs, rhs)
    for c in range(n_chunks):
        qf = mq_ref[rows(c), :]
        a_mat = am_ref[rows(c), :]
        vx = mvx_ref[rows(c), :]
        h_sum = None
        for d in range(2):
            log_intra = logi_ref[d, rows(c), :]
            log_inter = bc_ref[d, rows(c), :] + cs_ref[_M_PREV + d, c:c + 1, :]
            m_t = jnp.maximum(log_inter, jnp.max(log_intra, axis=1, keepdims=True))
            w_inter = jnp.exp(log_inter - m_t)
            w_intra = jnp.exp(log_intra - m_t)
            lhs = jnp.concatenate([(a_mat * w_intra).astype(BF16), (qf * w_inter).astype(BF16)], axis=1)
            rhs = jnp.concatenate([vx, ckv_ref[d, c].astype(BF16)], axis=0)
            res = _dot(lhs, rhs)
            h_dir = res[:, 0:128] / jnp.maximum(jnp.abs(res[:, 128:256]), jnp.exp(-m_t))
            h_sum = h_dir if h_sum is None else h_sum + h_dir
        am_ref[rows(c), :] = h_sum

    def finish(i, carry):
        r = pl.ds(pl.multiple_of(i * ROW_TILE, ROW_TILE), ROW_TILE)
        rg = rg_ref[r, :]
        ret_y = _group_norm(rq_ref[r, :], hp[6:7]) * (rg * _sigmoid(rg))
        y_ref[r, 0:128] = ret_y.astype(BF16)
        ml_y = _group_norm(am_ref[r, :], hp[7:8]) * _sigmoid(mog_ref[r, :])
        y_ref[r, 128:256] = ml_y.astype(BF16)
        return carry

    lax.fori_loop(0, rows_total // ROW_TILE, finish, 0)


def _mixer(x2d, mod3, mod_row, norm1_g, w_pair, w_gate, head_params, *, latent, n_seq, cps,
           rope=None, states=None):
    n_tok = x2d.shape[0]
    rows_blk = n_seq * cps * CHUNK
    n_blk = n_tok // rows_blk
    n_chunks = n_seq * cps
    kern = functools.partial(_mixer_kernel, latent=latent, n_seq=n_seq, cps=cps)
    once = pl.Buffered(1)
    in_specs = [
        pl.BlockSpec((rows_blk, D_MODEL), lambda b, h: (b, 0), pipeline_mode=once if latent else None),
        pl.BlockSpec((None, 6, D_MODEL), lambda b, h: (mod_row(b), 0, 0)),
        pl.BlockSpec((1, D_MODEL), lambda b, h: (0, 0)),
        pl.BlockSpec((D_MODEL, PAIR_COLS), lambda b, h: (0, h)),
        pl.BlockSpec((D_MODEL, HEAD_DIM), lambda b, h: (0, h)),
        pl.BlockSpec((None, 16, HEAD_DIM), lambda b, h: (h, 0, 0)),
    ]
    args = [x2d, mod3, norm1_g, w_pair, w_gate, head_params]
    y_shape = jax.ShapeDtypeStruct((n_tok, D_MODEL), BF16)
    y_spec = pl.BlockSpec((rows_blk, 2 * HEAD_DIM), lambda b, h: (b, h))
    if latent:
        assert n_seq == 1
        cos2, sin2 = rope
        s_ret, s_c, s_n, s_m = states
        in_specs += [
            pl.BlockSpec((rows_blk, HEAD_DIM), lambda b, h: (0, 0), pipeline_mode=once),
            pl.BlockSpec((rows_blk, HEAD_DIM), lambda b, h: (0, 0), pipeline_mode=once),
            pl.BlockSpec((None, None, 2, None, HEAD_DIM, HEAD_DIM), lambda b, h: (b, 0, 0, h, 0, 0)),
            pl.BlockSpec((None, None, 2, None, HEAD_DIM, HEAD_DIM), lambda b, h: (b, 0, 0, h, 0, 0)),
            pl.BlockSpec((None, None, 2, 1, HEAD_DIM), lambda b, h: (b, h, 0, 0, 0)),
            pl.BlockSpec((None, None, 2, HEAD_DIM), lambda b, h: (b, h, 0, 0)),
        ]
        args += [cos2, sin2, s_ret, s_c, s_n, s_m]
        out_shape = y_shape
        out_specs = y_spec
    else:
        bsz = n_blk * n_seq
        st = jax.ShapeDtypeStruct((bsz, 1, 2, N_HEADS, HEAD_DIM, HEAD_DIM), F32)
        vec = jax.ShapeDtypeStruct((bsz, N_HEADS, 2, HEAD_DIM), F32)
        st_spec = pl.BlockSpec((n_seq, None, 2, None, HEAD_DIM, HEAD_DIM), lambda b, h: (b, 0, 0, h, 0, 0))
        vec_spec = pl.BlockSpec((n_seq, None, 2, HEAD_DIM), lambda b, h: (b, h, 0, 0))
        out_shape = (y_shape, st, st, vec, vec)
        out_specs = (y_spec, st_spec, st_spec, vec_spec, vec_spec)
    col = lambda dt: pltpu.VMEM((rows_blk, HEAD_DIM), dt)
    scratch = [
        pltpu.VMEM((rows_blk, D_MODEL), BF16),
        col(F32), col(F32), col(BF16), col(F32),
        col(F32), col(F32),
        pltpu.VMEM((rows_blk, 2 * HEAD_DIM), BF16),
        col(F32), col(F32),
        col(BF16), col(F32),
        pltpu.VMEM((2, rows_blk, HEAD_DIM), F32),
        pltpu.VMEM((2, rows_blk, HEAD_DIM), F32),
        pltpu.VMEM((2, n_chunks, HEAD_DIM, HEAD_DIM), F32),
        pltpu.VMEM((2, n_chunks, HEAD_DIM, 2 * HEAD_DIM), F32),
        pltpu.VMEM((10, max(n_chunks, 8), HEAD_DIM), F32),
        pltpu.VMEM((5, CHUNK, CHUNK), F32),
    ]
    return pl.pallas_call(
        kern,
        out_shape=out_shape,
        grid=(n_blk, N_HEADS),
        in_specs=in_specs,
        out_specs=out_specs,
        scratch_shapes=scratch,
        compiler_params=pltpu.CompilerParams(
            dimension_semantics=("arbitrary", "arbitrary"), vmem_limit_bytes=VMEM_LIMIT),
        name="mixer_latent" if latent else "mixer_context",
    )(*args)


def _ffn_kernel(x_ref, y_ref, mod_ref, wo_ref, g2_ref, w1_ref, w2_ref, gf_ref, out_ref):
    g1 = mod_ref[2:3, :]
    sh2 = mod_ref[3:4, :]
    sc2 = mod_ref[4:5, :]
    g2 = mod_ref[5:6, :]
    x1 = x_ref[...] + g1 * _dot(y_ref[...], wo_ref[...])
    h2 = (_rms(x1, g2_ref[...]) * (1.0 + sc2) + sh2).astype(BF16)
    f = jnp.zeros_like(x1)
    for j in range(D_FF // D_MODEL):
        cols = slice(j * D_MODEL, (j + 1) * D_MODEL)
        hid = jnp.maximum(_dot(h2, w1_ref[:, cols]), 0.0)
        f = f + _dot((hid * hid).astype(BF16), w2_ref[cols, :])
    out_ref[...] = _rms(x1 + g2 * f, gf_ref[...])


def _ffn(x2d, y2d, mod3, mod_row, w_out, norm2_g, w_ff1, w_ff2, final_g):
    n_tok = x2d.shape[0]
    const = lambda i: (0, 0)
    return pl.pallas_call(
        _ffn_kernel,
        out_shape=jax.ShapeDtypeStruct((n_tok, D_MODEL), F32),
        grid=(n_tok // FFN_ROWS,),
        in_specs=[
            pl.BlockSpec((FFN_ROWS, D_MODEL), lambda i: (i, 0)),
            pl.BlockSpec((FFN_ROWS, D_MODEL), lambda i: (i, 0)),
            pl.BlockSpec((None, 6, D_MODEL), lambda i: (mod_row(i), 0, 0)),
            pl.BlockSpec((D_MODEL, D_MODEL), const, pipeline_mode=pl.Buffered(1)),
            pl.BlockSpec((1, D_MODEL), const),
            pl.BlockSpec((D_MODEL, D_FF), const, pipeline_mode=pl.Buffered(1)),
            pl.BlockSpec((D_FF, D_MODEL), const, pipeline_mode=pl.Buffered(1)),
            pl.BlockSpec((1, D_MODEL), const),
        ],
        out_specs=pl.BlockSpec((FFN_ROWS, D_MODEL), lambda i: (i, 0)),
        compiler_params=pltpu.CompilerParams(
            dimension_semantics=("arbitrary",), vmem_limit_bytes=VMEM_LIMIT),
        name="outproj_mlp",
    )(x2d, y2d, mod3, w_out, norm2_g, w_ff1, w_ff2, final_g)


def _rope_tables(seq):
    pos = jnp.arange(seq)
    row = (pos // GRID_W).astype(F32)
    col = (pos % GRID_W).astype(F32)
    nf = HEAD_DIM // 4
    inv = ROPE_BASE ** (-jnp.arange(nf, dtype=F32) / nf)
    ang = jnp.concatenate([row[:, None] * inv, col[:, None] * inv], -1)
    cos = jnp.cos(ang)
    sin = jnp.sin(ang)
    return jnp.concatenate([cos, cos], -1), jnp.concatenate([-sin, sin], -1)


def kernel(x_prompt, x_sample, state_ret, state_mlstm_C, state_mlstm_n, state_mlstm_m, c, c_ctx,
           w_ada, b_ada, norm1_g, norm2_g, w_in, conv_w, ret_decay_logit, mlstm_gate_bias,
           ret_gn_g, mlstm_gn_g, w_out, w_ff1, w_ff2, final_g):
    assert w_ada.shape[0] == 1, "single-layer kernel"
    bp, tp, _ = x_prompt.shape
    bs, ts, _ = x_sample.shape
    assert tp % CHUNK == 0 and ts % CHUNK == 0 and bp % CTX_SEQS_PER_STEP == 0

    cond = jnp.concatenate([c_ctx[None, :], c, jnp.zeros((8 - 1 - bs, D_MODEL), F32)], 0)
    mod = _modulation(cond, w_ada[0], b_ada)
    mod3 = mod[:1 + bs].reshape(1 + bs, 6, D_MODEL)

    w_in2d = w_in.reshape(D_MODEL, w_in.shape[-1])
    w_pair = _regroup_in_proj(w_in2d)
    w_gate = w_in2d[:, N_HEADS * PAIR_COLS:].reshape(D_MODEL, 4, N_HEADS).transpose(0, 2, 1)
    w_gate = jnp.pad(w_gate, ((0, 0), (0, 0), (0, HEAD_DIM - 4))).reshape(D_MODEL, N_HEADS * HEAD_DIM)
    w_gate = w_gate.astype(BF16)
    w_out_p = _regroup_out_proj(w_out.reshape(D_MODEL, D_MODEL))
    w1 = w_ff1[0].astype(BF16)
    w2 = w_ff2[0].astype(BF16)

    cw = conv_w[0]
    hp_rows = [cw[j, :512].reshape(N_HEADS, HEAD_DIM) for j in range(3)]
    hp_rows += [cw[j, 512:].reshape(N_HEADS, HEAD_DIM) for j in range(3)]
    hp_rows += [ret_gn_g[0].reshape(N_HEADS, HEAD_DIM), mlstm_gn_g[0].reshape(N_HEADS, HEAD_DIM)]
    hp_rows += [jnp.broadcast_to(ret_decay_logit[0, d][:, None], (N_HEADS, HEAD_DIM)) for d in range(2)]
    gate_bias = mlstm_gate_bias[0].reshape(4, N_HEADS).T
    hp_rows += [jnp.pad(gate_bias, ((0, 0), (0, HEAD_DIM - 4)))]
    hp_rows += [jnp.zeros((N_HEADS, HEAD_DIM), F32)] * (16 - len(hp_rows))
    head_params = jnp.stack(hp_rows, axis=1).astype(F32)

    g1 = norm1_g[0][None, :]
    g2 = norm2_g[0][None, :]
    gf = final_g[None, :]
    xp2d = x_prompt.reshape(bp * tp, D_MODEL)
    xs2d = x_sample.reshape(bs * ts, D_MODEL)

    y_p, new_ret, new_c, new_n, new_m = _mixer(
        xp2d, mod3, lambda b: 0, g1, w_pair, w_gate, head_params,
        latent=False, n_seq=CTX_SEQS_PER_STEP, cps=tp // CHUNK)
    out_p = _ffn(xp2d, y_p, mod3, lambda i: 0, w_out_p, g2, w1, w2, gf).reshape(bp, tp, D_MODEL)

    s_n = jnp.transpose(state_mlstm_n[:, 0], (0, 2, 1, 3))[:, :, :, None, :]
    s_m = jnp.broadcast_to(jnp.transpose(state_mlstm_m[:, 0], (0, 2, 1))[..., None],
                           (bs, N_HEADS, 2, HEAD_DIM))
    y_s = _mixer(xs2d, mod3, lambda b: 1 + b, g1, w_pair, w_gate, head_params,
                 latent=True, n_seq=1, cps=ts // CHUNK,
                 rope=_rope_tables(ts), states=(state_ret, state_mlstm_C, s_n, s_m))
    tiles_per_seq = ts // FFN_ROWS
    out_s = _ffn(xs2d, y_s, mod3, lambda i: 1 + i // tiles_per_seq,
                 w_out_p, g2, w1, w2, gf).reshape(bs, ts, D_MODEL)

    new_n = jnp.transpose(new_n, (0, 2, 1, 3))[:, None]
    new_m = jnp.transpose(new_m[..., 0], (0, 2, 1))[:, None]
    return out_p, out_s, new_ret, new_c, new_n, new_m
```

```python
import functools

import jax
import jax.numpy as jnp
from jax import lax
from jax.experimental import pallas as pl
from jax.experimental.pallas import tpu as pltpu

F32 = jnp.float32
BF16 = jnp.bfloat16

D_MODEL = 1024
N_HEADS = 4
HEAD_DIM = 128
CHUNK = 128
GRID_W = 64
D_FF = 4 * D_MODEL
EPS = 1e-6
ROPE_BASE = 10000.0
PAIR_COLS = 8 * HEAD_DIM
ROW_TILE = 256
FFN_ROWS = 512
CTX_SEQS_PER_STEP = 4
VMEM_LIMIT = 60 * 1024 * 1024


def _dot(a, b):
    return jnp.dot(a, b, preferred_element_type=F32)


def _dot_nt(a, b):
    return lax.dot_general(a, b, (((1,), (1,)), ((), ())), preferred_element_type=F32)


def _dot_tn(a, b):
    return lax.dot_general(a, b, (((0,), (0,)), ((), ())), preferred_element_type=F32)


def _rms(x, g):
    return x * lax.rsqrt(jnp.mean(x * x, axis=-1, keepdims=True) + EPS) * g


def _group_norm(o, g):
    mu = jnp.mean(o, axis=-1, keepdims=True)
    c = o - mu
    var = jnp.mean(c * c, axis=-1, keepdims=True)
    return c * lax.rsqrt(var + EPS) * g


def _log_sigmoid(x):
    return jnp.minimum(x, 0.0) - jnp.log1p(jnp.exp(-jnp.abs(x)))


def _sigmoid(x):
    return 1.0 / (1.0 + jnp.exp(-x))


def _split2(x):
    hi = x.astype(BF16)
    lo = (x - hi.astype(F32)).astype(BF16)
    return hi, lo


def _mod_kernel(cond_ref, w_ref, b_ref, out_ref):
    c = cond_ref[...]
    s = (c * _sigmoid(c)).astype(BF16)
    out_ref[...] = _dot(s, w_ref[...].astype(BF16)) + b_ref[...]


def _modulation(cond, w_ada, b_ada):
    n = w_ada.shape[1]
    tn = 1024
    return pl.pallas_call(
        _mod_kernel,
        out_shape=jax.ShapeDtypeStruct((cond.shape[0], n), F32),
        grid=(n // tn,),
        in_specs=[pl.BlockSpec(cond.shape, lambda j: (0, 0)),
                  pl.BlockSpec((D_MODEL, tn), lambda j: (0, j)),
                  pl.BlockSpec((1, tn), lambda j: (0, j))],
        out_specs=pl.BlockSpec((cond.shape[0], tn), lambda j: (0, j)),
        compiler_params=pltpu.CompilerParams(dimension_semantics=("arbitrary",)),
        name="adaln_mod",
    )(cond, w_ada, b_ada)


REGROUP_ROWS = 256


def _regroup_in_kernel(w_ref, out_ref):
    for h in range(N_HEADS):
        for g in range(8):
            src = (g * N_HEADS + h) * HEAD_DIM
            dst = (h * 8 + g) * HEAD_DIM
            out_ref[:, dst:dst + HEAD_DIM] = w_ref[:, src:src + HEAD_DIM].astype(BF16)


def _regroup_in_proj(w_in):
    n_cols = w_in.shape[-1]
    return pl.pallas_call(
        _regroup_in_kernel,
        out_shape=jax.ShapeDtypeStruct((D_MODEL, N_HEADS * PAIR_COLS), BF16),
        grid=(D_MODEL // REGROUP_ROWS,),
        in_specs=[pl.BlockSpec((None, REGROUP_ROWS, n_cols), lambda i: (0, i, 0))],
        out_specs=pl.BlockSpec((REGROUP_ROWS, N_HEADS * PAIR_COLS), lambda i: (i, 0)),
        compiler_params=pltpu.CompilerParams(dimension_semantics=("arbitrary",)),
        name="regroup_w_in",
    )(w_in)


def _regroup_out_kernel(w_ref, out_ref):
    for h in range(N_HEADS):
        for g in range(2):
            src = (g * N_HEADS + h) * HEAD_DIM
            dst = (h * 2 + g) * HEAD_DIM
            out_ref[dst:dst + HEAD_DIM, :] = w_ref[src:src + HEAD_DIM, :].astype(BF16)


def _regroup_out_proj(w_out):
    cols = 2 * HEAD_DIM
    return pl.pallas_call(
        _regroup_out_kernel,
        out_shape=jax.ShapeDtypeStruct((D_MODEL, D_MODEL), BF16),
        grid=(D_MODEL // cols,),
        in_specs=[pl.BlockSpec((None, D_MODEL, cols), lambda j: (0, 0, j))],
        out_specs=pl.BlockSpec((D_MODEL, cols), lambda j: (0, j)),
        compiler_params=pltpu.CompilerParams(dimension_semantics=("arbitrary",)),
        name="regroup_w_out",
    )(w_out)


_B_END, _MAX_LWE, _M_PREV, _M_NEW, _DECAY = 0, 2, 4, 6, 8


def _mixer_kernel(*refs, latent, n_seq, cps):
    L = CHUNK
    n_chunks = n_seq * cps
    rows_total = n_chunks * L
    seq_len = cps * L
    assert seq_len & (seq_len - 1) == 0
    if latent:
        (x_ref, mod_ref, g1_ref, w_ref, wg_ref, hp_ref, cos_ref, sin_ref, sr_in, sc_in, sn_in, sm_in,
         y_ref, *scratch) = refs
    else:
        (x_ref, mod_ref, g1_ref, w_ref, wg_ref, hp_ref,
         y_ref, so_ref, co_ref, no_ref, mo_ref, *scratch) = refs
    (hn_ref, rq_ref, rk_ref, rv_ref, rg_ref, mq_ref, mk_ref, mvx_ref, mog_ref, gt_ref,
     ar_ref, am_ref, logi_ref, bc_ref, skv_ref, ckv_ref, cs_ref, rc_ref) = scratch

    head = pl.program_id(1)

    @pl.when(head == 0)
    def _():
        sh1 = mod_ref[0:1, :]
        sc1 = mod_ref[1:2, :]
        g1 = g1_ref[...]

        def body(i, carry):
            r = pl.ds(pl.multiple_of(i * ROW_TILE, ROW_TILE), ROW_TILE)
            hn_ref[r, :] = (_rms(x_ref[r, :], g1) * (1.0 + sc1) + sh1).astype(BF16)
            return carry

        lax.fori_loop(0, rows_total // ROW_TILE, body, 0)

    hp = hp_ref[...]
    scale = HEAD_DIM ** -0.5
    hn = hn_ref[...]

    pr = _dot(hn, w_ref[:, 0:256])
    q = pr[:, 0:128]
    k = pr[:, 128:256] * scale
    if latent:
        cos2 = cos_ref[...]
        sin2 = sin_ref[...]
        q = q * cos2 + pltpu.roll(q, HEAD_DIM // 2, axis=1) * sin2
        k = k * cos2 + pltpu.roll(k, HEAD_DIM // 2, axis=1) * sin2
    rq_ref[...] = q
    rk_ref[...] = k

    pv = _dot(hn, w_ref[:, 256:512])
    rv_ref[...] = pv[:, 0:128].astype(BF16)
    rg_ref[...] = pv[:, 128:256]

    row_t = lax.broadcasted_iota(jnp.int32, (rows_total, HEAD_DIM), 0)
    pos = jnp.bitwise_and(row_t, seq_len - 1)
    lane_t = lax.broadcasted_iota(jnp.int32, (rows_total, HEAD_DIM), 1)

    def conv_silu(xc, w0, w1, w2):
        prev = jnp.where(pos == 0, 0.0, pltpu.roll(xc, 1, axis=0))
        nxt = jnp.where(pos == seq_len - 1, 0.0, pltpu.roll(xc, rows_total - 1, axis=0))
        out = prev * w0 + xc * w1 + nxt * w2
        return out * _sigmoid(out)

    pm = _dot(hn, w_ref[:, 512:768])
    mq_ref[...] = conv_silu(pm[:, 0:128], hp[0:1], hp[1:2], hp[2:3])
    mk_ref[...] = conv_silu(pm[:, 128:256], hp[3:4], hp[4:5], hp[5:6]) * scale

    po = _dot(hn, w_ref[:, 768:1024])
    mvx_ref[:, 0:128] = po[:, 0:128].astype(BF16)
    mvx_ref[:, 128:256] = jnp.ones((rows_total, HEAD_DIM), BF16)
    mog_ref[...] = po[:, 128:256]

    gates = _dot(hn, wg_ref[...]) + hp[10:11]
    gt_ref[...] = jnp.where(jnp.bitwise_and(lane_t, 1) == 1, _log_sigmoid(gates), gates)

    r_i = lax.broadcasted_iota(jnp.int32, (L, L), 0)
    s_i = lax.broadcasted_iota(jnp.int32, (L, L), 1)
    r_f = r_i.astype(F32)
    s_f = s_i.astype(F32)
    lg_f = _log_sigmoid(hp[8:9])
    lg_b = _log_sigmoid(hp[9:10])
    rc_ref[0] = (jnp.where(r_i >= s_i, jnp.exp(lg_f * jnp.where(r_i >= s_i, r_f - s_f, 0.0)), 0.0)
                 + jnp.where(s_i >= r_i, jnp.exp(lg_b * jnp.where(s_i >= r_i, s_f - r_f, 0.0)), 0.0))
    rc_ref[1] = jnp.exp(lg_f * (r_f + 1.0))
    rc_ref[2] = jnp.exp(lg_b * (L - r_f))
    rc_ref[3] = jnp.exp(lg_f * (L - 1.0 - r_f))
    rc_ref[4] = jnp.exp(lg_b * r_f)
    chunk_decay = (jnp.exp(lg_f * float(L)), jnp.exp(lg_b * float(L)))

    tri = (jnp.where(r_i >= s_i, 1.0, 0.0).astype(BF16),
           jnp.where(s_i >= r_i, 1.0, 0.0).astype(BF16))

    def rows(c):
        return slice(c * L, (c + 1) * L)

    def gate_col(c, lane):
        g = gt_ref[rows(c), :]
        return jnp.broadcast_to(g[:, lane:lane + 1], (L, HEAD_DIM))

    for c in range(n_chunks):
        ar_ref[rows(c), :] = (_dot_nt(rq_ref[rows(c), :].astype(BF16), rk_ref[rows(c), :].astype(BF16))
                              * rc_ref[0]).astype(BF16)
        am_ref[rows(c), :] = _dot_nt(mq_ref[rows(c), :].astype(BF16), mk_ref[rows(c), :].astype(BF16))
    for c in range(n_chunks):
        for d in range(2):
            li = gate_col(c, 2 * d)
            lf = gate_col(c, 2 * d + 1)
            strict = (r_i > s_i) if d == 0 else (r_i < s_i)
            x_mat = jnp.where(strict, lf, jnp.where(r_i == s_i, li, 0.0))
            hi, lo = _split2(jnp.concatenate([x_mat, lf], axis=1))
            d_ext = _dot(tri[d], hi) + _dot(tri[d], lo)
            bcum = d_ext[:, 128:256]
            causal = (s_i <= r_i) if d == 0 else (s_i >= r_i)
            logi_ref[d, rows(c), :] = jnp.where(causal, d_ext[:, 0:128], -jnp.inf)
            bc_ref[d, rows(c), :] = bcum
            b_end = bcum[L - 1:L, :] if d == 0 else bcum[0:1, :]
            cs_ref[_B_END + d, c:c + 1, :] = b_end
            cs_ref[_MAX_LWE + d, c:c + 1, :] = jnp.max(b_end - bcum + li, axis=0, keepdims=True)

    m_final = {}
    for s in range(n_seq):
        for d in range(2):
            m = sm_in[d:d + 1, :] if latent else jnp.zeros((1, HEAD_DIM), F32)
            order = range(cps) if d == 0 else range(cps - 1, -1, -1)
            for j in order:
                c = s * cps + j
                cs_ref[_M_PREV + d, c:c + 1, :] = m
                b_end = cs_ref[_B_END + d, c:c + 1, :]
                m_new = jnp.maximum(b_end + m, cs_ref[_MAX_LWE + d, c:c + 1, :])
                cs_ref[_M_NEW + d, c:c + 1, :] = m_new
                cs_ref[_DECAY + d, c:c + 1, :] = jnp.exp(b_end + m - m_new)
                m = m_new
            m_final[(s, d)] = m

    for c in range(n_chunks):
        kf = rk_ref[rows(c), :]
        vb = rv_ref[rows(c), :]
        for d in range(2):
            skv_ref[d, c] = _dot_tn((kf * rc_ref[3 + d]).astype(BF16), vb)
    for c in range(n_chunks):
        kf = mk_ref[rows(c), :]
        vx = mvx_ref[rows(c), :]
        for d in range(2):
            log_w_end = cs_ref[_B_END + d, c:c + 1, :] - bc_ref[d, rows(c), :] + gate_col(c, 2 * d)
            w_end = jnp.exp(log_w_end - cs_ref[_M_NEW + d, c:c + 1, :])
            ckv_ref[d, c] = _dot_tn((kf * w_end).astype(BF16), vx)

    for s in range(n_seq):
        for d in range(2):
            if latent:
                s_state = sr_in[d]
                n_rep = jnp.broadcast_to(sn_in[d], (HEAD_DIM, HEAD_DIM)).T
                c_state = jnp.concatenate([sc_in[d], n_rep], axis=1)
            else:
                s_state = jnp.zeros((HEAD_DIM, HEAD_DIM), F32)
                c_state = jnp.zeros((HEAD_DIM, 2 * HEAD_DIM), F32)
            order = range(cps) if d == 0 else range(cps - 1, -1, -1)
            for j in order:
                c = s * cps + j
                inc = skv_ref[d, c]
                skv_ref[d, c] = s_state
                s_state = s_state * chunk_decay[d] + inc
                inc = ckv_ref[d, c]
                ckv_ref[d, c] = c_state
                decay = cs_ref[_DECAY + d, c:c + 1, :]
                c_state = c_state * jnp.concatenate([decay, decay], axis=1) + inc
            if not latent:
                so_ref[s, d] = s_state
                co_ref[s, d] = c_state[:, 0:128]
                no_ref[s, d:d + 1, :] = c_state[:, 128:256].T[0:1, :]
                mo_ref[s, d:d + 1, :] = m_final[(s, d)]

    for c in range(n_chunks):
        qf = rq_ref[rows(c), :]
        lhs = jnp.concatenate([ar_ref[rows(c), :], (qf * rc_ref[1]).astype(BF16),
                               (qf * rc_ref[2]).astype(BF16)], axis=1)
        rhs = jnp.concatenate([rv_ref[rows(c), :], skv_ref[0, c].astype(BF16),
                               skv_ref[1, c].astype(BF16)], axis=0)
        rq_ref[rows(c), :] = _dot(lhs, rhs)
    for c in range(n_chunks):
        qf = mq_ref[rows(c), :]
        a_mat = am_ref[rows(c), :]
        vx = mvx_ref[rows(c), :]
        h_sum = None
        for d in range(2):
            log_intra = logi_ref[d, rows(c), :]
            log_inter = bc_ref[d, rows(c), :] + cs_ref[_M_PREV + d, c:c + 1, :]
            m_t = jnp.maximum(log_inter, jnp.max(log_intra, axis=1, keepdims=True))
            w_inter = jnp.exp(log_inter - m_t)
            w_intra = jnp.exp(log_intra - m_t)
            lhs = jnp.concatenate([(a_mat * w_intra).astype(BF16), (qf * w_inter).astype(BF16)], axis=1)
            rhs = jnp.concatenate([vx, ckv_ref[d, c].astype(BF16)], axis=0)
            res = _dot(lhs, rhs)
            h_dir = res[:, 0:128] / jnp.maximum(jnp.abs(res[:, 128:256]), jnp.exp(-m_t))
            h_sum = h_dir if h_sum is None else h_sum + h_dir
        am_ref[rows(c), :] = h_sum

    def finish(i, carry):
        r = pl.ds(pl.multiple_of(i * ROW_TILE, ROW_TILE), ROW_TILE)
        rg = rg_ref[r, :]
        ret_y = _group_norm(rq_ref[r, :], hp[6:7]) * (rg * _sigmoid(rg))
        y_ref[r, 0:128] = ret_y.astype(BF16)
        ml_y = _group_norm(am_ref[r, :], hp[7:8]) * _sigmoid(mog_ref[r, :])
        y_ref[r, 128:256] = ml_y.astype(BF16)
        return carry

    lax.fori_loop(0, rows_total // ROW_TILE, finish, 0)


def _mixer(x2d, mod3, mod_row, norm1_g, w_pair, w_gate, head_params, *, latent, n_seq, cps,
           rope=None, states=None):
    n_tok = x2d.shape[0]
    rows_blk = n_seq * cps * CHUNK
    n_blk = n_tok // rows_blk
    n_chunks = n_seq * cps
    kern = functools.partial(_mixer_kernel, latent=latent, n_seq=n_seq, cps=cps)
    once = pl.Buffered(1)
    in_specs = [
        pl.BlockSpec((rows_blk, D_MODEL), lambda b, h: (b, 0), pipeline_mode=once if latent else None),
        pl.BlockSpec((None, 6, D_MODEL), lambda b, h: (mod_row(b), 0, 0)),
        pl.BlockSpec((1, D_MODEL), lambda b, h: (0, 0)),
        pl.BlockSpec((D_MODEL, PAIR_COLS), lambda b, h: (0, h)),
        pl.BlockSpec((D_MODEL, HEAD_DIM), lambda b, h: (0, h)),
        pl.BlockSpec((None, 16, HEAD_DIM), lambda b, h: (h, 0, 0)),
    ]
    args = [x2d, mod3, norm1_g, w_pair, w_gate, head_params]
    y_shape = jax.ShapeDtypeStruct((n_tok, D_MODEL), BF16)
    y_spec = pl.BlockSpec((rows_blk, 2 * HEAD_DIM), lambda b, h: (b, h))
    if latent:
        assert n_seq == 1
        cos2, sin2 = rope
        s_ret, s_c, s_n, s_m = states
        in_specs += [
            pl.BlockSpec((rows_blk, HEAD_DIM), lambda b, h: (0, 0), pipeline_mode=once),
            pl.BlockSpec((rows_blk, HEAD_DIM), lambda b, h: (0, 0), pipeline_mode=once),
            pl.BlockSpec((None, None, 2, None, HEAD_DIM, HEAD_DIM), lambda b, h: (b, 0, 0, h, 0, 0)),
            pl.BlockSpec((None, None, 2, None, HEAD_DIM, HEAD_DIM), lambda b, h: (b, 0, 0, h, 0, 0)),
            pl.BlockSpec((None, None, 2, 1, HEAD_DIM), lambda b, h: (b, h, 0, 0, 0)),
            pl.BlockSpec((None, None, 2, HEAD_DIM), lambda b, h: (b, h, 0, 0)),
        ]
        args += [cos2, sin2, s_ret, s_c, s_n, s_m]
        out_shape = y_shape
        out_specs = y_spec
    else:
        bsz = n_blk * n_seq
        st = jax.ShapeDtypeStruct((bsz, 1, 2, N_HEADS, HEAD_DIM, HEAD_DIM), F32)
        vec = jax.ShapeDtypeStruct((bsz, N_HEADS, 2, HEAD_DIM), F32)
        st_spec = pl.BlockSpec((n_seq, None, 2, None, HEAD_DIM, HEAD_DIM), lambda b, h: (b, 0, 0, h, 0, 0))
        vec_spec = pl.BlockSpec((n_seq, None, 2, HEAD_DIM), lambda b, h: (b, h, 0, 0))
        out_shape = (y_shape, st, st, vec, vec)
        out_specs = (y_spec, st_spec, st_spec, vec_spec, vec_spec)
    col = lambda dt: pltpu.VMEM((rows_blk, HEAD_DIM), dt)
    scratch = [
        pltpu.VMEM((rows_blk, D_MODEL), BF16),
        col(F32), col(F32), col(BF16), col(F32),
        col(F32), col(F32),
        pltpu.VMEM((rows_blk, 2 * HEAD_DIM), BF16),
        col(F32), col(F32),
        col(BF16), col(F32),
        pltpu.VMEM((2, rows_blk, HEAD_DIM), F32),
        pltpu.VMEM((2, rows_blk, HEAD_DIM), F32),
        pltpu.VMEM((2, n_chunks, HEAD_DIM, HEAD_DIM), F32),
        pltpu.VMEM((2, n_chunks, HEAD_DIM, 2 * HEAD_DIM), F32),
        pltpu.VMEM((10, max(n_chunks, 8), HEAD_DIM), F32),
        pltpu.VMEM((5, CHUNK, CHUNK), F32),
    ]
    return pl.pallas_call(
        kern,
        out_shape=out_shape,
        grid=(n_blk, N_HEADS),
        in_specs=in_specs,
        out_specs=out_specs,
        scratch_shapes=scratch,
        compiler_params=pltpu.CompilerParams(
            dimension_semantics=("arbitrary", "arbitrary"), vmem_limit_bytes=VMEM_LIMIT),
        name="mixer_latent" if latent else "mixer_context",
    )(*args)


def _ffn_kernel(x_ref, y_ref, mod_ref, wo_ref, g2_ref, w1_ref, w2_ref, gf_ref, out_ref):
    g1 = mod_ref[2:3, :]
    sh2 = mod_ref[3:4, :]
    sc2 = mod_ref[4:5, :]
    g2 = mod_ref[5:6, :]
    x1 = x_ref[...] + g1 * _dot(y_ref[...], wo_ref[...])
    h2 = (_rms(x1, g2_ref[...]) * (1.0 + sc2) + sh2).astype(BF16)
    f = jnp.zeros_like(x1)
    for j in range(D_FF // D_MODEL):
        cols = slice(j * D_MODEL, (j + 1) * D_MODEL)
        hid = jnp.maximum(_dot(h2, w1_ref[:, cols]), 0.0)
        f = f + _dot((hid * hid).astype(BF16), w2_ref[cols, :])
    out_ref[...] = _rms(x1 + g2 * f, gf_ref[...])


def _ffn(x2d, y2d, mod3, mod_row, w_out, norm2_g, w_ff1, w_ff2, final_g):
    n_tok = x2d.shape[0]
    const = lambda i: (0, 0)
    return pl.pallas_call(
        _ffn_kernel,
        out_shape=jax.ShapeDtypeStruct((n_tok, D_MODEL), F32),
        grid=(n_tok // FFN_ROWS,),
        in_specs=[
            pl.BlockSpec((FFN_ROWS, D_MODEL), lambda i: (i, 0)),
            pl.BlockSpec((FFN_ROWS, D_MODEL), lambda i: (i, 0)),
            pl.BlockSpec((None, 6, D_MODEL), lambda i: (mod_row(i), 0, 0)),
            pl.BlockSpec((D_MODEL, D_MODEL), const, pipeline_mode=pl.Buffered(1)),
            pl.BlockSpec((1, D_MODEL), const),
            pl.BlockSpec((D_MODEL, D_FF), const, pipeline_mode=pl.Buffered(1)),
            pl.BlockSpec((D_FF, D_MODEL), const, pipeline_mode=pl.Buffered(1)),
            pl.BlockSpec((1, D_MODEL), const),
        ],
        out_specs=pl.BlockSpec((FFN_ROWS, D_MODEL), lambda i: (i, 0)),
        compiler_params=pltpu.CompilerParams(
            dimension_semantics=("arbitrary",), vmem_limit_bytes=VMEM_LIMIT),
        name="outproj_mlp",
    )(x2d, y2d, mod3, w_out, norm2_g, w_ff1, w_ff2, final_g)


def _rope_tables(seq):
    pos = jnp.arange(seq)
    row = (pos // GRID_W).astype(F32)
    col = (pos % GRID_W).astype(F32)
    nf = HEAD_DIM // 4
    inv = ROPE_BASE ** (-jnp.arange(nf, dtype=F32) / nf)
    ang = jnp.concatenate([row[:, None] * inv, col[:, None] * inv], -1)
    cos = jnp.cos(ang)
    sin = jnp.sin(ang)
    return jnp.concatenate([cos, cos], -1), jnp.concatenate([-sin, sin], -1)


def kernel(x_prompt, x_sample, state_ret, state_mlstm_C, state_mlstm_n, state_mlstm_m, c, c_ctx,
           w_ada, b_ada, norm1_g, norm2_g, w_in, conv_w, ret_decay_logit, mlstm_gate_bias,
           ret_gn_g, mlstm_gn_g, w_out, w_ff1, w_ff2, final_g):
    assert w_ada.shape[0] == 1, "single-layer kernel"
    bp, tp, _ = x_prompt.shape
    bs, ts, _ = x_sample.shape
    assert tp % CHUNK == 0 and ts % CHUNK == 0 and bp % CTX_SEQS_PER_STEP == 0

    cond = jnp.concatenate([c_ctx[None, :], c, jnp.zeros((8 - 1 - bs, D_MODEL), F32)], 0)
    mod = _modulation(cond, w_ada[0], b_ada)
    mod3 = mod[:1 + bs].reshape(1 + bs, 6, D_MODEL)

    w_pair = _regroup_in_proj(w_in)
    w_gate = w_in[0, :, N_HEADS * PAIR_COLS:].reshape(D_MODEL, 4, N_HEADS).transpose(0, 2, 1)
    w_gate = jnp.pad(w_gate, ((0, 0), (0, 0), (0, HEAD_DIM - 4))).reshape(D_MODEL, N_HEADS * HEAD_DIM)
    w_gate = w_gate.astype(BF16)
    w_out_p = _regroup_out_proj(w_out)
    w1 = w_ff1[0].astype(BF16)
    w2 = w_ff2[0].astype(BF16)

    cw = conv_w[0]
    hp_rows = [cw[j, :512].reshape(N_HEADS, HEAD_DIM) for j in range(3)]
    hp_rows += [cw[j, 512:].reshape(N_HEADS, HEAD_DIM) for j in range(3)]
    hp_rows += [ret_gn_g[0].reshape(N_HEADS, HEAD_DIM), mlstm_gn_g[0].reshape(N_HEADS, HEAD_DIM)]
    hp_rows += [jnp.broadcast_to(ret_decay_logit[0, d][:, None], (N_HEADS, HEAD_DIM)) for d in range(2)]
    gate_bias = mlstm_gate_bias[0].reshape(4, N_HEADS).T
    hp_rows += [jnp.pad(gate_bias, ((0, 0), (0, HEAD_DIM - 4)))]
    hp_rows += [jnp.zeros((N_HEADS, HEAD_DIM), F32)] * (16 - len(hp_rows))
    head_params = jnp.stack(hp_rows, axis=1).astype(F32)

    g1 = norm1_g[0][None, :]
    g2 = norm2_g[0][None, :]
    gf = final_g[None, :]
    xp2d = x_prompt.reshape(bp * tp, D_MODEL)
    xs2d = x_sample.reshape(bs * ts, D_MODEL)

    y_p, new_ret, new_c, new_n, new_m = _mixer(
        xp2d, mod3, lambda b: 0, g1, w_pair, w_gate, head_params,
        latent=False, n_seq=CTX_SEQS_PER_STEP, cps=tp // CHUNK)
    out_p = _ffn(xp2d, y_p, mod3, lambda i: 0, w_out_p, g2, w1, w2, gf).reshape(bp, tp, D_MODEL)

    s_n = jnp.transpose(state_mlstm_n[:, 0], (0, 2, 1, 3))[:, :, :, None, :]
    s_m = jnp.broadcast_to(jnp.transpose(state_mlstm_m[:, 0], (0, 2, 1))[..., None],
                           (bs, N_HEADS, 2, HEAD_DIM))
    y_s = _mixer(xs2d, mod3, lambda b: 1 + b, g1, w_pair, w_gate, head_params,
                 latent=True, n_seq=1, cps=ts // CHUNK,
                 rope=_rope_tables(ts), states=(state_ret, state_mlstm_C, s_n, s_m))
    tiles_per_seq = ts // FFN_ROWS
    out_s = _ffn(xs2d, y_s, mod3, lambda i: 1 + i // tiles_per_seq,
                 w_out_p, g2, w1, w2, gf).reshape(bs, ts, D_MODEL)

    new_n = jnp.transpose(new_n, (0, 2, 1, 3))[:, None]
    new_m = jnp.transpose(new_m[..., 0], (0, 2, 1))[:, None]
    return out_p, out_s, new_ret, new_c, new_n, new_m
```

```python
import functools

import jax
import jax.numpy as jnp
from jax import lax
from jax.experimental import pallas as pl
from jax.experimental.pallas import tpu as pltpu

F32 = jnp.float32
BF16 = jnp.bfloat16

D_MODEL = 1024
N_HEADS = 4
HEAD_DIM = 128
CHUNK = 128
GRID_W = 64
D_FF = 4 * D_MODEL
EPS = 1e-6
ROPE_BASE = 10000.0
PAIR_COLS = 8 * HEAD_DIM
N_GATE_COLS = 4 * N_HEADS
ROW_TILE = 256
FFN_ROWS = 512
CTX_SEQS_PER_STEP = 4
VMEM_LIMIT = 60 * 1024 * 1024


def _dot(a, b):
    return jnp.dot(a, b, preferred_element_type=F32)


def _dot_nt(a, b):
    return lax.dot_general(a, b, (((1,), (1,)), ((), ())), preferred_element_type=F32)


def _dot_tn(a, b):
    return lax.dot_general(a, b, (((0,), (0,)), ((), ())), preferred_element_type=F32)


def _rms(x, g):
    return x * lax.rsqrt(jnp.mean(x * x, axis=-1, keepdims=True) + EPS) * g


def _group_norm(o, g):
    mu = jnp.mean(o, axis=-1, keepdims=True)
    c = o - mu
    var = jnp.mean(c * c, axis=-1, keepdims=True)
    return c * lax.rsqrt(var + EPS) * g


def _log_sigmoid(x):
    return jnp.minimum(x, 0.0) - jnp.log1p(jnp.exp(-jnp.abs(x)))


def _sigmoid(x):
    return 1.0 / (1.0 + jnp.exp(-x))


def _split2(x):
    hi = x.astype(BF16)
    lo = (x - hi.astype(F32)).astype(BF16)
    return hi, lo


def _mod_kernel(cond_ref, w_ref, b_ref, out_ref):
    c = cond_ref[...]
    s = (c * _sigmoid(c)).astype(BF16)
    out_ref[...] = _dot(s, w_ref[...].astype(BF16)) + b_ref[...]


def _modulation(cond, w_ada, b_ada):
    n = w_ada.shape[1]
    tn = 1024
    return pl.pallas_call(
        _mod_kernel,
        out_shape=jax.ShapeDtypeStruct((cond.shape[0], n), F32),
        grid=(n // tn,),
        in_specs=[pl.BlockSpec(cond.shape, lambda j: (0, 0)),
                  pl.BlockSpec((D_MODEL, tn), lambda j: (0, j)),
                  pl.BlockSpec((1, tn), lambda j: (0, j))],
        out_specs=pl.BlockSpec((cond.shape[0], tn), lambda j: (0, j)),
        compiler_params=pltpu.CompilerParams(dimension_semantics=("arbitrary",)),
        name="adaln_mod",
    )(cond, w_ada, b_ada)


REGROUP_ROWS = 256


def _regroup_in_kernel(wt_ref, out_ref, gate_ref):
    for h in range(N_HEADS):
        for g in range(8):
            src = (g * N_HEADS + h) * HEAD_DIM
            dst = (h * 8 + g) * HEAD_DIM
            out_ref[:, dst:dst + HEAD_DIM] = wt_ref[src:src + HEAD_DIM, :].T.astype(BF16)
    n_rows = wt_ref.shape[0]
    n_gate = n_rows - N_HEADS * PAIR_COLS
    tail = wt_ref[n_rows - HEAD_DIM:n_rows, :].T
    lane = lax.broadcasted_iota(jnp.int32, tail.shape, 1)
    gate_ref[...] = jnp.where(lane >= HEAD_DIM - n_gate, tail, 0.0).astype(BF16)


def _regroup_in_proj(w_in):
    w_t = jnp.transpose(w_in[0])
    n_cols = w_t.shape[0]
    assert n_cols == N_HEADS * PAIR_COLS + N_GATE_COLS
    return pl.pallas_call(
        _regroup_in_kernel,
        out_shape=(jax.ShapeDtypeStruct((D_MODEL, N_HEADS * PAIR_COLS), BF16),
                   jax.ShapeDtypeStruct((D_MODEL, HEAD_DIM), BF16)),
        grid=(D_MODEL // REGROUP_ROWS,),
        in_specs=[pl.BlockSpec((n_cols, REGROUP_ROWS), lambda i: (0, i))],
        out_specs=(pl.BlockSpec((REGROUP_ROWS, N_HEADS * PAIR_COLS), lambda i: (i, 0)),
                   pl.BlockSpec((REGROUP_ROWS, HEAD_DIM), lambda i: (i, 0))),
        compiler_params=pltpu.CompilerParams(dimension_semantics=("arbitrary",)),
        name="regroup_w_in",
    )(w_t)


def _regroup_out_kernel(w_ref, out_ref):
    for h in range(N_HEADS):
        for g in range(2):
            src = (g * N_HEADS + h) * HEAD_DIM
            dst = (h * 2 + g) * HEAD_DIM
            out_ref[dst:dst + HEAD_DIM, :] = w_ref[src:src + HEAD_DIM, :].astype(BF16)


def _regroup_out_proj(w_out):
    cols = 2 * HEAD_DIM
    return pl.pallas_call(
        _regroup_out_kernel,
        out_shape=jax.ShapeDtypeStruct((D_MODEL, D_MODEL), BF16),
        grid=(D_MODEL // cols,),
        in_specs=[pl.BlockSpec((None, D_MODEL, cols), lambda j: (0, 0, j))],
        out_specs=pl.BlockSpec((D_MODEL, cols), lambda j: (0, j)),
        compiler_params=pltpu.CompilerParams(dimension_semantics=("arbitrary",)),
        name="regroup_w_out",
    )(w_out)


_B_END, _MAX_LWE, _M_PREV, _M_NEW, _DECAY = 0, 2, 4, 6, 8


def _mixer_kernel(*refs, latent, n_seq, cps):
    L = CHUNK
    n_chunks = n_seq * cps
    rows_total = n_chunks * L
    seq_len = cps * L
    assert seq_len & (seq_len - 1) == 0
    if latent:
        (x_ref, mod_ref, g1_ref, w_ref, wg_ref, hp_ref, cos_ref, sin_ref, sr_in, sc_in, sn_in, sm_in,
         y_ref, *scratch) = refs
    else:
        (x_ref, mod_ref, g1_ref, w_ref, wg_ref, hp_ref,
         y_ref, so_ref, co_ref, no_ref, mo_ref, *scratch) = refs
    (hn_ref, rq_ref, rk_ref, rv_ref, rg_ref, mq_ref, mk_ref, mvx_ref, mog_ref, gt_ref,
     ar_ref, am_ref, logi_ref, bc_ref, skv_ref, ckv_ref, cs_ref, rc_ref) = scratch

    head = pl.program_id(1)

    @pl.when(head == 0)
    def _():
        sh1 = mod_ref[0:1, :]
        sc1 = mod_ref[1:2, :]
        g1 = g1_ref[...]

        def body(i, carry):
            r = pl.ds(pl.multiple_of(i * ROW_TILE, ROW_TILE), ROW_TILE)
            hn_ref[r, :] = (_rms(x_ref[r, :], g1) * (1.0 + sc1) + sh1).astype(BF16)
            return carry

        lax.fori_loop(0, rows_total // ROW_TILE, body, 0)

    hp = hp_ref[...]
    scale = HEAD_DIM ** -0.5
    hn = hn_ref[...]

    pr = _dot(hn, w_ref[:, 0:256])
    q = pr[:, 0:128]
    k = pr[:, 128:256] * scale
    if latent:
        cos2 = cos_ref[...]
        sin2 = sin_ref[...]
        q = q * cos2 + pltpu.roll(q, HEAD_DIM // 2, axis=1) * sin2
        k = k * cos2 + pltpu.roll(k, HEAD_DIM // 2, axis=1) * sin2
    rq_ref[...] = q
    rk_ref[...] = k

    pv = _dot(hn, w_ref[:, 256:512])
    rv_ref[...] = pv[:, 0:128].astype(BF16)
    rg_ref[...] = pv[:, 128:256]

    row_t = lax.broadcasted_iota(jnp.int32, (rows_total, HEAD_DIM), 0)
    pos = jnp.bitwise_and(row_t, seq_len - 1)
    lane_t = lax.broadcasted_iota(jnp.int32, (rows_total, HEAD_DIM), 1)

    def conv_silu(xc, w0, w1, w2):
        prev = jnp.where(pos == 0, 0.0, pltpu.roll(xc, 1, axis=0))
        nxt = jnp.where(pos == seq_len - 1, 0.0, pltpu.roll(xc, rows_total - 1, axis=0))
        out = prev * w0 + xc * w1 + nxt * w2
        return out * _sigmoid(out)

    pm = _dot(hn, w_ref[:, 512:768])
    mq_ref[...] = conv_silu(pm[:, 0:128], hp[0:1], hp[1:2], hp[2:3])
    mk_ref[...] = conv_silu(pm[:, 128:256], hp[3:4], hp[4:5], hp[5:6]) * scale

    po = _dot(hn, w_ref[:, 768:1024])
    mvx_ref[:, 0:128] = po[:, 0:128].astype(BF16)
    mvx_ref[:, 128:256] = jnp.ones((rows_total, HEAD_DIM), BF16)
    mog_ref[...] = po[:, 128:256]

    gates = _dot(hn, wg_ref[...]) + hp[10:11]
    is_forget = jnp.bitwise_and(lane_t, 4) == 4
    gates = jnp.where(is_forget, _log_sigmoid(gates), gates)
    gt_ref[...] = pltpu.roll(gates, N_GATE_COLS - head, axis=1)

    r_i = lax.broadcasted_iota(jnp.int32, (L, L), 0)
    s_i = lax.broadcasted_iota(jnp.int32, (L, L), 1)
    r_f = r_i.astype(F32)
    s_f = s_i.astype(F32)
    lg_f = _log_sigmoid(hp[8:9])
    lg_b = _log_sigmoid(hp[9:10])
    rc_ref[0] = (jnp.where(r_i >= s_i, jnp.exp(lg_f * jnp.where(r_i >= s_i, r_f - s_f, 0.0)), 0.0)
                 + jnp.where(s_i >= r_i, jnp.exp(lg_b * jnp.where(s_i >= r_i, s_f - r_f, 0.0)), 0.0))
    rc_ref[1] = jnp.exp(lg_f * (r_f + 1.0))
    rc_ref[2] = jnp.exp(lg_b * (L - r_f))
    rc_ref[3] = jnp.exp(lg_f * (L - 1.0 - r_f))
    rc_ref[4] = jnp.exp(lg_b * r_f)
    chunk_decay = (jnp.exp(lg_f * float(L)), jnp.exp(lg_b * float(L)))

    tri = (jnp.where(r_i >= s_i, 1.0, 0.0).astype(BF16),
           jnp.where(s_i >= r_i, 1.0, 0.0).astype(BF16))

    def rows(c):
        return slice(c * L, (c + 1) * L)

    def gate_col(c, lane):
        g = gt_ref[rows(c), :]
        return jnp.broadcast_to(g[:, lane:lane + 1], (L, HEAD_DIM))

    for c in range(n_chunks):
        ar_ref[rows(c), :] = (_dot_nt(rq_ref[rows(c), :].astype(BF16), rk_ref[rows(c), :].astype(BF16))
                              * rc_ref[0]).astype(BF16)
        am_ref[rows(c), :] = _dot_nt(mq_ref[rows(c), :].astype(BF16), mk_ref[rows(c), :].astype(BF16))
    for c in range(n_chunks):
        for d in range(2):
            li = gate_col(c, 8 * d)
            lf = gate_col(c, 8 * d + 4)
            strict = (r_i > s_i) if d == 0 else (r_i < s_i)
            x_mat = jnp.where(strict, lf, jnp.where(r_i == s_i, li, 0.0))
            hi, lo = _split2(jnp.concatenate([x_mat, lf], axis=1))
            d_ext = _dot(tri[d], hi) + _dot(tri[d], lo)
            bcum = d_ext[:, 128:256]
            causal = (s_i <= r_i) if d == 0 else (s_i >= r_i)
            logi_ref[d, rows(c), :] = jnp.where(causal, d_ext[:, 0:128], -jnp.inf)
            bc_ref[d, rows(c), :] = bcum
            b_end = bcum[L - 1:L, :] if d == 0 else bcum[0:1, :]
            cs_ref[_B_END + d, c:c + 1, :] = b_end
            cs_ref[_MAX_LWE + d, c:c + 1, :] = jnp.max(b_end - bcum + li, axis=0, keepdims=True)

    m_final = {}
    for s in range(n_seq):
        for d in range(2):
            m = sm_in[d:d + 1, :] if latent else jnp.zeros((1, HEAD_DIM), F32)
            order = range(cps) if d == 0 else range(cps - 1, -1, -1)
            for j in order:
                c = s * cps + j
                cs_ref[_M_PREV + d, c:c + 1, :] = m
                b_end = cs_ref[_B_END + d, c:c + 1, :]
                m_new = jnp.maximum(b_end + m, cs_ref[_MAX_LWE + d, c:c + 1, :])
                cs_ref[_M_NEW + d, c:c + 1, :] = m_new
                cs_ref[_DECAY + d, c:c + 1, :] = jnp.exp(b_end + m - m_new)
                m = m_new
            m_final[(s, d)] = m

    for c in range(n_chunks):
        kf = rk_ref[rows(c), :]
        vb = rv_ref[rows(c), :]
        for d in range(2):
            skv_ref[d, c] = _dot_tn((kf * rc_ref[3 + d]).astype(BF16), vb)
    for c in range(n_chunks):
        kf = mk_ref[rows(c), :]
        vx = mvx_ref[rows(c), :]
        for d in range(2):
            log_w_end = cs_ref[_B_END + d, c:c + 1, :] - bc_ref[d, rows(c), :] + gate_col(c, 8 * d)
            w_end = jnp.exp(log_w_end - cs_ref[_M_NEW + d, c:c + 1, :])
            ckv_ref[d, c] = _dot_tn((kf * w_end).astype(BF16), vx)

    for s in range(n_seq):
        for d in range(2):
            if latent:
                s_state = sr_in[d]
                n_rep = jnp.broadcast_to(sn_in[d], (HEAD_DIM, HEAD_DIM)).T
                c_state = jnp.concatenate([sc_in[d], n_rep], axis=1)
            else:
                s_state = jnp.zeros((HEAD_DIM, HEAD_DIM), F32)
                c_state = jnp.zeros((HEAD_DIM, 2 * HEAD_DIM), F32)
            order = range(cps) if d == 0 else range(cps - 1, -1, -1)
            for j in order:
                c = s * cps + j
                inc = skv_ref[d, c]
                skv_ref[d, c] = s_state
                s_state = s_state * chunk_decay[d] + inc
                inc = ckv_ref[d, c]
                ckv_ref[d, c] = c_state
                decay = cs_ref[_DECAY + d, c:c + 1, :]
                c_state = c_state * jnp.concatenate([decay, decay], axis=1) + inc
            if not latent:
                so_ref[s, d] = s_state
                co_ref[s, d] = c_state[:, 0:128]
                no_ref[s, d:d + 1, :] = c_state[:, 128:256].T[0:1, :]
                mo_ref[s, d:d + 1, :] = m_final[(s, d)]

    for c in range(n_chunks):
        qf = rq_ref[rows(c), :]
        lhs = jnp.concatenate([ar_ref[rows(c), :], (qf * rc_ref[1]).astype(BF16),
                               (qf * rc_ref[2]).astype(BF16)], axis=1)
        rhs = jnp.concatenate([rv_ref[rows(c), :], skv_ref[0, c].astype(BF16),
                               skv_ref[1, c].astype(BF16)], axis=0)
        rq_ref[rows(c), :] = _dot(lhs, rhs)
    for c in range(n_chunks):
        qf = mq_ref[rows(c), :]
        a_mat = am_ref[rows(c), :]
        vx = mvx_ref[rows(c), :]
        h_sum = None
        for d in range(2):
            log_intra = logi_ref[d, rows(c), :]
            log_inter = bc_ref[d, rows(c), :] + cs_ref[_M_PREV + d, c:c + 1, :]
            m_t = jnp.maximum(log_inter, jnp.max(log_intra, axis=1, keepdims=True))
            w_inter = jnp.exp(log_inter - m_t)
            w_intra = jnp.exp(log_intra - m_t)
            lhs = jnp.concatenate([(a_mat * w_intra).astype(BF16), (qf * w_inter).astype(BF16)], axis=1)
            rhs = jnp.concatenate([vx, ckv_ref[d, c].astype(BF16)], axis=0)
            res = _dot(lhs, rhs)
            h_dir = res[:, 0:128] / jnp.maximum(jnp.abs(res[:, 128:256]), jnp.exp(-m_t))
            h_sum = h_dir if h_sum is None else h_sum + h_dir
        am_ref[rows(c), :] = h_sum

    def finish(i, carry):
        r = pl.ds(pl.multiple_of(i * ROW_TILE, ROW_TILE), ROW_TILE)
        rg = rg_ref[r, :]
        ret_y = _group_norm(rq_ref[r, :], hp[6:7]) * (rg * _sigmoid(rg))
        y_ref[r, 0:128] = ret_y.astype(BF16)
        ml_y = _group_norm(am_ref[r, :], hp[7:8]) * _sigmoid(mog_ref[r, :])
        y_ref[r, 128:256] = ml_y.astype(BF16)
        return carry

    lax.fori_loop(0, rows_total // ROW_TILE, finish, 0)


def _mixer(x2d, mod3, mod_row, norm1_g, w_pair, w_gate, head_params, *, latent, n_seq, cps,
           rope=None, states=None):
    n_tok = x2d.shape[0]
    rows_blk = n_seq * cps * CHUNK
    n_blk = n_tok // rows_blk
    n_chunks = n_seq * cps
    kern = functools.partial(_mixer_kernel, latent=latent, n_seq=n_seq, cps=cps)
    once = pl.Buffered(1)
    in_specs = [
        pl.BlockSpec((rows_blk, D_MODEL), lambda b, h: (b, 0), pipeline_mode=once if latent else None),
        pl.BlockSpec((None, 6, D_MODEL), lambda b, h: (mod_row(b), 0, 0)),
        pl.BlockSpec((1, D_MODEL), lambda b, h: (0, 0)),
        pl.BlockSpec((D_MODEL, PAIR_COLS), lambda b, h: (0, h)),
        pl.BlockSpec((D_MODEL, HEAD_DIM), lambda b, h: (0, 0)),
        pl.BlockSpec((None, 16, HEAD_DIM), lambda b, h: (h, 0, 0)),
    ]
    args = [x2d, mod3, norm1_g, w_pair, w_gate, head_params]
    y_shape = jax.ShapeDtypeStruct((n_tok, D_MODEL), BF16)
    y_spec = pl.BlockSpec((rows_blk, 2 * HEAD_DIM), lambda b, h: (b, h))
    if latent:
        assert n_seq == 1
        cos2, sin2 = rope
        s_ret, s_c, s_n, s_m = states
        in_specs += [
            pl.BlockSpec((rows_blk, HEAD_DIM), lambda b, h: (0, 0), pipeline_mode=once),
            pl.BlockSpec((rows_blk, HEAD_DIM), lambda b, h: (0, 0), pipeline_mode=once),
            pl.BlockSpec((None, None, 2, None, HEAD_DIM, HEAD_DIM), lambda b, h: (b, 0, 0, h, 0, 0)),
            pl.BlockSpec((None, None, 2, None, HEAD_DIM, HEAD_DIM), lambda b, h: (b, 0, 0, h, 0, 0)),
            pl.BlockSpec((None, None, 2, 1, HEAD_DIM), lambda b, h: (b, h, 0, 0, 0)),
            pl.BlockSpec((None, None, 2, HEAD_DIM), lambda b, h: (b, h, 0, 0)),
        ]
        args += [cos2, sin2, s_ret, s_c, s_n, s_m]
        out_shape = y_shape
        out_specs = y_spec
    else:
        bsz = n_blk * n_seq
        st = jax.ShapeDtypeStruct((bsz, 1, 2, N_HEADS, HEAD_DIM, HEAD_DIM), F32)
        vec = jax.ShapeDtypeStruct((bsz, N_HEADS, 2, HEAD_DIM), F32)
        st_spec = pl.BlockSpec((n_seq, None, 2, None, HEAD_DIM, HEAD_DIM), lambda b, h: (b, 0, 0, h, 0, 0))
        vec_spec = pl.BlockSpec((n_seq, None, 2, HEAD_DIM), lambda b, h: (b, h, 0, 0))
        out_shape = (y_shape, st, st, vec, vec)
        out_specs = (y_spec, st_spec, st_spec, vec_spec, vec_spec)
    col = lambda dt: pltpu.VMEM((rows_blk, HEAD_DIM), dt)
    scratch = [
        pltpu.VMEM((rows_blk, D_MODEL), BF16),
        col(F32), col(F32), col(BF16), col(F32),
        col(F32), col(F32),
        pltpu.VMEM((rows_blk, 2 * HEAD_DIM), BF16),
        col(F32), col(F32),
        col(BF16), col(F32),
        pltpu.VMEM((2, rows_blk, HEAD_DIM), F32),
        pltpu.VMEM((2, rows_blk, HEAD_DIM), F32),
        pltpu.VMEM((2, n_chunks, HEAD_DIM, HEAD_DIM), F32),
        pltpu.VMEM((2, n_chunks, HEAD_DIM, 2 * HEAD_DIM), F32),
        pltpu.VMEM((10, max(n_chunks, 8), HEAD_DIM), F32),
        pltpu.VMEM((5, CHUNK, CHUNK), F32),
    ]
    return pl.pallas_call(
        kern,
        out_shape=out_shape,
        grid=(n_blk, N_HEADS),
        in_specs=in_specs,
        out_specs=out_specs,
        scratch_shapes=scratch,
        compiler_params=pltpu.CompilerParams(
            dimension_semantics=("arbitrary", "arbitrary"), vmem_limit_bytes=VMEM_LIMIT),
        name="mixer_latent" if latent else "mixer_context",
    )(*args)


def _ffn_kernel(x_ref, y_ref, mod_ref, wo_ref, g2_ref, w1_ref, w2_ref, gf_ref, out_ref):
    g1 = mod_ref[2:3, :]
    sh2 = mod_ref[3:4, :]
    sc2 = mod_ref[4:5, :]
    g2 = mod_ref[5:6, :]
    x1 = x_ref[...] + g1 * _dot(y_ref[...], wo_ref[...])
    h2 = (_rms(x1, g2_ref[...]) * (1.0 + sc2) + sh2).astype(BF16)
    f = jnp.zeros_like(x1)
    for j in range(D_FF // D_MODEL):
        cols = slice(j * D_MODEL, (j + 1) * D_MODEL)
        hid = jnp.maximum(_dot(h2, w1_ref[:, cols]), 0.0)
        f = f + _dot((hid * hid).astype(BF16), w2_ref[cols, :])
    out_ref[...] = _rms(x1 + g2 * f, gf_ref[...])


def _ffn(x2d, y2d, mod3, mod_row, w_out, norm2_g, w_ff1, w_ff2, final_g):
    n_tok = x2d.shape[0]
    const = lambda i: (0, 0)
    return pl.pallas_call(
        _ffn_kernel,
        out_shape=jax.ShapeDtypeStruct((n_tok, D_MODEL), F32),
        grid=(n_tok // FFN_ROWS,),
        in_specs=[
            pl.BlockSpec((FFN_ROWS, D_MODEL), lambda i: (i, 0)),
            pl.BlockSpec((FFN_ROWS, D_MODEL), lambda i: (i, 0)),
            pl.BlockSpec((None, 6, D_MODEL), lambda i: (mod_row(i), 0, 0)),
            pl.BlockSpec((D_MODEL, D_MODEL), const, pipeline_mode=pl.Buffered(1)),
            pl.BlockSpec((1, D_MODEL), const),
            pl.BlockSpec((D_MODEL, D_FF), const, pipeline_mode=pl.Buffered(1)),
            pl.BlockSpec((D_FF, D_MODEL), const, pipeline_mode=pl.Buffered(1)),
            pl.BlockSpec((1, D_MODEL), const),
        ],
        out_specs=pl.BlockSpec((FFN_ROWS, D_MODEL), lambda i: (i, 0)),
        compiler_params=pltpu.CompilerParams(
            dimension_semantics=("arbitrary",), vmem_limit_bytes=VMEM_LIMIT),
        name="outproj_mlp",
    )(x2d, y2d, mod3, w_out, norm2_g, w_ff1, w_ff2, final_g)


def _rope_tables(seq):
    pos = jnp.arange(seq)
    row = (pos // GRID_W).astype(F32)
    col = (pos % GRID_W).astype(F32)
    nf = HEAD_DIM // 4
    inv = ROPE_BASE ** (-jnp.arange(nf, dtype=F32) / nf)
    ang = jnp.concatenate([row[:, None] * inv, col[:, None] * inv], -1)
    cos = jnp.cos(ang)
    sin = jnp.sin(ang)
    return jnp.concatenate([cos, cos], -1), jnp.concatenate([-sin, sin], -1)


def kernel(x_prompt, x_sample, state_ret, state_mlstm_C, state_mlstm_n, state_mlstm_m, c, c_ctx,
           w_ada, b_ada, norm1_g, norm2_g, w_in, conv_w, ret_decay_logit, mlstm_gate_bias,
           ret_gn_g, mlstm_gn_g, w_out, w_ff1, w_ff2, final_g):
    assert w_ada.shape[0] == 1, "single-layer kernel"
    bp, tp, _ = x_prompt.shape
    bs, ts, _ = x_sample.shape
    assert tp % CHUNK == 0 and ts % CHUNK == 0 and bp % CTX_SEQS_PER_STEP == 0

    cond = jnp.concatenate([c_ctx[None, :], c, jnp.zeros((8 - 1 - bs, D_MODEL), F32)], 0)
    mod = _modulation(cond, w_ada[0], b_ada)
    mod3 = mod[:1 + bs].reshape(1 + bs, 6, D_MODEL)

    w_pair, w_gate = _regroup_in_proj(w_in)
    w_out_p = _regroup_out_proj(w_out)
    w1 = w_ff1[0].astype(BF16)
    w2 = w_ff2[0].astype(BF16)

    cw = conv_w[0]
    hp_rows = [cw[j, :512].reshape(N_HEADS, HEAD_DIM) for j in range(3)]
    hp_rows += [cw[j, 512:].reshape(N_HEADS, HEAD_DIM) for j in range(3)]
    hp_rows += [ret_gn_g[0].reshape(N_HEADS, HEAD_DIM), mlstm_gn_g[0].reshape(N_HEADS, HEAD_DIM)]
    hp_rows += [jnp.broadcast_to(ret_decay_logit[0, d][:, None], (N_HEADS, HEAD_DIM)) for d in range(2)]
    gate_bias = jnp.pad(mlstm_gate_bias[0].reshape(1, N_GATE_COLS), ((0, 0), (HEAD_DIM - N_GATE_COLS, 0)))
    hp_rows += [jnp.broadcast_to(gate_bias, (N_HEADS, HEAD_DIM))]
    hp_rows += [jnp.zeros((N_HEADS, HEAD_DIM), F32)] * (16 - len(hp_rows))
    head_params = jnp.stack(hp_rows, axis=1).astype(F32)

    g1 = norm1_g[0][None, :]
    g2 = norm2_g[0][None, :]
    gf = final_g[None, :]
    xp2d = x_prompt.reshape(bp * tp, D_MODEL)
    xs2d = x_sample.reshape(bs * ts, D_MODEL)

    y_p, new_ret, new_c, new_n, new_m = _mixer(
        xp2d, mod3, lambda b: 0, g1, w_pair, w_gate, head_params,
        latent=False, n_seq=CTX_SEQS_PER_STEP, cps=tp // CHUNK)
    out_p = _ffn(xp2d, y_p, mod3, lambda i: 0, w_out_p, g2, w1, w2, gf).reshape(bp, tp, D_MODEL)

    s_n = jnp.transpose(state_mlstm_n[:, 0], (0, 2, 1, 3))[:, :, :, None, :]
    s_m = jnp.broadcast_to(jnp.transpose(state_mlstm_m[:, 0], (0, 2, 1))[..., None],
                           (bs, N_HEADS, 2, HEAD_DIM))
    y_s = _mixer(xs2d, mod3, lambda b: 1 + b, g1, w_pair, w_gate, head_params,
                 latent=True, n_seq=1, cps=ts // CHUNK,
                 rope=_rope_tables(ts), states=(state_ret, state_mlstm_C, s_n, s_m))
    tiles_per_seq = ts // FFN_ROWS
    out_s = _ffn(xs2d, y_s, mod3, lambda i: 1 + i // tiles_per_seq,
                 w_out_p, g2, w1, w2, gf).reshape(bs, ts, D_MODEL)

    new_n = jnp.transpose(new_n, (0, 2, 1, 3))[:, None]
    new_m = jnp.transpose(new_m[..., 0], (0, 2, 1))[:, None]
    return out_p, out_s, new_ret, new_c, new_n, new_m
```

```python
import functools

import jax
import jax.numpy as jnp
from jax import lax
from jax.experimental import pallas as pl
from jax.experimental.pallas import tpu as pltpu

F32 = jnp.float32
BF16 = jnp.bfloat16

D_MODEL = 1024
N_HEADS = 4
HEAD_DIM = 128
CHUNK = 128
GRID_W = 64
D_FF = 4 * D_MODEL
EPS = 1e-6
ROPE_BASE = 10000.0
PAIR_COLS = 8 * HEAD_DIM
N_GATE_COLS = 4 * N_HEADS
ROW_TILE = 256
FFN_ROWS = 512
CTX_SEQS_PER_STEP = 4
PROJ_SLAB_ROWS = 256
CONV_HALO = 8
VMEM_LIMIT = 60 * 1024 * 1024


def _dot(a, b):
    return jnp.dot(a, b, preferred_element_type=F32)


def _dot_nt(a, b):
    return lax.dot_general(a, b, (((1,), (1,)), ((), ())), preferred_element_type=F32)


def _dot_tn(a, b):
    return lax.dot_general(a, b, (((0,), (0,)), ((), ())), preferred_element_type=F32)


def _rms(x, g):
    return x * lax.rsqrt(jnp.mean(x * x, axis=-1, keepdims=True) + EPS) * g


def _group_norm(o, g):
    mu = jnp.mean(o, axis=-1, keepdims=True)
    c = o - mu
    var = jnp.mean(c * c, axis=-1, keepdims=True)
    return c * lax.rsqrt(var + EPS) * g


def _log_sigmoid(x):
    return jnp.minimum(x, 0.0) - jnp.log1p(jnp.exp(-jnp.abs(x)))


def _sigmoid(x):
    return 1.0 / (1.0 + jnp.exp(-x))


def _split2(x):
    hi = x.astype(BF16)
    lo = (x - hi.astype(F32)).astype(BF16)
    return hi, lo


def _mod_kernel(cond_ref, w_ref, b_ref, out_ref):
    c = cond_ref[...]
    s = (c * _sigmoid(c)).astype(BF16)
    out_ref[...] = _dot(s, w_ref[...].astype(BF16)) + b_ref[...]


def _modulation(cond, w_ada, b_ada):
    n = w_ada.shape[1]
    tn = 1024
    return pl.pallas_call(
        _mod_kernel,
        out_shape=jax.ShapeDtypeStruct((cond.shape[0], n), F32),
        grid=(n // tn,),
        in_specs=[pl.BlockSpec(cond.shape, lambda j: (0, 0)),
                  pl.BlockSpec((D_MODEL, tn), lambda j: (0, j)),
                  pl.BlockSpec((1, tn), lambda j: (0, j))],
        out_specs=pl.BlockSpec((cond.shape[0], tn), lambda j: (0, j)),
        compiler_params=pltpu.CompilerParams(dimension_semantics=("arbitrary",)),
        name="adaln_mod",
    )(cond, w_ada, b_ada)


REGROUP_ROWS = 256


def _regroup_in_kernel(wt_ref, out_ref, gate_ref):
    for h in range(N_HEADS):
        for g in range(8):
            src = (g * N_HEADS + h) * HEAD_DIM
            dst = (h * 8 + g) * HEAD_DIM
            out_ref[:, dst:dst + HEAD_DIM] = wt_ref[src:src + HEAD_DIM, :].T.astype(BF16)
    n_rows = wt_ref.shape[0]
    n_gate = n_rows - N_HEADS * PAIR_COLS
    tail = wt_ref[n_rows - HEAD_DIM:n_rows, :].T
    lane = lax.broadcasted_iota(jnp.int32, tail.shape, 1)
    gate_ref[...] = jnp.where(lane >= HEAD_DIM - n_gate, tail, 0.0).astype(BF16)


def _regroup_in_proj(w_in):
    w_t = jnp.transpose(w_in[0])
    n_cols = w_t.shape[0]
    assert n_cols == N_HEADS * PAIR_COLS + N_GATE_COLS
    return pl.pallas_call(
        _regroup_in_kernel,
        out_shape=(jax.ShapeDtypeStruct((D_MODEL, N_HEADS * PAIR_COLS), BF16),
                   jax.ShapeDtypeStruct((D_MODEL, HEAD_DIM), BF16)),
        grid=(D_MODEL // REGROUP_ROWS,),
        in_specs=[pl.BlockSpec((n_cols, REGROUP_ROWS), lambda i: (0, i))],
        out_specs=(pl.BlockSpec((REGROUP_ROWS, N_HEADS * PAIR_COLS), lambda i: (i, 0)),
                   pl.BlockSpec((REGROUP_ROWS, HEAD_DIM), lambda i: (i, 0))),
        compiler_params=pltpu.CompilerParams(dimension_semantics=("arbitrary",)),
        name="regroup_w_in",
    )(w_t)


def _regroup_out_kernel(w_ref, out_ref):
    for h in range(N_HEADS):
        for g in range(2):
            src = (g * N_HEADS + h) * HEAD_DIM
            dst = (h * 2 + g) * HEAD_DIM
            out_ref[dst:dst + HEAD_DIM, :] = w_ref[src:src + HEAD_DIM, :].astype(BF16)


def _regroup_out_proj(w_out):
    cols = 2 * HEAD_DIM
    return pl.pallas_call(
        _regroup_out_kernel,
        out_shape=jax.ShapeDtypeStruct((D_MODEL, D_MODEL), BF16),
        grid=(D_MODEL // cols,),
        in_specs=[pl.BlockSpec((None, D_MODEL, cols), lambda j: (0, 0, j))],
        out_specs=pl.BlockSpec((D_MODEL, cols), lambda j: (0, j)),
        compiler_params=pltpu.CompilerParams(dimension_semantics=("arbitrary",)),
        name="regroup_w_out",
    )(w_out)


_B_END, _MAX_LWE, _M_PREV, _M_NEW, _DECAY = 0, 2, 4, 6, 8


def _mixer_kernel(*refs, latent, n_seq, cps):
    L = CHUNK
    n_chunks = n_seq * cps
    rows_total = n_chunks * L
    seq_len = cps * L
    assert seq_len & (seq_len - 1) == 0
    if latent:
        (x_ref, mod_ref, g1_ref, w_ref, wg_ref, hp_ref, cos_ref, sin_ref, sr_in, sc_in, sn_in, sm_in,
         y_ref, *scratch) = refs
    else:
        (x_ref, mod_ref, g1_ref, w_ref, wg_ref, hp_ref,
         y_ref, so_ref, co_ref, no_ref, mo_ref, *scratch) = refs
    (hn_ref, rq_ref, rk_ref, rv_ref, rg_ref, mq_ref, mk_ref, mvx_ref, mog_ref, gt_ref,
     ar_ref, am_ref, logi_ref, bc_ref, skv_ref, ckv_ref, cs_ref, rc_ref, rawq_ref, rawk_ref) = scratch

    head = pl.program_id(1)

    @pl.when(head == 0)
    def _():
        sh1 = mod_ref[0:1, :]
        sc1 = mod_ref[1:2, :]
        g1 = g1_ref[...]

        def body(i, carry):
            r = pl.ds(pl.multiple_of(i * ROW_TILE, ROW_TILE), ROW_TILE)
            hn_ref[r, :] = (_rms(x_ref[r, :], g1) * (1.0 + sc1) + sh1).astype(BF16)
            return carry

        lax.fori_loop(0, rows_total // ROW_TILE, body, 0)

    hp = hp_ref[...]
    scale = HEAD_DIM ** -0.5

    r_i = lax.broadcasted_iota(jnp.int32, (L, L), 0)
    s_i = lax.broadcasted_iota(jnp.int32, (L, L), 1)
    r_f = r_i.astype(F32)
    s_f = s_i.astype(F32)
    lg_f = _log_sigmoid(hp[8:9])
    lg_b = _log_sigmoid(hp[9:10])
    rc_ref[0] = (jnp.where(r_i >= s_i, jnp.exp(lg_f * jnp.where(r_i >= s_i, r_f - s_f, 0.0)), 0.0)
                 + jnp.where(s_i >= r_i, jnp.exp(lg_b * jnp.where(s_i >= r_i, s_f - r_f, 0.0)), 0.0))
    rc_ref[1] = jnp.exp(lg_f * (r_f + 1.0))
    rc_ref[2] = jnp.exp(lg_b * (L - r_f))
    rc_ref[3] = jnp.exp(lg_f * (L - 1.0 - r_f))
    rc_ref[4] = jnp.exp(lg_b * r_f)
    chunk_decay = (jnp.exp(lg_f * float(L)), jnp.exp(lg_b * float(L)))

    tri = (jnp.where(r_i >= s_i, 1.0, 0.0).astype(BF16),
           jnp.where(s_i >= r_i, 1.0, 0.0).astype(BF16))

    for raw_ref in (rawq_ref, rawk_ref):
        raw_ref[0:CONV_HALO, :] = jnp.zeros((CONV_HALO, HEAD_DIM), F32)
        raw_ref[rows_total + CONV_HALO:rows_total + 2 * CONV_HALO, :] = jnp.zeros((CONV_HALO, HEAD_DIM), F32)

    def rows(c):
        return slice(c * L, (c + 1) * L)

    def gate_col(c, lane):
        g = gt_ref[rows(c), :]
        return jnp.broadcast_to(g[:, lane:lane + 1], (L, HEAD_DIM))

    def project(blk, n_rows):
        hn = hn_ref[blk, :]
        pr = _dot(hn, w_ref[:, 0:256])
        q = pr[:, 0:128]
        k = pr[:, 128:256] * scale
        if latent:
            cos2 = cos_ref[blk, :]
            sin2 = sin_ref[blk, :]
            q = q * cos2 + pltpu.roll(q, HEAD_DIM // 2, axis=1) * sin2
            k = k * cos2 + pltpu.roll(k, HEAD_DIM // 2, axis=1) * sin2
        rq_ref[blk, :] = q
        rk_ref[blk, :] = k

        pv = _dot(hn, w_ref[:, 256:512])
        rv_ref[blk, :] = pv[:, 0:128].astype(BF16)
        rg_ref[blk, :] = pv[:, 128:256]

        pm = _dot(hn, w_ref[:, 512:768])
        halo_blk = slice(blk.start + CONV_HALO, blk.stop + CONV_HALO)
        rawq_ref[halo_blk, :] = pm[:, 0:128]
        rawk_ref[halo_blk, :] = pm[:, 128:256]

        po = _dot(hn, w_ref[:, 768:1024])
        mvx_ref[blk, 0:128] = po[:, 0:128].astype(BF16)
        mvx_ref[blk, 128:256] = jnp.ones((n_rows, HEAD_DIM), BF16)
        mog_ref[blk, :] = po[:, 128:256]

        gates = _dot(hn, wg_ref[...]) + hp[10:11]
        lane_t = lax.broadcasted_iota(jnp.int32, (n_rows, HEAD_DIM), 1)
        is_forget = jnp.bitwise_and(lane_t, 4) == 4
        gates = jnp.where(is_forget, _log_sigmoid(gates), gates)
        gt_ref[blk, :] = pltpu.roll(gates, N_GATE_COLS - head, axis=1)

    def conv_silu(blk, n_rows):
        row_t = lax.broadcasted_iota(jnp.int32, (n_rows, HEAD_DIM), 0)
        pos = jnp.bitwise_and(row_t + blk.start, seq_len - 1)
        for raw_ref, dst_ref, taps, post in ((rawq_ref, mq_ref, hp[0:3], 1.0), (rawk_ref, mk_ref, hp[3:6], scale)):
            lo = blk.start + CONV_HALO
            prev = jnp.where(pos == 0, 0.0, raw_ref[lo - 1:lo - 1 + n_rows, :])
            nxt = jnp.where(pos == seq_len - 1, 0.0, raw_ref[lo + 1:lo + 1 + n_rows, :])
            out = prev * taps[0:1] + raw_ref[lo:lo + n_rows, :] * taps[1:2] + nxt * taps[2:3]
            out = out * _sigmoid(out)
            dst_ref[blk, :] = out if post == 1.0 else out * post

    def project_and_scan():
        chunks = range(n_chunks)
        seqs = range(n_seq)
        slab_rows = min(rows_total, PROJ_SLAB_ROWS)
        slabs = [slice(s0, s0 + slab_rows) for s0 in range(0, rows_total, slab_rows)]
        for i, slab in enumerate(slabs):
            project(slab, slab_rows)
            if i > 0:
                conv_silu(slabs[i - 1], slab_rows)
        conv_silu(slabs[-1], slab_rows)

        for c in chunks:
            ar_ref[rows(c), :] = (_dot_nt(rq_ref[rows(c), :].astype(BF16), rk_ref[rows(c), :].astype(BF16))
                                  * rc_ref[0]).astype(BF16)
            am_ref[rows(c), :] = _dot_nt(mq_ref[rows(c), :].astype(BF16), mk_ref[rows(c), :].astype(BF16))
        for c in chunks:
            for d in range(2):
                li = gate_col(c, 8 * d)
                lf = gate_col(c, 8 * d + 4)
                strict = (r_i > s_i) if d == 0 else (r_i < s_i)
                x_mat = jnp.where(strict, lf, jnp.where(r_i == s_i, li, 0.0))
                hi, lo = _split2(jnp.concatenate([x_mat, lf], axis=1))
                d_ext = _dot(tri[d], hi) + _dot(tri[d], lo)
                bcum = d_ext[:, 128:256]
                causal = (s_i <= r_i) if d == 0 else (s_i >= r_i)
                logi_ref[d, rows(c), :] = jnp.where(causal, d_ext[:, 0:128], -jnp.inf)
                bc_ref[d, rows(c), :] = bcum
                b_end = bcum[L - 1:L, :] if d == 0 else bcum[0:1, :]
                cs_ref[_B_END + d, c:c + 1, :] = b_end
                cs_ref[_MAX_LWE + d, c:c + 1, :] = jnp.max(b_end - bcum + li, axis=0, keepdims=True)

        m_final = {}
        for s in seqs:
            for d in range(2):
                m = sm_in[d:d + 1, :] if latent else jnp.zeros((1, HEAD_DIM), F32)
                order = range(cps) if d == 0 else range(cps - 1, -1, -1)
                for j in order:
                    c = s * cps + j
                    cs_ref[_M_PREV + d, c:c + 1, :] = m
                    b_end = cs_ref[_B_END + d, c:c + 1, :]
                    m_new = jnp.maximum(b_end + m, cs_ref[_MAX_LWE + d, c:c + 1, :])
                    cs_ref[_M_NEW + d, c:c + 1, :] = m_new
                    cs_ref[_DECAY + d, c:c + 1, :] = jnp.exp(b_end + m - m_new)
                    m = m_new
                m_final[(s, d)] = m

        for c in chunks:
            kf = rk_ref[rows(c), :]
            vb = rv_ref[rows(c), :]
            for d in range(2):
                skv_ref[d, c] = _dot_tn((kf * rc_ref[3 + d]).astype(BF16), vb)
        for c in chunks:
            kf = mk_ref[rows(c), :]
            vx = mvx_ref[rows(c), :]
            for d in range(2):
                log_w_end = cs_ref[_B_END + d, c:c + 1, :] - bc_ref[d, rows(c), :] + gate_col(c, 8 * d)
                w_end = jnp.exp(log_w_end - cs_ref[_M_NEW + d, c:c + 1, :])
                ckv_ref[d, c] = _dot_tn((kf * w_end).astype(BF16), vx)

        for s in seqs:
            for d in range(2):
                if latent:
                    s_state = sr_in[d]
                    n_rep = jnp.broadcast_to(sn_in[d], (HEAD_DIM, HEAD_DIM)).T
                    c_state = jnp.concatenate([sc_in[d], n_rep], axis=1)
                else:
                    s_state = jnp.zeros((HEAD_DIM, HEAD_DIM), F32)
                    c_state = jnp.zeros((HEAD_DIM, 2 * HEAD_DIM), F32)
                order = range(cps) if d == 0 else range(cps - 1, -1, -1)
                for j in order:
                    c = s * cps + j
                    inc = skv_ref[d, c]
                    skv_ref[d, c] = s_state
                    s_state = s_state * chunk_decay[d] + inc
                    inc = ckv_ref[d, c]
                    ckv_ref[d, c] = c_state
                    decay = cs_ref[_DECAY + d, c:c + 1, :]
                    c_state = c_state * jnp.concatenate([decay, decay], axis=1) + inc
                if not latent:
                    so_ref[s, d] = s_state
                    co_ref[s, d] = c_state[:, 0:128]
                    no_ref[s, d:d + 1, :] = c_state[:, 128:256].T[0:1, :]
                    mo_ref[s, d:d + 1, :] = m_final[(s, d)]

        for c in chunks:
            qf = rq_ref[rows(c), :]
            lhs = jnp.concatenate([ar_ref[rows(c), :], (qf * rc_ref[1]).astype(BF16),
                                   (qf * rc_ref[2]).astype(BF16)], axis=1)
            rhs = jnp.concatenate([rv_ref[rows(c), :], skv_ref[0, c].astype(BF16),
                                   skv_ref[1, c].astype(BF16)], axis=0)
            rq_ref[rows(c), :] = _dot(lhs, rhs)
        for c in chunks:
            qf = mq_ref[rows(c), :]
            a_mat = am_ref[rows(c), :]
            vx = mvx_ref[rows(c), :]
            h_sum = None
            for d in range(2):
                log_intra = logi_ref[d, rows(c), :]
                log_inter = bc_ref[d, rows(c), :] + cs_ref[_M_PREV + d, c:c + 1, :]
                m_t = jnp.maximum(log_inter, jnp.max(log_intra, axis=1, keepdims=True))
                w_inter = jnp.exp(log_inter - m_t)
                w_intra = jnp.exp(log_intra - m_t)
                lhs = jnp.concatenate([(a_mat * w_intra).astype(BF16), (qf * w_inter).astype(BF16)], axis=1)
                rhs = jnp.concatenate([vx, ckv_ref[d, c].astype(BF16)], axis=0)
                res = _dot(lhs, rhs)
                h_dir = res[:, 0:128] / jnp.maximum(jnp.abs(res[:, 128:256]), jnp.exp(-m_t))
                h_sum = h_dir if h_sum is None else h_sum + h_dir
            am_ref[rows(c), :] = h_sum

    project_and_scan()

    def finish(i, carry):
        r = pl.ds(pl.multiple_of(i * ROW_TILE, ROW_TILE), ROW_TILE)
        rg = rg_ref[r, :]
        ret_y = _group_norm(rq_ref[r, :], hp[6:7]) * (rg * _sigmoid(rg))
        y_ref[r, 0:128] = ret_y.astype(BF16)
        ml_y = _group_norm(am_ref[r, :], hp[7:8]) * _sigmoid(mog_ref[r, :])
        y_ref[r, 128:256] = ml_y.astype(BF16)
        return carry

    lax.fori_loop(0, rows_total // ROW_TILE, finish, 0)


def _mixer(x2d, mod3, mod_row, norm1_g, w_pair, w_gate, head_params, *, latent, n_seq, cps,
           rope=None, states=None):
    n_tok = x2d.shape[0]
    rows_blk = n_seq * cps * CHUNK
    n_blk = n_tok // rows_blk
    n_chunks = n_seq * cps
    kern = functools.partial(_mixer_kernel, latent=latent, n_seq=n_seq, cps=cps)
    once = pl.Buffered(1)
    in_specs = [
        pl.BlockSpec((rows_blk, D_MODEL), lambda b, h: (b, 0), pipeline_mode=once if latent else None),
        pl.BlockSpec((None, 6, D_MODEL), lambda b, h: (mod_row(b), 0, 0)),
        pl.BlockSpec((1, D_MODEL), lambda b, h: (0, 0)),
        pl.BlockSpec((D_MODEL, PAIR_COLS), lambda b, h: (0, h)),
        pl.BlockSpec((D_MODEL, HEAD_DIM), lambda b, h: (0, 0)),
        pl.BlockSpec((None, 16, HEAD_DIM), lambda b, h: (h, 0, 0)),
    ]
    args = [x2d, mod3, norm1_g, w_pair, w_gate, head_params]
    y_shape = jax.ShapeDtypeStruct((n_tok, D_MODEL), BF16)
    y_spec = pl.BlockSpec((rows_blk, 2 * HEAD_DIM), lambda b, h: (b, h))
    if latent:
        assert n_seq == 1
        cos2, sin2 = rope
        s_ret, s_c, s_n, s_m = states
        in_specs += [
            pl.BlockSpec((rows_blk, HEAD_DIM), lambda b, h: (0, 0), pipeline_mode=once),
            pl.BlockSpec((rows_blk, HEAD_DIM), lambda b, h: (0, 0), pipeline_mode=once),
            pl.BlockSpec((None, None, 2, None, HEAD_DIM, HEAD_DIM), lambda b, h: (b, 0, 0, h, 0, 0)),
            pl.BlockSpec((None, None, 2, None, HEAD_DIM, HEAD_DIM), lambda b, h: (b, 0, 0, h, 0, 0)),
            pl.BlockSpec((None, None, 2, 1, HEAD_DIM), lambda b, h: (b, h, 0, 0, 0)),
            pl.BlockSpec((None, None, 2, HEAD_DIM), lambda b, h: (b, h, 0, 0)),
        ]
        args += [cos2, sin2, s_ret, s_c, s_n, s_m]
        out_shape = y_shape
        out_specs = y_spec
    else:
        bsz = n_blk * n_seq
        st = jax.ShapeDtypeStruct((bsz, 1, 2, N_HEADS, HEAD_DIM, HEAD_DIM), F32)
        vec = jax.ShapeDtypeStruct((bsz, N_HEADS, 2, HEAD_DIM), F32)
        st_spec = pl.BlockSpec((n_seq, None, 2, None, HEAD_DIM, HEAD_DIM), lambda b, h: (b, 0, 0, h, 0, 0))
        vec_spec = pl.BlockSpec((n_seq, None, 2, HEAD_DIM), lambda b, h: (b, h, 0, 0))
        out_shape = (y_shape, st, st, vec, vec)
        out_specs = (y_spec, st_spec, st_spec, vec_spec, vec_spec)
    col = lambda dt: pltpu.VMEM((rows_blk, HEAD_DIM), dt)
    scratch = [
        pltpu.VMEM((rows_blk, D_MODEL), BF16),
        col(F32), col(F32), col(BF16), col(F32),
        col(F32), col(F32),
        pltpu.VMEM((rows_blk, 2 * HEAD_DIM), BF16),
        col(F32), col(F32),
        col(BF16), col(F32),
        pltpu.VMEM((2, rows_blk, HEAD_DIM), F32),
        pltpu.VMEM((2, rows_blk, HEAD_DIM), F32),
        pltpu.VMEM((2, n_chunks, HEAD_DIM, HEAD_DIM), F32),
        pltpu.VMEM((2, n_chunks, HEAD_DIM, 2 * HEAD_DIM), F32),
        pltpu.VMEM((10, max(n_chunks, 8), HEAD_DIM), F32),
        pltpu.VMEM((5, CHUNK, CHUNK), F32),
        pltpu.VMEM((rows_blk + 2 * CONV_HALO, HEAD_DIM), F32),
        pltpu.VMEM((rows_blk + 2 * CONV_HALO, HEAD_DIM), F32),
    ]
    return pl.pallas_call(
        kern,
        out_shape=out_shape,
        grid=(n_blk, N_HEADS),
        in_specs=in_specs,
        out_specs=out_specs,
        scratch_shapes=scratch,
        compiler_params=pltpu.CompilerParams(
            dimension_semantics=("arbitrary", "arbitrary"), vmem_limit_bytes=VMEM_LIMIT),
        name="mixer_latent" if latent else "mixer_context",
    )(*args)


def _ffn_kernel(x_ref, y_ref, mod_ref, wo_ref, g2_ref, w1_ref, w2_ref, gf_ref, out_ref):
    g1 = mod_ref[2:3, :]
    sh2 = mod_ref[3:4, :]
    sc2 = mod_ref[4:5, :]
    g2 = mod_ref[5:6, :]
    x1 = x_ref[...] + g1 * _dot(y_ref[...], wo_ref[...])
    h2 = (_rms(x1, g2_ref[...]) * (1.0 + sc2) + sh2).astype(BF16)
    f = jnp.zeros_like(x1)
    for j in range(D_FF // D_MODEL):
        cols = slice(j * D_MODEL, (j + 1) * D_MODEL)
        hid = jnp.maximum(_dot(h2, w1_ref[:, cols]), 0.0)
        f = f + _dot((hid * hid).astype(BF16), w2_ref[cols, :])
    out_ref[...] = _rms(x1 + g2 * f, gf_ref[...])


def _ffn(x2d, y2d, mod3, mod_row, w_out, norm2_g, w_ff1, w_ff2, final_g):
    n_tok = x2d.shape[0]
    const = lambda i: (0, 0)
    return pl.pallas_call(
        _ffn_kernel,
        out_shape=jax.ShapeDtypeStruct((n_tok, D_MODEL), F32),
        grid=(n_tok // FFN_ROWS,),
        in_specs=[
            pl.BlockSpec((FFN_ROWS, D_MODEL), lambda i: (i, 0)),
            pl.BlockSpec((FFN_ROWS, D_MODEL), lambda i: (i, 0)),
            pl.BlockSpec((None, 6, D_MODEL), lambda i: (mod_row(i), 0, 0)),
            pl.BlockSpec((D_MODEL, D_MODEL), const, pipeline_mode=pl.Buffered(1)),
            pl.BlockSpec((1, D_MODEL), const),
            pl.BlockSpec((D_MODEL, D_FF), const, pipeline_mode=pl.Buffered(1)),
            pl.BlockSpec((D_FF, D_MODEL), const, pipeline_mode=pl.Buffered(1)),
            pl.BlockSpec((1, D_MODEL), const),
        ],
        out_specs=pl.BlockSpec((FFN_ROWS, D_MODEL), lambda i: (i, 0)),
        compiler_params=pltpu.CompilerParams(
            dimension_semantics=("arbitrary",), vmem_limit_bytes=VMEM_LIMIT),
        name="outproj_mlp",
    )(x2d, y2d, mod3, w_out, norm2_g, w_ff1, w_ff2, final_g)


def _rope_tables(seq):
    pos = jnp.arange(seq)
    row = (pos // GRID_W).astype(F32)
    col = (pos % GRID_W).astype(F32)
    nf = HEAD_DIM // 4
    inv = ROPE_BASE ** (-jnp.arange(nf, dtype=F32) / nf)
    ang = jnp.concatenate([row[:, None] * inv, col[:, None] * inv], -1)
    cos = jnp.cos(ang)
    sin = jnp.sin(ang)
    return jnp.concatenate([cos, cos], -1), jnp.concatenate([-sin, sin], -1)


def kernel(x_prompt, x_sample, state_ret, state_mlstm_C, state_mlstm_n, state_mlstm_m, c, c_ctx,
           w_ada, b_ada, norm1_g, norm2_g, w_in, conv_w, ret_decay_logit, mlstm_gate_bias,
           ret_gn_g, mlstm_gn_g, w_out, w_ff1, w_ff2, final_g):
    assert w_ada.shape[0] == 1, "single-layer kernel"
    bp, tp, _ = x_prompt.shape
    bs, ts, _ = x_sample.shape
    assert tp % CHUNK == 0 and ts % CHUNK == 0 and bp % CTX_SEQS_PER_STEP == 0

    cond = jnp.concatenate([c_ctx[None, :], c, jnp.zeros((8 - 1 - bs, D_MODEL), F32)], 0)
    mod = _modulation(cond, w_ada[0], b_ada)
    mod3 = mod[:1 + bs].reshape(1 + bs, 6, D_MODEL)

    w_pair, w_gate = _regroup_in_proj(w_in)
    w_out_p = _regroup_out_proj(w_out)
    w1 = w_ff1[0].astype(BF16)
    w2 = w_ff2[0].astype(BF16)

    cw = conv_w[0]
    hp_rows = [cw[j, :512].reshape(N_HEADS, HEAD_DIM) for j in range(3)]
    hp_rows += [cw[j, 512:].reshape(N_HEADS, HEAD_DIM) for j in range(3)]
    hp_rows += [ret_gn_g[0].reshape(N_HEADS, HEAD_DIM), mlstm_gn_g[0].reshape(N_HEADS, HEAD_DIM)]
    hp_rows += [jnp.broadcast_to(ret_decay_logit[0, d][:, None], (N_HEADS, HEAD_DIM)) for d in range(2)]
    gate_bias = jnp.pad(mlstm_gate_bias[0].reshape(1, N_GATE_COLS), ((0, 0), (HEAD_DIM - N_GATE_COLS, 0)))
    hp_rows += [jnp.broadcast_to(gate_bias, (N_HEADS, HEAD_DIM))]
    hp_rows += [jnp.zeros((N_HEADS, HEAD_DIM), F32)] * (16 - len(hp_rows))
    head_params = jnp.stack(hp_rows, axis=1).astype(F32)

    g1 = norm1_g[0][None, :]
    g2 = norm2_g[0][None, :]
    gf = final_g[None, :]
    xp2d = x_prompt.reshape(bp * tp, D_MODEL)
    xs2d = x_sample.reshape(bs * ts, D_MODEL)

    y_p, new_ret, new_c, new_n, new_m = _mixer(
        xp2d, mod3, lambda b: 0, g1, w_pair, w_gate, head_params,
        latent=False, n_seq=CTX_SEQS_PER_STEP, cps=tp // CHUNK)
    out_p = _ffn(xp2d, y_p, mod3, lambda i: 0, w_out_p, g2, w1, w2, gf).reshape(bp, tp, D_MODEL)

    s_n = jnp.transpose(state_mlstm_n[:, 0], (0, 2, 1, 3))[:, :, :, None, :]
    s_m = jnp.broadcast_to(jnp.transpose(state_mlstm_m[:, 0], (0, 2, 1))[..., None],
                           (bs, N_HEADS, 2, HEAD_DIM))
    y_s = _mixer(xs2d, mod3, lambda b: 1 + b, g1, w_pair, w_gate, head_params,
                 latent=True, n_seq=1, cps=ts // CHUNK,
                 rope=_rope_tables(ts), states=(state_ret, state_mlstm_C, s_n, s_m))
    tiles_per_seq = ts // FFN_ROWS
    out_s = _ffn(xs2d, y_s, mod3, lambda i: 1 + i // tiles_per_seq,
                 w_out_p, g2, w1, w2, gf).reshape(bs, ts, D_MODEL)

    new_n = jnp.transpose(new_n, (0, 2, 1, 3))[:, None]
    new_m = jnp.transpose(new_m[..., 0], (0, 2, 1))[:, None]
    return out_p, out_s, new_ret, new_c, new_n, new_m
```

```python
import functools

import jax
import jax.numpy as jnp
from jax import lax
from jax.experimental import pallas as pl
from jax.experimental.pallas import tpu as pltpu

F32 = jnp.float32
BF16 = jnp.bfloat16

D_MODEL = 1024
N_HEADS = 4
HEAD_DIM = 128
CHUNK = 128
GRID_W = 64
D_FF = 4 * D_MODEL
EPS = 1e-6
ROPE_BASE = 10000.0
LOG2_E = 1.4426950408889634
LN_2 = 0.6931471805599453
PAIR_COLS = 8 * HEAD_DIM
N_GATE_COLS = 4 * N_HEADS
ROW_TILE = 256
FFN_ROWS = 512
CTX_SEQS_PER_STEP = 4
PROJ_SLAB_ROWS = 256
CONV_HALO = 8
VMEM_LIMIT = 60 * 1024 * 1024


def _dot(a, b):
    return jnp.dot(a, b, preferred_element_type=F32)


def _dot_nt(a, b):
    return lax.dot_general(a, b, (((1,), (1,)), ((), ())), preferred_element_type=F32)


def _dot_tn(a, b):
    return lax.dot_general(a, b, (((0,), (0,)), ((), ())), preferred_element_type=F32)


def _rms(x, g):
    return x * lax.rsqrt(jnp.mean(x * x, axis=-1, keepdims=True) + EPS) * g


def _group_norm(o, g):
    mu = jnp.mean(o, axis=-1, keepdims=True)
    c = o - mu
    var = jnp.mean(c * c, axis=-1, keepdims=True)
    return c * lax.rsqrt(var + EPS) * g


def _log_sigmoid(x):
    return jnp.minimum(x, 0.0) - jnp.log(1.0 + jnp.exp(-jnp.abs(x)))


def _sigmoid(x):
    return 1.0 / (1.0 + jnp.exp(-x))


def _split2(x):
    hi = x.astype(BF16)
    lo = (x - hi.astype(F32)).astype(BF16)
    return hi, lo


def _mod_kernel(cond_ref, w_ref, b_ref, out_ref):
    c = cond_ref[...]
    s = (c * _sigmoid(c)).astype(BF16)
    out_ref[...] = _dot(s, w_ref[...].astype(BF16)) + b_ref[...]


def _modulation(cond, w_ada, b_ada):
    n = w_ada.shape[1]
    tn = 1024
    return pl.pallas_call(
        _mod_kernel,
        out_shape=jax.ShapeDtypeStruct((cond.shape[0], n), F32),
        grid=(n // tn,),
        in_specs=[pl.BlockSpec(cond.shape, lambda j: (0, 0)),
                  pl.BlockSpec((D_MODEL, tn), lambda j: (0, j)),
                  pl.BlockSpec((1, tn), lambda j: (0, j))],
        out_specs=pl.BlockSpec((cond.shape[0], tn), lambda j: (0, j)),
        compiler_params=pltpu.CompilerParams(dimension_semantics=("arbitrary",)),
        name="adaln_mod",
    )(cond, w_ada, b_ada)


REGROUP_ROWS = 256


def _regroup_in_kernel(wt_ref, out_ref, gate_ref):
    for h in range(N_HEADS):
        for g in range(8):
            src = (g * N_HEADS + h) * HEAD_DIM
            dst = (h * 8 + g) * HEAD_DIM
            out_ref[:, dst:dst + HEAD_DIM] = wt_ref[src:src + HEAD_DIM, :].T.astype(BF16)
    n_rows = wt_ref.shape[0]
    n_gate = n_rows - N_HEADS * PAIR_COLS
    tail = wt_ref[n_rows - HEAD_DIM:n_rows, :].T
    lane = lax.broadcasted_iota(jnp.int32, tail.shape, 1)
    gate_ref[...] = jnp.where(lane >= HEAD_DIM - n_gate, tail, 0.0).astype(BF16)


def _regroup_in_proj(w_in):
    w_t = jnp.transpose(w_in[0])
    n_cols = w_t.shape[0]
    assert n_cols == N_HEADS * PAIR_COLS + N_GATE_COLS
    return pl.pallas_call(
        _regroup_in_kernel,
        out_shape=(jax.ShapeDtypeStruct((D_MODEL, N_HEADS * PAIR_COLS), BF16),
                   jax.ShapeDtypeStruct((D_MODEL, HEAD_DIM), BF16)),
        grid=(D_MODEL // REGROUP_ROWS,),
        in_specs=[pl.BlockSpec((n_cols, REGROUP_ROWS), lambda i: (0, i))],
        out_specs=(pl.BlockSpec((REGROUP_ROWS, N_HEADS * PAIR_COLS), lambda i: (i, 0)),
                   pl.BlockSpec((REGROUP_ROWS, HEAD_DIM), lambda i: (i, 0))),
        compiler_params=pltpu.CompilerParams(dimension_semantics=("arbitrary",)),
        name="regroup_w_in",
    )(w_t)


def _regroup_out_kernel(w_ref, out_ref):
    for h in range(N_HEADS):
        for g in range(2):
            src = (g * N_HEADS + h) * HEAD_DIM
            dst = (h * 2 + g) * HEAD_DIM
            out_ref[dst:dst + HEAD_DIM, :] = w_ref[src:src + HEAD_DIM, :].astype(BF16)


def _regroup_out_proj(w_out):
    cols = 2 * HEAD_DIM
    return pl.pallas_call(
        _regroup_out_kernel,
        out_shape=jax.ShapeDtypeStruct((D_MODEL, D_MODEL), BF16),
        grid=(D_MODEL // cols,),
        in_specs=[pl.BlockSpec((None, D_MODEL, cols), lambda j: (0, 0, j))],
        out_specs=pl.BlockSpec((D_MODEL, cols), lambda j: (0, j)),
        compiler_params=pltpu.CompilerParams(dimension_semantics=("arbitrary",)),
        name="regroup_w_out",
    )(w_out)


_B_END, _MAX_LWE, _M_PREV, _M_NEW, _DECAY = 0, 2, 4, 6, 8


def _mixer_kernel(*refs, latent, n_seq, cps, n_steps):
    L = CHUNK
    n_chunks = n_seq * cps
    rows_total = n_chunks * L
    seq_len = cps * L
    assert seq_len & (seq_len - 1) == 0
    if latent:
        (x_ref, mod_ref, g1_ref, w_ref, wg_ref, hp_ref, hp_prev_ref, cos_ref, sin_ref,
         sr_in, sc_in, sn_in, sm_in, y_ref, *scratch) = refs
    else:
        (x_ref, mod_ref, g1_ref, w_ref, wg_ref, hp_ref, hp_prev_ref,
         y_ref, so_ref, co_ref, no_ref, mo_ref, *scratch) = refs
    (hn_ref, rq_ref, rk_ref, rv_ref, rg_ref, mq_ref, mk_ref, mvx_ref, mog_ref, gt_ref,
     ar_ref, am_ref, logi_ref, bc_ref, skv_ref, ckv_ref, cs_ref, rc_ref, rawq_ref, rawk_ref) = scratch

    step = pl.program_id(0)
    live = step < n_steps
    head = lax.rem(jnp.minimum(step, n_steps - 1), N_HEADS)

    @pl.when(step == 0)
    def _():
        for ref in (rq_ref, am_ref, rg_ref, mog_ref):
            ref[...] = jnp.zeros_like(ref)

    @pl.when(jnp.logical_and(head == 0, live))
    def _():
        sh1 = mod_ref[0:1, :]
        sc1 = mod_ref[1:2, :]
        g1 = g1_ref[...]

        def body(i, carry):
            r = pl.ds(pl.multiple_of(i * ROW_TILE, ROW_TILE), ROW_TILE)
            hn_ref[r, :] = (_rms(x_ref[r, :], g1) * (1.0 + sc1) + sh1).astype(BF16)
            return carry

        lax.fori_loop(0, rows_total // ROW_TILE, body, 0)

    hp = hp_ref[...]
    scale = HEAD_DIM ** -0.5

    r_i = lax.broadcasted_iota(jnp.int32, (L, L), 0)
    s_i = lax.broadcasted_iota(jnp.int32, (L, L), 1)
    r_f = r_i.astype(F32)
    s_f = s_i.astype(F32)
    lg_f = _log_sigmoid(hp[8:9])
    lg_b = _log_sigmoid(hp[9:10])
    rc_ref[0] = (jnp.where(r_i >= s_i, jnp.exp(lg_f * jnp.where(r_i >= s_i, r_f - s_f, 0.0)), 0.0)
                 + jnp.where(s_i >= r_i, jnp.exp(lg_b * jnp.where(s_i >= r_i, s_f - r_f, 0.0)), 0.0))
    rc_ref[1] = jnp.exp(lg_f * (r_f + 1.0))
    rc_ref[2] = jnp.exp(lg_b * (L - r_f))
    rc_ref[3] = jnp.exp(lg_f * (L - 1.0 - r_f))
    rc_ref[4] = jnp.exp(lg_b * r_f)
    chunk_decay = (jnp.exp(lg_f * float(L)), jnp.exp(lg_b * float(L)))

    tri = (jnp.where(r_i >= s_i, 1.0, 0.0).astype(BF16),
           jnp.where(s_i >= r_i, 1.0, 0.0).astype(BF16))

    for raw_ref in (rawq_ref, rawk_ref):
        raw_ref[0:CONV_HALO, :] = jnp.zeros((CONV_HALO, HEAD_DIM), F32)
        raw_ref[rows_total + CONV_HALO:rows_total + 2 * CONV_HALO, :] = jnp.zeros((CONV_HALO, HEAD_DIM), F32)

    def rows(c):
        return slice(c * L, (c + 1) * L)

    def gate_col(c, lane):
        g = gt_ref[rows(c), :]
        return jnp.broadcast_to(g[:, lane:lane + 1], (L, HEAD_DIM))

    hp_prev = hp_prev_ref[...]

    def finish_rows(r):
        rg = rg_ref[r, :]
        ret_y = _group_norm(rq_ref[r, :], hp_prev[6:7]) * (rg * _sigmoid(rg))
        y_ref[r, 0:128] = ret_y.astype(BF16)
        ml_y = _group_norm(am_ref[r, :], hp_prev[7:8]) * _sigmoid(mog_ref[r, :])
        y_ref[r, 128:256] = ml_y.astype(BF16)

    def project(blk, n_rows):
        hn = hn_ref[blk, :]
        pr = _dot(hn, w_ref[:, 0:256])
        q = pr[:, 0:128]
        k = pr[:, 128:256] * scale
        if latent:
            cos2 = cos_ref[blk, :]
            sin2 = sin_ref[blk, :]
            q = q * cos2 + pltpu.roll(q, HEAD_DIM // 2, axis=1) * sin2
            k = k * cos2 + pltpu.roll(k, HEAD_DIM // 2, axis=1) * sin2
        rq_ref[blk, :] = q
        rk_ref[blk, :] = k

        pv = _dot(hn, w_ref[:, 256:512])
        rv_ref[blk, :] = pv[:, 0:128].astype(BF16)
        rg_ref[blk, :] = pv[:, 128:256]

        pm = _dot(hn, w_ref[:, 512:768])
        halo_blk = slice(blk.start + CONV_HALO, blk.stop + CONV_HALO)
        rawq_ref[halo_blk, :] = pm[:, 0:128]
        rawk_ref[halo_blk, :] = pm[:, 128:256]

        po = _dot(hn, w_ref[:, 768:1024])
        mvx_ref[blk, 0:128] = po[:, 0:128].astype(BF16)
        mvx_ref[blk, 128:256] = jnp.ones((n_rows, HEAD_DIM), BF16)
        mog_ref[blk, :] = po[:, 128:256]

        gates = _dot(hn, wg_ref[...]) + hp[10:11]
        lane_t = lax.broadcasted_iota(jnp.int32, (n_rows, HEAD_DIM), 1)
        is_forget = jnp.bitwise_and(lane_t, 4) == 4
        gates = jnp.where(is_forget, _log_sigmoid(gates), gates) * LOG2_E
        gt_ref[blk, :] = pltpu.roll(gates, N_GATE_COLS - head, axis=1)

    def conv_silu(blk, n_rows):
        sub = lax.broadcasted_iota(jnp.int32, (CONV_HALO, HEAD_DIM), 0)
        starts_seq = blk.start % seq_len == 0
        ends_seq = blk.stop % seq_len == 0
        for raw_ref, dst_ref, taps, post in ((rawq_ref, mq_ref, hp[0:3], 1.0), (rawk_ref, mk_ref, hp[3:6], scale)):
            lo = blk.start + CONV_HALO
            prev = raw_ref[lo - 1:lo - 1 + n_rows, :]
            nxt = raw_ref[lo + 1:lo + 1 + n_rows, :]
            if starts_seq:
                first = jnp.where(sub == 0, 0.0, prev[0:CONV_HALO, :])
                prev = jnp.concatenate([first, prev[CONV_HALO:, :]], axis=0)
            if ends_seq:
                last = jnp.where(sub == CONV_HALO - 1, 0.0, nxt[n_rows - CONV_HALO:, :])
                nxt = jnp.concatenate([nxt[:n_rows - CONV_HALO, :], last], axis=0)
            out = prev * taps[0:1] + raw_ref[lo:lo + n_rows, :] * taps[1:2] + nxt * taps[2:3]
            out = out * _sigmoid(out)
            dst_ref[blk, :] = out if post == 1.0 else out * post

    def project_and_scan():
        chunks = range(n_chunks)
        seqs = range(n_seq)
        slab_rows = min(rows_total, PROJ_SLAB_ROWS)
        assert seq_len % slab_rows == 0
        slabs = [slice(s0, s0 + slab_rows) for s0 in range(0, rows_total, slab_rows)]
        for i, slab in enumerate(slabs):
            finish_rows(slab)
            project(slab, slab_rows)
            if i > 0:
                conv_silu(slabs[i - 1], slab_rows)
        conv_silu(slabs[-1], slab_rows)

        for c in chunks:
            ar_ref[rows(c), :] = (_dot_nt(rq_ref[rows(c), :].astype(BF16), rk_ref[rows(c), :].astype(BF16))
                                  * rc_ref[0]).astype(BF16)
            am_ref[rows(c), :] = _dot_nt(mq_ref[rows(c), :].astype(BF16), mk_ref[rows(c), :].astype(BF16))
        for c in chunks:
            for d in range(2):
                li = gate_col(c, 8 * d)
                lf = gate_col(c, 8 * d + 4)
                strict = (r_i > s_i) if d == 0 else (r_i < s_i)
                x_mat = jnp.where(strict, lf, jnp.where(r_i == s_i, li, 0.0))
                hi, lo = _split2(jnp.concatenate([x_mat, lf], axis=1))
                d_ext = _dot(tri[d], hi) + _dot(tri[d], lo)
                bcum = d_ext[:, 128:256]
                causal = (s_i <= r_i) if d == 0 else (s_i >= r_i)
                logi_ref[d, rows(c), :] = jnp.where(causal, d_ext[:, 0:128], -jnp.inf)
                bc_ref[d, rows(c), :] = bcum
                b_end = bcum[L - 1:L, :] if d == 0 else bcum[0:1, :]
                cs_ref[_B_END + d, c:c + 1, :] = b_end
                cs_ref[_MAX_LWE + d, c:c + 1, :] = jnp.max(b_end - bcum + li, axis=0, keepdims=True)

        m_final = {}
        for s in seqs:
            for d in range(2):
                m = sm_in[d:d + 1, :] * LOG2_E if latent else jnp.zeros((1, HEAD_DIM), F32)
                order = range(cps) if d == 0 else range(cps - 1, -1, -1)
                for j in order:
                    c = s * cps + j
                    cs_ref[_M_PREV + d, c:c + 1, :] = m
                    b_end = cs_ref[_B_END + d, c:c + 1, :]
                    m_new = jnp.maximum(b_end + m, cs_ref[_MAX_LWE + d, c:c + 1, :])
                    cs_ref[_M_NEW + d, c:c + 1, :] = m_new
                    cs_ref[_DECAY + d, c:c + 1, :] = jnp.exp2(b_end + m - m_new)
                    m = m_new
                m_final[(s, d)] = m

        for c in chunks:
            kf = rk_ref[rows(c), :]
            vb = rv_ref[rows(c), :]
            for d in range(2):
                skv_ref[d, c] = _dot_tn((kf * rc_ref[3 + d]).astype(BF16), vb)
        for c in chunks:
            kf = mk_ref[rows(c), :]
            vx = mvx_ref[rows(c), :]
            for d in range(2):
                log_w_end = cs_ref[_B_END + d, c:c + 1, :] - bc_ref[d, rows(c), :] + gate_col(c, 8 * d)
                w_end = jnp.exp2(log_w_end - cs_ref[_M_NEW + d, c:c + 1, :])
                ckv_ref[d, c] = _dot_tn((kf * w_end).astype(BF16), vx)

        for s in seqs:
            for d in range(2):
                if latent:
                    s_state = sr_in[d]
                    n_rep = jnp.broadcast_to(sn_in[d], (HEAD_DIM, HEAD_DIM)).T
                    c_state = jnp.concatenate([sc_in[d], n_rep], axis=1)
                else:
                    s_state = jnp.zeros((HEAD_DIM, HEAD_DIM), F32)
                    c_state = jnp.zeros((HEAD_DIM, 2 * HEAD_DIM), F32)
                order = range(cps) if d == 0 else range(cps - 1, -1, -1)
                for j in order:
                    c = s * cps + j
                    inc = skv_ref[d, c]
                    skv_ref[d, c] = s_state
                    s_state = s_state * chunk_decay[d] + inc
                    inc = ckv_ref[d, c]
                    ckv_ref[d, c] = c_state
                    decay = cs_ref[_DECAY + d, c:c + 1, :]
                    c_state = c_state * jnp.concatenate([decay, decay], axis=1) + inc
                if not latent:
                    so_ref[s, d] = s_state
                    co_ref[s, d] = c_state[:, 0:128]
                    no_ref[s, d:d + 1, :] = c_state[:, 128:256].T[0:1, :]
                    mo_ref[s, d:d + 1, :] = m_final[(s, d)] * LN_2

        for c in chunks:
            qf = rq_ref[rows(c), :]
            lhs = jnp.concatenate([ar_ref[rows(c), :], (qf * rc_ref[1]).astype(BF16),
                                   (qf * rc_ref[2]).astype(BF16)], axis=1)
            rhs = jnp.concatenate([rv_ref[rows(c), :], skv_ref[0, c].astype(BF16),
                                   skv_ref[1, c].astype(BF16)], axis=0)
            rq_ref[rows(c), :] = _dot(lhs, rhs)
        for c in chunks:
            qf = mq_ref[rows(c), :]
            a_mat = am_ref[rows(c), :]
            vx = mvx_ref[rows(c), :]
            h_sum = None
            for d in range(2):
                log_intra = logi_ref[d, rows(c), :]
                log_inter = bc_ref[d, rows(c), :] + cs_ref[_M_PREV + d, c:c + 1, :]
                m_t = jnp.maximum(log_inter, jnp.max(log_intra, axis=1, keepdims=True))
                w_inter = jnp.exp2(log_inter - m_t)
                w_intra = jnp.exp2(log_intra - m_t)
                lhs = jnp.concatenate([(a_mat * w_intra).astype(BF16), (qf * w_inter).astype(BF16)], axis=1)
                rhs = jnp.concatenate([vx, ckv_ref[d, c].astype(BF16)], axis=0)
                res = _dot(lhs, rhs)
                h_dir = res[:, 0:128] / jnp.maximum(jnp.abs(res[:, 128:256]), jnp.exp2(-m_t))
                h_sum = h_dir if h_sum is None else h_sum + h_dir
            am_ref[rows(c), :] = h_sum

    pl.when(live)(project_and_scan)

    @pl.when(step == n_steps)
    def _():
        def body(i, carry):
            finish_rows(pl.ds(pl.multiple_of(i * ROW_TILE, ROW_TILE), ROW_TILE))
            return carry

        lax.fori_loop(0, rows_total // ROW_TILE, body, 0)


def _mixer(x2d, mod3, mod_row, norm1_g, w_pair, w_gate, head_params, *, latent, n_seq, cps,
           rope=None, states=None):
    n_tok = x2d.shape[0]
    rows_blk = n_seq * cps * CHUNK
    n_blk = n_tok // rows_blk
    n_chunks = n_seq * cps
    n_steps = n_blk * N_HEADS
    kern = functools.partial(_mixer_kernel, latent=latent, n_seq=n_seq, cps=cps, n_steps=n_steps)

    def cur(f):
        def index_map(j):
            item = jnp.minimum(j, n_steps - 1)
            return f(item // N_HEADS, item % N_HEADS)
        return index_map

    def prev(f):
        def index_map(j):
            item = jnp.maximum(j - 1, 0)
            return f(item // N_HEADS, item % N_HEADS)
        return index_map

    once = pl.Buffered(1)
    in_specs = [
        pl.BlockSpec((rows_blk, D_MODEL), cur(lambda b, h: (b, 0)), pipeline_mode=once if latent else None),
        pl.BlockSpec((None, 6, D_MODEL), cur(lambda b, h: (mod_row(b), 0, 0))),
        pl.BlockSpec((1, D_MODEL), lambda j: (0, 0)),
        pl.BlockSpec((D_MODEL, PAIR_COLS), cur(lambda b, h: (0, h))),
        pl.BlockSpec((D_MODEL, HEAD_DIM), lambda j: (0, 0)),
        pl.BlockSpec((None, 16, HEAD_DIM), cur(lambda b, h: (h, 0, 0))),
        pl.BlockSpec((None, 16, HEAD_DIM), prev(lambda b, h: (h, 0, 0))),
    ]
    args = [x2d, mod3, norm1_g, w_pair, w_gate, head_params, head_params]
    y_shape = jax.ShapeDtypeStruct((n_tok, D_MODEL), BF16)
    y_spec = pl.BlockSpec((rows_blk, 2 * HEAD_DIM), prev(lambda b, h: (b, h)))
    if latent:
        assert n_seq == 1
        cos2, sin2 = rope
        s_ret, s_c, s_n, s_m = states
        in_specs += [
            pl.BlockSpec((rows_blk, HEAD_DIM), lambda j: (0, 0), pipeline_mode=once),
            pl.BlockSpec((rows_blk, HEAD_DIM), lambda j: (0, 0), pipeline_mode=once),
            pl.BlockSpec((None, None, 2, None, HEAD_DIM, HEAD_DIM), cur(lambda b, h: (b, 0, 0, h, 0, 0))),
            pl.BlockSpec((None, None, 2, None, HEAD_DIM, HEAD_DIM), cur(lambda b, h: (b, 0, 0, h, 0, 0))),
            pl.BlockSpec((None, None, 2, 1, HEAD_DIM), cur(lambda b, h: (b, h, 0, 0, 0))),
            pl.BlockSpec((None, None, 2, HEAD_DIM), cur(lambda b, h: (b, h, 0, 0))),
        ]
        args += [cos2, sin2, s_ret, s_c, s_n, s_m]
        out_shape = y_shape
        out_specs = y_spec
    else:
        bsz = n_blk * n_seq
        st = jax.ShapeDtypeStruct((bsz, 1, 2, N_HEADS, HEAD_DIM, HEAD_DIM), F32)
        vec = jax.ShapeDtypeStruct((bsz, N_HEADS, 2, HEAD_DIM), F32)
        st_spec = pl.BlockSpec((n_seq, None, 2, None, HEAD_DIM, HEAD_DIM), cur(lambda b, h: (b, 0, 0, h, 0, 0)))
        vec_spec = pl.BlockSpec((n_seq, None, 2, HEAD_DIM), cur(lambda b, h: (b, h, 0, 0)))
        out_shape = (y_shape, st, st, vec, vec)
        out_specs = (y_spec, st_spec, st_spec, vec_spec, vec_spec)
    col = lambda dt: pltpu.VMEM((rows_blk, HEAD_DIM), dt)
    scratch = [
        pltpu.VMEM((rows_blk, D_MODEL), BF16),
        col(F32), col(F32), col(BF16), col(F32),
        col(F32), col(F32),
        pltpu.VMEM((rows_blk, 2 * HEAD_DIM), BF16),
        col(F32), col(F32),
        col(BF16), col(F32),
        pltpu.VMEM((2, rows_blk, HEAD_DIM), F32),
        pltpu.VMEM((2, rows_blk, HEAD_DIM), F32),
        pltpu.VMEM((2, n_chunks, HEAD_DIM, HEAD_DIM), F32),
        pltpu.VMEM((2, n_chunks, HEAD_DIM, 2 * HEAD_DIM), F32),
        pltpu.VMEM((10, max(n_chunks, 8), HEAD_DIM), F32),
        pltpu.VMEM((5, CHUNK, CHUNK), F32),
        pltpu.VMEM((rows_blk + 2 * CONV_HALO, HEAD_DIM), F32),
        pltpu.VMEM((rows_blk + 2 * CONV_HALO, HEAD_DIM), F32),
    ]
    return pl.pallas_call(
        kern,
        out_shape=out_shape,
        grid=(n_steps + 1,),
        in_specs=in_specs,
        out_specs=out_specs,
        scratch_shapes=scratch,
        compiler_params=pltpu.CompilerParams(
            dimension_semantics=("arbitrary",), vmem_limit_bytes=VMEM_LIMIT),
        name="mixer_latent" if latent else "mixer_context",
    )(*args)


def _ffn_kernel(x_ref, y_ref, mod_ref, wo_ref, g2_ref, w1_ref, w2_ref, gf_ref, out_ref):
    g1 = mod_ref[2:3, :]
    sh2 = mod_ref[3:4, :]
    sc2 = mod_ref[4:5, :]
    g2 = mod_ref[5:6, :]
    x1 = x_ref[...] + g1 * _dot(y_ref[...], wo_ref[...])
    h2 = (_rms(x1, g2_ref[...]) * (1.0 + sc2) + sh2).astype(BF16)
    f = jnp.zeros_like(x1)
    for j in range(D_FF // D_MODEL):
        cols = slice(j * D_MODEL, (j + 1) * D_MODEL)
        hid = jnp.maximum(_dot(h2, w1_ref[:, cols]), 0.0)
        f = f + _dot((hid * hid).astype(BF16), w2_ref[cols, :])
    out_ref[...] = _rms(x1 + g2 * f, gf_ref[...])


def _ffn(x2d, y2d, mod3, mod_row, w_out, norm2_g, w_ff1, w_ff2, final_g):
    n_tok = x2d.shape[0]
    const = lambda i: (0, 0)
    return pl.pallas_call(
        _ffn_kernel,
        out_shape=jax.ShapeDtypeStruct((n_tok, D_MODEL), F32),
        grid=(n_tok // FFN_ROWS,),
        in_specs=[
            pl.BlockSpec((FFN_ROWS, D_MODEL), lambda i: (i, 0)),
            pl.BlockSpec((FFN_ROWS, D_MODEL), lambda i: (i, 0)),
            pl.BlockSpec((None, 6, D_MODEL), lambda i: (mod_row(i), 0, 0)),
            pl.BlockSpec((D_MODEL, D_MODEL), const, pipeline_mode=pl.Buffered(1)),
            pl.BlockSpec((1, D_MODEL), const),
            pl.BlockSpec((D_MODEL, D_FF), const, pipeline_mode=pl.Buffered(1)),
            pl.BlockSpec((D_FF, D_MODEL), const, pipeline_mode=pl.Buffered(1)),
            pl.BlockSpec((1, D_MODEL), const),
        ],
        out_specs=pl.BlockSpec((FFN_ROWS, D_MODEL), lambda i: (i, 0)),
        compiler_params=pltpu.CompilerParams(
            dimension_semantics=("arbitrary",), vmem_limit_bytes=VMEM_LIMIT),
        name="outproj_mlp",
    )(x2d, y2d, mod3, w_out, norm2_g, w_ff1, w_ff2, final_g)


def _rope_tables(seq):
    pos = jnp.arange(seq)
    row = (pos // GRID_W).astype(F32)
    col = (pos % GRID_W).astype(F32)
    nf = HEAD_DIM // 4
    inv = ROPE_BASE ** (-jnp.arange(nf, dtype=F32) / nf)
    ang = jnp.concatenate([row[:, None] * inv, col[:, None] * inv], -1)
    cos = jnp.cos(ang)
    sin = jnp.sin(ang)
    return jnp.concatenate([cos, cos], -1), jnp.concatenate([-sin, sin], -1)


def kernel(x_prompt, x_sample, state_ret, state_mlstm_C, state_mlstm_n, state_mlstm_m, c, c_ctx,
           w_ada, b_ada, norm1_g, norm2_g, w_in, conv_w, ret_decay_logit, mlstm_gate_bias,
           ret_gn_g, mlstm_gn_g, w_out, w_ff1, w_ff2, final_g):
    assert w_ada.shape[0] == 1, "single-layer kernel"
    bp, tp, _ = x_prompt.shape
    bs, ts, _ = x_sample.shape
    assert tp % CHUNK == 0 and ts % CHUNK == 0 and bp % CTX_SEQS_PER_STEP == 0

    cond = jnp.concatenate([c_ctx[None, :], c, jnp.zeros((8 - 1 - bs, D_MODEL), F32)], 0)
    mod = _modulation(cond, w_ada[0], b_ada)
    mod3 = mod[:1 + bs].reshape(1 + bs, 6, D_MODEL)

    w_pair, w_gate = _regroup_in_proj(w_in)
    w_out_p = _regroup_out_proj(w_out)
    w1 = w_ff1[0].astype(BF16)
    w2 = w_ff2[0].astype(BF16)

    cw = conv_w[0]
    hp_rows = [cw[j, :512].reshape(N_HEADS, HEAD_DIM) for j in range(3)]
    hp_rows += [cw[j, 512:].reshape(N_HEADS, HEAD_DIM) for j in range(3)]
    hp_rows += [ret_gn_g[0].reshape(N_HEADS, HEAD_DIM), mlstm_gn_g[0].reshape(N_HEADS, HEAD_DIM)]
    hp_rows += [jnp.broadcast_to(ret_decay_logit[0, d][:, None], (N_HEADS, HEAD_DIM)) for d in range(2)]
    gate_bias = jnp.pad(mlstm_gate_bias[0].reshape(1, N_GATE_COLS), ((0, 0), (HEAD_DIM - N_GATE_COLS, 0)))
    hp_rows += [jnp.broadcast_to(gate_bias, (N_HEADS, HEAD_DIM))]
    hp_rows += [jnp.zeros((N_HEADS, HEAD_DIM), F32)] * (16 - len(hp_rows))
    head_params = jnp.stack(hp_rows, axis=1).astype(F32)

    g1 = norm1_g[0][None, :]
    g2 = norm2_g[0][None, :]
    gf = final_g[None, :]
    xp2d = x_prompt.reshape(bp * tp, D_MODEL)
    xs2d = x_sample.reshape(bs * ts, D_MODEL)

    y_p, new_ret, new_c, new_n, new_m = _mixer(
        xp2d, mod3, lambda b: 0, g1, w_pair, w_gate, head_params,
        latent=False, n_seq=CTX_SEQS_PER_STEP, cps=tp // CHUNK)
    out_p = _ffn(xp2d, y_p, mod3, lambda i: 0, w_out_p, g2, w1, w2, gf).reshape(bp, tp, D_MODEL)

    s_n = jnp.transpose(state_mlstm_n[:, 0], (0, 2, 1, 3))[:, :, :, None, :]
    s_m = jnp.broadcast_to(jnp.transpose(state_mlstm_m[:, 0], (0, 2, 1))[..., None],
                           (bs, N_HEADS, 2, HEAD_DIM))
    y_s = _mixer(xs2d, mod3, lambda b: 1 + b, g1, w_pair, w_gate, head_params,
                 latent=True, n_seq=1, cps=ts // CHUNK,
                 rope=_rope_tables(ts), states=(state_ret, state_mlstm_C, s_n, s_m))
    tiles_per_seq = ts // FFN_ROWS
    out_s = _ffn(xs2d, y_s, mod3, lambda i: 1 + i // tiles_per_seq,
                 w_out_p, g2, w1, w2, gf).reshape(bs, ts, D_MODEL)

    new_n = jnp.transpose(new_n, (0, 2, 1, 3))[:, None]
    new_m = jnp.transpose(new_m[..., 0], (0, 2, 1))[:, None]
    return out_p, out_s, new_ret, new_c, new_n, new_m
```

```python
import functools

import jax
import jax.numpy as jnp
from jax import lax
from jax.experimental import pallas as pl
from jax.experimental.pallas import tpu as pltpu

F32 = jnp.float32
BF16 = jnp.bfloat16

D_MODEL = 1024
N_HEADS = 4
HEAD_DIM = 128
CHUNK = 128
GRID_W = 64
D_FF = 4 * D_MODEL
EPS = 1e-6
ROPE_BASE = 10000.0
LOG2_E = 1.4426950408889634
LN_2 = 0.6931471805599453
PAIR_COLS = 8 * HEAD_DIM
N_GATE_COLS = 4 * N_HEADS
ROW_TILE = 256
FFN_ROWS = 512
CTX_SEQS_PER_STEP = 4
PROJ_SLAB_ROWS = 256
CONV_HALO = 8
VMEM_LIMIT = 60 * 1024 * 1024


def _dot(a, b):
    return jnp.dot(a, b, preferred_element_type=F32)


def _dot_nt(a, b):
    return lax.dot_general(a, b, (((1,), (1,)), ((), ())), preferred_element_type=F32)


def _dot_tn(a, b):
    return lax.dot_general(a, b, (((0,), (0,)), ((), ())), preferred_element_type=F32)


def _rms(x, g):
    return x * lax.rsqrt(jnp.mean(x * x, axis=-1, keepdims=True) + EPS) * g


def _group_norm(o, g):
    mu = jnp.mean(o, axis=-1, keepdims=True)
    c = o - mu
    var = jnp.mean(c * c, axis=-1, keepdims=True)
    return c * lax.rsqrt(var + EPS) * g


def _log_sigmoid(x):
    return jnp.minimum(x, 0.0) - jnp.log(1.0 + jnp.exp(-jnp.abs(x)))


def _sigmoid(x):
    return 1.0 / (1.0 + jnp.exp(-x))


def _split2(x):
    hi = x.astype(BF16)
    lo = (x - hi.astype(F32)).astype(BF16)
    return hi, lo


def _mod_kernel(cond_ref, w_ref, b_ref, out_ref):
    c = cond_ref[...]
    s = (c * _sigmoid(c)).astype(BF16)
    out_ref[...] = _dot(s, w_ref[...].astype(BF16)) + b_ref[...]


def _modulation(cond, w_ada, b_ada):
    n = w_ada.shape[1]
    tn = 1024
    return pl.pallas_call(
        _mod_kernel,
        out_shape=jax.ShapeDtypeStruct((cond.shape[0], n), F32),
        grid=(n // tn,),
        in_specs=[pl.BlockSpec(cond.shape, lambda j: (0, 0)),
                  pl.BlockSpec((D_MODEL, tn), lambda j: (0, j)),
                  pl.BlockSpec((1, tn), lambda j: (0, j))],
        out_specs=pl.BlockSpec((cond.shape[0], tn), lambda j: (0, j)),
        compiler_params=pltpu.CompilerParams(dimension_semantics=("arbitrary",)),
        name="adaln_mod",
    )(cond, w_ada, b_ada)


REGROUP_ROWS = 256


def _regroup_in_kernel(wt_ref, out_ref, gate_ref):
    for h in range(N_HEADS):
        for g in range(8):
            src = (g * N_HEADS + h) * HEAD_DIM
            dst = (h * 8 + g) * HEAD_DIM
            out_ref[:, dst:dst + HEAD_DIM] = wt_ref[src:src + HEAD_DIM, :].T.astype(BF16)
    n_rows = wt_ref.shape[0]
    n_gate = n_rows - N_HEADS * PAIR_COLS
    tail = wt_ref[n_rows - HEAD_DIM:n_rows, :].T
    lane = lax.broadcasted_iota(jnp.int32, tail.shape, 1)
    gate_ref[...] = jnp.where(lane >= HEAD_DIM - n_gate, tail, 0.0).astype(BF16)


def _regroup_in_proj(w_in):
    w_t = jnp.transpose(w_in[0])
    n_cols = w_t.shape[0]
    assert n_cols == N_HEADS * PAIR_COLS + N_GATE_COLS
    return pl.pallas_call(
        _regroup_in_kernel,
        out_shape=(jax.ShapeDtypeStruct((D_MODEL, N_HEADS * PAIR_COLS), BF16),
                   jax.ShapeDtypeStruct((D_MODEL, HEAD_DIM), BF16)),
        grid=(D_MODEL // REGROUP_ROWS,),
        in_specs=[pl.BlockSpec((n_cols, REGROUP_ROWS), lambda i: (0, i))],
        out_specs=(pl.BlockSpec((REGROUP_ROWS, N_HEADS * PAIR_COLS), lambda i: (i, 0)),
                   pl.BlockSpec((REGROUP_ROWS, HEAD_DIM), lambda i: (i, 0))),
        compiler_params=pltpu.CompilerParams(dimension_semantics=("arbitrary",)),
        name="regroup_w_in",
    )(w_t)


def _regroup_out_kernel(w_ref, out_ref):
    for h in range(N_HEADS):
        for g in range(2):
            src = (g * N_HEADS + h) * HEAD_DIM
            dst = (h * 2 + g) * HEAD_DIM
            out_ref[dst:dst + HEAD_DIM, :] = w_ref[src:src + HEAD_DIM, :].astype(BF16)


def _regroup_out_proj(w_out):
    cols = 2 * HEAD_DIM
    return pl.pallas_call(
        _regroup_out_kernel,
        out_shape=jax.ShapeDtypeStruct((D_MODEL, D_MODEL), BF16),
        grid=(D_MODEL // cols,),
        in_specs=[pl.BlockSpec((None, D_MODEL, cols), lambda j: (0, 0, j))],
        out_specs=pl.BlockSpec((D_MODEL, cols), lambda j: (0, j)),
        compiler_params=pltpu.CompilerParams(dimension_semantics=("arbitrary",)),
        name="regroup_w_out",
    )(w_out)


_B_END, _MAX_LWE, _M_PREV, _M_NEW, _DECAY = 0, 2, 4, 6, 8


def _mixer_kernel(*refs, latent, n_seq, cps, n_steps):
    L = CHUNK
    n_chunks = n_seq * cps
    rows_total = n_chunks * L
    seq_len = cps * L
    assert seq_len & (seq_len - 1) == 0
    if latent:
        (x_ref, mod_ref, g1_ref, w_ref, wg_ref, hp_ref, hp_prev_ref, cos_ref, sin_ref,
         sr_in, sc_in, sn_in, sm_in, y_ref, *scratch) = refs
    else:
        (x_ref, mod_ref, g1_ref, w_ref, wg_ref, hp_ref, hp_prev_ref,
         y_ref, so_ref, co_ref, no_ref, mo_ref, *scratch) = refs
    (hn_ref, rq_ref, rk_ref, rv_ref, rg_ref, mq_ref, mk_ref, mvx_ref, mog_ref, gt_ref,
     ar_ref, am_ref, logi_ref, bc_ref, skv_ref, ckv_ref, cs_ref, rc_ref, rawq_ref, rawk_ref) = scratch

    step = pl.program_id(0)
    live = step < n_steps
    item = jnp.minimum(step, n_steps - 1)
    head = lax.rem(item, N_HEADS)
    hn_slot = lax.rem(item // N_HEADS, 2)
    norm_next = jnp.logical_and(head == N_HEADS - 1, step + 1 < n_steps)

    def norm_rows(r, slot):
        gain = g1_ref[...] * (1.0 + mod_ref[1:2, :])
        xr = x_ref[r, :]
        inv = lax.rsqrt(jnp.mean(xr * xr, axis=-1, keepdims=True) + EPS)
        hn_ref[slot, r, :] = (xr * inv * gain + mod_ref[0:1, :]).astype(BF16)

    @pl.when(step == 0)
    def _():
        for ref in (rq_ref, am_ref, rg_ref, mog_ref):
            ref[...] = jnp.zeros_like(ref)

        def body(i, carry):
            norm_rows(pl.ds(pl.multiple_of(i * ROW_TILE, ROW_TILE), ROW_TILE), 0)
            return carry

        lax.fori_loop(0, rows_total // ROW_TILE, body, 0)

    hp = hp_ref[...]
    scale = HEAD_DIM ** -0.5

    r_i = lax.broadcasted_iota(jnp.int32, (L, L), 0)
    s_i = lax.broadcasted_iota(jnp.int32, (L, L), 1)
    r_f = r_i.astype(F32)
    s_f = s_i.astype(F32)
    lg_f = _log_sigmoid(hp[8:9])
    lg_b = _log_sigmoid(hp[9:10])
    rc_ref[0] = (jnp.where(r_i >= s_i, jnp.exp(lg_f * jnp.where(r_i >= s_i, r_f - s_f, 0.0)), 0.0)
                 + jnp.where(s_i >= r_i, jnp.exp(lg_b * jnp.where(s_i >= r_i, s_f - r_f, 0.0)), 0.0))
    rc_ref[1] = jnp.exp(lg_f * (r_f + 1.0))
    rc_ref[2] = jnp.exp(lg_b * (L - r_f))
    rc_ref[3] = jnp.exp(lg_f * (L - 1.0 - r_f))
    rc_ref[4] = jnp.exp(lg_b * r_f)
    chunk_decay = (jnp.exp(lg_f * float(L)), jnp.exp(lg_b * float(L)))

    tri = (jnp.where(r_i >= s_i, 1.0, 0.0).astype(BF16),
           jnp.where(s_i >= r_i, 1.0, 0.0).astype(BF16))

    for raw_ref in (rawq_ref, rawk_ref):
        raw_ref[0:CONV_HALO, :] = jnp.zeros((CONV_HALO, HEAD_DIM), F32)
        raw_ref[rows_total + CONV_HALO:rows_total + 2 * CONV_HALO, :] = jnp.zeros((CONV_HALO, HEAD_DIM), F32)

    def rows(c):
        return slice(c * L, (c + 1) * L)

    def gate_col(c, lane):
        g = gt_ref[rows(c), :]
        return jnp.broadcast_to(g[:, lane:lane + 1], (L, HEAD_DIM))

    hp_prev = hp_prev_ref[...]

    def finish_rows(r):
        rg = rg_ref[r, :]
        ret_y = _group_norm(rq_ref[r, :], hp_prev[6:7]) * (rg * _sigmoid(rg))
        y_ref[r, 0:128] = ret_y.astype(BF16)
        ml_y = _group_norm(am_ref[r, :], hp_prev[7:8]) * _sigmoid(mog_ref[r, :])
        y_ref[r, 128:256] = ml_y.astype(BF16)

    def project(blk, n_rows):
        hn = hn_ref[hn_slot, blk, :]
        pr = _dot(hn, w_ref[:, 0:256])
        q = pr[:, 0:128]
        k = pr[:, 128:256] * scale
        if latent:
            cos2 = cos_ref[blk, :]
            sin2 = sin_ref[blk, :]
            q = q * cos2 + pltpu.roll(q, HEAD_DIM // 2, axis=1) * sin2
            k = k * cos2 + pltpu.roll(k, HEAD_DIM // 2, axis=1) * sin2
        rq_ref[blk, :] = q
        rk_ref[blk, :] = k

        pv = _dot(hn, w_ref[:, 256:512])
        rv_ref[blk, :] = pv[:, 0:128].astype(BF16)
        rg_ref[blk, :] = pv[:, 128:256]

        pm = _dot(hn, w_ref[:, 512:768])
        halo_blk = slice(blk.start + CONV_HALO, blk.stop + CONV_HALO)
        rawq_ref[halo_blk, :] = pm[:, 0:128]
        rawk_ref[halo_blk, :] = pm[:, 128:256]

        po = _dot(hn, w_ref[:, 768:1024])
        mvx_ref[blk, 0:128] = po[:, 0:128].astype(BF16)
        mvx_ref[blk, 128:256] = jnp.ones((n_rows, HEAD_DIM), BF16)
        mog_ref[blk, :] = po[:, 128:256]

        gates = _dot(hn, wg_ref[...]) + hp[10:11]
        lane_t = lax.broadcasted_iota(jnp.int32, (n_rows, HEAD_DIM), 1)
        is_forget = jnp.bitwise_and(lane_t, 4) == 4
        gates = jnp.where(is_forget, _log_sigmoid(gates), gates) * LOG2_E
        gt_ref[blk, :] = pltpu.roll(gates, N_GATE_COLS - head, axis=1)

    def conv_silu(blk, n_rows):
        sub = lax.broadcasted_iota(jnp.int32, (CONV_HALO, HEAD_DIM), 0)
        starts_seq = blk.start % seq_len == 0
        ends_seq = blk.stop % seq_len == 0
        for raw_ref, dst_ref, taps, post in ((rawq_ref, mq_ref, hp[0:3], 1.0), (rawk_ref, mk_ref, hp[3:6], scale)):
            lo = blk.start + CONV_HALO
            prev = raw_ref[lo - 1:lo - 1 + n_rows, :]
            nxt = raw_ref[lo + 1:lo + 1 + n_rows, :]
            if starts_seq:
                first = jnp.where(sub == 0, 0.0, prev[0:CONV_HALO, :])
                prev = jnp.concatenate([first, prev[CONV_HALO:, :]], axis=0)
            if ends_seq:
                last = jnp.where(sub == CONV_HALO - 1, 0.0, nxt[n_rows - CONV_HALO:, :])
                nxt = jnp.concatenate([nxt[:n_rows - CONV_HALO, :], last], axis=0)
            out = prev * taps[0:1] + raw_ref[lo:lo + n_rows, :] * taps[1:2] + nxt * taps[2:3]
            out = out * _sigmoid(out)
            dst_ref[blk, :] = out if post == 1.0 else out * post

    def project_and_scan(with_norm):
        slab_rows = min(rows_total, PROJ_SLAB_ROWS)
        assert seq_len % slab_rows == 0
        slabs = [slice(s0, s0 + slab_rows) for s0 in range(0, rows_total, slab_rows)]
        for i, slab in enumerate(slabs):
            finish_rows(slab)
            if with_norm:
                norm_rows(slab, 1 - hn_slot)
            project(slab, slab_rows)
            if i > 0:
                conv_silu(slabs[i - 1], slab_rows)
        conv_silu(slabs[-1], slab_rows)
        pre_pass(range(n_chunks))
        scan_passes(range(n_chunks), range(n_seq))

    def pre_pass(chunks):
        for c in chunks:
            ar_ref[rows(c), :] = (_dot_nt(rq_ref[rows(c), :].astype(BF16), rk_ref[rows(c), :].astype(BF16))
                                  * rc_ref[0]).astype(BF16)
            am_ref[rows(c), :] = _dot_nt(mq_ref[rows(c), :].astype(BF16), mk_ref[rows(c), :].astype(BF16))
        for c in chunks:
            for d in range(2):
                li = gate_col(c, 8 * d)
                lf = gate_col(c, 8 * d + 4)
                strict = (r_i > s_i) if d == 0 else (r_i < s_i)
                x_mat = jnp.where(strict, lf, jnp.where(r_i == s_i, li, 0.0))
                hi, lo = _split2(jnp.concatenate([x_mat, lf], axis=1))
                d_ext = _dot(tri[d], hi) + _dot(tri[d], lo)
                bcum = d_ext[:, 128:256]
                causal = (s_i <= r_i) if d == 0 else (s_i >= r_i)
                logi_ref[d, rows(c), :] = jnp.where(causal, d_ext[:, 0:128], -jnp.inf)
                bc_ref[d, rows(c), :] = bcum
                b_end = bcum[L - 1:L, :] if d == 0 else bcum[0:1, :]
                cs_ref[_B_END + d, c:c + 1, :] = b_end
                cs_ref[_MAX_LWE + d, c:c + 1, :] = jnp.max(b_end - bcum + li, axis=0, keepdims=True)
        for c in chunks:
            kf = rk_ref[rows(c), :]
            vb = rv_ref[rows(c), :]
            for d in range(2):
                skv_ref[d, c] = _dot_tn((kf * rc_ref[3 + d]).astype(BF16), vb)

    def scan_passes(chunks, seqs):
        m_final = {}
        for s in seqs:
            for d in range(2):
                m = sm_in[d:d + 1, :] * LOG2_E if latent else jnp.zeros((1, HEAD_DIM), F32)
                order = range(cps) if d == 0 else range(cps - 1, -1, -1)
                for j in order:
                    c = s * cps + j
                    cs_ref[_M_PREV + d, c:c + 1, :] = m
                    b_end = cs_ref[_B_END + d, c:c + 1, :]
                    m_new = jnp.maximum(b_end + m, cs_ref[_MAX_LWE + d, c:c + 1, :])
                    cs_ref[_M_NEW + d, c:c + 1, :] = m_new
                    cs_ref[_DECAY + d, c:c + 1, :] = jnp.exp2(b_end + m - m_new)
                    m = m_new
                m_final[(s, d)] = m

        for c in chunks:
            kf = mk_ref[rows(c), :]
            vx = mvx_ref[rows(c), :]
            for d in range(2):
                log_w_end = cs_ref[_B_END + d, c:c + 1, :] - bc_ref[d, rows(c), :] + gate_col(c, 8 * d)
                w_end = jnp.exp2(log_w_end - cs_ref[_M_NEW + d, c:c + 1, :])
                ckv_ref[d, c] = _dot_tn((kf * w_end).astype(BF16), vx)

        for s in seqs:
            for d in range(2):
                if latent:
                    s_state = sr_in[d]
                    n_rep = jnp.broadcast_to(sn_in[d], (HEAD_DIM, HEAD_DIM)).T
                    c_state = jnp.concatenate([sc_in[d], n_rep], axis=1)
                else:
                    s_state = jnp.zeros((HEAD_DIM, HEAD_DIM), F32)
                    c_state = jnp.zeros((HEAD_DIM, 2 * HEAD_DIM), F32)
                order = range(cps) if d == 0 else range(cps - 1, -1, -1)
                for j in order:
                    c = s * cps + j
                    inc = skv_ref[d, c]
                    skv_ref[d, c] = s_state
                    s_state = s_state * chunk_decay[d] + inc
                    inc = ckv_ref[d, c]
                    ckv_ref[d, c] = c_state
                    decay = cs_ref[_DECAY + d, c:c + 1, :]
                    c_state = c_state * jnp.concatenate([decay, decay], axis=1) + inc
                if not latent:
                    so_ref[s, d] = s_state
                    co_ref[s, d] = c_state[:, 0:128]
                    no_ref[s, d:d + 1, :] = c_state[:, 128:256].T[0:1, :]
                    mo_ref[s, d:d + 1, :] = m_final[(s, d)] * LN_2

        for c in chunks:
            qf = rq_ref[rows(c), :]
            lhs = jnp.concatenate([ar_ref[rows(c), :], (qf * rc_ref[1]).astype(BF16),
                                   (qf * rc_ref[2]).astype(BF16)], axis=1)
            rhs = jnp.concatenate([rv_ref[rows(c), :], skv_ref[0, c].astype(BF16),
                                   skv_ref[1, c].astype(BF16)], axis=0)
            rq_ref[rows(c), :] = _dot(lhs, rhs)
            qf = mq_ref[rows(c), :]
            a_mat = am_ref[rows(c), :]
            vx = mvx_ref[rows(c), :]
            h_sum = None
            for d in range(2):
                log_intra = logi_ref[d, rows(c), :]
                log_inter = bc_ref[d, rows(c), :] + cs_ref[_M_PREV + d, c:c + 1, :]
                m_t = jnp.maximum(log_inter, jnp.max(log_intra, axis=1, keepdims=True))
                w_inter = jnp.exp2(log_inter - m_t)
                w_intra = jnp.exp2(log_intra - m_t)
                lhs = jnp.concatenate([(a_mat * w_intra).astype(BF16), (qf * w_inter).astype(BF16)], axis=1)
                rhs = jnp.concatenate([vx, ckv_ref[d, c].astype(BF16)], axis=0)
                res = _dot(lhs, rhs)
                h_dir = res[:, 0:128] / jnp.maximum(jnp.abs(res[:, 128:256]), jnp.exp2(-m_t))
                h_sum = h_dir if h_sum is None else h_sum + h_dir
            am_ref[rows(c), :] = h_sum

    pl.when(jnp.logical_and(live, norm_next))(functools.partial(project_and_scan, True))
    pl.when(jnp.logical_and(live, jnp.logical_not(norm_next)))(functools.partial(project_and_scan, False))

    @pl.when(step == n_steps)
    def _():
        def body(i, carry):
            finish_rows(pl.ds(pl.multiple_of(i * ROW_TILE, ROW_TILE), ROW_TILE))
            return carry

        lax.fori_loop(0, rows_total // ROW_TILE, body, 0)


def _mixer(x2d, mod3, mod_row, norm1_g, w_pair, w_gate, head_params, *, latent, n_seq, cps,
           rope=None, states=None):
    n_tok = x2d.shape[0]
    rows_blk = n_seq * cps * CHUNK
    n_blk = n_tok // rows_blk
    n_chunks = n_seq * cps
    n_steps = n_blk * N_HEADS
    kern = functools.partial(_mixer_kernel, latent=latent, n_seq=n_seq, cps=cps, n_steps=n_steps)

    def cur(f):
        def index_map(j):
            item = jnp.minimum(j, n_steps - 1)
            return f(item // N_HEADS, item % N_HEADS)
        return index_map

    def nxt(f):
        def index_map(j):
            item = jnp.minimum(j + 1, n_steps - 1)
            return f(item // N_HEADS, item % N_HEADS)
        return index_map

    def prev(f):
        def index_map(j):
            item = jnp.maximum(j - 1, 0)
            return f(item // N_HEADS, item % N_HEADS)
        return index_map

    once = pl.Buffered(1)
    in_specs = [
        pl.BlockSpec((rows_blk, D_MODEL), nxt(lambda b, h: (b, 0)), pipeline_mode=once if latent else None),
        pl.BlockSpec((None, 6, D_MODEL), nxt(lambda b, h: (mod_row(b), 0, 0))),
        pl.BlockSpec((1, D_MODEL), lambda j: (0, 0)),
        pl.BlockSpec((D_MODEL, PAIR_COLS), cur(lambda b, h: (0, h))),
        pl.BlockSpec((D_MODEL, HEAD_DIM), lambda j: (0, 0)),
        pl.BlockSpec((None, 16, HEAD_DIM), cur(lambda b, h: (h, 0, 0))),
        pl.BlockSpec((None, 16, HEAD_DIM), prev(lambda b, h: (h, 0, 0))),
    ]
    args = [x2d, mod3, norm1_g, w_pair, w_gate, head_params, head_params]
    y_shape = jax.ShapeDtypeStruct((n_tok, D_MODEL), BF16)
    y_spec = pl.BlockSpec((rows_blk, 2 * HEAD_DIM), prev(lambda b, h: (b, h)))
    if latent:
        assert n_seq == 1
        cos2, sin2 = rope
        s_ret, s_c, s_n, s_m = states
        in_specs += [
            pl.BlockSpec((rows_blk, HEAD_DIM), lambda j: (0, 0), pipeline_mode=once),
            pl.BlockSpec((rows_blk, HEAD_DIM), lambda j: (0, 0), pipeline_mode=once),
            pl.BlockSpec((None, None, 2, None, HEAD_DIM, HEAD_DIM), cur(lambda b, h: (b, 0, 0, h, 0, 0))),
            pl.BlockSpec((None, None, 2, None, HEAD_DIM, HEAD_DIM), cur(lambda b, h: (b, 0, 0, h, 0, 0))),
            pl.BlockSpec((None, None, 2, 1, HEAD_DIM), cur(lambda b, h: (b, h, 0, 0, 0))),
            pl.BlockSpec((None, None, 2, HEAD_DIM), cur(lambda b, h: (b, h, 0, 0))),
        ]
        args += [cos2, sin2, s_ret, s_c, s_n, s_m]
        out_shape = y_shape
        out_specs = y_spec
    else:
        bsz = n_blk * n_seq
        st = jax.ShapeDtypeStruct((bsz, 1, 2, N_HEADS, HEAD_DIM, HEAD_DIM), F32)
        vec = jax.ShapeDtypeStruct((bsz, N_HEADS, 2, HEAD_DIM), F32)
        st_spec = pl.BlockSpec((n_seq, None, 2, None, HEAD_DIM, HEAD_DIM), cur(lambda b, h: (b, 0, 0, h, 0, 0)))
        vec_spec = pl.BlockSpec((n_seq, None, 2, HEAD_DIM), cur(lambda b, h: (b, h, 0, 0)))
        out_shape = (y_shape, st, st, vec, vec)
        out_specs = (y_spec, st_spec, st_spec, vec_spec, vec_spec)
    col = lambda dt: pltpu.VMEM((rows_blk, HEAD_DIM), dt)
    scratch = [
        pltpu.VMEM((2, rows_blk, D_MODEL), BF16),
        col(F32), col(F32), col(BF16), col(F32),
        col(F32), col(F32),
        pltpu.VMEM((rows_blk, 2 * HEAD_DIM), BF16),
        col(F32), col(F32),
        col(BF16), col(F32),
        pltpu.VMEM((2, rows_blk, HEAD_DIM), F32),
        pltpu.VMEM((2, rows_blk, HEAD_DIM), F32),
        pltpu.VMEM((2, n_chunks, HEAD_DIM, HEAD_DIM), F32),
        pltpu.VMEM((2, n_chunks, HEAD_DIM, 2 * HEAD_DIM), F32),
        pltpu.VMEM((10, max(n_chunks, 8), HEAD_DIM), F32),
        pltpu.VMEM((5, CHUNK, CHUNK), F32),
        pltpu.VMEM((rows_blk + 2 * CONV_HALO, HEAD_DIM), F32),
        pltpu.VMEM((rows_blk + 2 * CONV_HALO, HEAD_DIM), F32),
    ]
    return pl.pallas_call(
        kern,
        out_shape=out_shape,
        grid=(n_steps + 1,),
        in_specs=in_specs,
        out_specs=out_specs,
        scratch_shapes=scratch,
        compiler_params=pltpu.CompilerParams(
            dimension_semantics=("arbitrary",), vmem_limit_bytes=VMEM_LIMIT),
        name="mixer_latent" if latent else "mixer_context",
    )(*args)


def _ffn_kernel(x_ref, y_ref, mod_ref, wo_ref, g2_ref, w1_ref, w2_ref, gf_ref, out_ref):
    g1 = mod_ref[2:3, :]
    sh2 = mod_ref[3:4, :]
    sc2 = mod_ref[4:5, :]
    g2 = mod_ref[5:6, :]
    x1 = x_ref[...] + g1 * _dot(y_ref[...], wo_ref[...])
    h2 = (_rms(x1, g2_ref[...]) * (1.0 + sc2) + sh2).astype(BF16)
    f = jnp.zeros_like(x1)
    for j in range(D_FF // D_MODEL):
        cols = slice(j * D_MODEL, (j + 1) * D_MODEL)
        hid = jnp.maximum(_dot(h2, w1_ref[:, cols]), 0.0)
        f = f + _dot((hid * hid).astype(BF16), w2_ref[cols, :])
    out_ref[...] = _rms(x1 + g2 * f, gf_ref[...])


def _ffn(x2d, y2d, mod3, mod_row, w_out, norm2_g, w_ff1, w_ff2, final_g):
    n_tok = x2d.shape[0]
    const = lambda i: (0, 0)
    return pl.pallas_call(
        _ffn_kernel,
        out_shape=jax.ShapeDtypeStruct((n_tok, D_MODEL), F32),
        grid=(n_tok // FFN_ROWS,),
        in_specs=[
            pl.BlockSpec((FFN_ROWS, D_MODEL), lambda i: (i, 0)),
            pl.BlockSpec((FFN_ROWS, D_MODEL), lambda i: (i, 0)),
            pl.BlockSpec((None, 6, D_MODEL), lambda i: (mod_row(i), 0, 0)),
            pl.BlockSpec((D_MODEL, D_MODEL), const, pipeline_mode=pl.Buffered(1)),
            pl.BlockSpec((1, D_MODEL), const),
            pl.BlockSpec((D_MODEL, D_FF), const, pipeline_mode=pl.Buffered(1)),
            pl.BlockSpec((D_FF, D_MODEL), const, pipeline_mode=pl.Buffered(1)),
            pl.BlockSpec((1, D_MODEL), const),
        ],
        out_specs=pl.BlockSpec((FFN_ROWS, D_MODEL), lambda i: (i, 0)),
        compiler_params=pltpu.CompilerParams(
            dimension_semantics=("arbitrary",), vmem_limit_bytes=VMEM_LIMIT),
        name="outproj_mlp",
    )(x2d, y2d, mod3, w_out, norm2_g, w_ff1, w_ff2, final_g)


def _rope_tables(seq):
    pos = jnp.arange(seq)
    row = (pos // GRID_W).astype(F32)
    col = (pos % GRID_W).astype(F32)
    nf = HEAD_DIM // 4
    inv = ROPE_BASE ** (-jnp.arange(nf, dtype=F32) / nf)
    ang = jnp.concatenate([row[:, None] * inv, col[:, None] * inv], -1)
    cos = jnp.cos(ang)
    sin = jnp.sin(ang)
    return jnp.concatenate([cos, cos], -1), jnp.concatenate([-sin, sin], -1)


def kernel(x_prompt, x_sample, state_ret, state_mlstm_C, state_mlstm_n, state_mlstm_m, c, c_ctx,
           w_ada, b_ada, norm1_g, norm2_g, w_in, conv_w, ret_decay_logit, mlstm_gate_bias,
           ret_gn_g, mlstm_gn_g, w_out, w_ff1, w_ff2, final_g):
    assert w_ada.shape[0] == 1, "single-layer kernel"
    bp, tp, _ = x_prompt.shape
    bs, ts, _ = x_sample.shape
    assert tp % CHUNK == 0 and ts % CHUNK == 0 and bp % CTX_SEQS_PER_STEP == 0

    cond = jnp.concatenate([c_ctx[None, :], c, jnp.zeros((8 - 1 - bs, D_MODEL), F32)], 0)
    mod = _modulation(cond, w_ada[0], b_ada)
    mod3 = mod[:1 + bs].reshape(1 + bs, 6, D_MODEL)

    w_pair, w_gate = _regroup_in_proj(w_in)
    w_out_p = _regroup_out_proj(w_out)
    w1 = w_ff1[0].astype(BF16)
    w2 = w_ff2[0].astype(BF16)

    cw = conv_w[0]
    hp_rows = [cw[j, :512].reshape(N_HEADS, HEAD_DIM) for j in range(3)]
    hp_rows += [cw[j, 512:].reshape(N_HEADS, HEAD_DIM) for j in range(3)]
    hp_rows += [ret_gn_g[0].reshape(N_HEADS, HEAD_DIM), mlstm_gn_g[0].reshape(N_HEADS, HEAD_DIM)]
    hp_rows += [jnp.broadcast_to(ret_decay_logit[0, d][:, None], (N_HEADS, HEAD_DIM)) for d in range(2)]
    gate_bias = jnp.pad(mlstm_gate_bias[0].reshape(1, N_GATE_COLS), ((0, 0), (HEAD_DIM - N_GATE_COLS, 0)))
    hp_rows += [jnp.broadcast_to(gate_bias, (N_HEADS, HEAD_DIM))]
    hp_rows += [jnp.zeros((N_HEADS, HEAD_DIM), F32)] * (16 - len(hp_rows))
    head_params = jnp.stack(hp_rows, axis=1).astype(F32)

    g1 = norm1_g[0][None, :]
    g2 = norm2_g[0][None, :]
    gf = final_g[None, :]
    xp2d = x_prompt.reshape(bp * tp, D_MODEL)
    xs2d = x_sample.reshape(bs * ts, D_MODEL)

    y_p, new_ret, new_c, new_n, new_m = _mixer(
        xp2d, mod3, lambda b: 0, g1, w_pair, w_gate, head_params,
        latent=False, n_seq=CTX_SEQS_PER_STEP, cps=tp // CHUNK)
    out_p = _ffn(xp2d, y_p, mod3, lambda i: 0, w_out_p, g2, w1, w2, gf).reshape(bp, tp, D_MODEL)

    s_n = jnp.transpose(state_mlstm_n[:, 0], (0, 2, 1, 3))[:, :, :, None, :]
    s_m = jnp.broadcast_to(jnp.transpose(state_mlstm_m[:, 0], (0, 2, 1))[..., None],
                           (bs, N_HEADS, 2, HEAD_DIM))
    y_s = _mixer(xs2d, mod3, lambda b: 1 + b, g1, w_pair, w_gate, head_params,
                 latent=True, n_seq=1, cps=ts // CHUNK,
                 rope=_rope_tables(ts), states=(state_ret, state_mlstm_C, s_n, s_m))
    tiles_per_seq = ts // FFN_ROWS
    out_s = _ffn(xs2d, y_s, mod3, lambda i: 1 + i // tiles_per_seq,
                 w_out_p, g2, w1, w2, gf).reshape(bs, ts, D_MODEL)

    new_n = jnp.transpose(new_n, (0, 2, 1, 3))[:, None]
    new_m = jnp.transpose(new_m[..., 0], (0, 2, 1))[:, None]
    return out_p, out_s, new_ret, new_c, new_n, new_m
```

```python
import functools

import jax
import jax.numpy as jnp
from jax import lax
from jax.experimental import pallas as pl
from jax.experimental.pallas import tpu as pltpu

F32 = jnp.float32
BF16 = jnp.bfloat16

D_MODEL = 1024
N_HEADS = 4
HEAD_DIM = 128
CHUNK = 128
GRID_W = 64
D_FF = 4 * D_MODEL
EPS = 1e-6
ROPE_BASE = 10000.0
LOG2_E = 1.4426950408889634
LN_2 = 0.6931471805599453
PAIR_COLS = 8 * HEAD_DIM
N_GATE_COLS = 4 * N_HEADS
ROW_TILE = 256
FFN_ROWS = 512
CTX_SEQS_PER_STEP = 4
PROJ_SLAB_ROWS = 256
CONV_HALO = 8
VMEM_LIMIT = 60 * 1024 * 1024


def _dot(a, b):
    return jnp.dot(a, b, preferred_element_type=F32)


def _dot_nt(a, b):
    return lax.dot_general(a, b, (((1,), (1,)), ((), ())), preferred_element_type=F32)


def _dot_tn(a, b):
    return lax.dot_general(a, b, (((0,), (0,)), ((), ())), preferred_element_type=F32)


def _rms(x, g):
    return x * lax.rsqrt(jnp.mean(x * x, axis=-1, keepdims=True) + EPS) * g


def _group_norm(o, g):
    mu = jnp.mean(o, axis=-1, keepdims=True)
    c = o - mu
    var = jnp.mean(c * c, axis=-1, keepdims=True)
    return c * lax.rsqrt(var + EPS) * g


def _log_sigmoid(x):
    return jnp.minimum(x, 0.0) - jnp.log(1.0 + jnp.exp(-jnp.abs(x)))


def _sigmoid(x):
    return 1.0 / (1.0 + jnp.exp(-x))


def _split2(x):
    hi = x.astype(BF16)
    lo = (x - hi.astype(F32)).astype(BF16)
    return hi, lo


def _mod_kernel(cond_ref, w_ref, b_ref, out_ref):
    c = cond_ref[...]
    s = (c * _sigmoid(c)).astype(BF16)
    out_ref[...] = _dot(s, w_ref[...].astype(BF16)) + b_ref[...]


def _modulation(cond, w_ada, b_ada):
    n = w_ada.shape[1]
    tn = 1024
    return pl.pallas_call(
        _mod_kernel,
        out_shape=jax.ShapeDtypeStruct((cond.shape[0], n), F32),
        grid=(n // tn,),
        in_specs=[pl.BlockSpec(cond.shape, lambda j: (0, 0)),
                  pl.BlockSpec((D_MODEL, tn), lambda j: (0, j)),
                  pl.BlockSpec((1, tn), lambda j: (0, j))],
        out_specs=pl.BlockSpec((cond.shape[0], tn), lambda j: (0, j)),
        compiler_params=pltpu.CompilerParams(dimension_semantics=("arbitrary",)),
        name="adaln_mod",
    )(cond, w_ada, b_ada)


REGROUP_ROWS = 256


def _regroup_in_kernel(wt_ref, out_ref, gate_ref):
    for h in range(N_HEADS):
        for g in range(8):
            src = (g * N_HEADS + h) * HEAD_DIM
            dst = (h * 8 + g) * HEAD_DIM
            out_ref[:, dst:dst + HEAD_DIM] = wt_ref[src:src + HEAD_DIM, :].T.astype(BF16)
    n_rows = wt_ref.shape[0]
    n_gate = n_rows - N_HEADS * PAIR_COLS
    tail = wt_ref[n_rows - HEAD_DIM:n_rows, :].T
    lane = lax.broadcasted_iota(jnp.int32, tail.shape, 1)
    gate_ref[...] = jnp.where(lane >= HEAD_DIM - n_gate, tail, 0.0).astype(BF16)


def _regroup_in_proj(w_in):
    w_t = jnp.transpose(w_in[0])
    n_cols = w_t.shape[0]
    assert n_cols == N_HEADS * PAIR_COLS + N_GATE_COLS
    return pl.pallas_call(
        _regroup_in_kernel,
        out_shape=(jax.ShapeDtypeStruct((D_MODEL, N_HEADS * PAIR_COLS), BF16),
                   jax.ShapeDtypeStruct((D_MODEL, HEAD_DIM), BF16)),
        grid=(D_MODEL // REGROUP_ROWS,),
        in_specs=[pl.BlockSpec((n_cols, REGROUP_ROWS), lambda i: (0, i))],
        out_specs=(pl.BlockSpec((REGROUP_ROWS, N_HEADS * PAIR_COLS), lambda i: (i, 0)),
                   pl.BlockSpec((REGROUP_ROWS, HEAD_DIM), lambda i: (i, 0))),
        compiler_params=pltpu.CompilerParams(dimension_semantics=("arbitrary",)),
        name="regroup_w_in",
    )(w_t)


def _regroup_out_kernel(w_ref, out_ref):
    for h in range(N_HEADS):
        for g in range(2):
            src = (g * N_HEADS + h) * HEAD_DIM
            dst = (h * 2 + g) * HEAD_DIM
            out_ref[dst:dst + HEAD_DIM, :] = w_ref[src:src + HEAD_DIM, :].astype(BF16)


def _regroup_out_proj(w_out):
    cols = 2 * HEAD_DIM
    return pl.pallas_call(
        _regroup_out_kernel,
        out_shape=jax.ShapeDtypeStruct((D_MODEL, D_MODEL), BF16),
        grid=(D_MODEL // cols,),
        in_specs=[pl.BlockSpec((None, D_MODEL, cols), lambda j: (0, 0, j))],
        out_specs=pl.BlockSpec((D_MODEL, cols), lambda j: (0, j)),
        compiler_params=pltpu.CompilerParams(dimension_semantics=("arbitrary",)),
        name="regroup_w_out",
    )(w_out)


_B_END, _MAX_LWE, _M_PREV, _M_NEW, _DECAY = 0, 2, 4, 6, 8


def _mixer_kernel(*refs, latent, n_seq, cps, n_steps):
    L = CHUNK
    n_chunks = n_seq * cps
    rows_total = n_chunks * L
    seq_len = cps * L
    assert seq_len & (seq_len - 1) == 0
    if latent:
        (x_ref, mod_ref, g1_ref, w_ref, wg_ref, hp_ref, hp_prev_ref, cos_ref, sin_ref,
         sr_in, sc_in, sn_in, sm_in, y_ref, *scratch) = refs
    else:
        (x_ref, mod_ref, g1_ref, w_ref, wg_ref, hp_ref, hp_prev_ref,
         y_ref, so_ref, co_ref, no_ref, mo_ref, *scratch) = refs
    (hn_ref, rq_ref, rk_ref, rv_ref, rg_ref, mq_ref, mk_ref, mvx_ref, mog_ref, gt_ref,
     ar_ref, am_ref, logi_ref, bc_ref, skv_ref, ckv_ref, cs_ref, rc_ref, rawq_ref, rawk_ref) = scratch

    step = pl.program_id(0)
    live = step < n_steps
    head = lax.rem(jnp.minimum(step, n_steps - 1), N_HEADS)

    @pl.when(step == 0)
    def _():
        for ref in (rq_ref, am_ref, rg_ref, mog_ref):
            ref[...] = jnp.zeros_like(ref)

    @pl.when(jnp.logical_and(head == 0, live))
    def _():
        gain = g1_ref[...] * (1.0 + mod_ref[1:2, :])
        sh1 = mod_ref[0:1, :]

        def body(i, carry):
            r = pl.ds(pl.multiple_of(i * ROW_TILE, ROW_TILE), ROW_TILE)
            xr = x_ref[r, :]
            inv = lax.rsqrt(jnp.mean(xr * xr, axis=-1, keepdims=True) + EPS)
            hn_ref[r, :] = (xr * inv * gain + sh1).astype(BF16)
            return carry

        lax.fori_loop(0, rows_total // ROW_TILE, body, 0)

    hp = hp_ref[...]
    scale = HEAD_DIM ** -0.5

    r_i = lax.broadcasted_iota(jnp.int32, (L, L), 0)
    s_i = lax.broadcasted_iota(jnp.int32, (L, L), 1)
    r_f = r_i.astype(F32)
    s_f = s_i.astype(F32)
    lg_f = _log_sigmoid(hp[8:9])
    lg_b = _log_sigmoid(hp[9:10])
    rc_ref[0] = (jnp.where(r_i >= s_i, jnp.exp(lg_f * jnp.where(r_i >= s_i, r_f - s_f, 0.0)), 0.0)
                 + jnp.where(s_i >= r_i, jnp.exp(lg_b * jnp.where(s_i >= r_i, s_f - r_f, 0.0)), 0.0))
    rc_ref[1] = jnp.exp(lg_f * (r_f + 1.0))
    rc_ref[2] = jnp.exp(lg_b * (L - r_f))
    rc_ref[3] = jnp.exp(lg_f * (L - 1.0 - r_f))
    rc_ref[4] = jnp.exp(lg_b * r_f)
    chunk_decay = (jnp.exp(lg_f * float(L)), jnp.exp(lg_b * float(L)))

    tri = (jnp.where(r_i >= s_i, 1.0, 0.0).astype(BF16),
           jnp.where(s_i >= r_i, 1.0, 0.0).astype(BF16))

    for raw_ref in (rawq_ref, rawk_ref):
        raw_ref[0:CONV_HALO, :] = jnp.zeros((CONV_HALO, HEAD_DIM), F32)
        raw_ref[rows_total + CONV_HALO:rows_total + 2 * CONV_HALO, :] = jnp.zeros((CONV_HALO, HEAD_DIM), F32)

    def rows(c):
        return slice(c * L, (c + 1) * L)

    def gate_col(c, lane):
        g = gt_ref[rows(c), :]
        return jnp.broadcast_to(g[:, lane:lane + 1], (L, HEAD_DIM))

    hp_prev = hp_prev_ref[...]

    def finish_rows(r):
        rg = rg_ref[r, :]
        ret_y = _group_norm(rq_ref[r, :], hp_prev[6:7]) * (rg * _sigmoid(rg))
        y_ref[r, 0:128] = ret_y.astype(BF16)
        ml_y = _group_norm(am_ref[r, :], hp_prev[7:8]) * _sigmoid(mog_ref[r, :])
        y_ref[r, 128:256] = ml_y.astype(BF16)

    def project(blk, n_rows):
        hn = hn_ref[blk, :]
        pr = _dot(hn, w_ref[:, 0:256])
        q = pr[:, 0:128]
        k = pr[:, 128:256] * scale
        if latent:
            cos2 = cos_ref[blk, :]
            sin2 = sin_ref[blk, :]
            q = q * cos2 + pltpu.roll(q, HEAD_DIM // 2, axis=1) * sin2
            k = k * cos2 + pltpu.roll(k, HEAD_DIM // 2, axis=1) * sin2
        rq_ref[blk, :] = q
        rk_ref[blk, :] = k

        pv = _dot(hn, w_ref[:, 256:512])
        rv_ref[blk, :] = pv[:, 0:128].astype(BF16)
        rg_ref[blk, :] = pv[:, 128:256]

        pm = _dot(hn, w_ref[:, 512:768])
        halo_blk = slice(blk.start + CONV_HALO, blk.stop + CONV_HALO)
        rawq_ref[halo_blk, :] = pm[:, 0:128]
        rawk_ref[halo_blk, :] = pm[:, 128:256]

        po = _dot(hn, w_ref[:, 768:1024])
        mvx_ref[blk, 0:128] = po[:, 0:128].astype(BF16)
        mvx_ref[blk, 128:256] = jnp.ones((n_rows, HEAD_DIM), BF16)
        mog_ref[blk, :] = po[:, 128:256]

        gates = _dot(hn, wg_ref[...]) + hp[10:11]
        lane_t = lax.broadcasted_iota(jnp.int32, (n_rows, HEAD_DIM), 1)
        is_forget = jnp.bitwise_and(lane_t, 4) == 4
        gates = jnp.where(is_forget, _log_sigmoid(gates), gates) * LOG2_E
        gt_ref[blk, :] = pltpu.roll(gates, N_GATE_COLS - head, axis=1)

    def conv_silu(blk, n_rows):
        sub = lax.broadcasted_iota(jnp.int32, (CONV_HALO, HEAD_DIM), 0)
        starts_seq = blk.start % seq_len == 0
        ends_seq = blk.stop % seq_len == 0
        for raw_ref, dst_ref, taps, post in ((rawq_ref, mq_ref, hp[0:3], 1.0), (rawk_ref, mk_ref, hp[3:6], scale)):
            lo = blk.start + CONV_HALO
            prev = raw_ref[lo - 1:lo - 1 + n_rows, :]
            nxt = raw_ref[lo + 1:lo + 1 + n_rows, :]
            if starts_seq:
                first = jnp.where(sub == 0, 0.0, prev[0:CONV_HALO, :])
                prev = jnp.concatenate([first, prev[CONV_HALO:, :]], axis=0)
            if ends_seq:
                last = jnp.where(sub == CONV_HALO - 1, 0.0, nxt[n_rows - CONV_HALO:, :])
                nxt = jnp.concatenate([nxt[:n_rows - CONV_HALO, :], last], axis=0)
            out = prev * taps[0:1] + raw_ref[lo:lo + n_rows, :] * taps[1:2] + nxt * taps[2:3]
            out = out * _sigmoid(out)
            dst_ref[blk, :] = out if post == 1.0 else out * post

    def project_and_scan():
        slab_rows = min(rows_total, PROJ_SLAB_ROWS)
        assert seq_len % slab_rows == 0
        slabs = [slice(s0, s0 + slab_rows) for s0 in range(0, rows_total, slab_rows)]
        for i, slab in enumerate(slabs):
            finish_rows(slab)
            project(slab, slab_rows)
            if i > 0:
                conv_silu(slabs[i - 1], slab_rows)
        conv_silu(slabs[-1], slab_rows)
        pre_pass(range(n_chunks))
        scan_passes(range(n_chunks), range(n_seq))

    def pre_pass(chunks):
        for c in chunks:
            ar_ref[rows(c), :] = (_dot_nt(rq_ref[rows(c), :].astype(BF16), rk_ref[rows(c), :].astype(BF16))
                                  * rc_ref[0]).astype(BF16)
            am_ref[rows(c), :] = _dot_nt(mq_ref[rows(c), :].astype(BF16), mk_ref[rows(c), :].astype(BF16))
        for c in chunks:
            for d in range(2):
                li = gate_col(c, 8 * d)
                lf = gate_col(c, 8 * d + 4)
                strict = (r_i > s_i) if d == 0 else (r_i < s_i)
                x_mat = jnp.where(strict, lf, jnp.where(r_i == s_i, li, 0.0))
                hi, lo = _split2(jnp.concatenate([x_mat, lf], axis=1))
                d_ext = _dot(tri[d], hi) + _dot(tri[d], lo)
                bcum = d_ext[:, 128:256]
                causal = (s_i <= r_i) if d == 0 else (s_i >= r_i)
                logi_ref[d, rows(c), :] = jnp.where(causal, d_ext[:, 0:128], -jnp.inf)
                bc_ref[d, rows(c), :] = bcum
                b_end = bcum[L - 1:L, :] if d == 0 else bcum[0:1, :]
                cs_ref[_B_END + d, c:c + 1, :] = b_end
                cs_ref[_MAX_LWE + d, c:c + 1, :] = jnp.max(b_end - bcum + li, axis=0, keepdims=True)
        for c in chunks:
            kf = rk_ref[rows(c), :]
            vb = rv_ref[rows(c), :]
            for d in range(2):
                skv_ref[d, c] = _dot_tn((kf * rc_ref[3 + d]).astype(BF16), vb)

    def scan_passes(chunks, seqs):
        m_final = {}
        for s in seqs:
            for d in range(2):
                m = sm_in[d:d + 1, :] * LOG2_E if latent else jnp.zeros((1, HEAD_DIM), F32)
                order = range(cps) if d == 0 else range(cps - 1, -1, -1)
                for j in order:
                    c = s * cps + j
                    cs_ref[_M_PREV + d, c:c + 1, :] = m
                    b_end = cs_ref[_B_END + d, c:c + 1, :]
                    m_new = jnp.maximum(b_end + m, cs_ref[_MAX_LWE + d, c:c + 1, :])
                    cs_ref[_M_NEW + d, c:c + 1, :] = m_new
                    cs_ref[_DECAY + d, c:c + 1, :] = jnp.exp2(b_end + m - m_new)
                    m = m_new
                m_final[(s, d)] = m

        for c in chunks:
            kf = mk_ref[rows(c), :]
            vx = mvx_ref[rows(c), :]
            for d in range(2):
                log_w_end = cs_ref[_B_END + d, c:c + 1, :] - bc_ref[d, rows(c), :] + gate_col(c, 8 * d)
                w_end = jnp.exp2(log_w_end - cs_ref[_M_NEW + d, c:c + 1, :])
                ckv_ref[d, c] = _dot_tn((kf * w_end).astype(BF16), vx)

        for s in seqs:
            for d in range(2):
                if latent:
                    s_state = sr_in[d]
                    n_rep = jnp.broadcast_to(sn_in[d], (HEAD_DIM, HEAD_DIM)).T
                    c_state = jnp.concatenate([sc_in[d], n_rep], axis=1)
                else:
                    s_state = jnp.zeros((HEAD_DIM, HEAD_DIM), F32)
                    c_state = jnp.zeros((HEAD_DIM, 2 * HEAD_DIM), F32)
                order = range(cps) if d == 0 else range(cps - 1, -1, -1)
                for j in order:
                    c = s * cps + j
                    inc = skv_ref[d, c]
                    skv_ref[d, c] = s_state
                    s_state = s_state * chunk_decay[d] + inc
                    inc = ckv_ref[d, c]
                    ckv_ref[d, c] = c_state
                    decay = cs_ref[_DECAY + d, c:c + 1, :]
                    c_state = c_state * jnp.concatenate([decay, decay], axis=1) + inc
                if not latent:
                    so_ref[s, d] = s_state
                    co_ref[s, d] = c_state[:, 0:128]
                    no_ref[s, d:d + 1, :] = c_state[:, 128:256].T[0:1, :]
                    mo_ref[s, d:d + 1, :] = m_final[(s, d)] * LN_2

        for c in chunks:
            qf = rq_ref[rows(c), :]
            lhs = jnp.concatenate([ar_ref[rows(c), :], (qf * rc_ref[1]).astype(BF16),
                                   (qf * rc_ref[2]).astype(BF16)], axis=1)
            rhs = jnp.concatenate([rv_ref[rows(c), :], skv_ref[0, c].astype(BF16),
                                   skv_ref[1, c].astype(BF16)], axis=0)
            rq_ref[rows(c), :] = _dot(lhs, rhs)
            qf = mq_ref[rows(c), :]
            a_mat = am_ref[rows(c), :]
            vx = mvx_ref[rows(c), :]
            h_sum = None
            for d in range(2):
                log_intra = logi_ref[d, rows(c), :]
                log_inter = bc_ref[d, rows(c), :] + cs_ref[_M_PREV + d, c:c + 1, :]
                m_t = jnp.maximum(log_inter, jnp.max(log_intra, axis=1, keepdims=True))
                w_inter = jnp.exp2(log_inter - m_t)
                w_intra = jnp.exp2(log_intra - m_t)
                lhs = jnp.concatenate([(a_mat * w_intra).astype(BF16), (qf * w_inter).astype(BF16)], axis=1)
                rhs = jnp.concatenate([vx, ckv_ref[d, c].astype(BF16)], axis=0)
                res = _dot(lhs, rhs)
                h_dir = res[:, 0:128] / jnp.maximum(jnp.abs(res[:, 128:256]), jnp.exp2(-m_t))
                h_sum = h_dir if h_sum is None else h_sum + h_dir
            am_ref[rows(c), :] = h_sum

    pl.when(live)(project_and_scan)

    @pl.when(step == n_steps)
    def _():
        def body(i, carry):
            finish_rows(pl.ds(pl.multiple_of(i * ROW_TILE, ROW_TILE), ROW_TILE))
            return carry

        lax.fori_loop(0, rows_total // ROW_TILE, body, 0)


def _mixer(x2d, mod3, mod_row, norm1_g, w_pair, w_gate, head_params, *, latent, n_seq, cps,
           rope=None, states=None):
    n_tok = x2d.shape[0]
    rows_blk = n_seq * cps * CHUNK
    n_blk = n_tok // rows_blk
    n_chunks = n_seq * cps
    n_steps = n_blk * N_HEADS
    kern = functools.partial(_mixer_kernel, latent=latent, n_seq=n_seq, cps=cps, n_steps=n_steps)

    def cur(f):
        def index_map(j):
            item = jnp.minimum(j, n_steps - 1)
            return f(item // N_HEADS, item % N_HEADS)
        return index_map

    def prev(f):
        def index_map(j):
            item = jnp.maximum(j - 1, 0)
            return f(item // N_HEADS, item % N_HEADS)
        return index_map

    once = pl.Buffered(1)
    in_specs = [
        pl.BlockSpec((rows_blk, D_MODEL), cur(lambda b, h: (b, 0))),
        pl.BlockSpec((None, 6, D_MODEL), cur(lambda b, h: (mod_row(b), 0, 0))),
        pl.BlockSpec((1, D_MODEL), lambda j: (0, 0)),
        pl.BlockSpec((D_MODEL, PAIR_COLS), cur(lambda b, h: (0, h))),
        pl.BlockSpec((D_MODEL, HEAD_DIM), lambda j: (0, 0)),
        pl.BlockSpec((None, 16, HEAD_DIM), cur(lambda b, h: (h, 0, 0))),
        pl.BlockSpec((None, 16, HEAD_DIM), prev(lambda b, h: (h, 0, 0))),
    ]
    args = [x2d, mod3, norm1_g, w_pair, w_gate, head_params, head_params]
    y_shape = jax.ShapeDtypeStruct((n_tok, D_MODEL), BF16)
    y_spec = pl.BlockSpec((rows_blk, 2 * HEAD_DIM), prev(lambda b, h: (b, h)))
    if latent:
        assert n_seq == 1
        cos2, sin2 = rope
        s_ret, s_c, s_n, s_m = states
        in_specs += [
            pl.BlockSpec((rows_blk, HEAD_DIM), lambda j: (0, 0), pipeline_mode=once),
            pl.BlockSpec((rows_blk, HEAD_DIM), lambda j: (0, 0), pipeline_mode=once),
            pl.BlockSpec((None, None, 2, None, HEAD_DIM, HEAD_DIM), cur(lambda b, h: (b, 0, 0, h, 0, 0))),
            pl.BlockSpec((None, None, 2, None, HEAD_DIM, HEAD_DIM), cur(lambda b, h: (b, 0, 0, h, 0, 0))),
            pl.BlockSpec((None, None, 2, 1, HEAD_DIM), cur(lambda b, h: (b, h, 0, 0, 0))),
            pl.BlockSpec((None, None, 2, HEAD_DIM), cur(lambda b, h: (b, h, 0, 0))),
        ]
        args += [cos2, sin2, s_ret, s_c, s_n, s_m]
        out_shape = y_shape
        out_specs = y_spec
    else:
        bsz = n_blk * n_seq
        st = jax.ShapeDtypeStruct((bsz, 1, 2, N_HEADS, HEAD_DIM, HEAD_DIM), F32)
        vec = jax.ShapeDtypeStruct((bsz, N_HEADS, 2, HEAD_DIM), F32)
        st_spec = pl.BlockSpec((n_seq, None, 2, None, HEAD_DIM, HEAD_DIM), cur(lambda b, h: (b, 0, 0, h, 0, 0)))
        vec_spec = pl.BlockSpec((n_seq, None, 2, HEAD_DIM), cur(lambda b, h: (b, h, 0, 0)))
        out_shape = (y_shape, st, st, vec, vec)
        out_specs = (y_spec, st_spec, st_spec, vec_spec, vec_spec)
    col = lambda dt: pltpu.VMEM((rows_blk, HEAD_DIM), dt)
    scratch = [
        pltpu.VMEM((rows_blk, D_MODEL), BF16),
        col(F32), col(F32), col(BF16), col(F32),
        col(F32), col(F32),
        pltpu.VMEM((rows_blk, 2 * HEAD_DIM), BF16),
        col(F32), col(F32),
        col(BF16), col(F32),
        pltpu.VMEM((2, rows_blk, HEAD_DIM), F32),
        pltpu.VMEM((2, rows_blk, HEAD_DIM), F32),
        pltpu.VMEM((2, n_chunks, HEAD_DIM, HEAD_DIM), F32),
        pltpu.VMEM((2, n_chunks, HEAD_DIM, 2 * HEAD_DIM), F32),
        pltpu.VMEM((10, max(n_chunks, 8), HEAD_DIM), F32),
        pltpu.VMEM((5, CHUNK, CHUNK), F32),
        pltpu.VMEM((rows_blk + 2 * CONV_HALO, HEAD_DIM), F32),
        pltpu.VMEM((rows_blk + 2 * CONV_HALO, HEAD_DIM), F32),
    ]
    return pl.pallas_call(
        kern,
        out_shape=out_shape,
        grid=(n_steps + 1,),
        in_specs=in_specs,
        out_specs=out_specs,
        scratch_shapes=scratch,
        compiler_params=pltpu.CompilerParams(
            dimension_semantics=("arbitrary",), vmem_limit_bytes=VMEM_LIMIT),
        name="mixer_latent" if latent else "mixer_context",
    )(*args)


def _ffn_kernel(x_ref, y_ref, mod_ref, wo_ref, g2_ref, w1_ref, w2_ref, gf_ref, out_ref):
    g1 = mod_ref[2:3, :]
    sh2 = mod_ref[3:4, :]
    sc2 = mod_ref[4:5, :]
    g2 = mod_ref[5:6, :]
    x1 = x_ref[...] + g1 * _dot(y_ref[...], wo_ref[...])
    h2 = (_rms(x1, g2_ref[...]) * (1.0 + sc2) + sh2).astype(BF16)
    f = jnp.zeros_like(x1)
    for j in range(D_FF // D_MODEL):
        cols = slice(j * D_MODEL, (j + 1) * D_MODEL)
        hid = jnp.maximum(_dot(h2, w1_ref[:, cols]), 0.0)
        f = f + _dot((hid * hid).astype(BF16), w2_ref[cols, :])
    out_ref[...] = _rms(x1 + g2 * f, gf_ref[...])


def _ffn(x2d, y2d, mod3, mod_row, w_out, norm2_g, w_ff1, w_ff2, final_g):
    n_tok = x2d.shape[0]
    const = lambda i: (0, 0)
    return pl.pallas_call(
        _ffn_kernel,
        out_shape=jax.ShapeDtypeStruct((n_tok, D_MODEL), F32),
        grid=(n_tok // FFN_ROWS,),
        in_specs=[
            pl.BlockSpec((FFN_ROWS, D_MODEL), lambda i: (i, 0)),
            pl.BlockSpec((FFN_ROWS, D_MODEL), lambda i: (i, 0)),
            pl.BlockSpec((None, 6, D_MODEL), lambda i: (mod_row(i), 0, 0)),
            pl.BlockSpec((D_MODEL, D_MODEL), const, pipeline_mode=pl.Buffered(1)),
            pl.BlockSpec((1, D_MODEL), const),
            pl.BlockSpec((D_MODEL, D_FF), const, pipeline_mode=pl.Buffered(1)),
            pl.BlockSpec((D_FF, D_MODEL), const, pipeline_mode=pl.Buffered(1)),
            pl.BlockSpec((1, D_MODEL), const),
        ],
        out_specs=pl.BlockSpec((FFN_ROWS, D_MODEL), lambda i: (i, 0)),
        compiler_params=pltpu.CompilerParams(
            dimension_semantics=("arbitrary",), vmem_limit_bytes=VMEM_LIMIT),
        name="outproj_mlp",
    )(x2d, y2d, mod3, w_out, norm2_g, w_ff1, w_ff2, final_g)


def _rope_tables(seq):
    pos = jnp.arange(seq)
    row = (pos // GRID_W).astype(F32)
    col = (pos % GRID_W).astype(F32)
    nf = HEAD_DIM // 4
    inv = ROPE_BASE ** (-jnp.arange(nf, dtype=F32) / nf)
    ang = jnp.concatenate([row[:, None] * inv, col[:, None] * inv], -1)
    cos = jnp.cos(ang)
    sin = jnp.sin(ang)
    return jnp.concatenate([cos, cos], -1), jnp.concatenate([-sin, sin], -1)


def kernel(x_prompt, x_sample, state_ret, state_mlstm_C, state_mlstm_n, state_mlstm_m, c, c_ctx,
           w_ada, b_ada, norm1_g, norm2_g, w_in, conv_w, ret_decay_logit, mlstm_gate_bias,
           ret_gn_g, mlstm_gn_g, w_out, w_ff1, w_ff2, final_g):
    assert w_ada.shape[0] == 1, "single-layer kernel"
    bp, tp, _ = x_prompt.shape
    bs, ts, _ = x_sample.shape
    assert tp % CHUNK == 0 and ts % CHUNK == 0 and bp % CTX_SEQS_PER_STEP == 0

    cond = jnp.concatenate([c_ctx[None, :], c, jnp.zeros((8 - 1 - bs, D_MODEL), F32)], 0)
    mod = _modulation(cond, w_ada[0], b_ada)
    mod3 = mod[:1 + bs].reshape(1 + bs, 6, D_MODEL)

    w_pair, w_gate = _regroup_in_proj(w_in)
    w_out_p = _regroup_out_proj(w_out)
    w1 = w_ff1[0].astype(BF16)
    w2 = w_ff2[0].astype(BF16)

    cw = conv_w[0]
    hp_rows = [cw[j, :512].reshape(N_HEADS, HEAD_DIM) for j in range(3)]
    hp_rows += [cw[j, 512:].reshape(N_HEADS, HEAD_DIM) for j in range(3)]
    hp_rows += [ret_gn_g[0].reshape(N_HEADS, HEAD_DIM), mlstm_gn_g[0].reshape(N_HEADS, HEAD_DIM)]
    hp_rows += [jnp.broadcast_to(ret_decay_logit[0, d][:, None], (N_HEADS, HEAD_DIM)) for d in range(2)]
    gate_bias = jnp.pad(mlstm_gate_bias[0].reshape(1, N_GATE_COLS), ((0, 0), (HEAD_DIM - N_GATE_COLS, 0)))
    hp_rows += [jnp.broadcast_to(gate_bias, (N_HEADS, HEAD_DIM))]
    hp_rows += [jnp.zeros((N_HEADS, HEAD_DIM), F32)] * (16 - len(hp_rows))
    head_params = jnp.stack(hp_rows, axis=1).astype(F32)

    g1 = norm1_g[0][None, :]
    g2 = norm2_g[0][None, :]
    gf = final_g[None, :]
    xp2d = x_prompt.reshape(bp * tp, D_MODEL)
    xs2d = x_sample.reshape(bs * ts, D_MODEL)

    y_p, new_ret, new_c, new_n, new_m = _mixer(
        xp2d, mod3, lambda b: 0, g1, w_pair, w_gate, head_params,
        latent=False, n_seq=CTX_SEQS_PER_STEP, cps=tp // CHUNK)
    out_p = _ffn(xp2d, y_p, mod3, lambda i: 0, w_out_p, g2, w1, w2, gf).reshape(bp, tp, D_MODEL)

    s_n = jnp.transpose(state_mlstm_n[:, 0], (0, 2, 1, 3))[:, :, :, None, :]
    s_m = jnp.broadcast_to(jnp.transpose(state_mlstm_m[:, 0], (0, 2, 1))[..., None],
                           (bs, N_HEADS, 2, HEAD_DIM))
    y_s = _mixer(xs2d, mod3, lambda b: 1 + b, g1, w_pair, w_gate, head_params,
                 latent=True, n_seq=1, cps=ts // CHUNK,
                 rope=_rope_tables(ts), states=(state_ret, state_mlstm_C, s_n, s_m))
    tiles_per_seq = ts // FFN_ROWS
    out_s = _ffn(xs2d, y_s, mod3, lambda i: 1 + i // tiles_per_seq,
                 w_out_p, g2, w1, w2, gf).reshape(bs, ts, D_MODEL)

    new_n = jnp.transpose(new_n, (0, 2, 1, 3))[:, None]
    new_m = jnp.transpose(new_m[..., 0], (0, 2, 1))[:, None]
    return out_p, out_s, new_ret, new_c, new_n, new_m
```

```python
import functools

import numpy as np
import jax
import jax.numpy as jnp
from jax import lax
from jax.experimental import pallas as pl
from jax.experimental.pallas import tpu as pltpu

F32 = jnp.float32
BF16 = jnp.bfloat16

D_MODEL = 1024
N_HEADS = 4
HEAD_DIM = 128
CHUNK = 128
GRID_W = 64
D_FF = 4 * D_MODEL
EPS = 1e-6
ROPE_BASE = 10000.0
LOG2_E = 1.4426950408889634
LN_2 = 0.6931471805599453
PAIR_COLS = 8 * HEAD_DIM
N_GATE_COLS = 4 * N_HEADS
ROW_TILE = 256
FFN_ROWS = 512
CTX_SEQS_PER_STEP = 4
PROJ_SLAB_ROWS = 256
CONV_HALO = 8
VMEM_LIMIT = 60 * 1024 * 1024


def _dot(a, b):
    return jnp.dot(a, b, preferred_element_type=F32)


def _dot_nt(a, b):
    return lax.dot_general(a, b, (((1,), (1,)), ((), ())), preferred_element_type=F32)


def _dot_tn(a, b):
    return lax.dot_general(a, b, (((0,), (0,)), ((), ())), preferred_element_type=F32)


def _rms(x, g):
    return x * lax.rsqrt(jnp.mean(x * x, axis=-1, keepdims=True) + EPS) * g


def _group_norm(o, g):
    mu = jnp.mean(o, axis=-1, keepdims=True)
    c = o - mu
    var = jnp.mean(c * c, axis=-1, keepdims=True)
    return c * lax.rsqrt(var + EPS) * g


def _log_sigmoid(x):
    return jnp.minimum(x, 0.0) - jnp.log(1.0 + jnp.exp(-jnp.abs(x)))


def _sigmoid(x):
    return 1.0 / (1.0 + jnp.exp(-x))


def _split2(x):
    hi = x.astype(BF16)
    lo = (x - hi.astype(F32)).astype(BF16)
    return hi, lo


def _mod_kernel(cond_ref, w_ref, b_ref, out_ref):
    c = cond_ref[...]
    s = (c * _sigmoid(c)).astype(BF16)
    out_ref[...] = _dot(s, w_ref[...].astype(BF16)) + b_ref[...]


def _modulation(cond, w_ada, b_ada):
    n = w_ada.shape[1]
    tn = 1024
    return pl.pallas_call(
        _mod_kernel,
        out_shape=jax.ShapeDtypeStruct((cond.shape[0], n), F32),
        grid=(n // tn,),
        in_specs=[pl.BlockSpec(cond.shape, lambda j: (0, 0)),
                  pl.BlockSpec((D_MODEL, tn), lambda j: (0, j)),
                  pl.BlockSpec((1, tn), lambda j: (0, j))],
        out_specs=pl.BlockSpec((cond.shape[0], tn), lambda j: (0, j)),
        compiler_params=pltpu.CompilerParams(dimension_semantics=("arbitrary",)),
        name="adaln_mod",
    )(cond, w_ada, b_ada)


REGROUP_ROWS = 256


def _regroup_in_kernel(wt_ref, out_ref, gate_ref):
    for h in range(N_HEADS):
        for g in range(8):
            src = (g * N_HEADS + h) * HEAD_DIM
            dst = (h * 8 + g) * HEAD_DIM
            out_ref[:, dst:dst + HEAD_DIM] = wt_ref[src:src + HEAD_DIM, :].T.astype(BF16)
    n_rows = wt_ref.shape[0]
    n_gate = n_rows - N_HEADS * PAIR_COLS
    tail = wt_ref[n_rows - HEAD_DIM:n_rows, :].T
    lane = lax.broadcasted_iota(jnp.int32, tail.shape, 1)
    gate_ref[...] = jnp.where(lane >= HEAD_DIM - n_gate, tail, 0.0).astype(BF16)


def _regroup_in_proj(w_in):
    w_t = jnp.transpose(w_in[0])
    n_cols = w_t.shape[0]
    assert n_cols == N_HEADS * PAIR_COLS + N_GATE_COLS
    return pl.pallas_call(
        _regroup_in_kernel,
        out_shape=(jax.ShapeDtypeStruct((D_MODEL, N_HEADS * PAIR_COLS), BF16),
                   jax.ShapeDtypeStruct((D_MODEL, HEAD_DIM), BF16)),
        grid=(D_MODEL // REGROUP_ROWS,),
        in_specs=[pl.BlockSpec((n_cols, REGROUP_ROWS), lambda i: (0, i))],
        out_specs=(pl.BlockSpec((REGROUP_ROWS, N_HEADS * PAIR_COLS), lambda i: (i, 0)),
                   pl.BlockSpec((REGROUP_ROWS, HEAD_DIM), lambda i: (i, 0))),
        compiler_params=pltpu.CompilerParams(dimension_semantics=("arbitrary",)),
        name="regroup_w_in",
    )(w_t)


def _regroup_out_kernel(w_ref, out_ref):
    for h in range(N_HEADS):
        for g in range(2):
            src = (g * N_HEADS + h) * HEAD_DIM
            dst = (h * 2 + g) * HEAD_DIM
            out_ref[dst:dst + HEAD_DIM, :] = w_ref[src:src + HEAD_DIM, :].astype(BF16)


def _regroup_out_proj(w_out):
    cols = 2 * HEAD_DIM
    return pl.pallas_call(
        _regroup_out_kernel,
        out_shape=jax.ShapeDtypeStruct((D_MODEL, D_MODEL), BF16),
        grid=(D_MODEL // cols,),
        in_specs=[pl.BlockSpec((None, D_MODEL, cols), lambda j: (0, 0, j))],
        out_specs=pl.BlockSpec((D_MODEL, cols), lambda j: (0, j)),
        compiler_params=pltpu.CompilerParams(dimension_semantics=("arbitrary",)),
        name="regroup_w_out",
    )(w_out)


_B_END, _MAX_LWE, _M_PREV, _M_NEW, _DECAY = 0, 2, 4, 6, 8


def _mixer_kernel(*refs, latent, n_seq, cps, n_steps):
    L = CHUNK
    n_chunks = n_seq * cps
    rows_total = n_chunks * L
    seq_len = cps * L
    assert seq_len & (seq_len - 1) == 0
    if latent:
        (x_ref, mod_ref, g1_ref, w_ref, wg_ref, hp_ref, hp_prev_ref, cos_ref, sin_ref,
         sr_in, sc_in, sn_in, sm_in, y_ref, *scratch) = refs
    else:
        (x_ref, mod_ref, g1_ref, w_ref, wg_ref, hp_ref, hp_prev_ref,
         y_ref, so_ref, co_ref, no_ref, mo_ref, *scratch) = refs
    (hn_ref, rq_ref, rk_ref, rv_ref, rg_ref, mq_ref, mk_ref, mvx_ref, mog_ref, gt_ref,
     ar_ref, am_ref, logi_ref, bc_ref, skv_ref, ckv_ref, cs_ref, rc_ref, rawq_ref, rawk_ref) = scratch

    step = pl.program_id(0)
    live = step < n_steps
    head = lax.rem(jnp.minimum(step, n_steps - 1), N_HEADS)

    @pl.when(step == 0)
    def _():
        for ref in (rq_ref, am_ref, rg_ref, mog_ref):
            ref[...] = jnp.zeros_like(ref)

    @pl.when(jnp.logical_and(head == 0, live))
    def _():
        gain = g1_ref[...] * (1.0 + mod_ref[1:2, :])
        sh1 = mod_ref[0:1, :]

        def body(i, carry):
            r = pl.ds(pl.multiple_of(i * ROW_TILE, ROW_TILE), ROW_TILE)
            xr = x_ref[r, :]
            inv = lax.rsqrt(jnp.mean(xr * xr, axis=-1, keepdims=True) + EPS)
            hn_ref[r, :] = (xr * inv * gain + sh1).astype(BF16)
            return carry

        lax.fori_loop(0, rows_total // ROW_TILE, body, 0)

    hp = hp_ref[...]
    scale = HEAD_DIM ** -0.5

    r_i = lax.broadcasted_iota(jnp.int32, (L, L), 0)
    s_i = lax.broadcasted_iota(jnp.int32, (L, L), 1)
    r_f = r_i.astype(F32)
    s_f = s_i.astype(F32)
    lg_f = _log_sigmoid(hp[8:9])
    lg_b = _log_sigmoid(hp[9:10])
    rc_ref[0] = (jnp.where(r_i >= s_i, jnp.exp(lg_f * jnp.where(r_i >= s_i, r_f - s_f, 0.0)), 0.0)
                 + jnp.where(s_i >= r_i, jnp.exp(lg_b * jnp.where(s_i >= r_i, s_f - r_f, 0.0)), 0.0))
    rc_ref[1] = jnp.exp(lg_f * (r_f + 1.0))
    rc_ref[2] = jnp.exp(lg_b * (L - r_f))
    rc_ref[3] = jnp.exp(lg_f * (L - 1.0 - r_f))
    rc_ref[4] = jnp.exp(lg_b * r_f)
    chunk_decay = (jnp.exp(lg_f * float(L)), jnp.exp(lg_b * float(L)))

    tri = (jnp.where(r_i >= s_i, 1.0, 0.0).astype(BF16),
           jnp.where(s_i >= r_i, 1.0, 0.0).astype(BF16))

    for raw_ref in (rawq_ref, rawk_ref):
        raw_ref[0:CONV_HALO, :] = jnp.zeros((CONV_HALO, HEAD_DIM), F32)
        raw_ref[rows_total + CONV_HALO:rows_total + 2 * CONV_HALO, :] = jnp.zeros((CONV_HALO, HEAD_DIM), F32)

    def rows(c):
        return slice(c * L, (c + 1) * L)

    def gate_col(c, lane):
        g = gt_ref[rows(c), :]
        return jnp.broadcast_to(g[:, lane:lane + 1], (L, HEAD_DIM))

    hp_prev = hp_prev_ref[...]

    def finish_rows(r):
        rg = rg_ref[r, :]
        ret_y = _group_norm(rq_ref[r, :], hp_prev[6:7]) * (rg * _sigmoid(rg))
        y_ref[r, 0:128] = ret_y.astype(BF16)
        ml_y = _group_norm(am_ref[r, :], hp_prev[7:8]) * _sigmoid(mog_ref[r, :])
        y_ref[r, 128:256] = ml_y.astype(BF16)

    def project(blk, n_rows):
        hn = hn_ref[blk, :]
        pr = _dot(hn, w_ref[:, 0:256])
        q = pr[:, 0:128]
        k = pr[:, 128:256] * scale
        if latent:
            cos2 = cos_ref[blk, :]
            sin2 = sin_ref[blk, :]
            q = q * cos2 + pltpu.roll(q, HEAD_DIM // 2, axis=1) * sin2
            k = k * cos2 + pltpu.roll(k, HEAD_DIM // 2, axis=1) * sin2
        rq_ref[blk, :] = q
        rk_ref[blk, :] = k

        pv = _dot(hn, w_ref[:, 256:512])
        rv_ref[blk, :] = pv[:, 0:128].astype(BF16)
        rg_ref[blk, :] = pv[:, 128:256]

        pm = _dot(hn, w_ref[:, 512:768])
        halo_blk = slice(blk.start + CONV_HALO, blk.stop + CONV_HALO)
        rawq_ref[halo_blk, :] = pm[:, 0:128]
        rawk_ref[halo_blk, :] = pm[:, 128:256]

        po = _dot(hn, w_ref[:, 768:1024])
        mvx_ref[blk, 0:128] = po[:, 0:128].astype(BF16)
        mvx_ref[blk, 128:256] = jnp.ones((n_rows, HEAD_DIM), BF16)
        mog_ref[blk, :] = po[:, 128:256]

        gates = _dot(hn, wg_ref[...]) + hp[10:11]
        lane_t = lax.broadcasted_iota(jnp.int32, (n_rows, HEAD_DIM), 1)
        is_forget = jnp.bitwise_and(lane_t, 4) == 4
        gates = jnp.where(is_forget, _log_sigmoid(gates), gates) * LOG2_E
        gt_ref[blk, :] = pltpu.roll(gates, N_GATE_COLS - head, axis=1)

    def conv_silu(blk, n_rows):
        sub = lax.broadcasted_iota(jnp.int32, (CONV_HALO, HEAD_DIM), 0)
        starts_seq = blk.start % seq_len == 0
        ends_seq = blk.stop % seq_len == 0
        for raw_ref, dst_ref, taps, post in ((rawq_ref, mq_ref, hp[0:3], 1.0), (rawk_ref, mk_ref, hp[3:6], scale)):
            lo = blk.start + CONV_HALO
            prev = raw_ref[lo - 1:lo - 1 + n_rows, :]
            nxt = raw_ref[lo + 1:lo + 1 + n_rows, :]
            if starts_seq:
                first = jnp.where(sub == 0, 0.0, prev[0:CONV_HALO, :])
                prev = jnp.concatenate([first, prev[CONV_HALO:, :]], axis=0)
            if ends_seq:
                last = jnp.where(sub == CONV_HALO - 1, 0.0, nxt[n_rows - CONV_HALO:, :])
                nxt = jnp.concatenate([nxt[:n_rows - CONV_HALO, :], last], axis=0)
            out = prev * taps[0:1] + raw_ref[lo:lo + n_rows, :] * taps[1:2] + nxt * taps[2:3]
            out = out * _sigmoid(out)
            dst_ref[blk, :] = out if post == 1.0 else out * post

    def project_and_scan():
        slab_rows = min(seq_len, PROJ_SLAB_ROWS)
        assert seq_len % slab_rows == 0
        slabs = [slice(s0, s0 + slab_rows) for s0 in range(0, rows_total, slab_rows)]
        for i, slab in enumerate(slabs):
            finish_rows(slab)
            project(slab, slab_rows)
            if i > 0:
                conv_silu(slabs[i - 1], slab_rows)
        conv_silu(slabs[-1], slab_rows)
        pre_pass(range(n_chunks))
        scan_passes(range(n_chunks), range(n_seq))

    def pre_pass(chunks):
        for c in chunks:
            ar_ref[rows(c), :] = (_dot_nt(rq_ref[rows(c), :].astype(BF16), rk_ref[rows(c), :].astype(BF16))
                                  * rc_ref[0]).astype(BF16)
            am_ref[rows(c), :] = _dot_nt(mq_ref[rows(c), :].astype(BF16), mk_ref[rows(c), :].astype(BF16))
        for c in chunks:
            for d in range(2):
                li = gate_col(c, 8 * d)
                lf = gate_col(c, 8 * d + 4)
                strict = (r_i > s_i) if d == 0 else (r_i < s_i)
                x_mat = jnp.where(strict, lf, jnp.where(r_i == s_i, li, 0.0))
                hi, lo = _split2(jnp.concatenate([x_mat, lf], axis=1))
                d_ext = _dot(tri[d], hi) + _dot(tri[d], lo)
                bcum = d_ext[:, 128:256]
                causal = (s_i <= r_i) if d == 0 else (s_i >= r_i)
                logi_ref[d, rows(c), :] = jnp.where(causal, d_ext[:, 0:128], -jnp.inf)
                bc_ref[d, rows(c), :] = bcum
                b_end = bcum[L - 1:L, :] if d == 0 else bcum[0:1, :]
                cs_ref[_B_END + d, c:c + 1, :] = b_end
                cs_ref[_MAX_LWE + d, c:c + 1, :] = jnp.max(b_end - bcum + li, axis=0, keepdims=True)
        for c in chunks:
            kf = rk_ref[rows(c), :]
            vb = rv_ref[rows(c), :]
            for d in range(2):
                skv_ref[d, c] = _dot_tn((kf * rc_ref[3 + d]).astype(BF16), vb)

    def scan_passes(chunks, seqs):
        m_final = {}
        for s in seqs:
            for d in range(2):
                m = sm_in[d:d + 1, :] * LOG2_E if latent else jnp.zeros((1, HEAD_DIM), F32)
                order = range(cps) if d == 0 else range(cps - 1, -1, -1)
                for j in order:
                    c = s * cps + j
                    cs_ref[_M_PREV + d, c:c + 1, :] = m
                    b_end = cs_ref[_B_END + d, c:c + 1, :]
                    m_new = jnp.maximum(b_end + m, cs_ref[_MAX_LWE + d, c:c + 1, :])
                    cs_ref[_M_NEW + d, c:c + 1, :] = m_new
                    cs_ref[_DECAY + d, c:c + 1, :] = jnp.exp2(b_end + m - m_new)
                    m = m_new
                m_final[(s, d)] = m

        for c in chunks:
            kf = mk_ref[rows(c), :]
            vx = mvx_ref[rows(c), :]
            for d in range(2):
                log_w_end = cs_ref[_B_END + d, c:c + 1, :] - bc_ref[d, rows(c), :] + gate_col(c, 8 * d)
                w_end = jnp.exp2(log_w_end - cs_ref[_M_NEW + d, c:c + 1, :])
                ckv_ref[d, c] = _dot_tn((kf * w_end).astype(BF16), vx)

        for s in seqs:
            for d in range(2):
                if latent:
                    s_state = sr_in[d]
                    n_rep = jnp.broadcast_to(sn_in[d], (HEAD_DIM, HEAD_DIM)).T
                    c_state = jnp.concatenate([sc_in[d], n_rep], axis=1)
                else:
                    s_state = jnp.zeros((HEAD_DIM, HEAD_DIM), F32)
                    c_state = jnp.zeros((HEAD_DIM, 2 * HEAD_DIM), F32)
                order = range(cps) if d == 0 else range(cps - 1, -1, -1)
                for j in order:
                    c = s * cps + j
                    inc = skv_ref[d, c]
                    skv_ref[d, c] = s_state
                    s_state = s_state * chunk_decay[d] + inc
                    inc = ckv_ref[d, c]
                    ckv_ref[d, c] = c_state
                    decay = cs_ref[_DECAY + d, c:c + 1, :]
                    c_state = c_state * jnp.concatenate([decay, decay], axis=1) + inc
                if not latent:
                    so_ref[s, d] = s_state
                    co_ref[s, d] = c_state[:, 0:128]
                    no_ref[s, d:d + 1, :] = c_state[:, 128:256].T[0:1, :]
                    mo_ref[s, d:d + 1, :] = m_final[(s, d)] * LN_2

        for c in chunks:
            qf = rq_ref[rows(c), :]
            lhs = jnp.concatenate([ar_ref[rows(c), :], (qf * rc_ref[1]).astype(BF16),
                                   (qf * rc_ref[2]).astype(BF16)], axis=1)
            rhs = jnp.concatenate([rv_ref[rows(c), :], skv_ref[0, c].astype(BF16),
                                   skv_ref[1, c].astype(BF16)], axis=0)
            rq_ref[rows(c), :] = _dot(lhs, rhs)
            qf = mq_ref[rows(c), :]
            a_mat = am_ref[rows(c), :]
            vx = mvx_ref[rows(c), :]
            h_sum = None
            for d in range(2):
                log_intra = logi_ref[d, rows(c), :]
                log_inter = bc_ref[d, rows(c), :] + cs_ref[_M_PREV + d, c:c + 1, :]
                m_t = jnp.maximum(log_inter, jnp.max(log_intra, axis=1, keepdims=True))
                w_inter = jnp.exp2(log_inter - m_t)
                w_intra = jnp.exp2(log_intra - m_t)
                lhs = jnp.concatenate([(a_mat * w_intra).astype(BF16), (qf * w_inter).astype(BF16)], axis=1)
                rhs = jnp.concatenate([vx, ckv_ref[d, c].astype(BF16)], axis=0)
                res = _dot(lhs, rhs)
                h_dir = res[:, 0:128] / jnp.maximum(jnp.abs(res[:, 128:256]), jnp.exp2(-m_t))
                h_sum = h_dir if h_sum is None else h_sum + h_dir
            am_ref[rows(c), :] = h_sum

    pl.when(live)(project_and_scan)

    @pl.when(step == n_steps)
    def _():
        def body(i, carry):
            finish_rows(pl.ds(pl.multiple_of(i * ROW_TILE, ROW_TILE), ROW_TILE))
            return carry

        lax.fori_loop(0, rows_total // ROW_TILE, body, 0)


def _mixer(x2d, mod3, mod_row, norm1_g, w_pair, w_gate, head_params, *, latent, n_seq, cps,
           rope=None, states=None):
    n_tok = x2d.shape[0]
    rows_blk = n_seq * cps * CHUNK
    n_blk = n_tok // rows_blk
    n_chunks = n_seq * cps
    n_steps = n_blk * N_HEADS
    kern = functools.partial(_mixer_kernel, latent=latent, n_seq=n_seq, cps=cps, n_steps=n_steps)

    def cur(f):
        def index_map(j):
            item = jnp.minimum(j, n_steps - 1)
            return f(item // N_HEADS, item % N_HEADS)
        return index_map

    def prev(f):
        def index_map(j):
            item = jnp.maximum(j - 1, 0)
            return f(item // N_HEADS, item % N_HEADS)
        return index_map

    once = pl.Buffered(1)
    in_specs = [
        pl.BlockSpec((rows_blk, D_MODEL), cur(lambda b, h: (b, 0))),
        pl.BlockSpec((None, 6, D_MODEL), cur(lambda b, h: (mod_row(b), 0, 0))),
        pl.BlockSpec((1, D_MODEL), lambda j: (0, 0)),
        pl.BlockSpec((D_MODEL, PAIR_COLS), cur(lambda b, h: (0, h))),
        pl.BlockSpec((D_MODEL, HEAD_DIM), lambda j: (0, 0)),
        pl.BlockSpec((None, 16, HEAD_DIM), cur(lambda b, h: (h, 0, 0))),
        pl.BlockSpec((None, 16, HEAD_DIM), prev(lambda b, h: (h, 0, 0))),
    ]
    args = [x2d, mod3, norm1_g, w_pair, w_gate, head_params, head_params]
    y_shape = jax.ShapeDtypeStruct((n_tok, D_MODEL), BF16)
    y_spec = pl.BlockSpec((rows_blk, 2 * HEAD_DIM), prev(lambda b, h: (b, h)))
    if latent:
        assert n_seq == 1
        cos2, sin2 = rope
        s_ret, s_c, s_n, s_m = states
        in_specs += [
            pl.BlockSpec((rows_blk, HEAD_DIM), lambda j: (0, 0), pipeline_mode=once),
            pl.BlockSpec((rows_blk, HEAD_DIM), lambda j: (0, 0), pipeline_mode=once),
            pl.BlockSpec((None, None, 2, None, HEAD_DIM, HEAD_DIM), cur(lambda b, h: (b, 0, 0, h, 0, 0))),
            pl.BlockSpec((None, None, 2, None, HEAD_DIM, HEAD_DIM), cur(lambda b, h: (b, 0, 0, h, 0, 0))),
            pl.BlockSpec((None, None, 2, 1, HEAD_DIM), cur(lambda b, h: (b, h, 0, 0, 0))),
            pl.BlockSpec((None, None, 2, HEAD_DIM), cur(lambda b, h: (b, h, 0, 0))),
        ]
        args += [cos2, sin2, s_ret, s_c, s_n, s_m]
        out_shape = y_shape
        out_specs = y_spec
    else:
        bsz = n_blk * n_seq
        st = jax.ShapeDtypeStruct((bsz, 1, 2, N_HEADS, HEAD_DIM, HEAD_DIM), F32)
        vec = jax.ShapeDtypeStruct((bsz, N_HEADS, 2, HEAD_DIM), F32)
        st_spec = pl.BlockSpec((n_seq, None, 2, None, HEAD_DIM, HEAD_DIM), cur(lambda b, h: (b, 0, 0, h, 0, 0)))
        vec_spec = pl.BlockSpec((n_seq, None, 2, HEAD_DIM), cur(lambda b, h: (b, h, 0, 0)))
        out_shape = (y_shape, st, st, vec, vec)
        out_specs = (y_spec, st_spec, st_spec, vec_spec, vec_spec)
    col = lambda dt: pltpu.VMEM((rows_blk, HEAD_DIM), dt)
    scratch = [
        pltpu.VMEM((rows_blk, D_MODEL), BF16),
        col(F32), col(F32), col(BF16), col(F32),
        col(F32), col(F32),
        pltpu.VMEM((rows_blk, 2 * HEAD_DIM), BF16),
        col(F32), col(F32),
        col(BF16), col(F32),
        pltpu.VMEM((2, rows_blk, HEAD_DIM), F32),
        pltpu.VMEM((2, rows_blk, HEAD_DIM), F32),
        pltpu.VMEM((2, n_chunks, HEAD_DIM, HEAD_DIM), F32),
        pltpu.VMEM((2, n_chunks, HEAD_DIM, 2 * HEAD_DIM), F32),
        pltpu.VMEM((10, max(n_chunks, 8), HEAD_DIM), F32),
        pltpu.VMEM((5, CHUNK, CHUNK), F32),
        pltpu.VMEM((rows_blk + 2 * CONV_HALO, HEAD_DIM), F32),
        pltpu.VMEM((rows_blk + 2 * CONV_HALO, HEAD_DIM), F32),
    ]
    return pl.pallas_call(
        kern,
        out_shape=out_shape,
        grid=(n_steps + 1,),
        in_specs=in_specs,
        out_specs=out_specs,
        scratch_shapes=scratch,
        compiler_params=pltpu.CompilerParams(
            dimension_semantics=("arbitrary",), vmem_limit_bytes=VMEM_LIMIT),
        name="mixer_latent" if latent else "mixer_context",
    )(*args)


def _ffn_kernel(x_ref, y_ref, mod_ref, wo_ref, g2_ref, w1_ref, w2_ref, gf_ref, out_ref):
    g1 = mod_ref[2:3, :]
    sh2 = mod_ref[3:4, :]
    sc2 = mod_ref[4:5, :]
    g2 = mod_ref[5:6, :]
    x1 = x_ref[...] + g1 * _dot(y_ref[...], wo_ref[...])
    h2 = (_rms(x1, g2_ref[...]) * (1.0 + sc2) + sh2).astype(BF16)
    f = jnp.zeros_like(x1)
    for j in range(D_FF // D_MODEL):
        cols = slice(j * D_MODEL, (j + 1) * D_MODEL)
        hid = jnp.maximum(_dot(h2, w1_ref[:, cols]), 0.0)
        f = f + _dot((hid * hid).astype(BF16), w2_ref[cols, :])
    out_ref[...] = _rms(x1 + g2 * f, gf_ref[...])


def _ffn(x2d, y2d, mod3, mod_row, w_out, norm2_g, w_ff1, w_ff2, final_g):
    n_tok = x2d.shape[0]
    const = lambda i: (0, 0)
    return pl.pallas_call(
        _ffn_kernel,
        out_shape=jax.ShapeDtypeStruct((n_tok, D_MODEL), F32),
        grid=(n_tok // FFN_ROWS,),
        in_specs=[
            pl.BlockSpec((FFN_ROWS, D_MODEL), lambda i: (i, 0)),
            pl.BlockSpec((FFN_ROWS, D_MODEL), lambda i: (i, 0)),
            pl.BlockSpec((None, 6, D_MODEL), lambda i: (mod_row(i), 0, 0)),
            pl.BlockSpec((D_MODEL, D_MODEL), const, pipeline_mode=pl.Buffered(1)),
            pl.BlockSpec((1, D_MODEL), const),
            pl.BlockSpec((D_MODEL, D_FF), const, pipeline_mode=pl.Buffered(1)),
            pl.BlockSpec((D_FF, D_MODEL), const, pipeline_mode=pl.Buffered(1)),
            pl.BlockSpec((1, D_MODEL), const),
        ],
        out_specs=pl.BlockSpec((FFN_ROWS, D_MODEL), lambda i: (i, 0)),
        compiler_params=pltpu.CompilerParams(
            dimension_semantics=("arbitrary",), vmem_limit_bytes=VMEM_LIMIT),
        name="outproj_mlp",
    )(x2d, y2d, mod3, w_out, norm2_g, w_ff1, w_ff2, final_g)


def _rope_tables(seq):
    pos = np.arange(seq)
    row = (pos // GRID_W).astype(np.float64)
    col = (pos % GRID_W).astype(np.float64)
    nf = HEAD_DIM // 4
    inv = ROPE_BASE ** (-np.arange(nf, dtype=np.float64) / nf)
    ang = np.concatenate([row[:, None] * inv, col[:, None] * inv], -1)
    cos = np.cos(ang)
    sin = np.sin(ang)
    cos2 = np.concatenate([cos, cos], -1).astype(np.float32)
    sin2 = np.concatenate([-sin, sin], -1).astype(np.float32)
    return jnp.asarray(cos2), jnp.asarray(sin2)


def kernel(x_prompt, x_sample, state_ret, state_mlstm_C, state_mlstm_n, state_mlstm_m, c, c_ctx,
           w_ada, b_ada, norm1_g, norm2_g, w_in, conv_w, ret_decay_logit, mlstm_gate_bias,
           ret_gn_g, mlstm_gn_g, w_out, w_ff1, w_ff2, final_g):
    assert w_ada.shape[0] == 1, "single-layer kernel"
    bp, tp, _ = x_prompt.shape
    bs, ts, _ = x_sample.shape
    assert tp % CHUNK == 0 and ts % CHUNK == 0 and bp % CTX_SEQS_PER_STEP == 0

    cond = jnp.concatenate([c_ctx[None, :], c, jnp.zeros((8 - 1 - bs, D_MODEL), F32)], 0)
    mod = _modulation(cond, w_ada[0], b_ada)
    mod3 = mod[:1 + bs].reshape(1 + bs, 6, D_MODEL)

    w_pair, w_gate = _regroup_in_proj(w_in)
    w_out_p = _regroup_out_proj(w_out)
    w1 = w_ff1[0].astype(BF16)
    w2 = w_ff2[0].astype(BF16)

    cw = conv_w[0]
    hp_rows = [cw[j, :512].reshape(N_HEADS, HEAD_DIM) for j in range(3)]
    hp_rows += [cw[j, 512:].reshape(N_HEADS, HEAD_DIM) for j in range(3)]
    hp_rows += [ret_gn_g[0].reshape(N_HEADS, HEAD_DIM), mlstm_gn_g[0].reshape(N_HEADS, HEAD_DIM)]
    hp_rows += [jnp.broadcast_to(ret_decay_logit[0, d][:, None], (N_HEADS, HEAD_DIM)) for d in range(2)]
    gate_bias = jnp.pad(mlstm_gate_bias[0].reshape(1, N_GATE_COLS), ((0, 0), (HEAD_DIM - N_GATE_COLS, 0)))
    hp_rows += [jnp.broadcast_to(gate_bias, (N_HEADS, HEAD_DIM))]
    hp_rows += [jnp.zeros((N_HEADS, HEAD_DIM), F32)] * (16 - len(hp_rows))
    head_params = jnp.stack(hp_rows, axis=1).astype(F32)

    g1 = norm1_g[0][None, :]
    g2 = norm2_g[0][None, :]
    gf = final_g[None, :]
    xp2d = x_prompt.reshape(bp * tp, D_MODEL)
    xs2d = x_sample.reshape(bs * ts, D_MODEL)

    y_p, new_ret, new_c, new_n, new_m = _mixer(
        xp2d, mod3, lambda b: 0, g1, w_pair, w_gate, head_params,
        latent=False, n_seq=CTX_SEQS_PER_STEP, cps=tp // CHUNK)
    out_p = _ffn(xp2d, y_p, mod3, lambda i: 0, w_out_p, g2, w1, w2, gf).reshape(bp, tp, D_MODEL)

    s_n = jnp.transpose(state_mlstm_n[:, 0], (0, 2, 1, 3))[:, :, :, None, :]
    s_m = jnp.broadcast_to(jnp.transpose(state_mlstm_m[:, 0], (0, 2, 1))[..., None],
                           (bs, N_HEADS, 2, HEAD_DIM))
    y_s = _mixer(xs2d, mod3, lambda b: 1 + b, g1, w_pair, w_gate, head_params,
                 latent=True, n_seq=1, cps=ts // CHUNK,
                 rope=_rope_tables(ts), states=(state_ret, state_mlstm_C, s_n, s_m))
    tiles_per_seq = ts // FFN_ROWS
    out_s = _ffn(xs2d, y_s, mod3, lambda i: 1 + i // tiles_per_seq,
                 w_out_p, g2, w1, w2, gf).reshape(bs, ts, D_MODEL)

    new_n = jnp.transpose(new_n, (0, 2, 1, 3))[:, None]
    new_m = jnp.transpose(new_m[..., 0], (0, 2, 1))[:, None]
    return out_p, out_s, new_ret, new_c, new_n, new_m
```

```python
import functools

import numpy as np
import jax
import jax.numpy as jnp
from jax import lax
from jax.experimental import pallas as pl
from jax.experimental.pallas import tpu as pltpu

F32 = jnp.float32
BF16 = jnp.bfloat16

D_MODEL = 1024
N_HEADS = 4
HEAD_DIM = 128
CHUNK = 128
GRID_W = 64
D_FF = 4 * D_MODEL
EPS = 1e-6
ROPE_BASE = 10000.0
LOG2_E = 1.4426950408889634
LN_2 = 0.6931471805599453
PAIR_COLS = 8 * HEAD_DIM
N_GATE_COLS = 4 * N_HEADS
ROW_TILE = 256
FFN_ROWS = 512
CTX_SEQS_PER_STEP = 4
PROJ_SLAB_ROWS = 256
CONV_HALO = 8
VMEM_LIMIT = 60 * 1024 * 1024


def _dot(a, b):
    return jnp.dot(a, b, preferred_element_type=F32)


def _dot_nt(a, b):
    return lax.dot_general(a, b, (((1,), (1,)), ((), ())), preferred_element_type=F32)


def _dot_tn(a, b):
    return lax.dot_general(a, b, (((0,), (0,)), ((), ())), preferred_element_type=F32)


def _rms(x, g):
    return x * lax.rsqrt(jnp.mean(x * x, axis=-1, keepdims=True) + EPS) * g


def _group_norm(o, g):
    mu = jnp.mean(o, axis=-1, keepdims=True)
    c = o - mu
    var = jnp.mean(c * c, axis=-1, keepdims=True)
    return c * lax.rsqrt(var + EPS) * g


def _log_sigmoid(x):
    return jnp.minimum(x, 0.0) - jnp.log(1.0 + jnp.exp(-jnp.abs(x)))


def _sigmoid(x):
    return 1.0 / (1.0 + jnp.exp(-x))


def _split2(x):
    hi = x.astype(BF16)
    lo = (x - hi.astype(F32)).astype(BF16)
    return hi, lo


def _mod_kernel(cond_ref, w_ref, b_ref, out_ref):
    c = cond_ref[...]
    s = (c * _sigmoid(c)).astype(BF16)
    out_ref[...] = _dot(s, w_ref[...].astype(BF16)) + b_ref[...]


def _modulation(cond, w_ada, b_ada):
    n = w_ada.shape[1]
    tn = 2048
    return pl.pallas_call(
        _mod_kernel,
        out_shape=jax.ShapeDtypeStruct((cond.shape[0], n), F32),
        grid=(n // tn,),
        in_specs=[pl.BlockSpec(cond.shape, lambda j: (0, 0)),
                  pl.BlockSpec((D_MODEL, tn), lambda j: (0, j)),
                  pl.BlockSpec((1, tn), lambda j: (0, j))],
        out_specs=pl.BlockSpec((cond.shape[0], tn), lambda j: (0, j)),
        compiler_params=pltpu.CompilerParams(dimension_semantics=("arbitrary",)),
        name="adaln_mod",
    )(cond, w_ada, b_ada)


REGROUP_ROWS = 256


def _regroup_in_kernel(wt_ref, out_ref, gate_ref):
    for h in range(N_HEADS):
        for g in range(8):
            src = (g * N_HEADS + h) * HEAD_DIM
            dst = (h * 8 + g) * HEAD_DIM
            out_ref[:, dst:dst + HEAD_DIM] = wt_ref[src:src + HEAD_DIM, :].T.astype(BF16)
    n_rows = wt_ref.shape[0]
    n_gate = n_rows - N_HEADS * PAIR_COLS
    tail = wt_ref[n_rows - HEAD_DIM:n_rows, :].T
    lane = lax.broadcasted_iota(jnp.int32, tail.shape, 1)
    gate_ref[...] = jnp.where(lane >= HEAD_DIM - n_gate, tail, 0.0).astype(BF16)


def _regroup_in_proj(w_in):
    w_t = jnp.transpose(w_in[0])
    n_cols = w_t.shape[0]
    assert n_cols == N_HEADS * PAIR_COLS + N_GATE_COLS
    return pl.pallas_call(
        _regroup_in_kernel,
        out_shape=(jax.ShapeDtypeStruct((D_MODEL, N_HEADS * PAIR_COLS), BF16),
                   jax.ShapeDtypeStruct((D_MODEL, HEAD_DIM), BF16)),
        grid=(D_MODEL // REGROUP_ROWS,),
        in_specs=[pl.BlockSpec((n_cols, REGROUP_ROWS), lambda i: (0, i))],
        out_specs=(pl.BlockSpec((REGROUP_ROWS, N_HEADS * PAIR_COLS), lambda i: (i, 0)),
                   pl.BlockSpec((REGROUP_ROWS, HEAD_DIM), lambda i: (i, 0))),
        compiler_params=pltpu.CompilerParams(dimension_semantics=("arbitrary",)),
        name="regroup_w_in",
    )(w_t)


def _regroup_out_kernel(w_ref, out_ref):
    for h in range(N_HEADS):
        for g in range(2):
            src = (g * N_HEADS + h) * HEAD_DIM
            dst = (h * 2 + g) * HEAD_DIM
            out_ref[dst:dst + HEAD_DIM, :] = w_ref[src:src + HEAD_DIM, :].astype(BF16)


def _regroup_out_proj(w_out):
    cols = 2 * HEAD_DIM
    return pl.pallas_call(
        _regroup_out_kernel,
        out_shape=jax.ShapeDtypeStruct((D_MODEL, D_MODEL), BF16),
        grid=(D_MODEL // cols,),
        in_specs=[pl.BlockSpec((None, D_MODEL, cols), lambda j: (0, 0, j))],
        out_specs=pl.BlockSpec((D_MODEL, cols), lambda j: (0, j)),
        compiler_params=pltpu.CompilerParams(dimension_semantics=("arbitrary",)),
        name="regroup_w_out",
    )(w_out)


def _cast_pair_kernel(a_ref, b_ref, oa_ref, ob_ref):
    oa_ref[...] = a_ref[...].astype(BF16)
    ob_ref[...] = b_ref[...].astype(BF16)


def _cast_ffn_weights(w_ff1, w_ff2):
    n_slabs = 8
    cols = D_FF // n_slabs
    return pl.pallas_call(
        _cast_pair_kernel,
        out_shape=(jax.ShapeDtypeStruct((D_MODEL, D_FF), BF16), jax.ShapeDtypeStruct((D_FF, D_MODEL), BF16)),
        grid=(n_slabs,),
        in_specs=[pl.BlockSpec((None, D_MODEL, cols), lambda i: (0, 0, i)),
                  pl.BlockSpec((None, cols, D_MODEL), lambda i: (0, i, 0))],
        out_specs=(pl.BlockSpec((D_MODEL, cols), lambda i: (0, i)),
                   pl.BlockSpec((cols, D_MODEL), lambda i: (i, 0))),
        compiler_params=pltpu.CompilerParams(dimension_semantics=("arbitrary",)),
        name="cast_ffn_weights",
    )(w_ff1, w_ff2)


_B_END, _MAX_LWE, _M_PREV, _M_NEW, _DECAY = 0, 2, 4, 6, 8


def _mixer_kernel(*refs, latent, n_seq, cps, n_steps):
    L = CHUNK
    n_chunks = n_seq * cps
    rows_total = n_chunks * L
    seq_len = cps * L
    assert seq_len & (seq_len - 1) == 0
    if latent:
        (x_ref, mod_ref, g1_ref, w_ref, wg_ref, hp_ref, hp_prev_ref, cos_ref, sin_ref,
         sr_in, sc_in, sn_in, sm_in, y_ref, *scratch) = refs
    else:
        (x_ref, mod_ref, g1_ref, w_ref, wg_ref, hp_ref, hp_prev_ref,
         y_ref, so_ref, co_ref, no_ref, mo_ref, *scratch) = refs
    (hn_ref, rq_ref, rk_ref, rv_ref, rg_ref, mq_ref, mk_ref, mvx_ref, mog_ref, gt_ref,
     ar_ref, am_ref, logi_ref, bc_ref, skv_ref, ckv_ref, cs_ref, rc_ref, rawq_ref, rawk_ref) = scratch

    step = pl.program_id(0)
    live = step < n_steps
    head = lax.rem(jnp.minimum(step, n_steps - 1), N_HEADS)

    @pl.when(step == 0)
    def _():
        for ref in (rq_ref, am_ref, rg_ref, mog_ref):
            ref[...] = jnp.zeros_like(ref)

    @pl.when(jnp.logical_and(head == 0, live))
    def _():
        gain = g1_ref[...] * (1.0 + mod_ref[1:2, :])
        sh1 = mod_ref[0:1, :]

        def body(i, carry):
            r = pl.ds(pl.multiple_of(i * ROW_TILE, ROW_TILE), ROW_TILE)
            xr = x_ref[r, :]
            inv = lax.rsqrt(jnp.mean(xr * xr, axis=-1, keepdims=True) + EPS)
            hn_ref[r, :] = (xr * inv * gain + sh1).astype(BF16)
            return carry

        lax.fori_loop(0, rows_total // ROW_TILE, body, 0)

    hp = hp_ref[...]
    scale = HEAD_DIM ** -0.5

    r_i = lax.broadcasted_iota(jnp.int32, (L, L), 0)
    s_i = lax.broadcasted_iota(jnp.int32, (L, L), 1)
    r_f = r_i.astype(F32)
    s_f = s_i.astype(F32)
    lg_f = _log_sigmoid(hp[8:9])
    lg_b = _log_sigmoid(hp[9:10])
    rc_ref[0] = (jnp.where(r_i >= s_i, jnp.exp(lg_f * jnp.where(r_i >= s_i, r_f - s_f, 0.0)), 0.0)
                 + jnp.where(s_i >= r_i, jnp.exp(lg_b * jnp.where(s_i >= r_i, s_f - r_f, 0.0)), 0.0))
    rc_ref[1] = jnp.exp(lg_f * (r_f + 1.0))
    rc_ref[2] = jnp.exp(lg_b * (L - r_f))
    rc_ref[3] = jnp.exp(lg_f * (L - 1.0 - r_f))
    rc_ref[4] = jnp.exp(lg_b * r_f)
    chunk_decay = (jnp.exp(lg_f * float(L)), jnp.exp(lg_b * float(L)))

    tri = (jnp.where(r_i >= s_i, 1.0, 0.0).astype(BF16),
           jnp.where(s_i >= r_i, 1.0, 0.0).astype(BF16))

    for raw_ref in (rawq_ref, rawk_ref):
        raw_ref[0:CONV_HALO, :] = jnp.zeros((CONV_HALO, HEAD_DIM), F32)
        raw_ref[rows_total + CONV_HALO:rows_total + 2 * CONV_HALO, :] = jnp.zeros((CONV_HALO, HEAD_DIM), F32)

    def rows(c):
        return slice(c * L, (c + 1) * L)

    def gate_col(c, lane):
        g = gt_ref[rows(c), :]
        return jnp.broadcast_to(g[:, lane:lane + 1], (L, HEAD_DIM))

    hp_prev = hp_prev_ref[...]

    def finish_rows(r):
        rg = rg_ref[r, :]
        ret_y = _group_norm(rq_ref[r, :], hp_prev[6:7]) * (rg * _sigmoid(rg))
        y_ref[r, 0:128] = ret_y.astype(BF16)
        ml_y = _group_norm(am_ref[r, :], hp_prev[7:8]) * _sigmoid(mog_ref[r, :])
        y_ref[r, 128:256] = ml_y.astype(BF16)

    def project(blk, n_rows):
        hn = hn_ref[blk, :]
        pr = _dot(hn, w_ref[:, 0:256])
        q = pr[:, 0:128]
        k = pr[:, 128:256] * scale
        if latent:
            cos2 = cos_ref[blk, :]
            sin2 = sin_ref[blk, :]
            q = q * cos2 + pltpu.roll(q, HEAD_DIM // 2, axis=1) * sin2
            k = k * cos2 + pltpu.roll(k, HEAD_DIM // 2, axis=1) * sin2
        rq_ref[blk, :] = q
        rk_ref[blk, :] = k

        pv = _dot(hn, w_ref[:, 256:512])
        rv_ref[blk, :] = pv[:, 0:128].astype(BF16)
        rg_ref[blk, :] = pv[:, 128:256]

        pm = _dot(hn, w_ref[:, 512:768])
        halo_blk = slice(blk.start + CONV_HALO, blk.stop + CONV_HALO)
        rawq_ref[halo_blk, :] = pm[:, 0:128]
        rawk_ref[halo_blk, :] = pm[:, 128:256]

        po = _dot(hn, w_ref[:, 768:1024])
        mvx_ref[blk, 0:128] = po[:, 0:128].astype(BF16)
        mvx_ref[blk, 128:256] = jnp.ones((n_rows, HEAD_DIM), BF16)
        mog_ref[blk, :] = po[:, 128:256]

        gates = _dot(hn, wg_ref[...]) + hp[10:11]
        lane_t = lax.broadcasted_iota(jnp.int32, (n_rows, HEAD_DIM), 1)
        is_forget = jnp.bitwise_and(lane_t, 4) == 4
        gates = jnp.where(is_forget, _log_sigmoid(gates), gates) * LOG2_E
        gt_ref[blk, :] = pltpu.roll(gates, N_GATE_COLS - head, axis=1)

    def conv_silu(blk, n_rows):
        sub = lax.broadcasted_iota(jnp.int32, (CONV_HALO, HEAD_DIM), 0)
        starts_seq = blk.start % seq_len == 0
        ends_seq = blk.stop % seq_len == 0
        for raw_ref, dst_ref, taps, post in ((rawq_ref, mq_ref, hp[0:3], 1.0), (rawk_ref, mk_ref, hp[3:6], scale)):
            lo = blk.start + CONV_HALO
            prev = raw_ref[lo - 1:lo - 1 + n_rows, :]
            nxt = raw_ref[lo + 1:lo + 1 + n_rows, :]
            if starts_seq:
                first = jnp.where(sub == 0, 0.0, prev[0:CONV_HALO, :])
                prev = jnp.concatenate([first, prev[CONV_HALO:, :]], axis=0)
            if ends_seq:
                last = jnp.where(sub == CONV_HALO - 1, 0.0, nxt[n_rows - CONV_HALO:, :])
                nxt = jnp.concatenate([nxt[:n_rows - CONV_HALO, :], last], axis=0)
            out = prev * taps[0:1] + raw_ref[lo:lo + n_rows, :] * taps[1:2] + nxt * taps[2:3]
            out = out * _sigmoid(out)
            dst_ref[blk, :] = out if post == 1.0 else out * post

    def project_and_scan():
        slab_rows = min(seq_len, PROJ_SLAB_ROWS)
        assert seq_len % slab_rows == 0
        slabs = [slice(s0, s0 + slab_rows) for s0 in range(0, rows_total, slab_rows)]
        for i, slab in enumerate(slabs):
            finish_rows(slab)
            project(slab, slab_rows)
            if i > 0:
                conv_silu(slabs[i - 1], slab_rows)
        conv_silu(slabs[-1], slab_rows)
        pre_pass(range(n_chunks))
        scan_passes(range(n_chunks), range(n_seq))

    def pre_pass(chunks):
        for c in chunks:
            ar_ref[rows(c), :] = (_dot_nt(rq_ref[rows(c), :].astype(BF16), rk_ref[rows(c), :].astype(BF16))
                                  * rc_ref[0]).astype(BF16)
            am_ref[rows(c), :] = _dot_nt(mq_ref[rows(c), :].astype(BF16), mk_ref[rows(c), :].astype(BF16))
        for c in chunks:
            for d in range(2):
                li = gate_col(c, 8 * d)
                lf = gate_col(c, 8 * d + 4)
                strict = (r_i > s_i) if d == 0 else (r_i < s_i)
                x_mat = jnp.where(strict, lf, jnp.where(r_i == s_i, li, 0.0))
                hi, lo = _split2(jnp.concatenate([x_mat, lf], axis=1))
                d_ext = _dot(tri[d], hi) + _dot(tri[d], lo)
                bcum = d_ext[:, 128:256]
                causal = (s_i <= r_i) if d == 0 else (s_i >= r_i)
                logi_ref[d, rows(c), :] = jnp.where(causal, d_ext[:, 0:128], -jnp.inf)
                bc_ref[d, rows(c), :] = bcum
                b_end = bcum[L - 1:L, :] if d == 0 else bcum[0:1, :]
                cs_ref[_B_END + d, c:c + 1, :] = b_end
                cs_ref[_MAX_LWE + d, c:c + 1, :] = jnp.max(b_end - bcum + li, axis=0, keepdims=True)
        for c in chunks:
            kf = rk_ref[rows(c), :]
            vb = rv_ref[rows(c), :]
            for d in range(2):
                skv_ref[d, c] = _dot_tn((kf * rc_ref[3 + d]).astype(BF16), vb)

    def scan_passes(chunks, seqs):
        m_final = {}
        for s in seqs:
            for d in range(2):
                m = sm_in[d:d + 1, :] * LOG2_E if latent else jnp.zeros((1, HEAD_DIM), F32)
                order = range(cps) if d == 0 else range(cps - 1, -1, -1)
                for j in order:
                    c = s * cps + j
                    cs_ref[_M_PREV + d, c:c + 1, :] = m
                    b_end = cs_ref[_B_END + d, c:c + 1, :]
                    m_new = jnp.maximum(b_end + m, cs_ref[_MAX_LWE + d, c:c + 1, :])
                    cs_ref[_M_NEW + d, c:c + 1, :] = m_new
                    cs_ref[_DECAY + d, c:c + 1, :] = jnp.exp2(b_end + m - m_new)
                    m = m_new
                m_final[(s, d)] = m

        for c in chunks:
            kf = mk_ref[rows(c), :]
            vx = mvx_ref[rows(c), :]
            for d in range(2):
                log_w_end = cs_ref[_B_END + d, c:c + 1, :] - bc_ref[d, rows(c), :] + gate_col(c, 8 * d)
                w_end = jnp.exp2(log_w_end - cs_ref[_M_NEW + d, c:c + 1, :])
                ckv_ref[d, c] = _dot_tn((kf * w_end).astype(BF16), vx)

        for s in seqs:
            for d in range(2):
                if latent:
                    s_state = sr_in[d]
                    n_rep = jnp.broadcast_to(sn_in[d], (HEAD_DIM, HEAD_DIM)).T
                    c_state = jnp.concatenate([sc_in[d], n_rep], axis=1)
                else:
                    s_state = jnp.zeros((HEAD_DIM, HEAD_DIM), F32)
                    c_state = jnp.zeros((HEAD_DIM, 2 * HEAD_DIM), F32)
                order = range(cps) if d == 0 else range(cps - 1, -1, -1)
                for j in order:
                    c = s * cps + j
                    inc = skv_ref[d, c]
                    skv_ref[d, c] = s_state
                    s_state = s_state * chunk_decay[d] + inc
                    inc = ckv_ref[d, c]
                    ckv_ref[d, c] = c_state
                    decay = cs_ref[_DECAY + d, c:c + 1, :]
                    c_state = c_state * jnp.concatenate([decay, decay], axis=1) + inc
                if not latent:
                    so_ref[s, d] = s_state
                    co_ref[s, d] = c_state[:, 0:128]
                    no_ref[s, d:d + 1, :] = c_state[:, 128:256].T[0:1, :]
                    mo_ref[s, d:d + 1, :] = m_final[(s, d)] * LN_2

        for c in chunks:
            qf = rq_ref[rows(c), :]
            lhs = jnp.concatenate([ar_ref[rows(c), :], (qf * rc_ref[1]).astype(BF16),
                                   (qf * rc_ref[2]).astype(BF16)], axis=1)
            rhs = jnp.concatenate([rv_ref[rows(c), :], skv_ref[0, c].astype(BF16),
                                   skv_ref[1, c].astype(BF16)], axis=0)
            rq_ref[rows(c), :] = _dot(lhs, rhs)
            qf = mq_ref[rows(c), :]
            a_mat = am_ref[rows(c), :]
            vx = mvx_ref[rows(c), :]
            h_sum = None
            for d in range(2):
                log_intra = logi_ref[d, rows(c), :]
                log_inter = bc_ref[d, rows(c), :] + cs_ref[_M_PREV + d, c:c + 1, :]
                m_t = jnp.maximum(log_inter, jnp.max(log_intra, axis=1, keepdims=True))
                w_inter = jnp.exp2(log_inter - m_t)
                w_intra = jnp.exp2(log_intra - m_t)
                lhs = jnp.concatenate([(a_mat * w_intra).astype(BF16), (qf * w_inter).astype(BF16)], axis=1)
                rhs = jnp.concatenate([vx, ckv_ref[d, c].astype(BF16)], axis=0)
                res = _dot(lhs, rhs)
                h_dir = res[:, 0:128] / jnp.maximum(jnp.abs(res[:, 128:256]), jnp.exp2(-m_t))
                h_sum = h_dir if h_sum is None else h_sum + h_dir
            am_ref[rows(c), :] = h_sum

    pl.when(live)(project_and_scan)

    @pl.when(step == n_steps)
    def _():
        def body(i, carry):
            finish_rows(pl.ds(pl.multiple_of(i * ROW_TILE, ROW_TILE), ROW_TILE))
            return carry

        lax.fori_loop(0, rows_total // ROW_TILE, body, 0)


def _mixer(x2d, mod3, mod_row, norm1_g, w_pair, w_gate, head_params, *, latent, n_seq, cps,
           rope=None, states=None):
    n_tok = x2d.shape[0]
    rows_blk = n_seq * cps * CHUNK
    n_blk = n_tok // rows_blk
    n_chunks = n_seq * cps
    n_steps = n_blk * N_HEADS
    kern = functools.partial(_mixer_kernel, latent=latent, n_seq=n_seq, cps=cps, n_steps=n_steps)

    def cur(f):
        def index_map(j):
            item = jnp.minimum(j, n_steps - 1)
            return f(item // N_HEADS, item % N_HEADS)
        return index_map

    def prev(f):
        def index_map(j):
            item = jnp.maximum(j - 1, 0)
            return f(item // N_HEADS, item % N_HEADS)
        return index_map

    once = pl.Buffered(1)
    in_specs = [
        pl.BlockSpec((rows_blk, D_MODEL), cur(lambda b, h: (b, 0))),
        pl.BlockSpec((None, 6, D_MODEL), cur(lambda b, h: (mod_row(b), 0, 0))),
        pl.BlockSpec((1, D_MODEL), lambda j: (0, 0)),
        pl.BlockSpec((D_MODEL, PAIR_COLS), cur(lambda b, h: (0, h))),
        pl.BlockSpec((D_MODEL, HEAD_DIM), lambda j: (0, 0)),
        pl.BlockSpec((None, 16, HEAD_DIM), cur(lambda b, h: (h, 0, 0))),
        pl.BlockSpec((None, 16, HEAD_DIM), prev(lambda b, h: (h, 0, 0))),
    ]
    args = [x2d, mod3, norm1_g, w_pair, w_gate, head_params, head_params]
    y_shape = jax.ShapeDtypeStruct((n_tok, D_MODEL), BF16)
    y_spec = pl.BlockSpec((rows_blk, 2 * HEAD_DIM), prev(lambda b, h: (b, h)))
    if latent:
        assert n_seq == 1
        cos2, sin2 = rope
        s_ret, s_c, s_n, s_m = states
        in_specs += [
            pl.BlockSpec((rows_blk, HEAD_DIM), lambda j: (0, 0), pipeline_mode=once),
            pl.BlockSpec((rows_blk, HEAD_DIM), lambda j: (0, 0), pipeline_mode=once),
            pl.BlockSpec((None, None, 2, None, HEAD_DIM, HEAD_DIM), cur(lambda b, h: (b, 0, 0, h, 0, 0))),
            pl.BlockSpec((None, None, 2, None, HEAD_DIM, HEAD_DIM), cur(lambda b, h: (b, 0, 0, h, 0, 0))),
            pl.BlockSpec((None, None, 2, 1, HEAD_DIM), cur(lambda b, h: (b, h, 0, 0, 0))),
            pl.BlockSpec((None, None, 2, HEAD_DIM), cur(lambda b, h: (b, h, 0, 0))),
        ]
        args += [cos2, sin2, s_ret, s_c, s_n, s_m]
        out_shape = y_shape
        out_specs = y_spec
    else:
        bsz = n_blk * n_seq
        st = jax.ShapeDtypeStruct((bsz, 1, 2, N_HEADS, HEAD_DIM, HEAD_DIM), F32)
        vec = jax.ShapeDtypeStruct((bsz, N_HEADS, 2, HEAD_DIM), F32)
        st_spec = pl.BlockSpec((n_seq, None, 2, None, HEAD_DIM, HEAD_DIM), cur(lambda b, h: (b, 0, 0, h, 0, 0)))
        vec_spec = pl.BlockSpec((n_seq, None, 2, HEAD_DIM), cur(lambda b, h: (b, h, 0, 0)))
        out_shape = (y_shape, st, st, vec, vec)
        out_specs = (y_spec, st_spec, st_spec, vec_spec, vec_spec)
    col = lambda dt: pltpu.VMEM((rows_blk, HEAD_DIM), dt)
    scratch = [
        pltpu.VMEM((rows_blk, D_MODEL), BF16),
        col(F32), col(F32), col(BF16), col(F32),
        col(F32), col(F32),
        pltpu.VMEM((rows_blk, 2 * HEAD_DIM), BF16),
        col(F32), col(F32),
        col(BF16), col(F32),
        pltpu.VMEM((2, rows_blk, HEAD_DIM), F32),
        pltpu.VMEM((2, rows_blk, HEAD_DIM), F32),
        pltpu.VMEM((2, n_chunks, HEAD_DIM, HEAD_DIM), F32),
        pltpu.VMEM((2, n_chunks, HEAD_DIM, 2 * HEAD_DIM), F32),
        pltpu.VMEM((10, max(n_chunks, 8), HEAD_DIM), F32),
        pltpu.VMEM((5, CHUNK, CHUNK), F32),
        pltpu.VMEM((rows_blk + 2 * CONV_HALO, HEAD_DIM), F32),
        pltpu.VMEM((rows_blk + 2 * CONV_HALO, HEAD_DIM), F32),
    ]
    return pl.pallas_call(
        kern,
        out_shape=out_shape,
        grid=(n_steps + 1,),
        in_specs=in_specs,
        out_specs=out_specs,
        scratch_shapes=scratch,
        compiler_params=pltpu.CompilerParams(
            dimension_semantics=("arbitrary",), vmem_limit_bytes=VMEM_LIMIT),
        name="mixer_latent" if latent else "mixer_context",
    )(*args)


def _ffn_kernel(x_ref, y_ref, mod_ref, wo_ref, g2_ref, w1_ref, w2_ref, gf_ref, out_ref):
    g1 = mod_ref[2:3, :]
    sh2 = mod_ref[3:4, :]
    sc2 = mod_ref[4:5, :]
    g2 = mod_ref[5:6, :]
    x1 = x_ref[...] + g1 * _dot(y_ref[...], wo_ref[...])
    h2 = (_rms(x1, g2_ref[...]) * (1.0 + sc2) + sh2).astype(BF16)
    f = jnp.zeros_like(x1)
    for j in range(D_FF // D_MODEL):
        cols = slice(j * D_MODEL, (j + 1) * D_MODEL)
        hid = jnp.maximum(_dot(h2, w1_ref[:, cols]), 0.0)
        f = f + _dot((hid * hid).astype(BF16), w2_ref[cols, :])
    out_ref[...] = _rms(x1 + g2 * f, gf_ref[...])


def _ffn(x2d, y2d, mod3, mod_row, w_out, norm2_g, w_ff1, w_ff2, final_g):
    n_tok = x2d.shape[0]
    const = lambda i: (0, 0)
    return pl.pallas_call(
        _ffn_kernel,
        out_shape=jax.ShapeDtypeStruct((n_tok, D_MODEL), F32),
        grid=(n_tok // FFN_ROWS,),
        in_specs=[
            pl.BlockSpec((FFN_ROWS, D_MODEL), lambda i: (i, 0)),
            pl.BlockSpec((FFN_ROWS, D_MODEL), lambda i: (i, 0)),
            pl.BlockSpec((None, 6, D_MODEL), lambda i: (mod_row(i), 0, 0)),
            pl.BlockSpec((D_MODEL, D_MODEL), const, pipeline_mode=pl.Buffered(1)),
            pl.BlockSpec((1, D_MODEL), const),
            pl.BlockSpec((D_MODEL, D_FF), const, pipeline_mode=pl.Buffered(1)),
            pl.BlockSpec((D_FF, D_MODEL), const, pipeline_mode=pl.Buffered(1)),
            pl.BlockSpec((1, D_MODEL), const),
        ],
        out_specs=pl.BlockSpec((FFN_ROWS, D_MODEL), lambda i: (i, 0)),
        compiler_params=pltpu.CompilerParams(
            dimension_semantics=("arbitrary",), vmem_limit_bytes=VMEM_LIMIT),
        name="outproj_mlp",
    )(x2d, y2d, mod3, w_out, norm2_g, w_ff1, w_ff2, final_g)


def _rope_tables(seq):
    pos = np.arange(seq)
    row = (pos // GRID_W).astype(np.float64)
    col = (pos % GRID_W).astype(np.float64)
    nf = HEAD_DIM // 4
    inv = ROPE_BASE ** (-np.arange(nf, dtype=np.float64) / nf)
    ang = np.concatenate([row[:, None] * inv, col[:, None] * inv], -1)
    cos = np.cos(ang)
    sin = np.sin(ang)
    cos2 = np.concatenate([cos, cos], -1).astype(np.float32)
    sin2 = np.concatenate([-sin, sin], -1).astype(np.float32)
    return jnp.asarray(cos2), jnp.asarray(sin2)


def kernel(x_prompt, x_sample, state_ret, state_mlstm_C, state_mlstm_n, state_mlstm_m, c, c_ctx,
           w_ada, b_ada, norm1_g, norm2_g, w_in, conv_w, ret_decay_logit, mlstm_gate_bias,
           ret_gn_g, mlstm_gn_g, w_out, w_ff1, w_ff2, final_g):
    assert w_ada.shape[0] == 1, "single-layer kernel"
    bp, tp, _ = x_prompt.shape
    bs, ts, _ = x_sample.shape
    assert tp % CHUNK == 0 and ts % CHUNK == 0 and bp % CTX_SEQS_PER_STEP == 0

    cond = jnp.concatenate([c_ctx[None, :], c, jnp.zeros((8 - 1 - bs, D_MODEL), F32)], 0)
    mod = _modulation(cond, w_ada[0], b_ada)
    mod3 = mod[:1 + bs].reshape(1 + bs, 6, D_MODEL)

    w_pair, w_gate = _regroup_in_proj(w_in)
    w_out_p = _regroup_out_proj(w_out)
    w1, w2 = _cast_ffn_weights(w_ff1, w_ff2)

    cw = conv_w[0]
    hp_rows = [cw[j, :512].reshape(N_HEADS, HEAD_DIM) for j in range(3)]
    hp_rows += [cw[j, 512:].reshape(N_HEADS, HEAD_DIM) for j in range(3)]
    hp_rows += [ret_gn_g[0].reshape(N_HEADS, HEAD_DIM), mlstm_gn_g[0].reshape(N_HEADS, HEAD_DIM)]
    hp_rows += [jnp.broadcast_to(ret_decay_logit[0, d][:, None], (N_HEADS, HEAD_DIM)) for d in range(2)]
    gate_bias = jnp.pad(mlstm_gate_bias[0].reshape(1, N_GATE_COLS), ((0, 0), (HEAD_DIM - N_GATE_COLS, 0)))
    hp_rows += [jnp.broadcast_to(gate_bias, (N_HEADS, HEAD_DIM))]
    hp_rows += [jnp.zeros((N_HEADS, HEAD_DIM), F32)] * (16 - len(hp_rows))
    head_params = jnp.stack(hp_rows, axis=1).astype(F32)

    g1 = norm1_g[0][None, :]
    g2 = norm2_g[0][None, :]
    gf = final_g[None, :]
    xp2d = x_prompt.reshape(bp * tp, D_MODEL)
    xs2d = x_sample.reshape(bs * ts, D_MODEL)

    y_p, new_ret, new_c, new_n, new_m = _mixer(
        xp2d, mod3, lambda b: 0, g1, w_pair, w_gate, head_params,
        latent=False, n_seq=CTX_SEQS_PER_STEP, cps=tp // CHUNK)
    out_p = _ffn(xp2d, y_p, mod3, lambda i: 0, w_out_p, g2, w1, w2, gf).reshape(bp, tp, D_MODEL)

    s_n = jnp.transpose(state_mlstm_n[:, 0], (0, 2, 1, 3))[:, :, :, None, :]
    s_m = jnp.broadcast_to(jnp.transpose(state_mlstm_m[:, 0], (0, 2, 1))[..., None],
                           (bs, N_HEADS, 2, HEAD_DIM))
    y_s = _mixer(xs2d, mod3, lambda b: 1 + b, g1, w_pair, w_gate, head_params,
                 latent=True, n_seq=1, cps=ts // CHUNK,
                 rope=_rope_tables(ts), states=(state_ret, state_mlstm_C, s_n, s_m))
    tiles_per_seq = ts // FFN_ROWS
    out_s = _ffn(xs2d, y_s, mod3, lambda i: 1 + i // tiles_per_seq,
                 w_out_p, g2, w1, w2, gf).reshape(bs, ts, D_MODEL)

    new_n = jnp.transpose(new_n, (0, 2, 1, 3))[:, None]
    new_m = jnp.transpose(new_m[..., 0], (0, 2, 1))[:, None]
    return out_p, out_s, new_ret, new_c, new_n, new_m
```

```python
import functools

import numpy as np
import jax
import jax.numpy as jnp
from jax import lax
from jax.experimental import pallas as pl
from jax.experimental.pallas import tpu as pltpu

F32 = jnp.float32
BF16 = jnp.bfloat16

D_MODEL = 1024
N_HEADS = 4
HEAD_DIM = 128
CHUNK = 128
GRID_W = 64
D_FF = 4 * D_MODEL
EPS = 1e-6
ROPE_BASE = 10000.0
LOG2_E = 1.4426950408889634
LN_2 = 0.6931471805599453
PAIR_COLS = 8 * HEAD_DIM
N_GATE_COLS = 4 * N_HEADS
ROW_TILE = 256
FFN_ROWS = 512
CTX_SEQS_PER_STEP = 4
PROJ_SLAB_ROWS = 256
CONV_HALO = 8
VMEM_LIMIT = 60 * 1024 * 1024


def _dot(a, b):
    return jnp.dot(a, b, preferred_element_type=F32)


def _dot_nt(a, b):
    return lax.dot_general(a, b, (((1,), (1,)), ((), ())), preferred_element_type=F32)


def _dot_tn(a, b):
    return lax.dot_general(a, b, (((0,), (0,)), ((), ())), preferred_element_type=F32)


def _rms(x, g):
    return x * lax.rsqrt(jnp.mean(x * x, axis=-1, keepdims=True) + EPS) * g


def _group_norm(o, g):
    mu = jnp.mean(o, axis=-1, keepdims=True)
    c = o - mu
    var = jnp.mean(c * c, axis=-1, keepdims=True)
    return c * lax.rsqrt(var + EPS) * g


def _log_sigmoid(x):
    return jnp.minimum(x, 0.0) - jnp.log(1.0 + jnp.exp(-jnp.abs(x)))


def _sigmoid(x):
    return 1.0 / (1.0 + jnp.exp(-x))


def _split2(x):
    hi = x.astype(BF16)
    lo = (x - hi.astype(F32)).astype(BF16)
    return hi, lo


def _mod_kernel(cond_ref, w_ref, b_ref, out_ref):
    c = cond_ref[...]
    s = (c * _sigmoid(c)).astype(BF16)
    out_ref[...] = _dot(s, w_ref[...].astype(BF16)) + b_ref[...]


def _modulation(cond, w_ada, b_ada):
    n = w_ada.shape[1]
    tn = 2048
    return pl.pallas_call(
        _mod_kernel,
        out_shape=jax.ShapeDtypeStruct((cond.shape[0], n), F32),
        grid=(n // tn,),
        in_specs=[pl.BlockSpec(cond.shape, lambda j: (0, 0)),
                  pl.BlockSpec((D_MODEL, tn), lambda j: (0, j)),
                  pl.BlockSpec((1, tn), lambda j: (0, j))],
        out_specs=pl.BlockSpec((cond.shape[0], tn), lambda j: (0, j)),
        compiler_params=pltpu.CompilerParams(dimension_semantics=("arbitrary",)),
        name="adaln_mod",
    )(cond, w_ada, b_ada)


REGROUP_ROWS = 256


def _regroup_in_kernel(wt_ref, out_ref, gate_ref):
    for h in range(N_HEADS):
        for g in range(8):
            src = (g * N_HEADS + h) * HEAD_DIM
            dst = (h * 8 + g) * HEAD_DIM
            out_ref[:, dst:dst + HEAD_DIM] = wt_ref[src:src + HEAD_DIM, :].T.astype(BF16)
    n_rows = wt_ref.shape[0]
    n_gate = n_rows - N_HEADS * PAIR_COLS
    tail = wt_ref[n_rows - HEAD_DIM:n_rows, :].T
    lane = lax.broadcasted_iota(jnp.int32, tail.shape, 1)
    gate_ref[...] = jnp.where(lane >= HEAD_DIM - n_gate, tail, 0.0).astype(BF16)


def _regroup_in_proj(w_in):
    w_t = jnp.transpose(w_in[0])
    n_cols = w_t.shape[0]
    assert n_cols == N_HEADS * PAIR_COLS + N_GATE_COLS
    return pl.pallas_call(
        _regroup_in_kernel,
        out_shape=(jax.ShapeDtypeStruct((D_MODEL, N_HEADS * PAIR_COLS), BF16),
                   jax.ShapeDtypeStruct((D_MODEL, HEAD_DIM), BF16)),
        grid=(D_MODEL // REGROUP_ROWS,),
        in_specs=[pl.BlockSpec((n_cols, REGROUP_ROWS), lambda i: (0, i))],
        out_specs=(pl.BlockSpec((REGROUP_ROWS, N_HEADS * PAIR_COLS), lambda i: (i, 0)),
                   pl.BlockSpec((REGROUP_ROWS, HEAD_DIM), lambda i: (i, 0))),
        compiler_params=pltpu.CompilerParams(dimension_semantics=("arbitrary",)),
        name="regroup_w_in",
    )(w_t)


def _regroup_out_kernel(w_ref, out_ref):
    for h in range(N_HEADS):
        for g in range(2):
            src = (g * N_HEADS + h) * HEAD_DIM
            dst = (h * 2 + g) * HEAD_DIM
            out_ref[dst:dst + HEAD_DIM, :] = w_ref[src:src + HEAD_DIM, :].astype(BF16)


def _regroup_out_proj(w_out):
    cols = 2 * HEAD_DIM
    return pl.pallas_call(
        _regroup_out_kernel,
        out_shape=jax.ShapeDtypeStruct((D_MODEL, D_MODEL), BF16),
        grid=(D_MODEL // cols,),
        in_specs=[pl.BlockSpec((None, D_MODEL, cols), lambda j: (0, 0, j))],
        out_specs=pl.BlockSpec((D_MODEL, cols), lambda j: (0, j)),
        compiler_params=pltpu.CompilerParams(dimension_semantics=("arbitrary",)),
        name="regroup_w_out",
    )(w_out)


def _cast_pair_kernel(a_ref, b_ref, oa_ref, ob_ref):
    oa_ref[...] = a_ref[...].astype(BF16)
    ob_ref[...] = b_ref[...].astype(BF16)


def _cast_ffn_weights(w_ff1, w_ff2):
    n_slabs = 8
    cols = D_FF // n_slabs
    return pl.pallas_call(
        _cast_pair_kernel,
        out_shape=(jax.ShapeDtypeStruct((D_MODEL, D_FF), BF16), jax.ShapeDtypeStruct((D_FF, D_MODEL), BF16)),
        grid=(n_slabs,),
        in_specs=[pl.BlockSpec((None, D_MODEL, cols), lambda i: (0, 0, i)),
                  pl.BlockSpec((None, cols, D_MODEL), lambda i: (0, i, 0))],
        out_specs=(pl.BlockSpec((D_MODEL, cols), lambda i: (0, i)),
                   pl.BlockSpec((cols, D_MODEL), lambda i: (i, 0))),
        compiler_params=pltpu.CompilerParams(dimension_semantics=("arbitrary",)),
        name="cast_ffn_weights",
    )(w_ff1, w_ff2)


_B_END, _MAX_LWE, _M_PREV, _M_NEW, _DECAY = 0, 2, 4, 6, 8


def _mixer_kernel(*refs, latent, n_seq, cps, n_steps):
    L = CHUNK
    n_chunks = n_seq * cps
    rows_total = n_chunks * L
    seq_len = cps * L
    assert seq_len & (seq_len - 1) == 0
    if latent:
        (x_ref, mod_ref, g1_ref, w_ref, wg_ref, hp_ref, hp_prev_ref, cos_ref, sin_ref,
         sr_in, sc_in, sn_in, sm_in, y_ref, *scratch) = refs
    else:
        (x_ref, mod_ref, g1_ref, w_ref, wg_ref, hp_ref, hp_prev_ref,
         y_ref, so_ref, co_ref, no_ref, mo_ref, *scratch) = refs
    (hn_ref, rq_ref, rk_ref, rv_ref, rg_ref, mq_ref, mk_ref, mvx_ref, mog_ref, gt_ref,
     ar_ref, am_ref, logi_ref, bc_ref, skv_ref, ckv_ref, cs_ref, rc_ref, rawq_ref, rawk_ref,
     gall_ref) = scratch

    step = pl.program_id(0)
    live = step < n_steps
    head = lax.rem(jnp.minimum(step, n_steps - 1), N_HEADS)

    @pl.when(step == 0)
    def _():
        for ref in (rq_ref, am_ref, rg_ref, mog_ref):
            ref[...] = jnp.zeros_like(ref)

    @pl.when(jnp.logical_and(head == 0, live))
    def _():
        gain = g1_ref[...] * (1.0 + mod_ref[1:2, :])
        sh1 = mod_ref[0:1, :]
        gate_bias = hp_ref[10:11, :]
        lane_t = lax.broadcasted_iota(jnp.int32, (ROW_TILE, HEAD_DIM), 1)
        is_forget = jnp.bitwise_and(lane_t, 4) == 4

        def body(i, carry):
            r = pl.ds(pl.multiple_of(i * ROW_TILE, ROW_TILE), ROW_TILE)
            xr = x_ref[r, :]
            inv = lax.rsqrt(jnp.mean(xr * xr, axis=-1, keepdims=True) + EPS)
            hn_t = (xr * inv * gain + sh1).astype(BF16)
            hn_ref[r, :] = hn_t
            gates = _dot(hn_t, wg_ref[...]) + gate_bias
            gall_ref[r, :] = jnp.where(is_forget, _log_sigmoid(gates), gates) * LOG2_E
            return carry

        lax.fori_loop(0, rows_total // ROW_TILE, body, 0)

    hp = hp_ref[...]
    scale = HEAD_DIM ** -0.5

    r_i = lax.broadcasted_iota(jnp.int32, (L, L), 0)
    s_i = lax.broadcasted_iota(jnp.int32, (L, L), 1)
    r_f = r_i.astype(F32)
    s_f = s_i.astype(F32)
    lg_f = _log_sigmoid(hp[8:9])
    lg_b = _log_sigmoid(hp[9:10])
    rc_ref[0] = (jnp.where(r_i >= s_i, jnp.exp(lg_f * jnp.where(r_i >= s_i, r_f - s_f, 0.0)), 0.0)
                 + jnp.where(s_i >= r_i, jnp.exp(lg_b * jnp.where(s_i >= r_i, s_f - r_f, 0.0)), 0.0))
    rc_ref[1] = jnp.exp(lg_f * (r_f + 1.0))
    rc_ref[2] = jnp.exp(lg_b * (L - r_f))
    rc_ref[3] = jnp.exp(lg_f * (L - 1.0 - r_f))
    rc_ref[4] = jnp.exp(lg_b * r_f)
    chunk_decay = (jnp.exp(lg_f * float(L)), jnp.exp(lg_b * float(L)))

    tri = (jnp.where(r_i >= s_i, 1.0, 0.0).astype(BF16),
           jnp.where(s_i >= r_i, 1.0, 0.0).astype(BF16))

    for raw_ref in (rawq_ref, rawk_ref):
        raw_ref[0:CONV_HALO, :] = jnp.zeros((CONV_HALO, HEAD_DIM), F32)
        raw_ref[rows_total + CONV_HALO:rows_total + 2 * CONV_HALO, :] = jnp.zeros((CONV_HALO, HEAD_DIM), F32)

    def rows(c):
        return slice(c * L, (c + 1) * L)

    def gate_col(c, lane):
        g = gt_ref[rows(c), :]
        return jnp.broadcast_to(g[:, lane:lane + 1], (L, HEAD_DIM))

    hp_prev = hp_prev_ref[...]

    def finish_rows(r):
        rg = rg_ref[r, :]
        ret_y = _group_norm(rq_ref[r, :], hp_prev[6:7]) * (rg * _sigmoid(rg))
        y_ref[r, 0:128] = ret_y.astype(BF16)
        ml_y = _group_norm(am_ref[r, :], hp_prev[7:8]) * _sigmoid(mog_ref[r, :])
        y_ref[r, 128:256] = ml_y.astype(BF16)

    def project(blk, n_rows):
        hn = hn_ref[blk, :]
        pr = _dot(hn, w_ref[:, 0:256])
        q = pr[:, 0:128]
        k = pr[:, 128:256] * scale
        if latent:
            cos2 = cos_ref[blk, :]
            sin2 = sin_ref[blk, :]
            q = q * cos2 + pltpu.roll(q, HEAD_DIM // 2, axis=1) * sin2
            k = k * cos2 + pltpu.roll(k, HEAD_DIM // 2, axis=1) * sin2
        rq_ref[blk, :] = q
        rk_ref[blk, :] = k

        pv = _dot(hn, w_ref[:, 256:512])
        rv_ref[blk, :] = pv[:, 0:128].astype(BF16)
        rg_ref[blk, :] = pv[:, 128:256]

        pm = _dot(hn, w_ref[:, 512:768])
        halo_blk = slice(blk.start + CONV_HALO, blk.stop + CONV_HALO)
        rawq_ref[halo_blk, :] = pm[:, 0:128]
        rawk_ref[halo_blk, :] = pm[:, 128:256]

        po = _dot(hn, w_ref[:, 768:1024])
        mvx_ref[blk, 0:128] = po[:, 0:128].astype(BF16)
        mvx_ref[blk, 128:256] = jnp.ones((n_rows, HEAD_DIM), BF16)
        mog_ref[blk, :] = po[:, 128:256]

        gt_ref[blk, :] = pltpu.roll(gall_ref[blk, :], N_GATE_COLS - head, axis=1)

    def conv_silu(blk, n_rows):
        sub = lax.broadcasted_iota(jnp.int32, (CONV_HALO, HEAD_DIM), 0)
        starts_seq = blk.start % seq_len == 0
        ends_seq = blk.stop % seq_len == 0
        for raw_ref, dst_ref, taps, post in ((rawq_ref, mq_ref, hp[0:3], 1.0), (rawk_ref, mk_ref, hp[3:6], scale)):
            lo = blk.start + CONV_HALO
            prev = raw_ref[lo - 1:lo - 1 + n_rows, :]
            nxt = raw_ref[lo + 1:lo + 1 + n_rows, :]
            if starts_seq:
                first = jnp.where(sub == 0, 0.0, prev[0:CONV_HALO, :])
                prev = jnp.concatenate([first, prev[CONV_HALO:, :]], axis=0)
            if ends_seq:
                last = jnp.where(sub == CONV_HALO - 1, 0.0, nxt[n_rows - CONV_HALO:, :])
                nxt = jnp.concatenate([nxt[:n_rows - CONV_HALO, :], last], axis=0)
            out = prev * taps[0:1] + raw_ref[lo:lo + n_rows, :] * taps[1:2] + nxt * taps[2:3]
            out = out * _sigmoid(out)
            dst_ref[blk, :] = out if post == 1.0 else out * post

    def project_and_scan():
        slab_rows = min(seq_len, PROJ_SLAB_ROWS)
        assert seq_len % slab_rows == 0
        slabs = [slice(s0, s0 + slab_rows) for s0 in range(0, rows_total, slab_rows)]
        for i, slab in enumerate(slabs):
            finish_rows(slab)
            project(slab, slab_rows)
            if i > 0:
                conv_silu(slabs[i - 1], slab_rows)
        conv_silu(slabs[-1], slab_rows)
        pre_pass(range(n_chunks))
        scan_passes(range(n_chunks), range(n_seq))

    def pre_pass(chunks):
        for c in chunks:
            ar_ref[rows(c), :] = (_dot_nt(rq_ref[rows(c), :].astype(BF16), rk_ref[rows(c), :].astype(BF16))
                                  * rc_ref[0]).astype(BF16)
            am_ref[rows(c), :] = _dot_nt(mq_ref[rows(c), :].astype(BF16), mk_ref[rows(c), :].astype(BF16))
        for c in chunks:
            for d in range(2):
                li = gate_col(c, 8 * d)
                lf = gate_col(c, 8 * d + 4)
                strict = (r_i > s_i) if d == 0 else (r_i < s_i)
                x_mat = jnp.where(strict, lf, jnp.where(r_i == s_i, li, 0.0))
                hi, lo = _split2(jnp.concatenate([x_mat, lf], axis=1))
                d_ext = _dot(tri[d], hi) + _dot(tri[d], lo)
                bcum = d_ext[:, 128:256]
                causal = (s_i <= r_i) if d == 0 else (s_i >= r_i)
                logi_ref[d, rows(c), :] = jnp.where(causal, d_ext[:, 0:128], -jnp.inf)
                bc_ref[d, rows(c), :] = bcum
                b_end = bcum[L - 1:L, :] if d == 0 else bcum[0:1, :]
                cs_ref[_B_END + d, c:c + 1, :] = b_end
                cs_ref[_MAX_LWE + d, c:c + 1, :] = jnp.max(b_end - bcum + li, axis=0, keepdims=True)
        for c in chunks:
            kf = rk_ref[rows(c), :]
            vb = rv_ref[rows(c), :]
            for d in range(2):
                skv_ref[d, c] = _dot_tn((kf * rc_ref[3 + d]).astype(BF16), vb)

    def scan_passes(chunks, seqs):
        m_final = {}
        for s in seqs:
            for d in range(2):
                m = sm_in[d:d + 1, :] * LOG2_E if latent else jnp.zeros((1, HEAD_DIM), F32)
                order = range(cps) if d == 0 else range(cps - 1, -1, -1)
                for j in order:
                    c = s * cps + j
                    cs_ref[_M_PREV + d, c:c + 1, :] = m
                    b_end = cs_ref[_B_END + d, c:c + 1, :]
                    m_new = jnp.maximum(b_end + m, cs_ref[_MAX_LWE + d, c:c + 1, :])
                    cs_ref[_M_NEW + d, c:c + 1, :] = m_new
                    cs_ref[_DECAY + d, c:c + 1, :] = jnp.exp2(b_end + m - m_new)
                    m = m_new
                m_final[(s, d)] = m

        for c in chunks:
            kf = mk_ref[rows(c), :]
            vx = mvx_ref[rows(c), :]
            for d in range(2):
                log_w_end = cs_ref[_B_END + d, c:c + 1, :] - bc_ref[d, rows(c), :] + gate_col(c, 8 * d)
                w_end = jnp.exp2(log_w_end - cs_ref[_M_NEW + d, c:c + 1, :])
                ckv_ref[d, c] = _dot_tn((kf * w_end).astype(BF16), vx)

        for s in seqs:
            for d in range(2):
                if latent:
                    s_state = sr_in[d]
                    n_rep = jnp.broadcast_to(sn_in[d], (HEAD_DIM, HEAD_DIM)).T
                    c_state = jnp.concatenate([sc_in[d], n_rep], axis=1)
                else:
                    s_state = jnp.zeros((HEAD_DIM, HEAD_DIM), F32)
                    c_state = jnp.zeros((HEAD_DIM, 2 * HEAD_DIM), F32)
                order = range(cps) if d == 0 else range(cps - 1, -1, -1)
                for j in order:
                    c = s * cps + j
                    inc = skv_ref[d, c]
                    skv_ref[d, c] = s_state
                    s_state = s_state * chunk_decay[d] + inc
                    inc = ckv_ref[d, c]
                    ckv_ref[d, c] = c_state
                    decay = cs_ref[_DECAY + d, c:c + 1, :]
                    c_state = c_state * jnp.concatenate([decay, decay], axis=1) + inc
                if not latent:
                    so_ref[s, d] = s_state
                    co_ref[s, d] = c_state[:, 0:128]
                    no_ref[s, d:d + 1, :] = c_state[:, 128:256].T[0:1, :]
                    mo_ref[s, d:d + 1, :] = m_final[(s, d)] * LN_2

        for c in chunks:
            qf = rq_ref[rows(c), :]
            lhs = jnp.concatenate([ar_ref[rows(c), :], (qf * rc_ref[1]).astype(BF16),
                                   (qf * rc_ref[2]).astype(BF16)], axis=1)
            rhs = jnp.concatenate([rv_ref[rows(c), :], skv_ref[0, c].astype(BF16),
                                   skv_ref[1, c].astype(BF16)], axis=0)
            rq_ref[rows(c), :] = _dot(lhs, rhs)
            qf = mq_ref[rows(c), :]
            a_mat = am_ref[rows(c), :]
            vx = mvx_ref[rows(c), :]
            h_sum = None
            for d in range(2):
                log_intra = logi_ref[d, rows(c), :]
                log_inter = bc_ref[d, rows(c), :] + cs_ref[_M_PREV + d, c:c + 1, :]
                m_t = jnp.maximum(log_inter, jnp.max(log_intra, axis=1, keepdims=True))
                w_inter = jnp.exp2(log_inter - m_t)
                w_intra = jnp.exp2(log_intra - m_t)
                lhs = jnp.concatenate([(a_mat * w_intra).astype(BF16), (qf * w_inter).astype(BF16)], axis=1)
                rhs = jnp.concatenate([vx, ckv_ref[d, c].astype(BF16)], axis=0)
                res = _dot(lhs, rhs)
                h_dir = res[:, 0:128] / jnp.maximum(jnp.abs(res[:, 128:256]), jnp.exp2(-m_t))
                h_sum = h_dir if h_sum is None else h_sum + h_dir
            am_ref[rows(c), :] = h_sum

    pl.when(live)(project_and_scan)

    @pl.when(step == n_steps)
    def _():
        def body(i, carry):
            finish_rows(pl.ds(pl.multiple_of(i * ROW_TILE, ROW_TILE), ROW_TILE))
            return carry

        lax.fori_loop(0, rows_total // ROW_TILE, body, 0)


def _mixer(x2d, mod3, mod_row, norm1_g, w_pair, w_gate, head_params, *, latent, n_seq, cps,
           rope=None, states=None):
    n_tok = x2d.shape[0]
    rows_blk = n_seq * cps * CHUNK
    n_blk = n_tok // rows_blk
    n_chunks = n_seq * cps
    n_steps = n_blk * N_HEADS
    kern = functools.partial(_mixer_kernel, latent=latent, n_seq=n_seq, cps=cps, n_steps=n_steps)

    def cur(f):
        def index_map(j):
            item = jnp.minimum(j, n_steps - 1)
            return f(item // N_HEADS, item % N_HEADS)
        return index_map

    def prev(f):
        def index_map(j):
            item = jnp.maximum(j - 1, 0)
            return f(item // N_HEADS, item % N_HEADS)
        return index_map

    once = pl.Buffered(1)
    in_specs = [
        pl.BlockSpec((rows_blk, D_MODEL), cur(lambda b, h: (b, 0))),
        pl.BlockSpec((None, 6, D_MODEL), cur(lambda b, h: (mod_row(b), 0, 0))),
        pl.BlockSpec((1, D_MODEL), lambda j: (0, 0)),
        pl.BlockSpec((D_MODEL, PAIR_COLS), cur(lambda b, h: (0, h))),
        pl.BlockSpec((D_MODEL, HEAD_DIM), lambda j: (0, 0)),
        pl.BlockSpec((None, 16, HEAD_DIM), cur(lambda b, h: (h, 0, 0))),
        pl.BlockSpec((None, 16, HEAD_DIM), prev(lambda b, h: (h, 0, 0))),
    ]
    args = [x2d, mod3, norm1_g, w_pair, w_gate, head_params, head_params]
    y_shape = jax.ShapeDtypeStruct((n_tok, D_MODEL), BF16)
    y_spec = pl.BlockSpec((rows_blk, 2 * HEAD_DIM), prev(lambda b, h: (b, h)))
    if latent:
        assert n_seq == 1
        cos2, sin2 = rope
        s_ret, s_c, s_n, s_m = states
        in_specs += [
            pl.BlockSpec((rows_blk, HEAD_DIM), lambda j: (0, 0), pipeline_mode=once),
            pl.BlockSpec((rows_blk, HEAD_DIM), lambda j: (0, 0), pipeline_mode=once),
            pl.BlockSpec((None, None, 2, None, HEAD_DIM, HEAD_DIM), cur(lambda b, h: (b, 0, 0, h, 0, 0))),
            pl.BlockSpec((None, None, 2, None, HEAD_DIM, HEAD_DIM), cur(lambda b, h: (b, 0, 0, h, 0, 0))),
            pl.BlockSpec((None, None, 2, 1, HEAD_DIM), cur(lambda b, h: (b, h, 0, 0, 0))),
            pl.BlockSpec((None, None, 2, HEAD_DIM), cur(lambda b, h: (b, h, 0, 0))),
        ]
        args += [cos2, sin2, s_ret, s_c, s_n, s_m]
        out_shape = y_shape
        out_specs = y_spec
    else:
        bsz = n_blk * n_seq
        st = jax.ShapeDtypeStruct((bsz, 1, 2, N_HEADS, HEAD_DIM, HEAD_DIM), F32)
        vec = jax.ShapeDtypeStruct((bsz, N_HEADS, 2, HEAD_DIM), F32)
        st_spec = pl.BlockSpec((n_seq, None, 2, None, HEAD_DIM, HEAD_DIM), cur(lambda b, h: (b, 0, 0, h, 0, 0)))
        vec_spec = pl.BlockSpec((n_seq, None, 2, HEAD_DIM), cur(lambda b, h: (b, h, 0, 0)))
        out_shape = (y_shape, st, st, vec, vec)
        out_specs = (y_spec, st_spec, st_spec, vec_spec, vec_spec)
    col = lambda dt: pltpu.VMEM((rows_blk, HEAD_DIM), dt)
    scratch = [
        pltpu.VMEM((rows_blk, D_MODEL), BF16),
        col(F32), col(F32), col(BF16), col(F32),
        col(F32), col(F32),
        pltpu.VMEM((rows_blk, 2 * HEAD_DIM), BF16),
        col(F32), col(F32),
        col(BF16), col(F32),
        pltpu.VMEM((2, rows_blk, HEAD_DIM), F32),
        pltpu.VMEM((2, rows_blk, HEAD_DIM), F32),
        pltpu.VMEM((2, n_chunks, HEAD_DIM, HEAD_DIM), F32),
        pltpu.VMEM((2, n_chunks, HEAD_DIM, 2 * HEAD_DIM), F32),
        pltpu.VMEM((10, max(n_chunks, 8), HEAD_DIM), F32),
        pltpu.VMEM((5, CHUNK, CHUNK), F32),
        pltpu.VMEM((rows_blk + 2 * CONV_HALO, HEAD_DIM), F32),
        pltpu.VMEM((rows_blk + 2 * CONV_HALO, HEAD_DIM), F32),
        col(F32),
    ]
    return pl.pallas_call(
        kern,
        out_shape=out_shape,
        grid=(n_steps + 1,),
        in_specs=in_specs,
        out_specs=out_specs,
        scratch_shapes=scratch,
        compiler_params=pltpu.CompilerParams(
            dimension_semantics=("arbitrary",), vmem_limit_bytes=VMEM_LIMIT),
        name="mixer_latent" if latent else "mixer_context",
    )(*args)


def _ffn_kernel(x_ref, y_ref, mod_ref, wo_ref, g2_ref, w1_ref, w2_ref, gf_ref, out_ref):
    g1 = mod_ref[2:3, :]
    sh2 = mod_ref[3:4, :]
    sc2 = mod_ref[4:5, :]
    g2 = mod_ref[5:6, :]
    x1 = x_ref[...] + g1 * _dot(y_ref[...], wo_ref[...])
    h2 = (_rms(x1, g2_ref[...]) * (1.0 + sc2) + sh2).astype(BF16)
    f = jnp.zeros_like(x1)
    for j in range(D_FF // D_MODEL):
        cols = slice(j * D_MODEL, (j + 1) * D_MODEL)
        hid = jnp.maximum(_dot(h2, w1_ref[:, cols]), 0.0)
        f = f + _dot((hid * hid).astype(BF16), w2_ref[cols, :])
    out_ref[...] = _rms(x1 + g2 * f, gf_ref[...])


def _ffn(x2d, y2d, mod3, mod_row, w_out, norm2_g, w_ff1, w_ff2, final_g):
    n_tok = x2d.shape[0]
    const = lambda i: (0, 0)
    return pl.pallas_call(
        _ffn_kernel,
        out_shape=jax.ShapeDtypeStruct((n_tok, D_MODEL), F32),
        grid=(n_tok // FFN_ROWS,),
        in_specs=[
            pl.BlockSpec((FFN_ROWS, D_MODEL), lambda i: (i, 0)),
            pl.BlockSpec((FFN_ROWS, D_MODEL), lambda i: (i, 0)),
            pl.BlockSpec((None, 6, D_MODEL), lambda i: (mod_row(i), 0, 0)),
            pl.BlockSpec((D_MODEL, D_MODEL), const, pipeline_mode=pl.Buffered(1)),
            pl.BlockSpec((1, D_MODEL), const),
            pl.BlockSpec((D_MODEL, D_FF), const, pipeline_mode=pl.Buffered(1)),
            pl.BlockSpec((D_FF, D_MODEL), const, pipeline_mode=pl.Buffered(1)),
            pl.BlockSpec((1, D_MODEL), const),
        ],
        out_specs=pl.BlockSpec((FFN_ROWS, D_MODEL), lambda i: (i, 0)),
        compiler_params=pltpu.CompilerParams(
            dimension_semantics=("arbitrary",), vmem_limit_bytes=VMEM_LIMIT),
        name="outproj_mlp",
    )(x2d, y2d, mod3, w_out, norm2_g, w_ff1, w_ff2, final_g)


def _rope_tables(seq):
    pos = np.arange(seq)
    row = (pos // GRID_W).astype(np.float64)
    col = (pos % GRID_W).astype(np.float64)
    nf = HEAD_DIM // 4
    inv = ROPE_BASE ** (-np.arange(nf, dtype=np.float64) / nf)
    ang = np.concatenate([row[:, None] * inv, col[:, None] * inv], -1)
    cos = np.cos(ang)
    sin = np.sin(ang)
    cos2 = np.concatenate([cos, cos], -1).astype(np.float32)
    sin2 = np.concatenate([-sin, sin], -1).astype(np.float32)
    return jnp.asarray(cos2), jnp.asarray(sin2)


def kernel(x_prompt, x_sample, state_ret, state_mlstm_C, state_mlstm_n, state_mlstm_m, c, c_ctx,
           w_ada, b_ada, norm1_g, norm2_g, w_in, conv_w, ret_decay_logit, mlstm_gate_bias,
           ret_gn_g, mlstm_gn_g, w_out, w_ff1, w_ff2, final_g):
    assert w_ada.shape[0] == 1, "single-layer kernel"
    bp, tp, _ = x_prompt.shape
    bs, ts, _ = x_sample.shape
    assert tp % CHUNK == 0 and ts % CHUNK == 0 and bp % CTX_SEQS_PER_STEP == 0

    cond = jnp.concatenate([c_ctx[None, :], c, jnp.zeros((8 - 1 - bs, D_MODEL), F32)], 0)
    mod = _modulation(cond, w_ada[0], b_ada)
    mod3 = mod[:1 + bs].reshape(1 + bs, 6, D_MODEL)

    w_pair, w_gate = _regroup_in_proj(w_in)
    w_out_p = _regroup_out_proj(w_out)
    w1, w2 = _cast_ffn_weights(w_ff1, w_ff2)

    cw = conv_w[0]
    hp_rows = [cw[j, :512].reshape(N_HEADS, HEAD_DIM) for j in range(3)]
    hp_rows += [cw[j, 512:].reshape(N_HEADS, HEAD_DIM) for j in range(3)]
    hp_rows += [ret_gn_g[0].reshape(N_HEADS, HEAD_DIM), mlstm_gn_g[0].reshape(N_HEADS, HEAD_DIM)]
    hp_rows += [jnp.broadcast_to(ret_decay_logit[0, d][:, None], (N_HEADS, HEAD_DIM)) for d in range(2)]
    gate_bias = jnp.pad(mlstm_gate_bias[0].reshape(1, N_GATE_COLS), ((0, 0), (HEAD_DIM - N_GATE_COLS, 0)))
    hp_rows += [jnp.broadcast_to(gate_bias, (N_HEADS, HEAD_DIM))]
    hp_rows += [jnp.zeros((N_HEADS, HEAD_DIM), F32)] * (16 - len(hp_rows))
    head_params = jnp.stack(hp_rows, axis=1).astype(F32)

    g1 = norm1_g[0][None, :]
    g2 = norm2_g[0][None, :]
    gf = final_g[None, :]
    xp2d = x_prompt.reshape(bp * tp, D_MODEL)
    xs2d = x_sample.reshape(bs * ts, D_MODEL)

    y_p, new_ret, new_c, new_n, new_m = _mixer(
        xp2d, mod3, lambda b: 0, g1, w_pair, w_gate, head_params,
        latent=False, n_seq=CTX_SEQS_PER_STEP, cps=tp // CHUNK)
    out_p = _ffn(xp2d, y_p, mod3, lambda i: 0, w_out_p, g2, w1, w2, gf).reshape(bp, tp, D_MODEL)

    s_n = jnp.transpose(state_mlstm_n[:, 0], (0, 2, 1, 3))[:, :, :, None, :]
    s_m = jnp.broadcast_to(jnp.transpose(state_mlstm_m[:, 0], (0, 2, 1))[..., None],
                           (bs, N_HEADS, 2, HEAD_DIM))
    y_s = _mixer(xs2d, mod3, lambda b: 1 + b, g1, w_pair, w_gate, head_params,
                 latent=True, n_seq=1, cps=ts // CHUNK,
                 rope=_rope_tables(ts), states=(state_ret, state_mlstm_C, s_n, s_m))
    tiles_per_seq = ts // FFN_ROWS
    out_s = _ffn(xs2d, y_s, mod3, lambda i: 1 + i // tiles_per_seq,
                 w_out_p, g2, w1, w2, gf).reshape(bs, ts, D_MODEL)

    new_n = jnp.transpose(new_n, (0, 2, 1, 3))[:, None]
    new_m = jnp.transpose(new_m[..., 0], (0, 2, 1))[:, None]
    return out_p, out_s, new_ret, new_c, new_n, new_m
```

```python
import functools

import numpy as np
import jax
import jax.numpy as jnp
from jax import lax
from jax.experimental import pallas as pl
from jax.experimental.pallas import tpu as pltpu

F32 = jnp.float32
BF16 = jnp.bfloat16

D_MODEL = 1024
N_HEADS = 4
HEAD_DIM = 128
CHUNK = 128
GRID_W = 64
D_FF = 4 * D_MODEL
EPS = 1e-6
ROPE_BASE = 10000.0
LOG2_E = 1.4426950408889634
LN_2 = 0.6931471805599453
PAIR_COLS = 8 * HEAD_DIM
N_GATE_COLS = 4 * N_HEADS
ROW_TILE = 256
FFN_ROWS = 512
CTX_SEQS_PER_STEP = 4
PROJ_SLAB_ROWS = 256
CONV_HALO = 8
VMEM_LIMIT = 60 * 1024 * 1024


def _dot(a, b):
    return jnp.dot(a, b, preferred_element_type=F32)


def _dot_nt(a, b):
    return lax.dot_general(a, b, (((1,), (1,)), ((), ())), preferred_element_type=F32)


def _dot_tn(a, b):
    return lax.dot_general(a, b, (((0,), (0,)), ((), ())), preferred_element_type=F32)


def _rms(x, g):
    return x * lax.rsqrt(jnp.mean(x * x, axis=-1, keepdims=True) + EPS) * g


def _group_norm(o, g):
    mu = jnp.mean(o, axis=-1, keepdims=True)
    c = o - mu
    var = jnp.mean(c * c, axis=-1, keepdims=True)
    return c * lax.rsqrt(var + EPS) * g


def _log_sigmoid(x):
    return jnp.minimum(x, 0.0) - jnp.log(1.0 + jnp.exp(-jnp.abs(x)))


def _sigmoid(x):
    return 1.0 / (1.0 + jnp.exp(-x))


def _split2(x):
    hi = x.astype(BF16)
    lo = (x - hi.astype(F32)).astype(BF16)
    return hi, lo


def _mod_kernel(cond_ref, w_ref, b_ref, out_ref):
    c = cond_ref[...]
    s = (c * _sigmoid(c)).astype(BF16)
    out_ref[...] = _dot(s, w_ref[...].astype(BF16)) + b_ref[...]


def _modulation(cond, w_ada, b_ada):
    n = w_ada.shape[1]
    tn = 2048
    return pl.pallas_call(
        _mod_kernel,
        out_shape=jax.ShapeDtypeStruct((cond.shape[0], n), F32),
        grid=(n // tn,),
        in_specs=[pl.BlockSpec(cond.shape, lambda j: (0, 0)),
                  pl.BlockSpec((D_MODEL, tn), lambda j: (0, j)),
                  pl.BlockSpec((1, tn), lambda j: (0, j))],
        out_specs=pl.BlockSpec((cond.shape[0], tn), lambda j: (0, j)),
        compiler_params=pltpu.CompilerParams(dimension_semantics=("arbitrary",)),
        name="adaln_mod",
    )(cond, w_ada, b_ada)


REGROUP_ROWS = 256


def _regroup_in_kernel(wt_ref, out_ref, gate_ref):
    for h in range(N_HEADS):
        for g in range(8):
            src = (g * N_HEADS + h) * HEAD_DIM
            dst = (h * 8 + g) * HEAD_DIM
            out_ref[:, dst:dst + HEAD_DIM] = wt_ref[src:src + HEAD_DIM, :].T.astype(BF16)
    n_rows = wt_ref.shape[0]
    n_gate = n_rows - N_HEADS * PAIR_COLS
    tail = wt_ref[n_rows - HEAD_DIM:n_rows, :].T
    lane = lax.broadcasted_iota(jnp.int32, tail.shape, 1)
    gate_ref[...] = jnp.where(lane >= HEAD_DIM - n_gate, tail, 0.0).astype(BF16)


def _regroup_in_proj(w_in):
    w_t = jnp.transpose(w_in[0])
    n_cols = w_t.shape[0]
    assert n_cols == N_HEADS * PAIR_COLS + N_GATE_COLS
    return pl.pallas_call(
        _regroup_in_kernel,
        out_shape=(jax.ShapeDtypeStruct((D_MODEL, N_HEADS * PAIR_COLS), BF16),
                   jax.ShapeDtypeStruct((D_MODEL, HEAD_DIM), BF16)),
        grid=(D_MODEL // REGROUP_ROWS,),
        in_specs=[pl.BlockSpec((n_cols, REGROUP_ROWS), lambda i: (0, i))],
        out_specs=(pl.BlockSpec((REGROUP_ROWS, N_HEADS * PAIR_COLS), lambda i: (i, 0)),
                   pl.BlockSpec((REGROUP_ROWS, HEAD_DIM), lambda i: (i, 0))),
        compiler_params=pltpu.CompilerParams(dimension_semantics=("arbitrary",)),
        name="regroup_w_in",
    )(w_t)


def _regroup_out_kernel(w_ref, out_ref):
    for h in range(N_HEADS):
        for g in range(2):
            src = (g * N_HEADS + h) * HEAD_DIM
            dst = (h * 2 + g) * HEAD_DIM
            out_ref[dst:dst + HEAD_DIM, :] = w_ref[src:src + HEAD_DIM, :].astype(BF16)


def _regroup_out_proj(w_out):
    cols = 2 * HEAD_DIM
    return pl.pallas_call(
        _regroup_out_kernel,
        out_shape=jax.ShapeDtypeStruct((D_MODEL, D_MODEL), BF16),
        grid=(D_MODEL // cols,),
        in_specs=[pl.BlockSpec((None, D_MODEL, cols), lambda j: (0, 0, j))],
        out_specs=pl.BlockSpec((D_MODEL, cols), lambda j: (0, j)),
        compiler_params=pltpu.CompilerParams(dimension_semantics=("arbitrary",)),
        name="regroup_w_out",
    )(w_out)


def _cast_pair_kernel(a_ref, b_ref, oa_ref, ob_ref):
    oa_ref[...] = a_ref[...].astype(BF16)
    ob_ref[...] = b_ref[...].astype(BF16)


def _cast_ffn_weights(w_ff1, w_ff2):
    n_slabs = 8
    cols = D_FF // n_slabs
    return pl.pallas_call(
        _cast_pair_kernel,
        out_shape=(jax.ShapeDtypeStruct((D_MODEL, D_FF), BF16), jax.ShapeDtypeStruct((D_FF, D_MODEL), BF16)),
        grid=(n_slabs,),
        in_specs=[pl.BlockSpec((None, D_MODEL, cols), lambda i: (0, 0, i)),
                  pl.BlockSpec((None, cols, D_MODEL), lambda i: (0, i, 0))],
        out_specs=(pl.BlockSpec((D_MODEL, cols), lambda i: (0, i)),
                   pl.BlockSpec((cols, D_MODEL), lambda i: (i, 0))),
        compiler_params=pltpu.CompilerParams(dimension_semantics=("arbitrary",)),
        name="cast_ffn_weights",
    )(w_ff1, w_ff2)


_B_END, _MAX_LWE, _M_PREV, _M_NEW, _DECAY = 0, 2, 4, 6, 8


def _mixer_kernel(*refs, latent, n_seq, cps, n_steps):
    L = CHUNK
    n_chunks = n_seq * cps
    rows_total = n_chunks * L
    seq_len = cps * L
    assert seq_len & (seq_len - 1) == 0
    if latent:
        (x_ref, mod_ref, g1_ref, w_ref, wg_ref, hp_ref, hp_prev_ref, cos_ref, sin_ref,
         sr_in, sc_in, sn_in, sm_in, y_ref, *scratch) = refs
    else:
        (x_ref, mod_ref, g1_ref, w_ref, wg_ref, hp_ref, hp_prev_ref,
         y_ref, so_ref, co_ref, no_ref, mo_ref, *scratch) = refs
    (hn_ref, rq_ref, rk_ref, rv_ref, rg_ref, mq_ref, mk_ref, mvx_ref, mog_ref, gt_ref,
     ar_ref, am_ref, logi_ref, bc_ref, skv_ref, ckv_ref, cs_ref, rc_ref, rawq_ref, rawk_ref,
     gall_ref) = scratch

    step = pl.program_id(0)
    live = step < n_steps
    head = lax.rem(jnp.minimum(step, n_steps - 1), N_HEADS)

    @pl.when(step == 0)
    def _():
        for ref in (rq_ref, am_ref, rg_ref, mog_ref):
            ref[...] = jnp.zeros_like(ref)

    @pl.when(jnp.logical_and(head == 0, live))
    def _():
        gain = g1_ref[...] * (1.0 + mod_ref[1:2, :])
        sh1 = mod_ref[0:1, :]
        gate_bias = hp_ref[10:11, :]
        lane_t = lax.broadcasted_iota(jnp.int32, (ROW_TILE, HEAD_DIM), 1)
        is_forget = jnp.bitwise_and(lane_t, 4) == 4

        def body(i, carry):
            r = pl.ds(pl.multiple_of(i * ROW_TILE, ROW_TILE), ROW_TILE)
            xr = x_ref[r, :]
            inv = lax.rsqrt(jnp.mean(xr * xr, axis=-1, keepdims=True) + EPS)
            hn_t = (xr * inv * gain + sh1).astype(BF16)
            hn_ref[r, :] = hn_t
            gates = _dot(hn_t, wg_ref[...]) + gate_bias
            gall_ref[r, :] = jnp.where(is_forget, _log_sigmoid(gates), gates) * LOG2_E
            return carry

        lax.fori_loop(0, rows_total // ROW_TILE, body, 0)

    hp = hp_ref[...]
    scale = HEAD_DIM ** -0.5

    r_i = lax.broadcasted_iota(jnp.int32, (L, L), 0)
    s_i = lax.broadcasted_iota(jnp.int32, (L, L), 1)
    r_f = r_i.astype(F32)
    s_f = s_i.astype(F32)
    lg_f = _log_sigmoid(hp[8:9])
    lg_b = _log_sigmoid(hp[9:10])
    rc_ref[0] = (jnp.where(r_i >= s_i, jnp.exp(lg_f * jnp.where(r_i >= s_i, r_f - s_f, 0.0)), 0.0)
                 + jnp.where(s_i >= r_i, jnp.exp(lg_b * jnp.where(s_i >= r_i, s_f - r_f, 0.0)), 0.0))
    rc_ref[1] = jnp.exp(lg_f * (r_f + 1.0))
    rc_ref[2] = jnp.exp(lg_b * (L - r_f))
    rc_ref[3] = jnp.exp(lg_f * (L - 1.0 - r_f))
    rc_ref[4] = jnp.exp(lg_b * r_f)
    chunk_decay = (jnp.exp(lg_f * float(L)), jnp.exp(lg_b * float(L)))

    tri = (jnp.where(r_i >= s_i, 1.0, 0.0).astype(BF16),
           jnp.where(s_i >= r_i, 1.0, 0.0).astype(BF16))

    for raw_ref in (rawq_ref, rawk_ref):
        raw_ref[0:CONV_HALO, :] = jnp.zeros((CONV_HALO, HEAD_DIM), F32)
        raw_ref[rows_total + CONV_HALO:rows_total + 2 * CONV_HALO, :] = jnp.zeros((CONV_HALO, HEAD_DIM), F32)

    def rows(c):
        return slice(c * L, (c + 1) * L)

    def gate_col(c, lane):
        g = gt_ref[rows(c), :]
        return jnp.broadcast_to(g[:, lane:lane + 1], (L, HEAD_DIM))

    hp_prev = hp_prev_ref[...]

    def finish_rows(r):
        rg = rg_ref[r, :]
        ret_y = _group_norm(rq_ref[r, :], hp_prev[6:7]) * (rg * _sigmoid(rg))
        y_ref[r, 0:128] = ret_y.astype(BF16)
        ml_y = _group_norm(am_ref[r, :], hp_prev[7:8]) * _sigmoid(mog_ref[r, :])
        y_ref[r, 128:256] = ml_y.astype(BF16)

    def project(blk, n_rows):
        hn = hn_ref[blk, :]
        pr = _dot(hn, w_ref[:, 0:256])
        q = pr[:, 0:128]
        k = pr[:, 128:256] * scale
        if latent:
            cos2 = cos_ref[blk, :]
            sin2 = sin_ref[blk, :]
            q = q * cos2 + pltpu.roll(q, HEAD_DIM // 2, axis=1) * sin2
            k = k * cos2 + pltpu.roll(k, HEAD_DIM // 2, axis=1) * sin2
        rq_ref[blk, :] = q
        rk_ref[blk, :] = k

        pv = _dot(hn, w_ref[:, 256:512])
        rv_ref[blk, :] = pv[:, 0:128].astype(BF16)
        rg_ref[blk, :] = pv[:, 128:256]

        pm = _dot(hn, w_ref[:, 512:768])
        halo_blk = slice(blk.start + CONV_HALO, blk.stop + CONV_HALO)
        rawq_ref[halo_blk, :] = pm[:, 0:128]
        rawk_ref[halo_blk, :] = pm[:, 128:256]

        po = _dot(hn, w_ref[:, 768:1024])
        mvx_ref[blk, 0:128] = po[:, 0:128].astype(BF16)
        mvx_ref[blk, 128:256] = jnp.ones((n_rows, HEAD_DIM), BF16)
        mog_ref[blk, :] = po[:, 128:256]

        gt_ref[blk, :] = pltpu.roll(gall_ref[blk, :], N_GATE_COLS - head, axis=1)

    def conv_silu(blk, n_rows):
        sub = lax.broadcasted_iota(jnp.int32, (CONV_HALO, HEAD_DIM), 0)
        starts_seq = blk.start % seq_len == 0
        ends_seq = blk.stop % seq_len == 0
        for raw_ref, dst_ref, taps, post in ((rawq_ref, mq_ref, hp[0:3], 1.0), (rawk_ref, mk_ref, hp[3:6], scale)):
            lo = blk.start + CONV_HALO
            prev = raw_ref[lo - 1:lo - 1 + n_rows, :]
            nxt = raw_ref[lo + 1:lo + 1 + n_rows, :]
            if starts_seq:
                first = jnp.where(sub == 0, 0.0, prev[0:CONV_HALO, :])
                prev = jnp.concatenate([first, prev[CONV_HALO:, :]], axis=0)
            if ends_seq:
                last = jnp.where(sub == CONV_HALO - 1, 0.0, nxt[n_rows - CONV_HALO:, :])
                nxt = jnp.concatenate([nxt[:n_rows - CONV_HALO, :], last], axis=0)
            out = prev * taps[0:1] + raw_ref[lo:lo + n_rows, :] * taps[1:2] + nxt * taps[2:3]
            out = out * _sigmoid(out)
            dst_ref[blk, :] = out if post == 1.0 else out * post

    def project_and_scan():
        slab_rows = min(seq_len, PROJ_SLAB_ROWS)
        assert seq_len % slab_rows == 0
        slabs = [slice(s0, s0 + slab_rows) for s0 in range(0, rows_total, slab_rows)]
        for i, slab in enumerate(slabs):
            finish_rows(slab)
            project(slab, slab_rows)
            if i > 0:
                conv_silu(slabs[i - 1], slab_rows)
        conv_silu(slabs[-1], slab_rows)
        pre_pass(range(n_chunks))
        scan_passes(range(n_chunks), range(n_seq))

    def pre_pass(chunks):
        for c in chunks:
            ar_ref[rows(c), :] = (_dot_nt(rq_ref[rows(c), :].astype(BF16), rk_ref[rows(c), :].astype(BF16))
                                  * rc_ref[0]).astype(BF16)
            am_ref[rows(c), :] = _dot_nt(mq_ref[rows(c), :].astype(BF16), mk_ref[rows(c), :].astype(BF16))
        for c in chunks:
            for d in range(2):
                li = gate_col(c, 8 * d)
                lf = gate_col(c, 8 * d + 4)
                strict = (r_i > s_i) if d == 0 else (r_i < s_i)
                x_mat = jnp.where(strict, lf, jnp.where(r_i == s_i, li, 0.0))
                hi, lo = _split2(x_mat)
                d_mat = _dot(tri[d], hi) + _dot(tri[d], lo)
                e = 0 if d == 0 else L - 1
                bcum = (jnp.broadcast_to(d_mat[:, e:e + 1], (L, HEAD_DIM))
                        + (lf[e:e + 1, :] - li[e:e + 1, :]))
                causal = (s_i <= r_i) if d == 0 else (s_i >= r_i)
                logi_ref[d, rows(c), :] = jnp.where(causal, d_mat, -jnp.inf)
                bc_ref[d, rows(c), :] = bcum
                b_end = bcum[L - 1:L, :] if d == 0 else bcum[0:1, :]
                cs_ref[_B_END + d, c:c + 1, :] = b_end
                cs_ref[_MAX_LWE + d, c:c + 1, :] = jnp.max(b_end - bcum + li, axis=0, keepdims=True)
        for c in chunks:
            kf = rk_ref[rows(c), :]
            vb = rv_ref[rows(c), :]
            for d in range(2):
                skv_ref[d, c] = _dot_tn((kf * rc_ref[3 + d]).astype(BF16), vb)

    def scan_passes(chunks, seqs):
        m_final = {}
        for s in seqs:
            for d in range(2):
                m = sm_in[d:d + 1, :] * LOG2_E if latent else jnp.zeros((1, HEAD_DIM), F32)
                order = range(cps) if d == 0 else range(cps - 1, -1, -1)
                for j in order:
                    c = s * cps + j
                    cs_ref[_M_PREV + d, c:c + 1, :] = m
                    b_end = cs_ref[_B_END + d, c:c + 1, :]
                    m_new = jnp.maximum(b_end + m, cs_ref[_MAX_LWE + d, c:c + 1, :])
                    cs_ref[_M_NEW + d, c:c + 1, :] = m_new
                    cs_ref[_DECAY + d, c:c + 1, :] = jnp.exp2(b_end + m - m_new)
                    m = m_new
                m_final[(s, d)] = m

        for c in chunks:
            kf = mk_ref[rows(c), :]
            vx = mvx_ref[rows(c), :]
            for d in range(2):
                log_w_end = cs_ref[_B_END + d, c:c + 1, :] - bc_ref[d, rows(c), :] + gate_col(c, 8 * d)
                w_end = jnp.exp2(log_w_end - cs_ref[_M_NEW + d, c:c + 1, :])
                ckv_ref[d, c] = _dot_tn((kf * w_end).astype(BF16), vx)

        for s in seqs:
            for d in range(2):
                if latent:
                    s_state = sr_in[d]
                    n_rep = jnp.broadcast_to(sn_in[d], (HEAD_DIM, HEAD_DIM)).T
                    c_state = jnp.concatenate([sc_in[d], n_rep], axis=1)
                else:
                    s_state = jnp.zeros((HEAD_DIM, HEAD_DIM), F32)
                    c_state = jnp.zeros((HEAD_DIM, 2 * HEAD_DIM), F32)
                order = range(cps) if d == 0 else range(cps - 1, -1, -1)
                for j in order:
                    c = s * cps + j
                    inc = skv_ref[d, c]
                    skv_ref[d, c] = s_state
                    s_state = s_state * chunk_decay[d] + inc
                    inc = ckv_ref[d, c]
                    ckv_ref[d, c] = c_state
                    decay = cs_ref[_DECAY + d, c:c + 1, :]
                    c_state = c_state * jnp.concatenate([decay, decay], axis=1) + inc
                if not latent:
                    so_ref[s, d] = s_state
                    co_ref[s, d] = c_state[:, 0:128]
                    no_ref[s, d:d + 1, :] = c_state[:, 128:256].T[0:1, :]
                    mo_ref[s, d:d + 1, :] = m_final[(s, d)] * LN_2

        for c in chunks:
            qf = rq_ref[rows(c), :]
            lhs = jnp.concatenate([ar_ref[rows(c), :], (qf * rc_ref[1]).astype(BF16),
                                   (qf * rc_ref[2]).astype(BF16)], axis=1)
            rhs = jnp.concatenate([rv_ref[rows(c), :], skv_ref[0, c].astype(BF16),
                                   skv_ref[1, c].astype(BF16)], axis=0)
            rq_ref[rows(c), :] = _dot(lhs, rhs)
            qf = mq_ref[rows(c), :]
            a_mat = am_ref[rows(c), :]
            vx = mvx_ref[rows(c), :]
            h_sum = None
            for d in range(2):
                log_intra = logi_ref[d, rows(c), :]
                log_inter = bc_ref[d, rows(c), :] + cs_ref[_M_PREV + d, c:c + 1, :]
                m_t = jnp.maximum(log_inter, jnp.max(log_intra, axis=1, keepdims=True))
                w_inter = jnp.exp2(log_inter - m_t)
                w_intra = jnp.exp2(log_intra - m_t)
                lhs = jnp.concatenate([(a_mat * w_intra).astype(BF16), (qf * w_inter).astype(BF16)], axis=1)
                rhs = jnp.concatenate([vx, ckv_ref[d, c].astype(BF16)], axis=0)
                res = _dot(lhs, rhs)
                h_dir = res[:, 0:128] / jnp.maximum(jnp.abs(res[:, 128:256]), jnp.exp2(-m_t))
                h_sum = h_dir if h_sum is None else h_sum + h_dir
            am_ref[rows(c), :] = h_sum

    pl.when(live)(project_and_scan)

    @pl.when(step == n_steps)
    def _():
        def body(i, carry):
            finish_rows(pl.ds(pl.multiple_of(i * ROW_TILE, ROW_TILE), ROW_TILE))
            return carry

        lax.fori_loop(0, rows_total // ROW_TILE, body, 0)


def _mixer(x2d, mod3, mod_row, norm1_g, w_pair, w_gate, head_params, *, latent, n_seq, cps,
           rope=None, states=None):
    n_tok = x2d.shape[0]
    rows_blk = n_seq * cps * CHUNK
    n_blk = n_tok // rows_blk
    n_chunks = n_seq * cps
    n_steps = n_blk * N_HEADS
    kern = functools.partial(_mixer_kernel, latent=latent, n_seq=n_seq, cps=cps, n_steps=n_steps)

    def cur(f):
        def index_map(j):
            item = jnp.minimum(j, n_steps - 1)
            return f(item // N_HEADS, item % N_HEADS)
        return index_map

    def prev(f):
        def index_map(j):
            item = jnp.maximum(j - 1, 0)
            return f(item // N_HEADS, item % N_HEADS)
        return index_map

    once = pl.Buffered(1)
    in_specs = [
        pl.BlockSpec((rows_blk, D_MODEL), cur(lambda b, h: (b, 0))),
        pl.BlockSpec((None, 6, D_MODEL), cur(lambda b, h: (mod_row(b), 0, 0))),
        pl.BlockSpec((1, D_MODEL), lambda j: (0, 0)),
        pl.BlockSpec((D_MODEL, PAIR_COLS), cur(lambda b, h: (0, h))),
        pl.BlockSpec((D_MODEL, HEAD_DIM), lambda j: (0, 0)),
        pl.BlockSpec((None, 16, HEAD_DIM), cur(lambda b, h: (h, 0, 0))),
        pl.BlockSpec((None, 16, HEAD_DIM), prev(lambda b, h: (h, 0, 0))),
    ]
    args = [x2d, mod3, norm1_g, w_pair, w_gate, head_params, head_params]
    y_shape = jax.ShapeDtypeStruct((n_tok, D_MODEL), BF16)
    y_spec = pl.BlockSpec((rows_blk, 2 * HEAD_DIM), prev(lambda b, h: (b, h)))
    if latent:
        assert n_seq == 1
        cos2, sin2 = rope
        s_ret, s_c, s_n, s_m = states
        in_specs += [
            pl.BlockSpec((rows_blk, HEAD_DIM), lambda j: (0, 0), pipeline_mode=once),
            pl.BlockSpec((rows_blk, HEAD_DIM), lambda j: (0, 0), pipeline_mode=once),
            pl.BlockSpec((None, None, 2, None, HEAD_DIM, HEAD_DIM), cur(lambda b, h: (b, 0, 0, h, 0, 0))),
            pl.BlockSpec((None, None, 2, None, HEAD_DIM, HEAD_DIM), cur(lambda b, h: (b, 0, 0, h, 0, 0))),
            pl.BlockSpec((None, None, 2, 1, HEAD_DIM), cur(lambda b, h: (b, h, 0, 0, 0))),
            pl.BlockSpec((None, None, 2, HEAD_DIM), cur(lambda b, h: (b, h, 0, 0))),
        ]
        args += [cos2, sin2, s_ret, s_c, s_n, s_m]
        out_shape = y_shape
        out_specs = y_spec
    else:
        bsz = n_blk * n_seq
        st = jax.ShapeDtypeStruct((bsz, 1, 2, N_HEADS, HEAD_DIM, HEAD_DIM), F32)
        vec = jax.ShapeDtypeStruct((bsz, N_HEADS, 2, HEAD_DIM), F32)
        st_spec = pl.BlockSpec((n_seq, None, 2, None, HEAD_DIM, HEAD_DIM), cur(lambda b, h: (b, 0, 0, h, 0, 0)))
        vec_spec = pl.BlockSpec((n_seq, None, 2, HEAD_DIM), cur(lambda b, h: (b, h, 0, 0)))
        out_shape = (y_shape, st, st, vec, vec)
        out_specs = (y_spec, st_spec, st_spec, vec_spec, vec_spec)
    col = lambda dt: pltpu.VMEM((rows_blk, HEAD_DIM), dt)
    scratch = [
        pltpu.VMEM((rows_blk, D_MODEL), BF16),
        col(F32), col(F32), col(BF16), col(F32),
        col(F32), col(F32),
        pltpu.VMEM((rows_blk, 2 * HEAD_DIM), BF16),
        col(F32), col(F32),
        col(BF16), col(F32),
        pltpu.VMEM((2, rows_blk, HEAD_DIM), F32),
        pltpu.VMEM((2, rows_blk, HEAD_DIM), F32),
        pltpu.VMEM((2, n_chunks, HEAD_DIM, HEAD_DIM), F32),
        pltpu.VMEM((2, n_chunks, HEAD_DIM, 2 * HEAD_DIM), F32),
        pltpu.VMEM((10, max(n_chunks, 8), HEAD_DIM), F32),
        pltpu.VMEM((5, CHUNK, CHUNK), F32),
        pltpu.VMEM((rows_blk + 2 * CONV_HALO, HEAD_DIM), F32),
        pltpu.VMEM((rows_blk + 2 * CONV_HALO, HEAD_DIM), F32),
        col(F32),
    ]
    return pl.pallas_call(
        kern,
        out_shape=out_shape,
        grid=(n_steps + 1,),
        in_specs=in_specs,
        out_specs=out_specs,
        scratch_shapes=scratch,
        compiler_params=pltpu.CompilerParams(
            dimension_semantics=("arbitrary",), vmem_limit_bytes=VMEM_LIMIT),
        name="mixer_latent" if latent else "mixer_context",
    )(*args)


def _ffn_kernel(x_ref, y_ref, mod_ref, wo_ref, g2_ref, w1_ref, w2_ref, gf_ref, out_ref):
    g1 = mod_ref[2:3, :]
    sh2 = mod_ref[3:4, :]
    sc2 = mod_ref[4:5, :]
    g2 = mod_ref[5:6, :]
    x1 = x_ref[...] + g1 * _dot(y_ref[...], wo_ref[...])
    h2 = (_rms(x1, g2_ref[...]) * (1.0 + sc2) + sh2).astype(BF16)
    f = jnp.zeros_like(x1)
    for j in range(D_FF // D_MODEL):
        cols = slice(j * D_MODEL, (j + 1) * D_MODEL)
        hid = jnp.maximum(_dot(h2, w1_ref[:, cols]), 0.0)
        f = f + _dot((hid * hid).astype(BF16), w2_ref[cols, :])
    out_ref[...] = _rms(x1 + g2 * f, gf_ref[...])


def _ffn(x2d, y2d, mod3, mod_row, w_out, norm2_g, w_ff1, w_ff2, final_g):
    n_tok = x2d.shape[0]
    const = lambda i: (0, 0)
    return pl.pallas_call(
        _ffn_kernel,
        out_shape=jax.ShapeDtypeStruct((n_tok, D_MODEL), F32),
        grid=(n_tok // FFN_ROWS,),
        in_specs=[
            pl.BlockSpec((FFN_ROWS, D_MODEL), lambda i: (i, 0)),
            pl.BlockSpec((FFN_ROWS, D_MODEL), lambda i: (i, 0)),
            pl.BlockSpec((None, 6, D_MODEL), lambda i: (mod_row(i), 0, 0)),
            pl.BlockSpec((D_MODEL, D_MODEL), const, pipeline_mode=pl.Buffered(1)),
            pl.BlockSpec((1, D_MODEL), const),
            pl.BlockSpec((D_MODEL, D_FF), const, pipeline_mode=pl.Buffered(1)),
            pl.BlockSpec((D_FF, D_MODEL), const, pipeline_mode=pl.Buffered(1)),
            pl.BlockSpec((1, D_MODEL), const),
        ],
        out_specs=pl.BlockSpec((FFN_ROWS, D_MODEL), lambda i: (i, 0)),
        compiler_params=pltpu.CompilerParams(
            dimension_semantics=("arbitrary",), vmem_limit_bytes=VMEM_LIMIT),
        name="outproj_mlp",
    )(x2d, y2d, mod3, w_out, norm2_g, w_ff1, w_ff2, final_g)


def _rope_tables(seq):
    pos = np.arange(seq)
    row = (pos // GRID_W).astype(np.float64)
    col = (pos % GRID_W).astype(np.float64)
    nf = HEAD_DIM // 4
    inv = ROPE_BASE ** (-np.arange(nf, dtype=np.float64) / nf)
    ang = np.concatenate([row[:, None] * inv, col[:, None] * inv], -1)
    cos = np.cos(ang)
    sin = np.sin(ang)
    cos2 = np.concatenate([cos, cos], -1).astype(np.float32)
    sin2 = np.concatenate([-sin, sin], -1).astype(np.float32)
    return jnp.asarray(cos2), jnp.asarray(sin2)


def kernel(x_prompt, x_sample, state_ret, state_mlstm_C, state_mlstm_n, state_mlstm_m, c, c_ctx,
           w_ada, b_ada, norm1_g, norm2_g, w_in, conv_w, ret_decay_logit, mlstm_gate_bias,
           ret_gn_g, mlstm_gn_g, w_out, w_ff1, w_ff2, final_g):
    assert w_ada.shape[0] == 1, "single-layer kernel"
    bp, tp, _ = x_prompt.shape
    bs, ts, _ = x_sample.shape
    assert tp % CHUNK == 0 and ts % CHUNK == 0 and bp % CTX_SEQS_PER_STEP == 0

    cond = jnp.concatenate([c_ctx[None, :], c, jnp.zeros((8 - 1 - bs, D_MODEL), F32)], 0)
    mod = _modulation(cond, w_ada[0], b_ada)
    mod3 = mod[:1 + bs].reshape(1 + bs, 6, D_MODEL)

    w_pair, w_gate = _regroup_in_proj(w_in)
    w_out_p = _regroup_out_proj(w_out)
    w1, w2 = _cast_ffn_weights(w_ff1, w_ff2)

    cw = conv_w[0]
    hp_rows = [cw[j, :512].reshape(N_HEADS, HEAD_DIM) for j in range(3)]
    hp_rows += [cw[j, 512:].reshape(N_HEADS, HEAD_DIM) for j in range(3)]
    hp_rows += [ret_gn_g[0].reshape(N_HEADS, HEAD_DIM), mlstm_gn_g[0].reshape(N_HEADS, HEAD_DIM)]
    hp_rows += [jnp.broadcast_to(ret_decay_logit[0, d][:, None], (N_HEADS, HEAD_DIM)) for d in range(2)]
    gate_bias = jnp.pad(mlstm_gate_bias[0].reshape(1, N_GATE_COLS), ((0, 0), (HEAD_DIM - N_GATE_COLS, 0)))
    hp_rows += [jnp.broadcast_to(gate_bias, (N_HEADS, HEAD_DIM))]
    hp_rows += [jnp.zeros((N_HEADS, HEAD_DIM), F32)] * (16 - len(hp_rows))
    head_params = jnp.stack(hp_rows, axis=1).astype(F32)

    g1 = norm1_g[0][None, :]
    g2 = norm2_g[0][None, :]
    gf = final_g[None, :]
    xp2d = x_prompt.reshape(bp * tp, D_MODEL)
    xs2d = x_sample.reshape(bs * ts, D_MODEL)

    y_p, new_ret, new_c, new_n, new_m = _mixer(
        xp2d, mod3, lambda b: 0, g1, w_pair, w_gate, head_params,
        latent=False, n_seq=CTX_SEQS_PER_STEP, cps=tp // CHUNK)
    out_p = _ffn(xp2d, y_p, mod3, lambda i: 0, w_out_p, g2, w1, w2, gf).reshape(bp, tp, D_MODEL)

    s_n = jnp.transpose(state_mlstm_n[:, 0], (0, 2, 1, 3))[:, :, :, None, :]
    s_m = jnp.broadcast_to(jnp.transpose(state_mlstm_m[:, 0], (0, 2, 1))[..., None],
                           (bs, N_HEADS, 2, HEAD_DIM))
    y_s = _mixer(xs2d, mod3, lambda b: 1 + b, g1, w_pair, w_gate, head_params,
                 latent=True, n_seq=1, cps=ts // CHUNK,
                 rope=_rope_tables(ts), states=(state_ret, state_mlstm_C, s_n, s_m))
    tiles_per_seq = ts // FFN_ROWS
    out_s = _ffn(xs2d, y_s, mod3, lambda i: 1 + i // tiles_per_seq,
                 w_out_p, g2, w1, w2, gf).reshape(bs, ts, D_MODEL)

    new_n = jnp.transpose(new_n, (0, 2, 1, 3))[:, None]
    new_m = jnp.transpose(new_m[..., 0], (0, 2, 1))[:, None]
    return out_p, out_s, new_ret, new_c, new_n, new_m
```

```python
import functools

import numpy as np
import jax
import jax.numpy as jnp
from jax import lax
from jax.experimental import pallas as pl
from jax.experimental.pallas import tpu as pltpu

F32 = jnp.float32
BF16 = jnp.bfloat16

D_MODEL = 1024
N_HEADS = 4
HEAD_DIM = 128
CHUNK = 128
GRID_W = 64
D_FF = 4 * D_MODEL
EPS = 1e-6
ROPE_BASE = 10000.0
LOG2_E = 1.4426950408889634
LN_2 = 0.6931471805599453
PAIR_COLS = 8 * HEAD_DIM
N_GATE_COLS = 4 * N_HEADS
ROW_TILE = 256
FFN_ROWS = 512
CTX_SEQS_PER_STEP = 4
PROJ_SLAB_ROWS = 256
CONV_HALO = 8
VMEM_LIMIT = 60 * 1024 * 1024


def _dot(a, b):
    return jnp.dot(a, b, preferred_element_type=F32)


def _dot_nt(a, b):
    return lax.dot_general(a, b, (((1,), (1,)), ((), ())), preferred_element_type=F32)


def _dot_tn(a, b):
    return lax.dot_general(a, b, (((0,), (0,)), ((), ())), preferred_element_type=F32)


def _rms(x, g):
    return x * lax.rsqrt(jnp.mean(x * x, axis=-1, keepdims=True) + EPS) * g


def _group_norm(o, g):
    mu = jnp.mean(o, axis=-1, keepdims=True)
    c = o - mu
    var = jnp.mean(c * c, axis=-1, keepdims=True)
    return c * lax.rsqrt(var + EPS) * g


def _log_sigmoid(x):
    return jnp.minimum(x, 0.0) - jnp.log(1.0 + jnp.exp(-jnp.abs(x)))


def _sigmoid(x):
    return 1.0 / (1.0 + jnp.exp(-x))


def _split2(x):
    hi = x.astype(BF16)
    lo = (x - hi.astype(F32)).astype(BF16)
    return hi, lo


PREP_STEPS = 8


def _mod_kernel(cond_ref, w_ref, b_ref, out_ref):
    c = cond_ref[...]
    s = (c * _sigmoid(c)).astype(BF16)
    out_ref[...] = _dot(s, w_ref[...].astype(BF16)) + b_ref[...]


def _regroup_in_kernel(wt_ref, out_ref, gate_ref):
    for h in range(N_HEADS):
        for g in range(8):
            src = (g * N_HEADS + h) * HEAD_DIM
            dst = (h * 8 + g) * HEAD_DIM
            out_ref[:, dst:dst + HEAD_DIM] = wt_ref[src:src + HEAD_DIM, :].T.astype(BF16)
    n_rows = wt_ref.shape[0]
    n_gate = n_rows - N_HEADS * PAIR_COLS
    tail = wt_ref[n_rows - HEAD_DIM:n_rows, :].T
    lane = lax.broadcasted_iota(jnp.int32, tail.shape, 1)
    gate_ref[...] = jnp.where(lane >= HEAD_DIM - n_gate, tail, 0.0).astype(BF16)


def _regroup_out_kernel(w_ref, out_ref):
    for h in range(N_HEADS):
        for g in range(2):
            src = (g * N_HEADS + h) * HEAD_DIM
            dst = (h * 2 + g) * HEAD_DIM
            out_ref[dst:dst + HEAD_DIM, :] = w_ref[src:src + HEAD_DIM, :].astype(BF16)


def _prepare_kernel(cond_ref, wada_ref, bada_ref, wt_ref, wout_ref, w1_ref, w2_ref,
                    mod_ref, wpair_ref, wgate_ref, woutp_ref, w1o_ref, w2o_ref):
    _mod_kernel(cond_ref, wada_ref, bada_ref, mod_ref)
    _regroup_in_kernel(wt_ref, wpair_ref, wgate_ref)
    _regroup_out_kernel(wout_ref, woutp_ref)
    w1o_ref[...] = w1_ref[...].astype(BF16)
    w2o_ref[...] = w2_ref[...].astype(BF16)


def _prepare(cond, w_ada, b_ada, w_in, w_out, w_ff1, w_ff2):
    w_t = jnp.transpose(w_in[0])
    n_cols = w_t.shape[0]
    assert n_cols == N_HEADS * PAIR_COLS + N_GATE_COLS
    n_mod = w_ada.shape[-1]
    k_rows = D_MODEL // PREP_STEPS
    m_cols = n_mod // PREP_STEPS
    f_cols = D_FF // PREP_STEPS
    bf = lambda shape: jax.ShapeDtypeStruct(shape, BF16)
    return pl.pallas_call(
        _prepare_kernel,
        out_shape=(jax.ShapeDtypeStruct((cond.shape[0], n_mod), F32),
                   bf((D_MODEL, N_HEADS * PAIR_COLS)), bf((D_MODEL, HEAD_DIM)),
                   bf((D_MODEL, D_MODEL)), bf((D_MODEL, D_FF)), bf((D_FF, D_MODEL))),
        grid=(PREP_STEPS,),
        in_specs=[pl.BlockSpec(cond.shape, lambda i: (0, 0)),
                  pl.BlockSpec((None, D_MODEL, m_cols), lambda i: (0, 0, i)),
                  pl.BlockSpec((1, m_cols), lambda i: (0, i)),
                  pl.BlockSpec((n_cols, k_rows), lambda i: (0, i)),
                  pl.BlockSpec((None, D_MODEL, k_rows), lambda i: (0, 0, i)),
                  pl.BlockSpec((None, D_MODEL, f_cols), lambda i: (0, 0, i)),
                  pl.BlockSpec((None, f_cols, D_MODEL), lambda i: (0, i, 0))],
        out_specs=(pl.BlockSpec((cond.shape[0], m_cols), lambda i: (0, i)),
                   pl.BlockSpec((k_rows, N_HEADS * PAIR_COLS), lambda i: (i, 0)),
                   pl.BlockSpec((k_rows, HEAD_DIM), lambda i: (i, 0)),
                   pl.BlockSpec((D_MODEL, k_rows), lambda i: (0, i)),
                   pl.BlockSpec((D_MODEL, f_cols), lambda i: (0, i)),
                   pl.BlockSpec((f_cols, D_MODEL), lambda i: (i, 0))),
        compiler_params=pltpu.CompilerParams(
            dimension_semantics=("arbitrary",), vmem_limit_bytes=VMEM_LIMIT),
        name="prepare_weights",
    )(cond, w_ada, b_ada, w_t, w_out, w_ff1, w_ff2)


_B_END, _MAX_LWE, _M_PREV, _M_NEW, _DECAY = 0, 2, 4, 6, 8


def _mixer_kernel(*refs, latent, n_seq, cps, n_steps):
    L = CHUNK
    n_chunks = n_seq * cps
    rows_total = n_chunks * L
    seq_len = cps * L
    assert seq_len & (seq_len - 1) == 0
    if latent:
        (x_ref, mod_ref, g1_ref, w_ref, wg_ref, hp_ref, hp_prev_ref, cos_ref, sin_ref,
         sr_in, sc_in, sn_in, sm_in, y_ref, *scratch) = refs
    else:
        (x_ref, mod_ref, g1_ref, w_ref, wg_ref, hp_ref, hp_prev_ref,
         y_ref, so_ref, co_ref, no_ref, mo_ref, *scratch) = refs
    (hn_ref, rq_ref, rk_ref, rv_ref, rg_ref, mq_ref, mk_ref, mvx_ref, mog_ref, gt_ref,
     ar_ref, am_ref, logi_ref, bc_ref, skv_ref, ckv_ref, cs_ref, rc_ref, rawq_ref, rawk_ref,
     gall_ref) = scratch

    step = pl.program_id(0)
    live = step < n_steps
    head = lax.rem(jnp.minimum(step, n_steps - 1), N_HEADS)

    @pl.when(step == 0)
    def _():
        for ref in (rq_ref, am_ref, rg_ref, mog_ref):
            ref[...] = jnp.zeros_like(ref)

    @pl.when(jnp.logical_and(head == 0, live))
    def _():
        gain = g1_ref[...] * (1.0 + mod_ref[1:2, :])
        sh1 = mod_ref[0:1, :]
        gate_bias = hp_ref[10:11, :]
        lane_t = lax.broadcasted_iota(jnp.int32, (ROW_TILE, HEAD_DIM), 1)
        is_forget = jnp.bitwise_and(lane_t, 4) == 4

        def body(i, carry):
            r = pl.ds(pl.multiple_of(i * ROW_TILE, ROW_TILE), ROW_TILE)
            xr = x_ref[r, :]
            inv = lax.rsqrt(jnp.mean(xr * xr, axis=-1, keepdims=True) + EPS)
            hn_t = (xr * inv * gain + sh1).astype(BF16)
            hn_ref[r, :] = hn_t
            gates = _dot(hn_t, wg_ref[...]) + gate_bias
            gall_ref[r, :] = jnp.where(is_forget, _log_sigmoid(gates), gates) * LOG2_E
            return carry

        lax.fori_loop(0, rows_total // ROW_TILE, body, 0)

    hp = hp_ref[...]
    scale = HEAD_DIM ** -0.5

    r_i = lax.broadcasted_iota(jnp.int32, (L, L), 0)
    s_i = lax.broadcasted_iota(jnp.int32, (L, L), 1)
    r_f = r_i.astype(F32)
    s_f = s_i.astype(F32)
    lg_f = _log_sigmoid(hp[8:9])
    lg_b = _log_sigmoid(hp[9:10])
    rc_ref[0] = (jnp.where(r_i >= s_i, jnp.exp(lg_f * jnp.where(r_i >= s_i, r_f - s_f, 0.0)), 0.0)
                 + jnp.where(s_i >= r_i, jnp.exp(lg_b * jnp.where(s_i >= r_i, s_f - r_f, 0.0)), 0.0))
    rc_ref[1] = jnp.exp(lg_f * (r_f + 1.0))
    rc_ref[2] = jnp.exp(lg_b * (L - r_f))
    rc_ref[3] = jnp.exp(lg_f * (L - 1.0 - r_f))
    rc_ref[4] = jnp.exp(lg_b * r_f)
    chunk_decay = (jnp.exp(lg_f * float(L)), jnp.exp(lg_b * float(L)))

    tri = (jnp.where(r_i >= s_i, 1.0, 0.0).astype(BF16),
           jnp.where(s_i >= r_i, 1.0, 0.0).astype(BF16))

    for raw_ref in (rawq_ref, rawk_ref):
        raw_ref[0:CONV_HALO, :] = jnp.zeros((CONV_HALO, HEAD_DIM), F32)
        raw_ref[rows_total + CONV_HALO:rows_total + 2 * CONV_HALO, :] = jnp.zeros((CONV_HALO, HEAD_DIM), F32)

    def rows(c):
        return slice(c * L, (c + 1) * L)

    def gate_col(c, lane):
        g = gt_ref[rows(c), :]
        return jnp.broadcast_to(g[:, lane:lane + 1], (L, HEAD_DIM))

    hp_prev = hp_prev_ref[...]

    def finish_rows(r):
        rg = rg_ref[r, :]
        ret_y = _group_norm(rq_ref[r, :], hp_prev[6:7]) * (rg * _sigmoid(rg))
        y_ref[r, 0:128] = ret_y.astype(BF16)
        ml_y = _group_norm(am_ref[r, :], hp_prev[7:8]) * _sigmoid(mog_ref[r, :])
        y_ref[r, 128:256] = ml_y.astype(BF16)

    def project(blk, n_rows):
        hn = hn_ref[blk, :]
        pr = _dot(hn, w_ref[:, 0:256])
        q = pr[:, 0:128]
        k = pr[:, 128:256] * scale
        if latent:
            cos2 = cos_ref[blk, :]
            sin2 = sin_ref[blk, :]
            q = q * cos2 + pltpu.roll(q, HEAD_DIM // 2, axis=1) * sin2
            k = k * cos2 + pltpu.roll(k, HEAD_DIM // 2, axis=1) * sin2
        rq_ref[blk, :] = q
        rk_ref[blk, :] = k

        pv = _dot(hn, w_ref[:, 256:512])
        rv_ref[blk, :] = pv[:, 0:128].astype(BF16)
        rg_ref[blk, :] = pv[:, 128:256]

        pm = _dot(hn, w_ref[:, 512:768])
        halo_blk = slice(blk.start + CONV_HALO, blk.stop + CONV_HALO)
        rawq_ref[halo_blk, :] = pm[:, 0:128]
        rawk_ref[halo_blk, :] = pm[:, 128:256]

        po = _dot(hn, w_ref[:, 768:1024])
        mvx_ref[blk, 0:128] = po[:, 0:128].astype(BF16)
        mvx_ref[blk, 128:256] = jnp.ones((n_rows, HEAD_DIM), BF16)
        mog_ref[blk, :] = po[:, 128:256]

        gt_ref[blk, :] = pltpu.roll(gall_ref[blk, :], N_GATE_COLS - head, axis=1)

    def conv_silu(blk, n_rows):
        sub = lax.broadcasted_iota(jnp.int32, (CONV_HALO, HEAD_DIM), 0)
        starts_seq = blk.start % seq_len == 0
        ends_seq = blk.stop % seq_len == 0
        for raw_ref, dst_ref, taps, post in ((rawq_ref, mq_ref, hp[0:3], 1.0), (rawk_ref, mk_ref, hp[3:6], scale)):
            lo = blk.start + CONV_HALO
            prev = raw_ref[lo - 1:lo - 1 + n_rows, :]
            nxt = raw_ref[lo + 1:lo + 1 + n_rows, :]
            if starts_seq:
                first = jnp.where(sub == 0, 0.0, prev[0:CONV_HALO, :])
                prev = jnp.concatenate([first, prev[CONV_HALO:, :]], axis=0)
            if ends_seq:
                last = jnp.where(sub == CONV_HALO - 1, 0.0, nxt[n_rows - CONV_HALO:, :])
                nxt = jnp.concatenate([nxt[:n_rows - CONV_HALO, :], last], axis=0)
            out = prev * taps[0:1] + raw_ref[lo:lo + n_rows, :] * taps[1:2] + nxt * taps[2:3]
            out = out * _sigmoid(out)
            dst_ref[blk, :] = out if post == 1.0 else out * post

    def project_and_scan():
        slab_rows = min(seq_len, PROJ_SLAB_ROWS)
        assert seq_len % slab_rows == 0
        slabs = [slice(s0, s0 + slab_rows) for s0 in range(0, rows_total, slab_rows)]
        for i, slab in enumerate(slabs):
            finish_rows(slab)
            project(slab, slab_rows)
            if i > 0:
                conv_silu(slabs[i - 1], slab_rows)
        conv_silu(slabs[-1], slab_rows)
        pre_pass(range(n_chunks))
        scan_passes(range(n_chunks), range(n_seq))

    def pre_pass(chunks):
        for c in chunks:
            ar_ref[rows(c), :] = (_dot_nt(rq_ref[rows(c), :].astype(BF16), rk_ref[rows(c), :].astype(BF16))
                                  * rc_ref[0]).astype(BF16)
            am_ref[rows(c), :] = _dot_nt(mq_ref[rows(c), :].astype(BF16), mk_ref[rows(c), :].astype(BF16))
        for c in chunks:
            for d in range(2):
                li = gate_col(c, 8 * d)
                lf = gate_col(c, 8 * d + 4)
                strict = (r_i > s_i) if d == 0 else (r_i < s_i)
                x_mat = jnp.where(strict, lf, jnp.where(r_i == s_i, li, 0.0))
                hi, lo = _split2(x_mat)
                d_mat = _dot(tri[d], hi) + _dot(tri[d], lo)
                e = 0 if d == 0 else L - 1
                bcum = (jnp.broadcast_to(d_mat[:, e:e + 1], (L, HEAD_DIM))
                        + (lf[e:e + 1, :] - li[e:e + 1, :]))
                causal = (s_i <= r_i) if d == 0 else (s_i >= r_i)
                logi_ref[d, rows(c), :] = jnp.where(causal, d_mat, -jnp.inf)
                bc_ref[d, rows(c), :] = bcum
                b_end = bcum[L - 1:L, :] if d == 0 else bcum[0:1, :]
                cs_ref[_B_END + d, c:c + 1, :] = b_end
                cs_ref[_MAX_LWE + d, c:c + 1, :] = jnp.max(b_end - bcum + li, axis=0, keepdims=True)
        for c in chunks:
            kf = rk_ref[rows(c), :]
            vb = rv_ref[rows(c), :]
            for d in range(2):
                skv_ref[d, c] = _dot_tn((kf * rc_ref[3 + d]).astype(BF16), vb)

    def scan_passes(chunks, seqs):
        m_final = {}
        for s in seqs:
            for d in range(2):
                m = sm_in[d:d + 1, :] * LOG2_E if latent else jnp.zeros((1, HEAD_DIM), F32)
                order = range(cps) if d == 0 else range(cps - 1, -1, -1)
                for j in order:
                    c = s * cps + j
                    cs_ref[_M_PREV + d, c:c + 1, :] = m
                    b_end = cs_ref[_B_END + d, c:c + 1, :]
                    m_new = jnp.maximum(b_end + m, cs_ref[_MAX_LWE + d, c:c + 1, :])
                    cs_ref[_M_NEW + d, c:c + 1, :] = m_new
                    cs_ref[_DECAY + d, c:c + 1, :] = jnp.exp2(b_end + m - m_new)
                    m = m_new
                m_final[(s, d)] = m

        for c in chunks:
            kf = mk_ref[rows(c), :]
            vx = mvx_ref[rows(c), :]
            for d in range(2):
                log_w_end = cs_ref[_B_END + d, c:c + 1, :] - bc_ref[d, rows(c), :] + gate_col(c, 8 * d)
                w_end = jnp.exp2(log_w_end - cs_ref[_M_NEW + d, c:c + 1, :])
                ckv_ref[d, c] = _dot_tn((kf * w_end).astype(BF16), vx)

        for s in seqs:
            for d in range(2):
                if latent:
                    s_state = sr_in[d]
                    n_rep = jnp.broadcast_to(sn_in[d], (HEAD_DIM, HEAD_DIM)).T
                    c_state = jnp.concatenate([sc_in[d], n_rep], axis=1)
                else:
                    s_state = jnp.zeros((HEAD_DIM, HEAD_DIM), F32)
                    c_state = jnp.zeros((HEAD_DIM, 2 * HEAD_DIM), F32)
                order = range(cps) if d == 0 else range(cps - 1, -1, -1)
                for j in order:
                    c = s * cps + j
                    inc = skv_ref[d, c]
                    skv_ref[d, c] = s_state
                    s_state = s_state * chunk_decay[d] + inc
                    inc = ckv_ref[d, c]
                    ckv_ref[d, c] = c_state
                    decay = cs_ref[_DECAY + d, c:c + 1, :]
                    c_state = c_state * jnp.concatenate([decay, decay], axis=1) + inc
                if not latent:
                    so_ref[s, d] = s_state
                    co_ref[s, d] = c_state[:, 0:128]
                    no_ref[s, d:d + 1, :] = c_state[:, 128:256].T[0:1, :]
                    mo_ref[s, d:d + 1, :] = m_final[(s, d)] * LN_2

        for c in chunks:
            qf = rq_ref[rows(c), :]
            lhs = jnp.concatenate([ar_ref[rows(c), :], (qf * rc_ref[1]).astype(BF16),
                                   (qf * rc_ref[2]).astype(BF16)], axis=1)
            rhs = jnp.concatenate([rv_ref[rows(c), :], skv_ref[0, c].astype(BF16),
                                   skv_ref[1, c].astype(BF16)], axis=0)
            rq_ref[rows(c), :] = _dot(lhs, rhs)
            qf = mq_ref[rows(c), :]
            a_mat = am_ref[rows(c), :]
            vx = mvx_ref[rows(c), :]
            h_sum = None
            for d in range(2):
                log_intra = logi_ref[d, rows(c), :]
                log_inter = bc_ref[d, rows(c), :] + cs_ref[_M_PREV + d, c:c + 1, :]
                m_t = jnp.maximum(log_inter, jnp.max(log_intra, axis=1, keepdims=True))
                w_inter = jnp.exp2(log_inter - m_t)
                w_intra = jnp.exp2(log_intra - m_t)
                lhs = jnp.concatenate([(a_mat * w_intra).astype(BF16), (qf * w_inter).astype(BF16)], axis=1)
                rhs = jnp.concatenate([vx, ckv_ref[d, c].astype(BF16)], axis=0)
                res = _dot(lhs, rhs)
                h_dir = res[:, 0:128] / jnp.maximum(jnp.abs(res[:, 128:256]), jnp.exp2(-m_t))
                h_sum = h_dir if h_sum is None else h_sum + h_dir
            am_ref[rows(c), :] = h_sum

    pl.when(live)(project_and_scan)

    @pl.when(step == n_steps)
    def _():
        def body(i, carry):
            finish_rows(pl.ds(pl.multiple_of(i * ROW_TILE, ROW_TILE), ROW_TILE))
            return carry

        lax.fori_loop(0, rows_total // ROW_TILE, body, 0)


def _mixer(x2d, mod3, mod_row, norm1_g, w_pair, w_gate, head_params, *, latent, n_seq, cps,
           rope=None, states=None):
    n_tok = x2d.shape[0]
    rows_blk = n_seq * cps * CHUNK
    n_blk = n_tok // rows_blk
    n_chunks = n_seq * cps
    n_steps = n_blk * N_HEADS
    kern = functools.partial(_mixer_kernel, latent=latent, n_seq=n_seq, cps=cps, n_steps=n_steps)

    def cur(f):
        def index_map(j):
            item = jnp.minimum(j, n_steps - 1)
            return f(item // N_HEADS, item % N_HEADS)
        return index_map

    def prev(f):
        def index_map(j):
            item = jnp.maximum(j - 1, 0)
            return f(item // N_HEADS, item % N_HEADS)
        return index_map

    once = pl.Buffered(1)
    in_specs = [
        pl.BlockSpec((rows_blk, D_MODEL), cur(lambda b, h: (b, 0))),
        pl.BlockSpec((None, 6, D_MODEL), cur(lambda b, h: (mod_row(b), 0, 0))),
        pl.BlockSpec((1, D_MODEL), lambda j: (0, 0)),
        pl.BlockSpec((D_MODEL, PAIR_COLS), cur(lambda b, h: (0, h))),
        pl.BlockSpec((D_MODEL, HEAD_DIM), lambda j: (0, 0)),
        pl.BlockSpec((None, 16, HEAD_DIM), cur(lambda b, h: (h, 0, 0))),
        pl.BlockSpec((None, 16, HEAD_DIM), prev(lambda b, h: (h, 0, 0))),
    ]
    args = [x2d, mod3, norm1_g, w_pair, w_gate, head_params, head_params]
    y_shape = jax.ShapeDtypeStruct((n_tok, D_MODEL), BF16)
    y_spec = pl.BlockSpec((rows_blk, 2 * HEAD_DIM), prev(lambda b, h: (b, h)))
    if latent:
        assert n_seq == 1
        cos2, sin2 = rope
        s_ret, s_c, s_n, s_m = states
        in_specs += [
            pl.BlockSpec((rows_blk, HEAD_DIM), lambda j: (0, 0), pipeline_mode=once),
            pl.BlockSpec((rows_blk, HEAD_DIM), lambda j: (0, 0), pipeline_mode=once),
            pl.BlockSpec((None, None, 2, None, HEAD_DIM, HEAD_DIM), cur(lambda b, h: (b, 0, 0, h, 0, 0))),
            pl.BlockSpec((None, None, 2, None, HEAD_DIM, HEAD_DIM), cur(lambda b, h: (b, 0, 0, h, 0, 0))),
            pl.BlockSpec((None, None, 2, 1, HEAD_DIM), cur(lambda b, h: (b, h, 0, 0, 0))),
            pl.BlockSpec((None, None, 2, HEAD_DIM), cur(lambda b, h: (b, h, 0, 0))),
        ]
        args += [cos2, sin2, s_ret, s_c, s_n, s_m]
        out_shape = y_shape
        out_specs = y_spec
    else:
        bsz = n_blk * n_seq
        st = jax.ShapeDtypeStruct((bsz, 1, 2, N_HEADS, HEAD_DIM, HEAD_DIM), F32)
        vec = jax.ShapeDtypeStruct((bsz, N_HEADS, 2, HEAD_DIM), F32)
        st_spec = pl.BlockSpec((n_seq, None, 2, None, HEAD_DIM, HEAD_DIM), cur(lambda b, h: (b, 0, 0, h, 0, 0)))
        vec_spec = pl.BlockSpec((n_seq, None, 2, HEAD_DIM), cur(lambda b, h: (b, h, 0, 0)))
        out_shape = (y_shape, st, st, vec, vec)
        out_specs = (y_spec, st_spec, st_spec, vec_spec, vec_spec)
    col = lambda dt: pltpu.VMEM((rows_blk, HEAD_DIM), dt)
    scratch = [
        pltpu.VMEM((rows_blk, D_MODEL), BF16),
        col(F32), col(F32), col(BF16), col(F32),
        col(F32), col(F32),
        pltpu.VMEM((rows_blk, 2 * HEAD_DIM), BF16),
        col(F32), col(F32),
        col(BF16), col(F32),
        pltpu.VMEM((2, rows_blk, HEAD_DIM), F32),
        pltpu.VMEM((2, rows_blk, HEAD_DIM), F32),
        pltpu.VMEM((2, n_chunks, HEAD_DIM, HEAD_DIM), F32),
        pltpu.VMEM((2, n_chunks, HEAD_DIM, 2 * HEAD_DIM), F32),
        pltpu.VMEM((10, max(n_chunks, 8), HEAD_DIM), F32),
        pltpu.VMEM((5, CHUNK, CHUNK), F32),
        pltpu.VMEM((rows_blk + 2 * CONV_HALO, HEAD_DIM), F32),
        pltpu.VMEM((rows_blk + 2 * CONV_HALO, HEAD_DIM), F32),
        col(F32),
    ]
    return pl.pallas_call(
        kern,
        out_shape=out_shape,
        grid=(n_steps + 1,),
        in_specs=in_specs,
        out_specs=out_specs,
        scratch_shapes=scratch,
        compiler_params=pltpu.CompilerParams(
            dimension_semantics=("arbitrary",), vmem_limit_bytes=VMEM_LIMIT),
        name="mixer_latent" if latent else "mixer_context",
    )(*args)


def _ffn_kernel(x_ref, y_ref, mod_ref, wo_ref, g2_ref, w1_ref, w2_ref, gf_ref, out_ref):
    g1 = mod_ref[2:3, :]
    sh2 = mod_ref[3:4, :]
    sc2 = mod_ref[4:5, :]
    g2 = mod_ref[5:6, :]
    x1 = x_ref[...] + g1 * _dot(y_ref[...], wo_ref[...])
    h2 = (_rms(x1, g2_ref[...]) * (1.0 + sc2) + sh2).astype(BF16)
    f = jnp.zeros_like(x1)
    for j in range(D_FF // D_MODEL):
        cols = slice(j * D_MODEL, (j + 1) * D_MODEL)
        hid = jnp.maximum(_dot(h2, w1_ref[:, cols]), 0.0)
        f = f + _dot((hid * hid).astype(BF16), w2_ref[cols, :])
    out_ref[...] = _rms(x1 + g2 * f, gf_ref[...])


def _ffn(x2d, y2d, mod3, mod_row, w_out, norm2_g, w_ff1, w_ff2, final_g):
    n_tok = x2d.shape[0]
    const = lambda i: (0, 0)
    return pl.pallas_call(
        _ffn_kernel,
        out_shape=jax.ShapeDtypeStruct((n_tok, D_MODEL), F32),
        grid=(n_tok // FFN_ROWS,),
        in_specs=[
            pl.BlockSpec((FFN_ROWS, D_MODEL), lambda i: (i, 0)),
            pl.BlockSpec((FFN_ROWS, D_MODEL), lambda i: (i, 0)),
            pl.BlockSpec((None, 6, D_MODEL), lambda i: (mod_row(i), 0, 0)),
            pl.BlockSpec((D_MODEL, D_MODEL), const, pipeline_mode=pl.Buffered(1)),
            pl.BlockSpec((1, D_MODEL), const),
            pl.BlockSpec((D_MODEL, D_FF), const, pipeline_mode=pl.Buffered(1)),
            pl.BlockSpec((D_FF, D_MODEL), const, pipeline_mode=pl.Buffered(1)),
            pl.BlockSpec((1, D_MODEL), const),
        ],
        out_specs=pl.BlockSpec((FFN_ROWS, D_MODEL), lambda i: (i, 0)),
        compiler_params=pltpu.CompilerParams(
            dimension_semantics=("arbitrary",), vmem_limit_bytes=VMEM_LIMIT),
        name="outproj_mlp",
    )(x2d, y2d, mod3, w_out, norm2_g, w_ff1, w_ff2, final_g)


def _rope_tables(seq):
    pos = np.arange(seq)
    row = (pos // GRID_W).astype(np.float64)
    col = (pos % GRID_W).astype(np.float64)
    nf = HEAD_DIM // 4
    inv = ROPE_BASE ** (-np.arange(nf, dtype=np.float64) / nf)
    ang = np.concatenate([row[:, None] * inv, col[:, None] * inv], -1)
    cos = np.cos(ang)
    sin = np.sin(ang)
    cos2 = np.concatenate([cos, cos], -1).astype(np.float32)
    sin2 = np.concatenate([-sin, sin], -1).astype(np.float32)
    return jnp.asarray(cos2), jnp.asarray(sin2)


def kernel(x_prompt, x_sample, state_ret, state_mlstm_C, state_mlstm_n, state_mlstm_m, c, c_ctx,
           w_ada, b_ada, norm1_g, norm2_g, w_in, conv_w, ret_decay_logit, mlstm_gate_bias,
           ret_gn_g, mlstm_gn_g, w_out, w_ff1, w_ff2, final_g):
    assert w_ada.shape[0] == 1, "single-layer kernel"
    bp, tp, _ = x_prompt.shape
    bs, ts, _ = x_sample.shape
    assert tp % CHUNK == 0 and ts % CHUNK == 0 and bp % CTX_SEQS_PER_STEP == 0

    cond = jnp.concatenate([c_ctx[None, :], c, jnp.zeros((8 - 1 - bs, D_MODEL), F32)], 0)
    mod, w_pair, w_gate, w_out_p, w1, w2 = _prepare(cond, w_ada, b_ada, w_in, w_out, w_ff1, w_ff2)
    mod3 = mod[:1 + bs].reshape(1 + bs, 6, D_MODEL)


    cw = conv_w[0]
    hp_rows = [cw[j, :512].reshape(N_HEADS, HEAD_DIM) for j in range(3)]
    hp_rows += [cw[j, 512:].reshape(N_HEADS, HEAD_DIM) for j in range(3)]
    hp_rows += [ret_gn_g[0].reshape(N_HEADS, HEAD_DIM), mlstm_gn_g[0].reshape(N_HEADS, HEAD_DIM)]
    hp_rows += [jnp.broadcast_to(ret_decay_logit[0, d][:, None], (N_HEADS, HEAD_DIM)) for d in range(2)]
    gate_bias = jnp.pad(mlstm_gate_bias[0].reshape(1, N_GATE_COLS), ((0, 0), (HEAD_DIM - N_GATE_COLS, 0)))
    hp_rows += [jnp.broadcast_to(gate_bias, (N_HEADS, HEAD_DIM))]
    hp_rows += [jnp.zeros((N_HEADS, HEAD_DIM), F32)] * (16 - len(hp_rows))
    head_params = jnp.stack(hp_rows, axis=1).astype(F32)

    g1 = norm1_g[0][None, :]
    g2 = norm2_g[0][None, :]
    gf = final_g[None, :]
    xp2d = x_prompt.reshape(bp * tp, D_MODEL)
    xs2d = x_sample.reshape(bs * ts, D_MODEL)

    y_p, new_ret, new_c, new_n, new_m = _mixer(
        xp2d, mod3, lambda b: 0, g1, w_pair, w_gate, head_params,
        latent=False, n_seq=CTX_SEQS_PER_STEP, cps=tp // CHUNK)
    out_p = _ffn(xp2d, y_p, mod3, lambda i: 0, w_out_p, g2, w1, w2, gf).reshape(bp, tp, D_MODEL)

    s_n = jnp.transpose(state_mlstm_n[:, 0], (0, 2, 1, 3))[:, :, :, None, :]
    s_m = jnp.broadcast_to(jnp.transpose(state_mlstm_m[:, 0], (0, 2, 1))[..., None],
                           (bs, N_HEADS, 2, HEAD_DIM))
    y_s = _mixer(xs2d, mod3, lambda b: 1 + b, g1, w_pair, w_gate, head_params,
                 latent=True, n_seq=1, cps=ts // CHUNK,
                 rope=_rope_tables(ts), states=(state_ret, state_mlstm_C, s_n, s_m))
    tiles_per_seq = ts // FFN_ROWS
    out_s = _ffn(xs2d, y_s, mod3, lambda i: 1 + i // tiles_per_seq,
                 w_out_p, g2, w1, w2, gf).reshape(bs, ts, D_MODEL)

    new_n = jnp.transpose(new_n, (0, 2, 1, 3))[:, None]
    new_m = jnp.transpose(new_m[..., 0], (0, 2, 1))[:, None]
    return out_p, out_s, new_ret, new_c, new_n, new_m
```

```python
import functools

import numpy as np
import jax
import jax.numpy as jnp
from jax import lax
from jax.experimental import pallas as pl
from jax.experimental.pallas import tpu as pltpu

F32 = jnp.float32
BF16 = jnp.bfloat16

D_MODEL = 1024
N_HEADS = 4
HEAD_DIM = 128
CHUNK = 128
GRID_W = 64
D_FF = 4 * D_MODEL
EPS = 1e-6
ROPE_BASE = 10000.0
LOG2_E = 1.4426950408889634
LN_2 = 0.6931471805599453
PAIR_COLS = 8 * HEAD_DIM
N_GATE_COLS = 4 * N_HEADS
ROW_TILE = 256
FFN_ROWS = 512
CTX_SEQS_PER_STEP = 4
PROJ_SLAB_ROWS = 256
CONV_HALO = 8
VMEM_LIMIT = 60 * 1024 * 1024


def _dot(a, b):
    return jnp.dot(a, b, preferred_element_type=F32)


def _dot_nt(a, b):
    return lax.dot_general(a, b, (((1,), (1,)), ((), ())), preferred_element_type=F32)


def _dot_tn(a, b):
    return lax.dot_general(a, b, (((0,), (0,)), ((), ())), preferred_element_type=F32)


def _rms(x, g):
    return x * lax.rsqrt(jnp.mean(x * x, axis=-1, keepdims=True) + EPS) * g


def _group_norm(o, g):
    mu = jnp.mean(o, axis=-1, keepdims=True)
    c = o - mu
    var = jnp.mean(c * c, axis=-1, keepdims=True)
    return c * lax.rsqrt(var + EPS) * g


def _log_sigmoid(x):
    return jnp.minimum(x, 0.0) - jnp.log(1.0 + jnp.exp(-jnp.abs(x)))


def _sigmoid(x):
    return 1.0 / (1.0 + jnp.exp(-x))


def _split2(x):
    hi = x.astype(BF16)
    lo = (x - hi.astype(F32)).astype(BF16)
    return hi, lo


PREP_STEPS = 8


def _mod_kernel(cond_ref, w_ref, b_ref, out_ref):
    c = cond_ref[...]
    s = (c * _sigmoid(c)).astype(BF16)
    out_ref[...] = _dot(s, w_ref[...].astype(BF16)) + b_ref[...]


def _regroup_in_kernel(wt_ref, out_ref, gate_ref):
    for h in range(N_HEADS):
        for g in range(8):
            src = (g * N_HEADS + h) * HEAD_DIM
            dst = (h * 8 + g) * HEAD_DIM
            out_ref[:, dst:dst + HEAD_DIM] = wt_ref[src:src + HEAD_DIM, :].T.astype(BF16)
    n_rows = wt_ref.shape[0]
    n_gate = n_rows - N_HEADS * PAIR_COLS
    tail = wt_ref[n_rows - HEAD_DIM:n_rows, :].T
    lane = lax.broadcasted_iota(jnp.int32, tail.shape, 1)
    gate_ref[...] = jnp.where(lane >= HEAD_DIM - n_gate, tail, 0.0).astype(BF16)


def _regroup_out_kernel(w_ref, out_ref):
    for h in range(N_HEADS):
        for g in range(2):
            src = (g * N_HEADS + h) * HEAD_DIM
            dst = (h * 2 + g) * HEAD_DIM
            out_ref[dst:dst + HEAD_DIM, :] = w_ref[src:src + HEAD_DIM, :].astype(BF16)


def _prepare_kernel(cond_ref, wada_ref, bada_ref, wt_ref, wout_ref, w1_ref, w2_ref,
                    mod_ref, wpair_ref, wgate_ref, woutp_ref, w1o_ref, w2o_ref):
    _mod_kernel(cond_ref, wada_ref, bada_ref, mod_ref)
    _regroup_in_kernel(wt_ref, wpair_ref, wgate_ref)
    _regroup_out_kernel(wout_ref, woutp_ref)
    w1o_ref[...] = w1_ref[...].astype(BF16)
    w2o_ref[...] = w2_ref[...].astype(BF16)


def _prepare(cond, w_ada, b_ada, w_in, w_out, w_ff1, w_ff2):
    w_t = jnp.transpose(w_in[0])
    n_cols = w_t.shape[0]
    assert n_cols == N_HEADS * PAIR_COLS + N_GATE_COLS
    n_mod = w_ada.shape[-1]
    k_rows = D_MODEL // PREP_STEPS
    m_cols = n_mod // PREP_STEPS
    f_cols = D_FF // PREP_STEPS
    bf = lambda shape: jax.ShapeDtypeStruct(shape, BF16)
    return pl.pallas_call(
        _prepare_kernel,
        out_shape=(jax.ShapeDtypeStruct((cond.shape[0], n_mod), F32),
                   bf((D_MODEL, N_HEADS * PAIR_COLS)), bf((D_MODEL, HEAD_DIM)),
                   bf((D_MODEL, D_MODEL)), bf((D_MODEL, D_FF)), bf((D_FF, D_MODEL))),
        grid=(PREP_STEPS,),
        in_specs=[pl.BlockSpec(cond.shape, lambda i: (0, 0)),
                  pl.BlockSpec((None, D_MODEL, m_cols), lambda i: (0, 0, i)),
                  pl.BlockSpec((1, m_cols), lambda i: (0, i)),
                  pl.BlockSpec((n_cols, k_rows), lambda i: (0, i)),
                  pl.BlockSpec((None, D_MODEL, k_rows), lambda i: (0, 0, i)),
                  pl.BlockSpec((None, D_MODEL, f_cols), lambda i: (0, 0, i)),
                  pl.BlockSpec((None, f_cols, D_MODEL), lambda i: (0, i, 0))],
        out_specs=(pl.BlockSpec((cond.shape[0], m_cols), lambda i: (0, i)),
                   pl.BlockSpec((k_rows, N_HEADS * PAIR_COLS), lambda i: (i, 0)),
                   pl.BlockSpec((k_rows, HEAD_DIM), lambda i: (i, 0)),
                   pl.BlockSpec((D_MODEL, k_rows), lambda i: (0, i)),
                   pl.BlockSpec((D_MODEL, f_cols), lambda i: (0, i)),
                   pl.BlockSpec((f_cols, D_MODEL), lambda i: (i, 0))),
        compiler_params=pltpu.CompilerParams(
            dimension_semantics=("arbitrary",), vmem_limit_bytes=VMEM_LIMIT),
        name="prepare_weights",
    )(cond, w_ada, b_ada, w_t, w_out, w_ff1, w_ff2)


_B_END, _MAX_LWE, _M_PREV, _M_NEW, _DECAY = 0, 2, 4, 6, 8


def _mixer_kernel(*refs, latent, n_seq, cps, n_steps):
    L = CHUNK
    n_chunks = n_seq * cps
    rows_total = n_chunks * L
    seq_len = cps * L
    assert seq_len & (seq_len - 1) == 0
    if latent:
        (x_ref, mod_ref, g1_ref, w_ref, wg_ref, hp_ref, hp_prev_ref, cos_ref, sin_ref,
         sr_in, sc_in, sn_in, sm_in, y_ref, *scratch) = refs
    else:
        (x_ref, mod_ref, g1_ref, w_ref, wg_ref, hp_ref, hp_prev_ref,
         y_ref, so_ref, co_ref, no_ref, mo_ref, *scratch) = refs
    (hn_ref, rq_ref, rk_ref, rv_ref, rg_ref, mq_ref, mk_ref, mvx_ref, mog_ref, gt_ref,
     ar_ref, am_ref, logi_ref, bc_ref, skv_ref, ckv_ref, cs_ref, rc_ref, rawq_ref, rawk_ref,
     gall_ref) = scratch

    step = pl.program_id(0)
    live = step < n_steps
    head = lax.rem(jnp.minimum(step, n_steps - 1), N_HEADS)

    @pl.when(step == 0)
    def _():
        for ref in (rq_ref, am_ref, rg_ref, mog_ref):
            ref[...] = jnp.zeros_like(ref)

    @pl.when(jnp.logical_and(head == 0, live))
    def _():
        gain = g1_ref[...] * (1.0 + mod_ref[1:2, :])
        sh1 = mod_ref[0:1, :]
        gate_bias = hp_ref[10:11, :]
        lane_t = lax.broadcasted_iota(jnp.int32, (ROW_TILE, HEAD_DIM), 1)
        is_forget = jnp.bitwise_and(lane_t, 4) == 4

        def body(i, carry):
            r = pl.ds(pl.multiple_of(i * ROW_TILE, ROW_TILE), ROW_TILE)
            xr = x_ref[r, :]
            inv = lax.rsqrt(jnp.mean(xr * xr, axis=-1, keepdims=True) + EPS)
            hn_t = (xr * inv * gain + sh1).astype(BF16)
            hn_ref[r, :] = hn_t
            gates = _dot(hn_t, wg_ref[...]) + gate_bias
            gall_ref[r, :] = jnp.where(is_forget, _log_sigmoid(gates), gates) * LOG2_E
            return carry

        lax.fori_loop(0, rows_total // ROW_TILE, body, 0, unroll=4)

    hp = hp_ref[...]
    scale = HEAD_DIM ** -0.5

    r_i = lax.broadcasted_iota(jnp.int32, (L, L), 0)
    s_i = lax.broadcasted_iota(jnp.int32, (L, L), 1)
    r_f = r_i.astype(F32)
    s_f = s_i.astype(F32)
    lg_f = _log_sigmoid(hp[8:9])
    lg_b = _log_sigmoid(hp[9:10])
    rc_ref[0] = (jnp.where(r_i >= s_i, jnp.exp(lg_f * jnp.where(r_i >= s_i, r_f - s_f, 0.0)), 0.0)
                 + jnp.where(s_i >= r_i, jnp.exp(lg_b * jnp.where(s_i >= r_i, s_f - r_f, 0.0)), 0.0))
    rc_ref[1] = jnp.exp(lg_f * (r_f + 1.0))
    rc_ref[2] = jnp.exp(lg_b * (L - r_f))
    rc_ref[3] = jnp.exp(lg_f * (L - 1.0 - r_f))
    rc_ref[4] = jnp.exp(lg_b * r_f)
    chunk_decay = (jnp.exp(lg_f * float(L)), jnp.exp(lg_b * float(L)))

    tri = (jnp.where(r_i >= s_i, 1.0, 0.0).astype(BF16),
           jnp.where(s_i >= r_i, 1.0, 0.0).astype(BF16))

    for raw_ref in (rawq_ref, rawk_ref):
        raw_ref[0:CONV_HALO, :] = jnp.zeros((CONV_HALO, HEAD_DIM), F32)
        raw_ref[rows_total + CONV_HALO:rows_total + 2 * CONV_HALO, :] = jnp.zeros((CONV_HALO, HEAD_DIM), F32)

    def rows(c):
        return slice(c * L, (c + 1) * L)

    def gate_col(c, lane):
        g = gt_ref[rows(c), :]
        return jnp.broadcast_to(g[:, lane:lane + 1], (L, HEAD_DIM))

    hp_prev = hp_prev_ref[...]

    def finish_rows(r):
        rg = rg_ref[r, :]
        ret_y = _group_norm(rq_ref[r, :], hp_prev[6:7]) * (rg * _sigmoid(rg))
        y_ref[r, 0:128] = ret_y.astype(BF16)
        ml_y = _group_norm(am_ref[r, :], hp_prev[7:8]) * _sigmoid(mog_ref[r, :])
        y_ref[r, 128:256] = ml_y.astype(BF16)

    def project(blk, n_rows):
        hn = hn_ref[blk, :]
        pr = _dot(hn, w_ref[:, 0:256])
        q = pr[:, 0:128]
        k = pr[:, 128:256] * scale
        if latent:
            cos2 = cos_ref[blk, :]
            sin2 = sin_ref[blk, :]
            q = q * cos2 + pltpu.roll(q, HEAD_DIM // 2, axis=1) * sin2
            k = k * cos2 + pltpu.roll(k, HEAD_DIM // 2, axis=1) * sin2
        rq_ref[blk, :] = q
        rk_ref[blk, :] = k

        pv = _dot(hn, w_ref[:, 256:512])
        rv_ref[blk, :] = pv[:, 0:128].astype(BF16)
        rg_ref[blk, :] = pv[:, 128:256]

        pm = _dot(hn, w_ref[:, 512:768])
        halo_blk = slice(blk.start + CONV_HALO, blk.stop + CONV_HALO)
        rawq_ref[halo_blk, :] = pm[:, 0:128]
        rawk_ref[halo_blk, :] = pm[:, 128:256]

        po = _dot(hn, w_ref[:, 768:1024])
        mvx_ref[blk, 0:128] = po[:, 0:128].astype(BF16)
        mvx_ref[blk, 128:256] = jnp.ones((n_rows, HEAD_DIM), BF16)
        mog_ref[blk, :] = po[:, 128:256]

        gt_ref[blk, :] = pltpu.roll(gall_ref[blk, :], N_GATE_COLS - head, axis=1)

    def conv_silu(blk, n_rows):
        sub = lax.broadcasted_iota(jnp.int32, (CONV_HALO, HEAD_DIM), 0)
        starts_seq = blk.start % seq_len == 0
        ends_seq = blk.stop % seq_len == 0
        for raw_ref, dst_ref, taps, post in ((rawq_ref, mq_ref, hp[0:3], 1.0), (rawk_ref, mk_ref, hp[3:6], scale)):
            lo = blk.start + CONV_HALO
            prev = raw_ref[lo - 1:lo - 1 + n_rows, :]
            nxt = raw_ref[lo + 1:lo + 1 + n_rows, :]
            if starts_seq:
                first = jnp.where(sub == 0, 0.0, prev[0:CONV_HALO, :])
                prev = jnp.concatenate([first, prev[CONV_HALO:, :]], axis=0)
            if ends_seq:
                last = jnp.where(sub == CONV_HALO - 1, 0.0, nxt[n_rows - CONV_HALO:, :])
                nxt = jnp.concatenate([nxt[:n_rows - CONV_HALO, :], last], axis=0)
            out = prev * taps[0:1] + raw_ref[lo:lo + n_rows, :] * taps[1:2] + nxt * taps[2:3]
            out = out * _sigmoid(out)
            dst_ref[blk, :] = out if post == 1.0 else out * post

    def project_and_scan():
        slab_rows = min(seq_len, PROJ_SLAB_ROWS)
        assert seq_len % slab_rows == 0
        slabs = [slice(s0, s0 + slab_rows) for s0 in range(0, rows_total, slab_rows)]
        for i, slab in enumerate(slabs):
            finish_rows(slab)
            project(slab, slab_rows)
            if i > 0:
                conv_silu(slabs[i - 1], slab_rows)
        conv_silu(slabs[-1], slab_rows)
        pre_pass(range(n_chunks))
        scan_passes(range(n_chunks), range(n_seq))

    def pre_pass(chunks):
        for c in chunks:
            ar_ref[rows(c), :] = (_dot_nt(rq_ref[rows(c), :].astype(BF16), rk_ref[rows(c), :].astype(BF16))
                                  * rc_ref[0]).astype(BF16)
            am_ref[rows(c), :] = _dot_nt(mq_ref[rows(c), :].astype(BF16), mk_ref[rows(c), :].astype(BF16))
        for c in chunks:
            for d in range(2):
                li = gate_col(c, 8 * d)
                lf = gate_col(c, 8 * d + 4)
                strict = (r_i > s_i) if d == 0 else (r_i < s_i)
                x_mat = jnp.where(strict, lf, jnp.where(r_i == s_i, li, 0.0))
                hi, lo = _split2(x_mat)
                d_mat = _dot(tri[d], hi) + _dot(tri[d], lo)
                e = 0 if d == 0 else L - 1
                bcum = (jnp.broadcast_to(d_mat[:, e:e + 1], (L, HEAD_DIM))
                        + (lf[e:e + 1, :] - li[e:e + 1, :]))
                causal = (s_i <= r_i) if d == 0 else (s_i >= r_i)
                logi_ref[d, rows(c), :] = jnp.where(causal, d_mat, -jnp.inf)
                bc_ref[d, rows(c), :] = bcum
                b_end = bcum[L - 1:L, :] if d == 0 else bcum[0:1, :]
                cs_ref[_B_END + d, c:c + 1, :] = b_end
                cs_ref[_MAX_LWE + d, c:c + 1, :] = jnp.max(b_end - bcum + li, axis=0, keepdims=True)
        for c in chunks:
            kf = rk_ref[rows(c), :]
            vb = rv_ref[rows(c), :]
            for d in range(2):
                skv_ref[d, c] = _dot_tn((kf * rc_ref[3 + d]).astype(BF16), vb)

    def scan_passes(chunks, seqs):
        m_final = {}
        for s in seqs:
            for d in range(2):
                m = sm_in[d:d + 1, :] * LOG2_E if latent else jnp.zeros((1, HEAD_DIM), F32)
                order = range(cps) if d == 0 else range(cps - 1, -1, -1)
                for j in order:
                    c = s * cps + j
                    cs_ref[_M_PREV + d, c:c + 1, :] = m
                    b_end = cs_ref[_B_END + d, c:c + 1, :]
                    m_new = jnp.maximum(b_end + m, cs_ref[_MAX_LWE + d, c:c + 1, :])
                    cs_ref[_M_NEW + d, c:c + 1, :] = m_new
                    cs_ref[_DECAY + d, c:c + 1, :] = jnp.exp2(b_end + m - m_new)
                    m = m_new
                m_final[(s, d)] = m

        for c in chunks:
            kf = mk_ref[rows(c), :]
            vx = mvx_ref[rows(c), :]
            for d in range(2):
                log_w_end = cs_ref[_B_END + d, c:c + 1, :] - bc_ref[d, rows(c), :] + gate_col(c, 8 * d)
                w_end = jnp.exp2(log_w_end - cs_ref[_M_NEW + d, c:c + 1, :])
                ckv_ref[d, c] = _dot_tn((kf * w_end).astype(BF16), vx)

        for s in seqs:
            for d in range(2):
                if latent:
                    s_state = sr_in[d]
                    n_rep = jnp.broadcast_to(sn_in[d], (HEAD_DIM, HEAD_DIM)).T
                    c_state = jnp.concatenate([sc_in[d], n_rep], axis=1)
                else:
                    s_state = jnp.zeros((HEAD_DIM, HEAD_DIM), F32)
                    c_state = jnp.zeros((HEAD_DIM, 2 * HEAD_DIM), F32)
                order = range(cps) if d == 0 else range(cps - 1, -1, -1)
                for j in order:
                    c = s * cps + j
                    inc = skv_ref[d, c]
                    skv_ref[d, c] = s_state
                    s_state = s_state * chunk_decay[d] + inc
                    inc = ckv_ref[d, c]
                    ckv_ref[d, c] = c_state
                    decay = cs_ref[_DECAY + d, c:c + 1, :]
                    c_state = c_state * jnp.concatenate([decay, decay], axis=1) + inc
                if not latent:
                    so_ref[s, d] = s_state
                    co_ref[s, d] = c_state[:, 0:128]
                    no_ref[s, d:d + 1, :] = c_state[:, 128:256].T[0:1, :]
                    mo_ref[s, d:d + 1, :] = m_final[(s, d)] * LN_2

        for c in chunks:
            qf = rq_ref[rows(c), :]
            lhs = jnp.concatenate([ar_ref[rows(c), :], (qf * rc_ref[1]).astype(BF16),
                                   (qf * rc_ref[2]).astype(BF16)], axis=1)
            rhs = jnp.concatenate([rv_ref[rows(c), :], skv_ref[0, c].astype(BF16),
                                   skv_ref[1, c].astype(BF16)], axis=0)
            rq_ref[rows(c), :] = _dot(lhs, rhs)
            qf = mq_ref[rows(c), :]
            a_mat = am_ref[rows(c), :]
            vx = mvx_ref[rows(c), :]
            h_sum = None
            for d in range(2):
                log_intra = logi_ref[d, rows(c), :]
                log_inter = bc_ref[d, rows(c), :] + cs_ref[_M_PREV + d, c:c + 1, :]
                m_t = jnp.maximum(log_inter, jnp.max(log_intra, axis=1, keepdims=True))
                w_inter = jnp.exp2(log_inter - m_t)
                w_intra = jnp.exp2(log_intra - m_t)
                lhs = jnp.concatenate([(a_mat * w_intra).astype(BF16), (qf * w_inter).astype(BF16)], axis=1)
                rhs = jnp.concatenate([vx, ckv_ref[d, c].astype(BF16)], axis=0)
                res = _dot(lhs, rhs)
                h_dir = res[:, 0:128] / jnp.maximum(jnp.abs(res[:, 128:256]), jnp.exp2(-m_t))
                h_sum = h_dir if h_sum is None else h_sum + h_dir
            am_ref[rows(c), :] = h_sum

    pl.when(live)(project_and_scan)

    @pl.when(step == n_steps)
    def _():
        def body(i, carry):
            finish_rows(pl.ds(pl.multiple_of(i * ROW_TILE, ROW_TILE), ROW_TILE))
            return carry

        lax.fori_loop(0, rows_total // ROW_TILE, body, 0)


def _mixer(x2d, mod3, mod_row, norm1_g, w_pair, w_gate, head_params, *, latent, n_seq, cps,
           rope=None, states=None):
    n_tok = x2d.shape[0]
    rows_blk = n_seq * cps * CHUNK
    n_blk = n_tok // rows_blk
    n_chunks = n_seq * cps
    n_steps = n_blk * N_HEADS
    kern = functools.partial(_mixer_kernel, latent=latent, n_seq=n_seq, cps=cps, n_steps=n_steps)

    def cur(f):
        def index_map(j):
            item = jnp.minimum(j, n_steps - 1)
            return f(item // N_HEADS, item % N_HEADS)
        return index_map

    def prev(f):
        def index_map(j):
            item = jnp.maximum(j - 1, 0)
            return f(item // N_HEADS, item % N_HEADS)
        return index_map

    once = pl.Buffered(1)
    in_specs = [
        pl.BlockSpec((rows_blk, D_MODEL), cur(lambda b, h: (b, 0))),
        pl.BlockSpec((None, 6, D_MODEL), cur(lambda b, h: (mod_row(b), 0, 0))),
        pl.BlockSpec((1, D_MODEL), lambda j: (0, 0)),
        pl.BlockSpec((D_MODEL, PAIR_COLS), cur(lambda b, h: (0, h))),
        pl.BlockSpec((D_MODEL, HEAD_DIM), lambda j: (0, 0)),
        pl.BlockSpec((None, 16, HEAD_DIM), cur(lambda b, h: (h, 0, 0))),
        pl.BlockSpec((None, 16, HEAD_DIM), prev(lambda b, h: (h, 0, 0))),
    ]
    args = [x2d, mod3, norm1_g, w_pair, w_gate, head_params, head_params]
    y_shape = jax.ShapeDtypeStruct((n_tok, D_MODEL), BF16)
    y_spec = pl.BlockSpec((rows_blk, 2 * HEAD_DIM), prev(lambda b, h: (b, h)))
    if latent:
        assert n_seq == 1
        cos2, sin2 = rope
        s_ret, s_c, s_n, s_m = states
        in_specs += [
            pl.BlockSpec((rows_blk, HEAD_DIM), lambda j: (0, 0), pipeline_mode=once),
            pl.BlockSpec((rows_blk, HEAD_DIM), lambda j: (0, 0), pipeline_mode=once),
            pl.BlockSpec((None, None, 2, None, HEAD_DIM, HEAD_DIM), cur(lambda b, h: (b, 0, 0, h, 0, 0))),
            pl.BlockSpec((None, None, 2, None, HEAD_DIM, HEAD_DIM), cur(lambda b, h: (b, 0, 0, h, 0, 0))),
            pl.BlockSpec((None, None, 2, 1, HEAD_DIM), cur(lambda b, h: (b, h, 0, 0, 0))),
            pl.BlockSpec((None, None, 2, HEAD_DIM), cur(lambda b, h: (b, h, 0, 0))),
        ]
        args += [cos2, sin2, s_ret, s_c, s_n, s_m]
        out_shape = y_shape
        out_specs = y_spec
    else:
        bsz = n_blk * n_seq
        st = jax.ShapeDtypeStruct((bsz, 1, 2, N_HEADS, HEAD_DIM, HEAD_DIM), F32)
        vec = jax.ShapeDtypeStruct((bsz, N_HEADS, 2, HEAD_DIM), F32)
        st_spec = pl.BlockSpec((n_seq, None, 2, None, HEAD_DIM, HEAD_DIM), cur(lambda b, h: (b, 0, 0, h, 0, 0)))
        vec_spec = pl.BlockSpec((n_seq, None, 2, HEAD_DIM), cur(lambda b, h: (b, h, 0, 0)))
        out_shape = (y_shape, st, st, vec, vec)
        out_specs = (y_spec, st_spec, st_spec, vec_spec, vec_spec)
    col = lambda dt: pltpu.VMEM((rows_blk, HEAD_DIM), dt)
    scratch = [
        pltpu.VMEM((rows_blk, D_MODEL), BF16),
        col(F32), col(F32), col(BF16), col(F32),
        col(F32), col(F32),
        pltpu.VMEM((rows_blk, 2 * HEAD_DIM), BF16),
        col(F32), col(F32),
        col(BF16), col(F32),
        pltpu.VMEM((2, rows_blk, HEAD_DIM), F32),
        pltpu.VMEM((2, rows_blk, HEAD_DIM), F32),
        pltpu.VMEM((2, n_chunks, HEAD_DIM, HEAD_DIM), F32),
        pltpu.VMEM((2, n_chunks, HEAD_DIM, 2 * HEAD_DIM), F32),
        pltpu.VMEM((10, max(n_chunks, 8), HEAD_DIM), F32),
        pltpu.VMEM((5, CHUNK, CHUNK), F32),
        pltpu.VMEM((rows_blk + 2 * CONV_HALO, HEAD_DIM), F32),
        pltpu.VMEM((rows_blk + 2 * CONV_HALO, HEAD_DIM), F32),
        col(F32),
    ]
    return pl.pallas_call(
        kern,
        out_shape=out_shape,
        grid=(n_steps + 1,),
        in_specs=in_specs,
        out_specs=out_specs,
        scratch_shapes=scratch,
        compiler_params=pltpu.CompilerParams(
            dimension_semantics=("arbitrary",), vmem_limit_bytes=VMEM_LIMIT),
        name="mixer_latent" if latent else "mixer_context",
    )(*args)


def _ffn_kernel(x_ref, y_ref, mod_ref, wo_ref, g2_ref, w1_ref, w2_ref, gf_ref, out_ref):
    g1 = mod_ref[2:3, :]
    sh2 = mod_ref[3:4, :]
    sc2 = mod_ref[4:5, :]
    g2 = mod_ref[5:6, :]
    x1 = x_ref[...] + g1 * _dot(y_ref[...], wo_ref[...])
    h2 = (_rms(x1, g2_ref[...]) * (1.0 + sc2) + sh2).astype(BF16)
    f = jnp.zeros_like(x1)
    for j in range(D_FF // D_MODEL):
        cols = slice(j * D_MODEL, (j + 1) * D_MODEL)
        hid = jnp.maximum(_dot(h2, w1_ref[:, cols]), 0.0)
        f = f + _dot((hid * hid).astype(BF16), w2_ref[cols, :])
    out_ref[...] = _rms(x1 + g2 * f, gf_ref[...])


def _ffn(x2d, y2d, mod3, mod_row, w_out, norm2_g, w_ff1, w_ff2, final_g):
    n_tok = x2d.shape[0]
    const = lambda i: (0, 0)
    return pl.pallas_call(
        _ffn_kernel,
        out_shape=jax.ShapeDtypeStruct((n_tok, D_MODEL), F32),
        grid=(n_tok // FFN_ROWS,),
        in_specs=[
            pl.BlockSpec((FFN_ROWS, D_MODEL), lambda i: (i, 0)),
            pl.BlockSpec((FFN_ROWS, D_MODEL), lambda i: (i, 0)),
            pl.BlockSpec((None, 6, D_MODEL), lambda i: (mod_row(i), 0, 0)),
            pl.BlockSpec((D_MODEL, D_MODEL), const, pipeline_mode=pl.Buffered(1)),
            pl.BlockSpec((1, D_MODEL), const),
            pl.BlockSpec((D_MODEL, D_FF), const, pipeline_mode=pl.Buffered(1)),
            pl.BlockSpec((D_FF, D_MODEL), const, pipeline_mode=pl.Buffered(1)),
            pl.BlockSpec((1, D_MODEL), const),
        ],
        out_specs=pl.BlockSpec((FFN_ROWS, D_MODEL), lambda i: (i, 0)),
        compiler_params=pltpu.CompilerParams(
            dimension_semantics=("arbitrary",), vmem_limit_bytes=VMEM_LIMIT),
        name="outproj_mlp",
    )(x2d, y2d, mod3, w_out, norm2_g, w_ff1, w_ff2, final_g)


def _rope_tables(seq):
    pos = np.arange(seq)
    row = (pos // GRID_W).astype(np.float64)
    col = (pos % GRID_W).astype(np.float64)
    nf = HEAD_DIM // 4
    inv = ROPE_BASE ** (-np.arange(nf, dtype=np.float64) / nf)
    ang = np.concatenate([row[:, None] * inv, col[:, None] * inv], -1)
    cos = np.cos(ang)
    sin = np.sin(ang)
    cos2 = np.concatenate([cos, cos], -1).astype(np.float32)
    sin2 = np.concatenate([-sin, sin], -1).astype(np.float32)
    return jnp.asarray(cos2), jnp.asarray(sin2)


def kernel(x_prompt, x_sample, state_ret, state_mlstm_C, state_mlstm_n, state_mlstm_m, c, c_ctx,
           w_ada, b_ada, norm1_g, norm2_g, w_in, conv_w, ret_decay_logit, mlstm_gate_bias,
           ret_gn_g, mlstm_gn_g, w_out, w_ff1, w_ff2, final_g):
    assert w_ada.shape[0] == 1, "single-layer kernel"
    bp, tp, _ = x_prompt.shape
    bs, ts, _ = x_sample.shape
    assert tp % CHUNK == 0 and ts % CHUNK == 0 and bp % CTX_SEQS_PER_STEP == 0

    cond = jnp.concatenate([c_ctx[None, :], c, jnp.zeros((8 - 1 - bs, D_MODEL), F32)], 0)
    mod, w_pair, w_gate, w_out_p, w1, w2 = _prepare(cond, w_ada, b_ada, w_in, w_out, w_ff1, w_ff2)
    mod3 = mod[:1 + bs].reshape(1 + bs, 6, D_MODEL)


    cw = conv_w[0]
    hp_rows = [cw[j, :512].reshape(N_HEADS, HEAD_DIM) for j in range(3)]
    hp_rows += [cw[j, 512:].reshape(N_HEADS, HEAD_DIM) for j in range(3)]
    hp_rows += [ret_gn_g[0].reshape(N_HEADS, HEAD_DIM), mlstm_gn_g[0].reshape(N_HEADS, HEAD_DIM)]
    hp_rows += [jnp.broadcast_to(ret_decay_logit[0, d][:, None], (N_HEADS, HEAD_DIM)) for d in range(2)]
    gate_bias = jnp.pad(mlstm_gate_bias[0].reshape(1, N_GATE_COLS), ((0, 0), (HEAD_DIM - N_GATE_COLS, 0)))
    hp_rows += [jnp.broadcast_to(gate_bias, (N_HEADS, HEAD_DIM))]
    hp_rows += [jnp.zeros((N_HEADS, HEAD_DIM), F32)] * (16 - len(hp_rows))
    head_params = jnp.stack(hp_rows, axis=1).astype(F32)

    g1 = norm1_g[0][None, :]
    g2 = norm2_g[0][None, :]
    gf = final_g[None, :]
    xp2d = x_prompt.reshape(bp * tp, D_MODEL)
    xs2d = x_sample.reshape(bs * ts, D_MODEL)

    y_p, new_ret, new_c, new_n, new_m = _mixer(
        xp2d, mod3, lambda b: 0, g1, w_pair, w_gate, head_params,
        latent=False, n_seq=CTX_SEQS_PER_STEP, cps=tp // CHUNK)
    out_p = _ffn(xp2d, y_p, mod3, lambda i: 0, w_out_p, g2, w1, w2, gf).reshape(bp, tp, D_MODEL)

    s_n = jnp.transpose(state_mlstm_n[:, 0], (0, 2, 1, 3))[:, :, :, None, :]
    s_m = jnp.broadcast_to(jnp.transpose(state_mlstm_m[:, 0], (0, 2, 1))[..., None],
                           (bs, N_HEADS, 2, HEAD_DIM))
    y_s = _mixer(xs2d, mod3, lambda b: 1 + b, g1, w_pair, w_gate, head_params,
                 latent=True, n_seq=1, cps=ts // CHUNK,
                 rope=_rope_tables(ts), states=(state_ret, state_mlstm_C, s_n, s_m))
    tiles_per_seq = ts // FFN_ROWS
    out_s = _ffn(xs2d, y_s, mod3, lambda i: 1 + i // tiles_per_seq,
                 w_out_p, g2, w1, w2, gf).reshape(bs, ts, D_MODEL)

    new_n = jnp.transpose(new_n, (0, 2, 1, 3))[:, None]
    new_m = jnp.transpose(new_m[..., 0], (0, 2, 1))[:, None]
    return out_p, out_s, new_ret, new_c, new_n, new_m
```

```python
import functools

import numpy as np
import jax
import jax.numpy as jnp
from jax import lax
from jax.experimental import pallas as pl
from jax.experimental.pallas import tpu as pltpu

F32 = jnp.float32
BF16 = jnp.bfloat16

D_MODEL = 1024
N_HEADS = 4
HEAD_DIM = 128
CHUNK = 128
GRID_W = 64
D_FF = 4 * D_MODEL
EPS = 1e-6
ROPE_BASE = 10000.0
LOG2_E = 1.4426950408889634
LN_2 = 0.6931471805599453
PAIR_COLS = 8 * HEAD_DIM
N_GATE_COLS = 4 * N_HEADS
ROW_TILE = 256
FFN_ROWS = 512
CTX_SEQS_PER_STEP = 8
PROJ_SLAB_ROWS = 256
CONV_HALO = 8
VMEM_LIMIT = 60 * 1024 * 1024


def _dot(a, b):
    return jnp.dot(a, b, preferred_element_type=F32)


def _dot_nt(a, b):
    return lax.dot_general(a, b, (((1,), (1,)), ((), ())), preferred_element_type=F32)


def _dot_tn(a, b):
    return lax.dot_general(a, b, (((0,), (0,)), ((), ())), preferred_element_type=F32)


def _rms(x, g):
    return x * lax.rsqrt(jnp.mean(x * x, axis=-1, keepdims=True) + EPS) * g


def _group_norm(o, g):
    mu = jnp.mean(o, axis=-1, keepdims=True)
    c = o - mu
    var = jnp.mean(c * c, axis=-1, keepdims=True)
    return c * lax.rsqrt(var + EPS) * g


def _log_sigmoid(x):
    return jnp.minimum(x, 0.0) - jnp.log(1.0 + jnp.exp(-jnp.abs(x)))


def _sigmoid(x):
    return 1.0 / (1.0 + jnp.exp(-x))


def _split2(x):
    hi = x.astype(BF16)
    lo = (x - hi.astype(F32)).astype(BF16)
    return hi, lo


PREP_STEPS = 8


def _mod_kernel(cond_ref, w_ref, b_ref, out_ref):
    c = cond_ref[...]
    s = (c * _sigmoid(c)).astype(BF16)
    out_ref[...] = _dot(s, w_ref[...].astype(BF16)) + b_ref[...]


def _regroup_in_kernel(wt_ref, out_ref, gate_ref):
    for h in range(N_HEADS):
        for g in range(8):
            src = (g * N_HEADS + h) * HEAD_DIM
            dst = (h * 8 + g) * HEAD_DIM
            out_ref[:, dst:dst + HEAD_DIM] = wt_ref[src:src + HEAD_DIM, :].T.astype(BF16)
    n_rows = wt_ref.shape[0]
    n_gate = n_rows - N_HEADS * PAIR_COLS
    tail = wt_ref[n_rows - HEAD_DIM:n_rows, :].T
    lane = lax.broadcasted_iota(jnp.int32, tail.shape, 1)
    gate_ref[...] = jnp.where(lane >= HEAD_DIM - n_gate, tail, 0.0).astype(BF16)


def _regroup_out_kernel(w_ref, out_ref):
    for h in range(N_HEADS):
        for g in range(2):
            src = (g * N_HEADS + h) * HEAD_DIM
            dst = (h * 2 + g) * HEAD_DIM
            out_ref[dst:dst + HEAD_DIM, :] = w_ref[src:src + HEAD_DIM, :].astype(BF16)


def _prepare_kernel(cond_ref, wada_ref, bada_ref, wt_ref, wout_ref, w1_ref, w2_ref,
                    mod_ref, wpair_ref, wgate_ref, woutp_ref, w1o_ref, w2o_ref):
    _mod_kernel(cond_ref, wada_ref, bada_ref, mod_ref)
    _regroup_in_kernel(wt_ref, wpair_ref, wgate_ref)
    _regroup_out_kernel(wout_ref, woutp_ref)
    w1o_ref[...] = w1_ref[...].astype(BF16)
    w2o_ref[...] = w2_ref[...].astype(BF16)


def _prepare(cond, w_ada, b_ada, w_in, w_out, w_ff1, w_ff2):
    w_t = jnp.transpose(w_in[0])
    n_cols = w_t.shape[0]
    assert n_cols == N_HEADS * PAIR_COLS + N_GATE_COLS
    n_mod = w_ada.shape[-1]
    k_rows = D_MODEL // PREP_STEPS
    m_cols = n_mod // PREP_STEPS
    f_cols = D_FF // PREP_STEPS
    bf = lambda shape: jax.ShapeDtypeStruct(shape, BF16)
    return pl.pallas_call(
        _prepare_kernel,
        out_shape=(jax.ShapeDtypeStruct((cond.shape[0], n_mod), F32),
                   bf((D_MODEL, N_HEADS * PAIR_COLS)), bf((D_MODEL, HEAD_DIM)),
                   bf((D_MODEL, D_MODEL)), bf((D_MODEL, D_FF)), bf((D_FF, D_MODEL))),
        grid=(PREP_STEPS,),
        in_specs=[pl.BlockSpec(cond.shape, lambda i: (0, 0)),
                  pl.BlockSpec((None, D_MODEL, m_cols), lambda i: (0, 0, i)),
                  pl.BlockSpec((1, m_cols), lambda i: (0, i)),
                  pl.BlockSpec((n_cols, k_rows), lambda i: (0, i)),
                  pl.BlockSpec((None, D_MODEL, k_rows), lambda i: (0, 0, i)),
                  pl.BlockSpec((None, D_MODEL, f_cols), lambda i: (0, 0, i)),
                  pl.BlockSpec((None, f_cols, D_MODEL), lambda i: (0, i, 0))],
        out_specs=(pl.BlockSpec((cond.shape[0], m_cols), lambda i: (0, i)),
                   pl.BlockSpec((k_rows, N_HEADS * PAIR_COLS), lambda i: (i, 0)),
                   pl.BlockSpec((k_rows, HEAD_DIM), lambda i: (i, 0)),
                   pl.BlockSpec((D_MODEL, k_rows), lambda i: (0, i)),
                   pl.BlockSpec((D_MODEL, f_cols), lambda i: (0, i)),
                   pl.BlockSpec((f_cols, D_MODEL), lambda i: (i, 0))),
        compiler_params=pltpu.CompilerParams(
            dimension_semantics=("arbitrary",), vmem_limit_bytes=VMEM_LIMIT),
        name="prepare_weights",
    )(cond, w_ada, b_ada, w_t, w_out, w_ff1, w_ff2)


_B_END, _MAX_LWE, _M_PREV, _M_NEW, _DECAY = 0, 2, 4, 6, 8


def _mixer_kernel(*refs, latent, n_seq, cps, n_steps):
    L = CHUNK
    n_chunks = n_seq * cps
    rows_total = n_chunks * L
    seq_len = cps * L
    assert seq_len & (seq_len - 1) == 0
    if latent:
        (x_ref, mod_ref, g1_ref, w_ref, wg_ref, hp_ref, hp_prev_ref, cos_ref, sin_ref,
         sr_in, sc_in, sn_in, sm_in, y_ref, *scratch) = refs
    else:
        (x_ref, mod_ref, g1_ref, w_ref, wg_ref, hp_ref, hp_prev_ref,
         y_ref, so_ref, co_ref, no_ref, mo_ref, *scratch) = refs
    (hn_ref, rq_ref, rk_ref, rv_ref, rg_ref, mq_ref, mk_ref, mvx_ref, mog_ref, gt_ref,
     ar_ref, am_ref, logi_ref, bc_ref, skv_ref, ckv_ref, cs_ref, rc_ref, rawq_ref, rawk_ref,
     gall_ref) = scratch

    step = pl.program_id(0)
    live = step < n_steps
    head = lax.rem(jnp.minimum(step, n_steps - 1), N_HEADS)

    @pl.when(step == 0)
    def _():
        for ref in (rq_ref, am_ref, rg_ref, mog_ref):
            ref[...] = jnp.zeros_like(ref)

    @pl.when(jnp.logical_and(head == 0, live))
    def _():
        gain = g1_ref[...] * (1.0 + mod_ref[1:2, :])
        sh1 = mod_ref[0:1, :]
        gate_bias = hp_ref[10:11, :]
        lane_t = lax.broadcasted_iota(jnp.int32, (ROW_TILE, HEAD_DIM), 1)
        is_forget = jnp.bitwise_and(lane_t, 4) == 4

        def body(i, carry):
            r = pl.ds(pl.multiple_of(i * ROW_TILE, ROW_TILE), ROW_TILE)
            xr = x_ref[r, :]
            inv = lax.rsqrt(jnp.mean(xr * xr, axis=-1, keepdims=True) + EPS)
            hn_t = (xr * inv * gain + sh1).astype(BF16)
            hn_ref[r, :] = hn_t
            gates = _dot(hn_t, wg_ref[...]) + gate_bias
            gall_ref[r, :] = jnp.where(is_forget, _log_sigmoid(gates), gates) * LOG2_E
            return carry

        lax.fori_loop(0, rows_total // ROW_TILE, body, 0, unroll=4)

    hp = hp_ref[...]
    scale = HEAD_DIM ** -0.5

    r_i = lax.broadcasted_iota(jnp.int32, (L, L), 0)
    s_i = lax.broadcasted_iota(jnp.int32, (L, L), 1)
    r_f = r_i.astype(F32)
    s_f = s_i.astype(F32)
    lg_f = _log_sigmoid(hp[8:9])
    lg_b = _log_sigmoid(hp[9:10])
    rc_ref[0] = (jnp.where(r_i >= s_i, jnp.exp(lg_f * jnp.where(r_i >= s_i, r_f - s_f, 0.0)), 0.0)
                 + jnp.where(s_i >= r_i, jnp.exp(lg_b * jnp.where(s_i >= r_i, s_f - r_f, 0.0)), 0.0))
    rc_ref[1] = jnp.exp(lg_f * (r_f + 1.0))
    rc_ref[2] = jnp.exp(lg_b * (L - r_f))
    rc_ref[3] = jnp.exp(lg_f * (L - 1.0 - r_f))
    rc_ref[4] = jnp.exp(lg_b * r_f)
    chunk_decay = (jnp.exp(lg_f * float(L)), jnp.exp(lg_b * float(L)))

    tri = (jnp.where(r_i >= s_i, 1.0, 0.0).astype(BF16),
           jnp.where(s_i >= r_i, 1.0, 0.0).astype(BF16))

    for raw_ref in (rawq_ref, rawk_ref):
        raw_ref[0:CONV_HALO, :] = jnp.zeros((CONV_HALO, HEAD_DIM), F32)
        raw_ref[rows_total + CONV_HALO:rows_total + 2 * CONV_HALO, :] = jnp.zeros((CONV_HALO, HEAD_DIM), F32)

    def rows(c):
        return slice(c * L, (c + 1) * L)

    def gate_col(c, lane):
        g = gt_ref[rows(c), :]
        return jnp.broadcast_to(g[:, lane:lane + 1], (L, HEAD_DIM))

    hp_prev = hp_prev_ref[...]

    def finish_rows(r):
        rg = rg_ref[r, :]
        ret_y = _group_norm(rq_ref[r, :], hp_prev[6:7]) * (rg * _sigmoid(rg))
        y_ref[r, 0:128] = ret_y.astype(BF16)
        ml_y = _group_norm(am_ref[r, :], hp_prev[7:8]) * _sigmoid(mog_ref[r, :])
        y_ref[r, 128:256] = ml_y.astype(BF16)

    def project(blk, n_rows):
        hn = hn_ref[blk, :]
        pr = _dot(hn, w_ref[:, 0:256])
        q = pr[:, 0:128]
        k = pr[:, 128:256] * scale
        if latent:
            cos2 = cos_ref[blk, :]
            sin2 = sin_ref[blk, :]
            q = q * cos2 + pltpu.roll(q, HEAD_DIM // 2, axis=1) * sin2
            k = k * cos2 + pltpu.roll(k, HEAD_DIM // 2, axis=1) * sin2
        rq_ref[blk, :] = q
        rk_ref[blk, :] = k

        pv = _dot(hn, w_ref[:, 256:512])
        rv_ref[blk, :] = pv[:, 0:128].astype(BF16)
        rg_ref[blk, :] = pv[:, 128:256]

        pm = _dot(hn, w_ref[:, 512:768])
        halo_blk = slice(blk.start + CONV_HALO, blk.stop + CONV_HALO)
        rawq_ref[halo_blk, :] = pm[:, 0:128]
        rawk_ref[halo_blk, :] = pm[:, 128:256]

        po = _dot(hn, w_ref[:, 768:1024])
        mvx_ref[blk, 0:128] = po[:, 0:128].astype(BF16)
        mvx_ref[blk, 128:256] = jnp.ones((n_rows, HEAD_DIM), BF16)
        mog_ref[blk, :] = po[:, 128:256]

        gt_ref[blk, :] = pltpu.roll(gall_ref[blk, :], N_GATE_COLS - head, axis=1)

    def conv_silu(blk, n_rows):
        sub = lax.broadcasted_iota(jnp.int32, (CONV_HALO, HEAD_DIM), 0)
        starts_seq = blk.start % seq_len == 0
        ends_seq = blk.stop % seq_len == 0
        for raw_ref, dst_ref, taps, post in ((rawq_ref, mq_ref, hp[0:3], 1.0), (rawk_ref, mk_ref, hp[3:6], scale)):
            lo = blk.start + CONV_HALO
            prev = raw_ref[lo - 1:lo - 1 + n_rows, :]
            nxt = raw_ref[lo + 1:lo + 1 + n_rows, :]
            if starts_seq:
                first = jnp.where(sub == 0, 0.0, prev[0:CONV_HALO, :])
                prev = jnp.concatenate([first, prev[CONV_HALO:, :]], axis=0)
            if ends_seq:
                last = jnp.where(sub == CONV_HALO - 1, 0.0, nxt[n_rows - CONV_HALO:, :])
                nxt = jnp.concatenate([nxt[:n_rows - CONV_HALO, :], last], axis=0)
            out = prev * taps[0:1] + raw_ref[lo:lo + n_rows, :] * taps[1:2] + nxt * taps[2:3]
            out = out * _sigmoid(out)
            dst_ref[blk, :] = out if post == 1.0 else out * post

    def project_and_scan():
        slab_rows = min(seq_len, PROJ_SLAB_ROWS)
        assert seq_len % slab_rows == 0
        slabs = [slice(s0, s0 + slab_rows) for s0 in range(0, rows_total, slab_rows)]
        for i, slab in enumerate(slabs):
            finish_rows(slab)
            project(slab, slab_rows)
            if i > 0:
                conv_silu(slabs[i - 1], slab_rows)
        conv_silu(slabs[-1], slab_rows)
        pre_pass(range(n_chunks))
        scan_passes(range(n_chunks), range(n_seq))

    def pre_pass(chunks):
        for c in chunks:
            ar_ref[rows(c), :] = (_dot_nt(rq_ref[rows(c), :].astype(BF16), rk_ref[rows(c), :].astype(BF16))
                                  * rc_ref[0]).astype(BF16)
            am_ref[rows(c), :] = _dot_nt(mq_ref[rows(c), :].astype(BF16), mk_ref[rows(c), :].astype(BF16))
        for c in chunks:
            for d in range(2):
                li = gate_col(c, 8 * d)
                lf = gate_col(c, 8 * d + 4)
                strict = (r_i > s_i) if d == 0 else (r_i < s_i)
                x_mat = jnp.where(strict, lf, jnp.where(r_i == s_i, li, 0.0))
                hi, lo = _split2(x_mat)
                d_mat = _dot(tri[d], hi) + _dot(tri[d], lo)
                e = 0 if d == 0 else L - 1
                bcum = (jnp.broadcast_to(d_mat[:, e:e + 1], (L, HEAD_DIM))
                        + (lf[e:e + 1, :] - li[e:e + 1, :]))
                causal = (s_i <= r_i) if d == 0 else (s_i >= r_i)
                logi_ref[d, rows(c), :] = jnp.where(causal, d_mat, -jnp.inf)
                bc_ref[d, rows(c), :] = bcum
                b_end = bcum[L - 1:L, :] if d == 0 else bcum[0:1, :]
                cs_ref[_B_END + d, c:c + 1, :] = b_end
                cs_ref[_MAX_LWE + d, c:c + 1, :] = jnp.max(b_end - bcum + li, axis=0, keepdims=True)
        for c in chunks:
            kf = rk_ref[rows(c), :]
            vb = rv_ref[rows(c), :]
            for d in range(2):
                skv_ref[d, c] = _dot_tn((kf * rc_ref[3 + d]).astype(BF16), vb)

    def scan_passes(chunks, seqs):
        m_final = {}
        for s in seqs:
            for d in range(2):
                m = sm_in[d:d + 1, :] * LOG2_E if latent else jnp.zeros((1, HEAD_DIM), F32)
                order = range(cps) if d == 0 else range(cps - 1, -1, -1)
                for j in order:
                    c = s * cps + j
                    cs_ref[_M_PREV + d, c:c + 1, :] = m
                    b_end = cs_ref[_B_END + d, c:c + 1, :]
                    m_new = jnp.maximum(b_end + m, cs_ref[_MAX_LWE + d, c:c + 1, :])
                    cs_ref[_M_NEW + d, c:c + 1, :] = m_new
                    cs_ref[_DECAY + d, c:c + 1, :] = jnp.exp2(b_end + m - m_new)
                    m = m_new
                m_final[(s, d)] = m

        for c in chunks:
            kf = mk_ref[rows(c), :]
            vx = mvx_ref[rows(c), :]
            for d in range(2):
                log_w_end = cs_ref[_B_END + d, c:c + 1, :] - bc_ref[d, rows(c), :] + gate_col(c, 8 * d)
                w_end = jnp.exp2(log_w_end - cs_ref[_M_NEW + d, c:c + 1, :])
                ckv_ref[d, c] = _dot_tn((kf * w_end).astype(BF16), vx)

        for s in seqs:
            for d in range(2):
                if latent:
                    s_state = sr_in[d]
                    n_rep = jnp.broadcast_to(sn_in[d], (HEAD_DIM, HEAD_DIM)).T
                    c_state = jnp.concatenate([sc_in[d], n_rep], axis=1)
                else:
                    s_state = jnp.zeros((HEAD_DIM, HEAD_DIM), F32)
                    c_state = jnp.zeros((HEAD_DIM, 2 * HEAD_DIM), F32)
                order = range(cps) if d == 0 else range(cps - 1, -1, -1)
                for j in order:
                    c = s * cps + j
                    inc = skv_ref[d, c]
                    skv_ref[d, c] = s_state
                    s_state = s_state * chunk_decay[d] + inc
                    inc = ckv_ref[d, c]
                    ckv_ref[d, c] = c_state
                    decay = cs_ref[_DECAY + d, c:c + 1, :]
                    c_state = c_state * jnp.concatenate([decay, decay], axis=1) + inc
                if not latent:
                    so_ref[s, d] = s_state
                    co_ref[s, d] = c_state[:, 0:128]
                    no_ref[s, d:d + 1, :] = c_state[:, 128:256].T[0:1, :]
                    mo_ref[s, d:d + 1, :] = m_final[(s, d)] * LN_2

        for c in chunks:
            qf = rq_ref[rows(c), :]
            lhs = jnp.concatenate([ar_ref[rows(c), :], (qf * rc_ref[1]).astype(BF16),
                                   (qf * rc_ref[2]).astype(BF16)], axis=1)
            rhs = jnp.concatenate([rv_ref[rows(c), :], skv_ref[0, c].astype(BF16),
                                   skv_ref[1, c].astype(BF16)], axis=0)
            rq_ref[rows(c), :] = _dot(lhs, rhs)
            qf = mq_ref[rows(c), :]
            a_mat = am_ref[rows(c), :]
            vx = mvx_ref[rows(c), :]
            h_sum = None
            for d in range(2):
                log_intra = logi_ref[d, rows(c), :]
                log_inter = bc_ref[d, rows(c), :] + cs_ref[_M_PREV + d, c:c + 1, :]
                m_t = jnp.maximum(log_inter, jnp.max(log_intra, axis=1, keepdims=True))
                w_inter = jnp.exp2(log_inter - m_t)
                w_intra = jnp.exp2(log_intra - m_t)
                lhs = jnp.concatenate([(a_mat * w_intra).astype(BF16), (qf * w_inter).astype(BF16)], axis=1)
                rhs = jnp.concatenate([vx, ckv_ref[d, c].astype(BF16)], axis=0)
                res = _dot(lhs, rhs)
                h_dir = res[:, 0:128] / jnp.maximum(jnp.abs(res[:, 128:256]), jnp.exp2(-m_t))
                h_sum = h_dir if h_sum is None else h_sum + h_dir
            am_ref[rows(c), :] = h_sum

    pl.when(live)(project_and_scan)

    @pl.when(step == n_steps)
    def _():
        def body(i, carry):
            finish_rows(pl.ds(pl.multiple_of(i * ROW_TILE, ROW_TILE), ROW_TILE))
            return carry

        lax.fori_loop(0, rows_total // ROW_TILE, body, 0)


def _mixer(x2d, mod3, mod_row, norm1_g, w_pair, w_gate, head_params, *, latent, n_seq, cps,
           rope=None, states=None):
    n_tok = x2d.shape[0]
    rows_blk = n_seq * cps * CHUNK
    n_blk = n_tok // rows_blk
    n_chunks = n_seq * cps
    n_steps = n_blk * N_HEADS
    kern = functools.partial(_mixer_kernel, latent=latent, n_seq=n_seq, cps=cps, n_steps=n_steps)

    def cur(f):
        def index_map(j):
            item = jnp.minimum(j, n_steps - 1)
            return f(item // N_HEADS, item % N_HEADS)
        return index_map

    def prev(f):
        def index_map(j):
            item = jnp.maximum(j - 1, 0)
            return f(item // N_HEADS, item % N_HEADS)
        return index_map

    once = pl.Buffered(1)
    in_specs = [
        pl.BlockSpec((rows_blk, D_MODEL), cur(lambda b, h: (b, 0))),
        pl.BlockSpec((None, 6, D_MODEL), cur(lambda b, h: (mod_row(b), 0, 0))),
        pl.BlockSpec((1, D_MODEL), lambda j: (0, 0)),
        pl.BlockSpec((D_MODEL, PAIR_COLS), cur(lambda b, h: (0, h))),
        pl.BlockSpec((D_MODEL, HEAD_DIM), lambda j: (0, 0)),
        pl.BlockSpec((None, 16, HEAD_DIM), cur(lambda b, h: (h, 0, 0))),
        pl.BlockSpec((None, 16, HEAD_DIM), prev(lambda b, h: (h, 0, 0))),
    ]
    args = [x2d, mod3, norm1_g, w_pair, w_gate, head_params, head_params]
    y_shape = jax.ShapeDtypeStruct((n_tok, D_MODEL), BF16)
    y_spec = pl.BlockSpec((rows_blk, 2 * HEAD_DIM), prev(lambda b, h: (b, h)))
    if latent:
        assert n_seq == 1
        cos2, sin2 = rope
        s_ret, s_c, s_n, s_m = states
        in_specs += [
            pl.BlockSpec((rows_blk, HEAD_DIM), lambda j: (0, 0), pipeline_mode=once),
            pl.BlockSpec((rows_blk, HEAD_DIM), lambda j: (0, 0), pipeline_mode=once),
            pl.BlockSpec((None, None, 2, None, HEAD_DIM, HEAD_DIM), cur(lambda b, h: (b, 0, 0, h, 0, 0))),
            pl.BlockSpec((None, None, 2, None, HEAD_DIM, HEAD_DIM), cur(lambda b, h: (b, 0, 0, h, 0, 0))),
            pl.BlockSpec((None, None, 2, 1, HEAD_DIM), cur(lambda b, h: (b, h, 0, 0, 0))),
            pl.BlockSpec((None, None, 2, HEAD_DIM), cur(lambda b, h: (b, h, 0, 0))),
        ]
        args += [cos2, sin2, s_ret, s_c, s_n, s_m]
        out_shape = y_shape
        out_specs = y_spec
    else:
        bsz = n_blk * n_seq
        st = jax.ShapeDtypeStruct((bsz, 1, 2, N_HEADS, HEAD_DIM, HEAD_DIM), F32)
        vec = jax.ShapeDtypeStruct((bsz, N_HEADS, 2, HEAD_DIM), F32)
        st_spec = pl.BlockSpec((n_seq, None, 2, None, HEAD_DIM, HEAD_DIM), cur(lambda b, h: (b, 0, 0, h, 0, 0)))
        vec_spec = pl.BlockSpec((n_seq, None, 2, HEAD_DIM), cur(lambda b, h: (b, h, 0, 0)))
        out_shape = (y_shape, st, st, vec, vec)
        out_specs = (y_spec, st_spec, st_spec, vec_spec, vec_spec)
    col = lambda dt: pltpu.VMEM((rows_blk, HEAD_DIM), dt)
    scratch = [
        pltpu.VMEM((rows_blk, D_MODEL), BF16),
        col(F32), col(F32), col(BF16), col(F32),
        col(F32), col(F32),
        pltpu.VMEM((rows_blk, 2 * HEAD_DIM), BF16),
        col(F32), col(F32),
        col(BF16), col(F32),
        pltpu.VMEM((2, rows_blk, HEAD_DIM), F32),
        pltpu.VMEM((2, rows_blk, HEAD_DIM), F32),
        pltpu.VMEM((2, n_chunks, HEAD_DIM, HEAD_DIM), F32),
        pltpu.VMEM((2, n_chunks, HEAD_DIM, 2 * HEAD_DIM), F32),
        pltpu.VMEM((10, max(n_chunks, 8), HEAD_DIM), F32),
        pltpu.VMEM((5, CHUNK, CHUNK), F32),
        pltpu.VMEM((rows_blk + 2 * CONV_HALO, HEAD_DIM), F32),
        pltpu.VMEM((rows_blk + 2 * CONV_HALO, HEAD_DIM), F32),
        col(F32),
    ]
    return pl.pallas_call(
        kern,
        out_shape=out_shape,
        grid=(n_steps + 1,),
        in_specs=in_specs,
        out_specs=out_specs,
        scratch_shapes=scratch,
        compiler_params=pltpu.CompilerParams(
            dimension_semantics=("arbitrary",), vmem_limit_bytes=VMEM_LIMIT),
        name="mixer_latent" if latent else "mixer_context",
    )(*args)


def _ffn_kernel(x_ref, y_ref, mod_ref, wo_ref, g2_ref, w1_ref, w2_ref, gf_ref, out_ref):
    g1 = mod_ref[2:3, :]
    sh2 = mod_ref[3:4, :]
    sc2 = mod_ref[4:5, :]
    g2 = mod_ref[5:6, :]
    x1 = x_ref[...] + g1 * _dot(y_ref[...], wo_ref[...])
    h2 = (_rms(x1, g2_ref[...]) * (1.0 + sc2) + sh2).astype(BF16)
    f = jnp.zeros_like(x1)
    for j in range(D_FF // D_MODEL):
        cols = slice(j * D_MODEL, (j + 1) * D_MODEL)
        hid = jnp.maximum(_dot(h2, w1_ref[:, cols]), 0.0)
        f = f + _dot((hid * hid).astype(BF16), w2_ref[cols, :])
    out_ref[...] = _rms(x1 + g2 * f, gf_ref[...])


def _ffn(x2d, y2d, mod3, mod_row, w_out, norm2_g, w_ff1, w_ff2, final_g):
    n_tok = x2d.shape[0]
    const = lambda i: (0, 0)
    return pl.pallas_call(
        _ffn_kernel,
        out_shape=jax.ShapeDtypeStruct((n_tok, D_MODEL), F32),
        grid=(n_tok // FFN_ROWS,),
        in_specs=[
            pl.BlockSpec((FFN_ROWS, D_MODEL), lambda i: (i, 0)),
            pl.BlockSpec((FFN_ROWS, D_MODEL), lambda i: (i, 0)),
            pl.BlockSpec((None, 6, D_MODEL), lambda i: (mod_row(i), 0, 0)),
            pl.BlockSpec((D_MODEL, D_MODEL), const, pipeline_mode=pl.Buffered(1)),
            pl.BlockSpec((1, D_MODEL), const),
            pl.BlockSpec((D_MODEL, D_FF), const, pipeline_mode=pl.Buffered(1)),
            pl.BlockSpec((D_FF, D_MODEL), const, pipeline_mode=pl.Buffered(1)),
            pl.BlockSpec((1, D_MODEL), const),
        ],
        out_specs=pl.BlockSpec((FFN_ROWS, D_MODEL), lambda i: (i, 0)),
        compiler_params=pltpu.CompilerParams(
            dimension_semantics=("arbitrary",), vmem_limit_bytes=VMEM_LIMIT),
        name="outproj_mlp",
    )(x2d, y2d, mod3, w_out, norm2_g, w_ff1, w_ff2, final_g)


def _rope_tables(seq):
    pos = np.arange(seq)
    row = (pos // GRID_W).astype(np.float64)
    col = (pos % GRID_W).astype(np.float64)
    nf = HEAD_DIM // 4
    inv = ROPE_BASE ** (-np.arange(nf, dtype=np.float64) / nf)
    ang = np.concatenate([row[:, None] * inv, col[:, None] * inv], -1)
    cos = np.cos(ang)
    sin = np.sin(ang)
    cos2 = np.concatenate([cos, cos], -1).astype(np.float32)
    sin2 = np.concatenate([-sin, sin], -1).astype(np.float32)
    return jnp.asarray(cos2), jnp.asarray(sin2)


def kernel(x_prompt, x_sample, state_ret, state_mlstm_C, state_mlstm_n, state_mlstm_m, c, c_ctx,
           w_ada, b_ada, norm1_g, norm2_g, w_in, conv_w, ret_decay_logit, mlstm_gate_bias,
           ret_gn_g, mlstm_gn_g, w_out, w_ff1, w_ff2, final_g):
    assert w_ada.shape[0] == 1, "single-layer kernel"
    bp, tp, _ = x_prompt.shape
    bs, ts, _ = x_sample.shape
    assert tp % CHUNK == 0 and ts % CHUNK == 0 and bp % CTX_SEQS_PER_STEP == 0

    cond = jnp.concatenate([c_ctx[None, :], c, jnp.zeros((8 - 1 - bs, D_MODEL), F32)], 0)
    mod, w_pair, w_gate, w_out_p, w1, w2 = _prepare(cond, w_ada, b_ada, w_in, w_out, w_ff1, w_ff2)
    mod3 = mod[:1 + bs].reshape(1 + bs, 6, D_MODEL)


    cw = conv_w[0]
    hp_rows = [cw[j, :512].reshape(N_HEADS, HEAD_DIM) for j in range(3)]
    hp_rows += [cw[j, 512:].reshape(N_HEADS, HEAD_DIM) for j in range(3)]
    hp_rows += [ret_gn_g[0].reshape(N_HEADS, HEAD_DIM), mlstm_gn_g[0].reshape(N_HEADS, HEAD_DIM)]
    hp_rows += [jnp.broadcast_to(ret_decay_logit[0, d][:, None], (N_HEADS, HEAD_DIM)) for d in range(2)]
    gate_bias = jnp.pad(mlstm_gate_bias[0].reshape(1, N_GATE_COLS), ((0, 0), (HEAD_DIM - N_GATE_COLS, 0)))
    hp_rows += [jnp.broadcast_to(gate_bias, (N_HEADS, HEAD_DIM))]
    hp_rows += [jnp.zeros((N_HEADS, HEAD_DIM), F32)] * (16 - len(hp_rows))
    head_params = jnp.stack(hp_rows, axis=1).astype(F32)

    g1 = norm1_g[0][None, :]
    g2 = norm2_g[0][None, :]
    gf = final_g[None, :]
    xp2d = x_prompt.reshape(bp * tp, D_MODEL)
    xs2d = x_sample.reshape(bs * ts, D_MODEL)

    y_p, new_ret, new_c, new_n, new_m = _mixer(
        xp2d, mod3, lambda b: 0, g1, w_pair, w_gate, head_params,
        latent=False, n_seq=CTX_SEQS_PER_STEP, cps=tp // CHUNK)
    out_p = _ffn(xp2d, y_p, mod3, lambda i: 0, w_out_p, g2, w1, w2, gf).reshape(bp, tp, D_MODEL)

    s_n = jnp.transpose(state_mlstm_n[:, 0], (0, 2, 1, 3))[:, :, :, None, :]
    s_m = jnp.broadcast_to(jnp.transpose(state_mlstm_m[:, 0], (0, 2, 1))[..., None],
                           (bs, N_HEADS, 2, HEAD_DIM))
    y_s = _mixer(xs2d, mod3, lambda b: 1 + b, g1, w_pair, w_gate, head_params,
                 latent=True, n_seq=1, cps=ts // CHUNK,
                 rope=_rope_tables(ts), states=(state_ret, state_mlstm_C, s_n, s_m))
    tiles_per_seq = ts // FFN_ROWS
    out_s = _ffn(xs2d, y_s, mod3, lambda i: 1 + i // tiles_per_seq,
                 w_out_p, g2, w1, w2, gf).reshape(bs, ts, D_MODEL)

    new_n = jnp.transpose(new_n, (0, 2, 1, 3))[:, None]
    new_m = jnp.transpose(new_m[..., 0], (0, 2, 1))[:, None]
    return out_p, out_s, new_ret, new_c, new_n, new_m
```

```python
import functools

import numpy as np
import jax
import jax.numpy as jnp
from jax import lax
from jax.experimental import pallas as pl
from jax.experimental.pallas import tpu as pltpu

F32 = jnp.float32
BF16 = jnp.bfloat16

D_MODEL = 1024
N_HEADS = 4
HEAD_DIM = 128
CHUNK = 128
GRID_W = 64
D_FF = 4 * D_MODEL
EPS = 1e-6
ROPE_BASE = 10000.0
LOG2_E = 1.4426950408889634
LN_2 = 0.6931471805599453
PAIR_COLS = 8 * HEAD_DIM
N_GATE_COLS = 4 * N_HEADS
ROW_TILE = 256
FFN_ROWS = 1024
CTX_SEQS_PER_STEP = 8
PROJ_SLAB_ROWS = 256
CONV_HALO = 8
VMEM_LIMIT = 60 * 1024 * 1024


def _dot(a, b):
    return jnp.dot(a, b, preferred_element_type=F32)


def _dot_nt(a, b):
    return lax.dot_general(a, b, (((1,), (1,)), ((), ())), preferred_element_type=F32)


def _dot_tn(a, b):
    return lax.dot_general(a, b, (((0,), (0,)), ((), ())), preferred_element_type=F32)


def _rms(x, g):
    return x * lax.rsqrt(jnp.mean(x * x, axis=-1, keepdims=True) + EPS) * g


def _group_norm(o, g):
    mu = jnp.mean(o, axis=-1, keepdims=True)
    c = o - mu
    var = jnp.mean(c * c, axis=-1, keepdims=True)
    return c * lax.rsqrt(var + EPS) * g


def _log_sigmoid(x):
    return jnp.minimum(x, 0.0) - jnp.log(1.0 + jnp.exp(-jnp.abs(x)))


def _sigmoid(x):
    return 1.0 / (1.0 + jnp.exp(-x))


def _split2(x):
    hi = x.astype(BF16)
    lo = (x - hi.astype(F32)).astype(BF16)
    return hi, lo


PREP_STEPS = 8


def _mod_kernel(cond_ref, w_ref, b_ref, out_ref):
    c = cond_ref[...]
    s = (c * _sigmoid(c)).astype(BF16)
    out_ref[...] = _dot(s, w_ref[...].astype(BF16)) + b_ref[...]


def _regroup_in_kernel(wt_ref, out_ref, gate_ref):
    for h in range(N_HEADS):
        for g in range(8):
            src = (g * N_HEADS + h) * HEAD_DIM
            dst = (h * 8 + g) * HEAD_DIM
            out_ref[:, dst:dst + HEAD_DIM] = wt_ref[src:src + HEAD_DIM, :].T.astype(BF16)
    n_rows = wt_ref.shape[0]
    n_gate = n_rows - N_HEADS * PAIR_COLS
    tail = wt_ref[n_rows - HEAD_DIM:n_rows, :].T
    lane = lax.broadcasted_iota(jnp.int32, tail.shape, 1)
    gate_ref[...] = jnp.where(lane >= HEAD_DIM - n_gate, tail, 0.0).astype(BF16)


def _regroup_out_kernel(w_ref, out_ref):
    for h in range(N_HEADS):
        for g in range(2):
            src = (g * N_HEADS + h) * HEAD_DIM
            dst = (h * 2 + g) * HEAD_DIM
            out_ref[dst:dst + HEAD_DIM, :] = w_ref[src:src + HEAD_DIM, :].astype(BF16)


def _prepare_kernel(cond_ref, wada_ref, bada_ref, wt_ref, wout_ref, w1_ref, w2_ref,
                    mod_ref, wpair_ref, wgate_ref, woutp_ref, w1o_ref, w2o_ref):
    _mod_kernel(cond_ref, wada_ref, bada_ref, mod_ref)
    _regroup_in_kernel(wt_ref, wpair_ref, wgate_ref)
    _regroup_out_kernel(wout_ref, woutp_ref)
    w1o_ref[...] = w1_ref[...].astype(BF16)
    w2o_ref[...] = w2_ref[...].astype(BF16)


def _prepare(cond, w_ada, b_ada, w_in, w_out, w_ff1, w_ff2):
    w_t = jnp.transpose(w_in[0])
    n_cols = w_t.shape[0]
    assert n_cols == N_HEADS * PAIR_COLS + N_GATE_COLS
    n_mod = w_ada.shape[-1]
    k_rows = D_MODEL // PREP_STEPS
    m_cols = n_mod // PREP_STEPS
    f_cols = D_FF // PREP_STEPS
    bf = lambda shape: jax.ShapeDtypeStruct(shape, BF16)
    return pl.pallas_call(
        _prepare_kernel,
        out_shape=(jax.ShapeDtypeStruct((cond.shape[0], n_mod), F32),
                   bf((D_MODEL, N_HEADS * PAIR_COLS)), bf((D_MODEL, HEAD_DIM)),
                   bf((D_MODEL, D_MODEL)), bf((D_MODEL, D_FF)), bf((D_FF, D_MODEL))),
        grid=(PREP_STEPS,),
        in_specs=[pl.BlockSpec(cond.shape, lambda i: (0, 0)),
                  pl.BlockSpec((None, D_MODEL, m_cols), lambda i: (0, 0, i)),
                  pl.BlockSpec((1, m_cols), lambda i: (0, i)),
                  pl.BlockSpec((n_cols, k_rows), lambda i: (0, i)),
                  pl.BlockSpec((None, D_MODEL, k_rows), lambda i: (0, 0, i)),
                  pl.BlockSpec((None, D_MODEL, f_cols), lambda i: (0, 0, i)),
                  pl.BlockSpec((None, f_cols, D_MODEL), lambda i: (0, i, 0))],
        out_specs=(pl.BlockSpec((cond.shape[0], m_cols), lambda i: (0, i)),
                   pl.BlockSpec((k_rows, N_HEADS * PAIR_COLS), lambda i: (i, 0)),
                   pl.BlockSpec((k_rows, HEAD_DIM), lambda i: (i, 0)),
                   pl.BlockSpec((D_MODEL, k_rows), lambda i: (0, i)),
                   pl.BlockSpec((D_MODEL, f_cols), lambda i: (0, i)),
                   pl.BlockSpec((f_cols, D_MODEL), lambda i: (i, 0))),
        compiler_params=pltpu.CompilerParams(
            dimension_semantics=("arbitrary",), vmem_limit_bytes=VMEM_LIMIT),
        name="prepare_weights",
    )(cond, w_ada, b_ada, w_t, w_out, w_ff1, w_ff2)


_B_END, _MAX_LWE, _M_PREV, _M_NEW, _DECAY = 0, 2, 4, 6, 8


def _mixer_kernel(*refs, latent, n_seq, cps, n_steps):
    L = CHUNK
    n_chunks = n_seq * cps
    rows_total = n_chunks * L
    seq_len = cps * L
    assert seq_len & (seq_len - 1) == 0
    if latent:
        (x_ref, mod_ref, g1_ref, w_ref, wg_ref, hp_ref, hp_prev_ref, cos_ref, sin_ref,
         sr_in, sc_in, sn_in, sm_in, y_ref, *scratch) = refs
    else:
        (x_ref, mod_ref, g1_ref, w_ref, wg_ref, hp_ref, hp_prev_ref,
         y_ref, so_ref, co_ref, no_ref, mo_ref, *scratch) = refs
    (hn_ref, rq_ref, rk_ref, rv_ref, rg_ref, mq_ref, mk_ref, mvx_ref, mog_ref, gt_ref,
     ar_ref, am_ref, logi_ref, bc_ref, skv_ref, ckv_ref, cs_ref, rc_ref, rawq_ref, rawk_ref,
     gall_ref) = scratch

    step = pl.program_id(0)
    live = step < n_steps
    head = lax.rem(jnp.minimum(step, n_steps - 1), N_HEADS)

    @pl.when(step == 0)
    def _():
        for ref in (rq_ref, am_ref, rg_ref, mog_ref):
            ref[...] = jnp.zeros_like(ref)

    @pl.when(jnp.logical_and(head == 0, live))
    def _():
        gain = g1_ref[...] * (1.0 + mod_ref[1:2, :])
        sh1 = mod_ref[0:1, :]
        gate_bias = hp_ref[10:11, :]
        lane_t = lax.broadcasted_iota(jnp.int32, (ROW_TILE, HEAD_DIM), 1)
        is_forget = jnp.bitwise_and(lane_t, 4) == 4

        def body(i, carry):
            r = pl.ds(pl.multiple_of(i * ROW_TILE, ROW_TILE), ROW_TILE)
            xr = x_ref[r, :]
            inv = lax.rsqrt(jnp.mean(xr * xr, axis=-1, keepdims=True) + EPS)
            hn_t = (xr * inv * gain + sh1).astype(BF16)
            hn_ref[r, :] = hn_t
            gates = _dot(hn_t, wg_ref[...]) + gate_bias
            gall_ref[r, :] = jnp.where(is_forget, _log_sigmoid(gates), gates) * LOG2_E
            return carry

        lax.fori_loop(0, rows_total // ROW_TILE, body, 0, unroll=4)

    hp = hp_ref[...]
    scale = HEAD_DIM ** -0.5

    r_i = lax.broadcasted_iota(jnp.int32, (L, L), 0)
    s_i = lax.broadcasted_iota(jnp.int32, (L, L), 1)
    r_f = r_i.astype(F32)
    s_f = s_i.astype(F32)
    lg_f = _log_sigmoid(hp[8:9])
    lg_b = _log_sigmoid(hp[9:10])
    rc_ref[0] = (jnp.where(r_i >= s_i, jnp.exp(lg_f * jnp.where(r_i >= s_i, r_f - s_f, 0.0)), 0.0)
                 + jnp.where(s_i >= r_i, jnp.exp(lg_b * jnp.where(s_i >= r_i, s_f - r_f, 0.0)), 0.0))
    rc_ref[1] = jnp.exp(lg_f * (r_f + 1.0))
    rc_ref[2] = jnp.exp(lg_b * (L - r_f))
    rc_ref[3] = jnp.exp(lg_f * (L - 1.0 - r_f))
    rc_ref[4] = jnp.exp(lg_b * r_f)
    chunk_decay = (jnp.exp(lg_f * float(L)), jnp.exp(lg_b * float(L)))

    tri = (jnp.where(r_i >= s_i, 1.0, 0.0).astype(BF16),
           jnp.where(s_i >= r_i, 1.0, 0.0).astype(BF16))

    for raw_ref in (rawq_ref, rawk_ref):
        raw_ref[0:CONV_HALO, :] = jnp.zeros((CONV_HALO, HEAD_DIM), F32)
        raw_ref[rows_total + CONV_HALO:rows_total + 2 * CONV_HALO, :] = jnp.zeros((CONV_HALO, HEAD_DIM), F32)

    def rows(c):
        return slice(c * L, (c + 1) * L)

    def gate_col(c, lane):
        g = gt_ref[rows(c), :]
        return jnp.broadcast_to(g[:, lane:lane + 1], (L, HEAD_DIM))

    hp_prev = hp_prev_ref[...]

    def finish_rows(r):
        rg = rg_ref[r, :]
        ret_y = _group_norm(rq_ref[r, :], hp_prev[6:7]) * (rg * _sigmoid(rg))
        y_ref[r, 0:128] = ret_y.astype(BF16)
        ml_y = _group_norm(am_ref[r, :], hp_prev[7:8]) * _sigmoid(mog_ref[r, :])
        y_ref[r, 128:256] = ml_y.astype(BF16)

    def project(blk, n_rows):
        hn = hn_ref[blk, :]
        pr = _dot(hn, w_ref[:, 0:256])
        q = pr[:, 0:128]
        k = pr[:, 128:256] * scale
        if latent:
            cos2 = cos_ref[blk, :]
            sin2 = sin_ref[blk, :]
            q = q * cos2 + pltpu.roll(q, HEAD_DIM // 2, axis=1) * sin2
            k = k * cos2 + pltpu.roll(k, HEAD_DIM // 2, axis=1) * sin2
        rq_ref[blk, :] = q
        rk_ref[blk, :] = k

        pv = _dot(hn, w_ref[:, 256:512])
        rv_ref[blk, :] = pv[:, 0:128].astype(BF16)
        rg_ref[blk, :] = pv[:, 128:256]

        pm = _dot(hn, w_ref[:, 512:768])
        halo_blk = slice(blk.start + CONV_HALO, blk.stop + CONV_HALO)
        rawq_ref[halo_blk, :] = pm[:, 0:128]
        rawk_ref[halo_blk, :] = pm[:, 128:256]

        po = _dot(hn, w_ref[:, 768:1024])
        mvx_ref[blk, 0:128] = po[:, 0:128].astype(BF16)
        mvx_ref[blk, 128:256] = jnp.ones((n_rows, HEAD_DIM), BF16)
        mog_ref[blk, :] = po[:, 128:256]

        gt_ref[blk, :] = pltpu.roll(gall_ref[blk, :], N_GATE_COLS - head, axis=1)

    def conv_silu(blk, n_rows):
        sub = lax.broadcasted_iota(jnp.int32, (CONV_HALO, HEAD_DIM), 0)
        starts_seq = blk.start % seq_len == 0
        ends_seq = blk.stop % seq_len == 0
        for raw_ref, dst_ref, taps, post in ((rawq_ref, mq_ref, hp[0:3], 1.0), (rawk_ref, mk_ref, hp[3:6], scale)):
            lo = blk.start + CONV_HALO
            prev = raw_ref[lo - 1:lo - 1 + n_rows, :]
            nxt = raw_ref[lo + 1:lo + 1 + n_rows, :]
            if starts_seq:
                first = jnp.where(sub == 0, 0.0, prev[0:CONV_HALO, :])
                prev = jnp.concatenate([first, prev[CONV_HALO:, :]], axis=0)
            if ends_seq:
                last = jnp.where(sub == CONV_HALO - 1, 0.0, nxt[n_rows - CONV_HALO:, :])
                nxt = jnp.concatenate([nxt[:n_rows - CONV_HALO, :], last], axis=0)
            out = prev * taps[0:1] + raw_ref[lo:lo + n_rows, :] * taps[1:2] + nxt * taps[2:3]
            out = out * _sigmoid(out)
            dst_ref[blk, :] = out if post == 1.0 else out * post

    def project_and_scan():
        slab_rows = min(seq_len, PROJ_SLAB_ROWS)
        assert seq_len % slab_rows == 0
        slabs = [slice(s0, s0 + slab_rows) for s0 in range(0, rows_total, slab_rows)]
        for i, slab in enumerate(slabs):
            finish_rows(slab)
            project(slab, slab_rows)
            if i > 0:
                conv_silu(slabs[i - 1], slab_rows)
        conv_silu(slabs[-1], slab_rows)
        pre_pass(range(n_chunks))
        scan_passes(range(n_chunks), range(n_seq))

    def pre_pass(chunks):
        for c in chunks:
            ar_ref[rows(c), :] = (_dot_nt(rq_ref[rows(c), :].astype(BF16), rk_ref[rows(c), :].astype(BF16))
                                  * rc_ref[0]).astype(BF16)
            am_ref[rows(c), :] = _dot_nt(mq_ref[rows(c), :].astype(BF16), mk_ref[rows(c), :].astype(BF16))
        for c in chunks:
            for d in range(2):
                li = gate_col(c, 8 * d)
                lf = gate_col(c, 8 * d + 4)
                strict = (r_i > s_i) if d == 0 else (r_i < s_i)
                x_mat = jnp.where(strict, lf, jnp.where(r_i == s_i, li, 0.0))
                hi, lo = _split2(x_mat)
                d_mat = _dot(tri[d], hi) + _dot(tri[d], lo)
                e = 0 if d == 0 else L - 1
                bcum = (jnp.broadcast_to(d_mat[:, e:e + 1], (L, HEAD_DIM))
                        + (lf[e:e + 1, :] - li[e:e + 1, :]))
                causal = (s_i <= r_i) if d == 0 else (s_i >= r_i)
                logi_ref[d, rows(c), :] = jnp.where(causal, d_mat, -jnp.inf)
                bc_ref[d, rows(c), :] = bcum
                b_end = bcum[L - 1:L, :] if d == 0 else bcum[0:1, :]
                cs_ref[_B_END + d, c:c + 1, :] = b_end
                cs_ref[_MAX_LWE + d, c:c + 1, :] = jnp.max(b_end - bcum + li, axis=0, keepdims=True)
        for c in chunks:
            kf = rk_ref[rows(c), :]
            vb = rv_ref[rows(c), :]
            for d in range(2):
                skv_ref[d, c] = _dot_tn((kf * rc_ref[3 + d]).astype(BF16), vb)

    def scan_passes(chunks, seqs):
        m_final = {}
        for s in seqs:
            for d in range(2):
                m = sm_in[d:d + 1, :] * LOG2_E if latent else jnp.zeros((1, HEAD_DIM), F32)
                order = range(cps) if d == 0 else range(cps - 1, -1, -1)
                for j in order:
                    c = s * cps + j
                    cs_ref[_M_PREV + d, c:c + 1, :] = m
                    b_end = cs_ref[_B_END + d, c:c + 1, :]
                    m_new = jnp.maximum(b_end + m, cs_ref[_MAX_LWE + d, c:c + 1, :])
                    cs_ref[_M_NEW + d, c:c + 1, :] = m_new
                    cs_ref[_DECAY + d, c:c + 1, :] = jnp.exp2(b_end + m - m_new)
                    m = m_new
                m_final[(s, d)] = m

        for c in chunks:
            kf = mk_ref[rows(c), :]
            vx = mvx_ref[rows(c), :]
            for d in range(2):
                log_w_end = cs_ref[_B_END + d, c:c + 1, :] - bc_ref[d, rows(c), :] + gate_col(c, 8 * d)
                w_end = jnp.exp2(log_w_end - cs_ref[_M_NEW + d, c:c + 1, :])
                ckv_ref[d, c] = _dot_tn((kf * w_end).astype(BF16), vx)

        for s in seqs:
            for d in range(2):
                if latent:
                    s_state = sr_in[d]
                    n_rep = jnp.broadcast_to(sn_in[d], (HEAD_DIM, HEAD_DIM)).T
                    c_state = jnp.concatenate([sc_in[d], n_rep], axis=1)
                else:
                    s_state = jnp.zeros((HEAD_DIM, HEAD_DIM), F32)
                    c_state = jnp.zeros((HEAD_DIM, 2 * HEAD_DIM), F32)
                order = range(cps) if d == 0 else range(cps - 1, -1, -1)
                for j in order:
                    c = s * cps + j
                    inc = skv_ref[d, c]
                    skv_ref[d, c] = s_state
                    s_state = s_state * chunk_decay[d] + inc
                    inc = ckv_ref[d, c]
                    ckv_ref[d, c] = c_state
                    decay = cs_ref[_DECAY + d, c:c + 1, :]
                    c_state = c_state * jnp.concatenate([decay, decay], axis=1) + inc
                if not latent:
                    so_ref[s, d] = s_state
                    co_ref[s, d] = c_state[:, 0:128]
                    no_ref[s, d:d + 1, :] = c_state[:, 128:256].T[0:1, :]
                    mo_ref[s, d:d + 1, :] = m_final[(s, d)] * LN_2

        for c in chunks:
            qf = rq_ref[rows(c), :]
            lhs = jnp.concatenate([ar_ref[rows(c), :], (qf * rc_ref[1]).astype(BF16),
                                   (qf * rc_ref[2]).astype(BF16)], axis=1)
            rhs = jnp.concatenate([rv_ref[rows(c), :], skv_ref[0, c].astype(BF16),
                                   skv_ref[1, c].astype(BF16)], axis=0)
            rq_ref[rows(c), :] = _dot(lhs, rhs)
            qf = mq_ref[rows(c), :]
            a_mat = am_ref[rows(c), :]
            vx = mvx_ref[rows(c), :]
            h_sum = None
            for d in range(2):
                log_intra = logi_ref[d, rows(c), :]
                log_inter = bc_ref[d, rows(c), :] + cs_ref[_M_PREV + d, c:c + 1, :]
                m_t = jnp.maximum(log_inter, jnp.max(log_intra, axis=1, keepdims=True))
                w_inter = jnp.exp2(log_inter - m_t)
                w_intra = jnp.exp2(log_intra - m_t)
                lhs = jnp.concatenate([(a_mat * w_intra).astype(BF16), (qf * w_inter).astype(BF16)], axis=1)
                rhs = jnp.concatenate([vx, ckv_ref[d, c].astype(BF16)], axis=0)
                res = _dot(lhs, rhs)
                h_dir = res[:, 0:128] / jnp.maximum(jnp.abs(res[:, 128:256]), jnp.exp2(-m_t))
                h_sum = h_dir if h_sum is None else h_sum + h_dir
            am_ref[rows(c), :] = h_sum

    pl.when(live)(project_and_scan)

    @pl.when(step == n_steps)
    def _():
        def body(i, carry):
            finish_rows(pl.ds(pl.multiple_of(i * ROW_TILE, ROW_TILE), ROW_TILE))
            return carry

        lax.fori_loop(0, rows_total // ROW_TILE, body, 0)


def _mixer(x2d, mod3, mod_row, norm1_g, w_pair, w_gate, head_params, *, latent, n_seq, cps,
           rope=None, states=None):
    n_tok = x2d.shape[0]
    rows_blk = n_seq * cps * CHUNK
    n_blk = n_tok // rows_blk
    n_chunks = n_seq * cps
    n_steps = n_blk * N_HEADS
    kern = functools.partial(_mixer_kernel, latent=latent, n_seq=n_seq, cps=cps, n_steps=n_steps)

    def cur(f):
        def index_map(j):
            item = jnp.minimum(j, n_steps - 1)
            return f(item // N_HEADS, item % N_HEADS)
        return index_map

    def prev(f):
        def index_map(j):
            item = jnp.maximum(j - 1, 0)
            return f(item // N_HEADS, item % N_HEADS)
        return index_map

    once = pl.Buffered(1)
    in_specs = [
        pl.BlockSpec((rows_blk, D_MODEL), cur(lambda b, h: (b, 0))),
        pl.BlockSpec((None, 6, D_MODEL), cur(lambda b, h: (mod_row(b), 0, 0))),
        pl.BlockSpec((1, D_MODEL), lambda j: (0, 0)),
        pl.BlockSpec((D_MODEL, PAIR_COLS), cur(lambda b, h: (0, h))),
        pl.BlockSpec((D_MODEL, HEAD_DIM), lambda j: (0, 0)),
        pl.BlockSpec((None, 16, HEAD_DIM), cur(lambda b, h: (h, 0, 0))),
        pl.BlockSpec((None, 16, HEAD_DIM), prev(lambda b, h: (h, 0, 0))),
    ]
    args = [x2d, mod3, norm1_g, w_pair, w_gate, head_params, head_params]
    y_shape = jax.ShapeDtypeStruct((n_tok, D_MODEL), BF16)
    y_spec = pl.BlockSpec((rows_blk, 2 * HEAD_DIM), prev(lambda b, h: (b, h)))
    if latent:
        assert n_seq == 1
        cos2, sin2 = rope
        s_ret, s_c, s_n, s_m = states
        in_specs += [
            pl.BlockSpec((rows_blk, HEAD_DIM), lambda j: (0, 0), pipeline_mode=once),
            pl.BlockSpec((rows_blk, HEAD_DIM), lambda j: (0, 0), pipeline_mode=once),
            pl.BlockSpec((None, None, 2, None, HEAD_DIM, HEAD_DIM), cur(lambda b, h: (b, 0, 0, h, 0, 0))),
            pl.BlockSpec((None, None, 2, None, HEAD_DIM, HEAD_DIM), cur(lambda b, h: (b, 0, 0, h, 0, 0))),
            pl.BlockSpec((None, None, 2, 1, HEAD_DIM), cur(lambda b, h: (b, h, 0, 0, 0))),
            pl.BlockSpec((None, None, 2, HEAD_DIM), cur(lambda b, h: (b, h, 0, 0))),
        ]
        args += [cos2, sin2, s_ret, s_c, s_n, s_m]
        out_shape = y_shape
        out_specs = y_spec
    else:
        bsz = n_blk * n_seq
        st = jax.ShapeDtypeStruct((bsz, 1, 2, N_HEADS, HEAD_DIM, HEAD_DIM), F32)
        vec = jax.ShapeDtypeStruct((bsz, N_HEADS, 2, HEAD_DIM), F32)
        st_spec = pl.BlockSpec((n_seq, None, 2, None, HEAD_DIM, HEAD_DIM), cur(lambda b, h: (b, 0, 0, h, 0, 0)))
        vec_spec = pl.BlockSpec((n_seq, None, 2, HEAD_DIM), cur(lambda b, h: (b, h, 0, 0)))
        out_shape = (y_shape, st, st, vec, vec)
        out_specs = (y_spec, st_spec, st_spec, vec_spec, vec_spec)
    col = lambda dt: pltpu.VMEM((rows_blk, HEAD_DIM), dt)
    scratch = [
        pltpu.VMEM((rows_blk, D_MODEL), BF16),
        col(F32), col(F32), col(BF16), col(F32),
        col(F32), col(F32),
        pltpu.VMEM((rows_blk, 2 * HEAD_DIM), BF16),
        col(F32), col(F32),
        col(BF16), col(F32),
        pltpu.VMEM((2, rows_blk, HEAD_DIM), F32),
        pltpu.VMEM((2, rows_blk, HEAD_DIM), F32),
        pltpu.VMEM((2, n_chunks, HEAD_DIM, HEAD_DIM), F32),
        pltpu.VMEM((2, n_chunks, HEAD_DIM, 2 * HEAD_DIM), F32),
        pltpu.VMEM((10, max(n_chunks, 8), HEAD_DIM), F32),
        pltpu.VMEM((5, CHUNK, CHUNK), F32),
        pltpu.VMEM((rows_blk + 2 * CONV_HALO, HEAD_DIM), F32),
        pltpu.VMEM((rows_blk + 2 * CONV_HALO, HEAD_DIM), F32),
        col(F32),
    ]
    return pl.pallas_call(
        kern,
        out_shape=out_shape,
        grid=(n_steps + 1,),
        in_specs=in_specs,
        out_specs=out_specs,
        scratch_shapes=scratch,
        compiler_params=pltpu.CompilerParams(
            dimension_semantics=("arbitrary",), vmem_limit_bytes=VMEM_LIMIT),
        name="mixer_latent" if latent else "mixer_context",
    )(*args)


def _ffn_kernel(x_ref, y_ref, mod_ref, wo_ref, g2_ref, w1_ref, w2_ref, gf_ref, out_ref):
    g1 = mod_ref[2:3, :]
    sh2 = mod_ref[3:4, :]
    sc2 = mod_ref[4:5, :]
    g2 = mod_ref[5:6, :]
    x1 = x_ref[...] + g1 * _dot(y_ref[...], wo_ref[...])
    h2 = (_rms(x1, g2_ref[...]) * (1.0 + sc2) + sh2).astype(BF16)
    f = jnp.zeros_like(x1)
    for j in range(D_FF // D_MODEL):
        cols = slice(j * D_MODEL, (j + 1) * D_MODEL)
        hid = jnp.maximum(_dot(h2, w1_ref[:, cols]), 0.0)
        f = f + _dot((hid * hid).astype(BF16), w2_ref[cols, :])
    out_ref[...] = _rms(x1 + g2 * f, gf_ref[...])


def _ffn(x2d, y2d, mod3, mod_row, w_out, norm2_g, w_ff1, w_ff2, final_g):
    n_tok = x2d.shape[0]
    const = lambda i: (0, 0)
    return pl.pallas_call(
        _ffn_kernel,
        out_shape=jax.ShapeDtypeStruct((n_tok, D_MODEL), F32),
        grid=(n_tok // FFN_ROWS,),
        in_specs=[
            pl.BlockSpec((FFN_ROWS, D_MODEL), lambda i: (i, 0)),
            pl.BlockSpec((FFN_ROWS, D_MODEL), lambda i: (i, 0)),
            pl.BlockSpec((None, 6, D_MODEL), lambda i: (mod_row(i), 0, 0)),
            pl.BlockSpec((D_MODEL, D_MODEL), const, pipeline_mode=pl.Buffered(1)),
            pl.BlockSpec((1, D_MODEL), const),
            pl.BlockSpec((D_MODEL, D_FF), const, pipeline_mode=pl.Buffered(1)),
            pl.BlockSpec((D_FF, D_MODEL), const, pipeline_mode=pl.Buffered(1)),
            pl.BlockSpec((1, D_MODEL), const),
        ],
        out_specs=pl.BlockSpec((FFN_ROWS, D_MODEL), lambda i: (i, 0)),
        compiler_params=pltpu.CompilerParams(
            dimension_semantics=("arbitrary",), vmem_limit_bytes=VMEM_LIMIT),
        name="outproj_mlp",
    )(x2d, y2d, mod3, w_out, norm2_g, w_ff1, w_ff2, final_g)


def _rope_tables(seq):
    pos = np.arange(seq)
    row = (pos // GRID_W).astype(np.float64)
    col = (pos % GRID_W).astype(np.float64)
    nf = HEAD_DIM // 4
    inv = ROPE_BASE ** (-np.arange(nf, dtype=np.float64) / nf)
    ang = np.concatenate([row[:, None] * inv, col[:, None] * inv], -1)
    cos = np.cos(ang)
    sin = np.sin(ang)
    cos2 = np.concatenate([cos, cos], -1).astype(np.float32)
    sin2 = np.concatenate([-sin, sin], -1).astype(np.float32)
    return jnp.asarray(cos2), jnp.asarray(sin2)


def kernel(x_prompt, x_sample, state_ret, state_mlstm_C, state_mlstm_n, state_mlstm_m, c, c_ctx,
           w_ada, b_ada, norm1_g, norm2_g, w_in, conv_w, ret_decay_logit, mlstm_gate_bias,
           ret_gn_g, mlstm_gn_g, w_out, w_ff1, w_ff2, final_g):
    assert w_ada.shape[0] == 1, "single-layer kernel"
    bp, tp, _ = x_prompt.shape
    bs, ts, _ = x_sample.shape
    assert tp % CHUNK == 0 and ts % CHUNK == 0 and bp % CTX_SEQS_PER_STEP == 0

    cond = jnp.concatenate([c_ctx[None, :], c, jnp.zeros((8 - 1 - bs, D_MODEL), F32)], 0)
    mod, w_pair, w_gate, w_out_p, w1, w2 = _prepare(cond, w_ada, b_ada, w_in, w_out, w_ff1, w_ff2)
    mod3 = mod[:1 + bs].reshape(1 + bs, 6, D_MODEL)


    cw = conv_w[0]
    hp_rows = [cw[j, :512].reshape(N_HEADS, HEAD_DIM) for j in range(3)]
    hp_rows += [cw[j, 512:].reshape(N_HEADS, HEAD_DIM) for j in range(3)]
    hp_rows += [ret_gn_g[0].reshape(N_HEADS, HEAD_DIM), mlstm_gn_g[0].reshape(N_HEADS, HEAD_DIM)]
    hp_rows += [jnp.broadcast_to(ret_decay_logit[0, d][:, None], (N_HEADS, HEAD_DIM)) for d in range(2)]
    gate_bias = jnp.pad(mlstm_gate_bias[0].reshape(1, N_GATE_COLS), ((0, 0), (HEAD_DIM - N_GATE_COLS, 0)))
    hp_rows += [jnp.broadcast_to(gate_bias, (N_HEADS, HEAD_DIM))]
    hp_rows += [jnp.zeros((N_HEADS, HEAD_DIM), F32)] * (16 - len(hp_rows))
    head_params = jnp.stack(hp_rows, axis=1).astype(F32)

    g1 = norm1_g[0][None, :]
    g2 = norm2_g[0][None, :]
    gf = final_g[None, :]
    xp2d = x_prompt.reshape(bp * tp, D_MODEL)
    xs2d = x_sample.reshape(bs * ts, D_MODEL)

    y_p, new_ret, new_c, new_n, new_m = _mixer(
        xp2d, mod3, lambda b: 0, g1, w_pair, w_gate, head_params,
        latent=False, n_seq=CTX_SEQS_PER_STEP, cps=tp // CHUNK)
    out_p = _ffn(xp2d, y_p, mod3, lambda i: 0, w_out_p, g2, w1, w2, gf).reshape(bp, tp, D_MODEL)

    s_n = jnp.transpose(state_mlstm_n[:, 0], (0, 2, 1, 3))[:, :, :, None, :]
    s_m = jnp.broadcast_to(jnp.transpose(state_mlstm_m[:, 0], (0, 2, 1))[..., None],
                           (bs, N_HEADS, 2, HEAD_DIM))
    y_s = _mixer(xs2d, mod3, lambda b: 1 + b, g1, w_pair, w_gate, head_params,
                 latent=True, n_seq=1, cps=ts // CHUNK,
                 rope=_rope_tables(ts), states=(state_ret, state_mlstm_C, s_n, s_m))
    tiles_per_seq = ts // FFN_ROWS
    out_s = _ffn(xs2d, y_s, mod3, lambda i: 1 + i // tiles_per_seq,
                 w_out_p, g2, w1, w2, gf).reshape(bs, ts, D_MODEL)

    new_n = jnp.transpose(new_n, (0, 2, 1, 3))[:, None]
    new_m = jnp.transpose(new_m[..., 0], (0, 2, 1))[:, None]
    return out_p, out_s, new_ret, new_c, new_n, new_m
```

```python
import functools

import numpy as np
import jax
import jax.numpy as jnp
from jax import lax
from jax.experimental import pallas as pl
from jax.experimental.pallas import tpu as pltpu

F32 = jnp.float32
BF16 = jnp.bfloat16

D_MODEL = 1024
N_HEADS = 4
HEAD_DIM = 128
CHUNK = 128
GRID_W = 64
D_FF = 4 * D_MODEL
EPS = 1e-6
ROPE_BASE = 10000.0
LOG2_E = 1.4426950408889634
LN_2 = 0.6931471805599453
PAIR_COLS = 8 * HEAD_DIM
N_GATE_COLS = 4 * N_HEADS
ROW_TILE = 256
FFN_ROWS = 512
CTX_SEQS_PER_STEP = 8
PROJ_SLAB_ROWS = 256
CONV_HALO = 8
VMEM_LIMIT = 60 * 1024 * 1024


def _dot(a, b):
    return jnp.dot(a, b, preferred_element_type=F32)


def _dot_nt(a, b):
    return lax.dot_general(a, b, (((1,), (1,)), ((), ())), preferred_element_type=F32)


def _dot_tn(a, b):
    return lax.dot_general(a, b, (((0,), (0,)), ((), ())), preferred_element_type=F32)


def _rms(x, g):
    return x * lax.rsqrt(jnp.mean(x * x, axis=-1, keepdims=True) + EPS) * g


def _group_norm(o, g):
    mu = jnp.mean(o, axis=-1, keepdims=True)
    c = o - mu
    var = jnp.mean(c * c, axis=-1, keepdims=True)
    return c * lax.rsqrt(var + EPS) * g


def _log_sigmoid(x):
    return jnp.minimum(x, 0.0) - jnp.log(1.0 + jnp.exp(-jnp.abs(x)))


def _sigmoid(x):
    return 1.0 / (1.0 + jnp.exp(-x))


def _split2(x):
    hi = x.astype(BF16)
    lo = (x - hi.astype(F32)).astype(BF16)
    return hi, lo


PREP_STEPS = 8


def _mod_kernel(cond_ref, w_ref, b_ref, out_ref):
    c = cond_ref[...]
    s = (c * _sigmoid(c)).astype(BF16)
    out_ref[...] = _dot(s, w_ref[...].astype(BF16)) + b_ref[...]


def _regroup_in_kernel(wt_ref, out_ref, gate_ref):
    for h in range(N_HEADS):
        for g in range(8):
            src = (g * N_HEADS + h) * HEAD_DIM
            dst = (h * 8 + g) * HEAD_DIM
            out_ref[:, dst:dst + HEAD_DIM] = wt_ref[src:src + HEAD_DIM, :].T.astype(BF16)
    n_rows = wt_ref.shape[0]
    n_gate = n_rows - N_HEADS * PAIR_COLS
    tail = wt_ref[n_rows - HEAD_DIM:n_rows, :].T
    lane = lax.broadcasted_iota(jnp.int32, tail.shape, 1)
    gate_ref[...] = jnp.where(lane >= HEAD_DIM - n_gate, tail, 0.0).astype(BF16)


def _regroup_out_kernel(w_ref, out_ref):
    for h in range(N_HEADS):
        for g in range(2):
            src = (g * N_HEADS + h) * HEAD_DIM
            dst = (h * 2 + g) * HEAD_DIM
            out_ref[dst:dst + HEAD_DIM, :] = w_ref[src:src + HEAD_DIM, :].astype(BF16)


def _prepare_kernel(cond_ref, wada_ref, bada_ref, wt_ref, wout_ref, w1_ref, w2_ref,
                    mod_ref, wpair_ref, wgate_ref, woutp_ref, w1o_ref, w2o_ref):
    _mod_kernel(cond_ref, wada_ref, bada_ref, mod_ref)
    _regroup_in_kernel(wt_ref, wpair_ref, wgate_ref)
    _regroup_out_kernel(wout_ref, woutp_ref)
    w1o_ref[...] = w1_ref[...].astype(BF16)
    w2o_ref[...] = w2_ref[...].astype(BF16)


def _prepare(cond, w_ada, b_ada, w_in, w_out, w_ff1, w_ff2):
    w_t = jnp.transpose(w_in[0])
    n_cols = w_t.shape[0]
    assert n_cols == N_HEADS * PAIR_COLS + N_GATE_COLS
    n_mod = w_ada.shape[-1]
    k_rows = D_MODEL // PREP_STEPS
    m_cols = n_mod // PREP_STEPS
    f_cols = D_FF // PREP_STEPS
    bf = lambda shape: jax.ShapeDtypeStruct(shape, BF16)
    return pl.pallas_call(
        _prepare_kernel,
        out_shape=(jax.ShapeDtypeStruct((cond.shape[0], n_mod), F32),
                   bf((D_MODEL, N_HEADS * PAIR_COLS)), bf((D_MODEL, HEAD_DIM)),
                   bf((D_MODEL, D_MODEL)), bf((D_MODEL, D_FF)), bf((D_FF, D_MODEL))),
        grid=(PREP_STEPS,),
        in_specs=[pl.BlockSpec(cond.shape, lambda i: (0, 0)),
                  pl.BlockSpec((None, D_MODEL, m_cols), lambda i: (0, 0, i)),
                  pl.BlockSpec((1, m_cols), lambda i: (0, i)),
                  pl.BlockSpec((n_cols, k_rows), lambda i: (0, i)),
                  pl.BlockSpec((None, D_MODEL, k_rows), lambda i: (0, 0, i)),
                  pl.BlockSpec((None, D_MODEL, f_cols), lambda i: (0, 0, i)),
                  pl.BlockSpec((None, f_cols, D_MODEL), lambda i: (0, i, 0))],
        out_specs=(pl.BlockSpec((cond.shape[0], m_cols), lambda i: (0, i)),
                   pl.BlockSpec((k_rows, N_HEADS * PAIR_COLS), lambda i: (i, 0)),
                   pl.BlockSpec((k_rows, HEAD_DIM), lambda i: (i, 0)),
                   pl.BlockSpec((D_MODEL, k_rows), lambda i: (0, i)),
                   pl.BlockSpec((D_MODEL, f_cols), lambda i: (0, i)),
                   pl.BlockSpec((f_cols, D_MODEL), lambda i: (i, 0))),
        compiler_params=pltpu.CompilerParams(
            dimension_semantics=("arbitrary",), vmem_limit_bytes=VMEM_LIMIT),
        name="prepare_weights",
    )(cond, w_ada, b_ada, w_t, w_out, w_ff1, w_ff2)


_B_END, _MAX_LWE, _M_PREV, _M_NEW, _DECAY = 0, 2, 4, 6, 8


def _mixer_kernel(*refs, latent, n_seq, cps, n_steps):
    L = CHUNK
    n_chunks = n_seq * cps
    rows_total = n_chunks * L
    seq_len = cps * L
    assert seq_len & (seq_len - 1) == 0
    if latent:
        (x_ref, mod_ref, g1_ref, w_ref, wg_ref, hp_ref, hp_prev_ref, cos_ref, sin_ref,
         sr_in, sc_in, sn_in, sm_in, y_ref, *scratch) = refs
    else:
        (x_ref, mod_ref, g1_ref, w_ref, wg_ref, hp_ref, hp_prev_ref,
         y_ref, so_ref, co_ref, no_ref, mo_ref, *scratch) = refs
    (hn_ref, rq_ref, rk_ref, rv_ref, rg_ref, mq_ref, mk_ref, mvx_ref, mog_ref, gt_ref,
     ar_ref, am_ref, logi_ref, bc_ref, skv_ref, ckv_ref, cs_ref, rc_ref, rawq_ref, rawk_ref,
     gall_ref) = scratch

    step = pl.program_id(0)
    live = step < n_steps
    head = lax.rem(jnp.minimum(step, n_steps - 1), N_HEADS)

    @pl.when(step == 0)
    def _():
        for ref in (rq_ref, am_ref, rg_ref, mog_ref):
            ref[...] = jnp.zeros_like(ref)

    @pl.when(jnp.logical_and(head == 0, live))
    def _():
        gain = g1_ref[...] * (1.0 + mod_ref[1:2, :])
        sh1 = mod_ref[0:1, :]
        gate_bias = hp_ref[10:11, :]
        lane_t = lax.broadcasted_iota(jnp.int32, (ROW_TILE, HEAD_DIM), 1)
        is_forget = jnp.bitwise_and(lane_t, 4) == 4

        def body(i, carry):
            r = pl.ds(pl.multiple_of(i * ROW_TILE, ROW_TILE), ROW_TILE)
            xr = x_ref[r, :]
            inv = lax.rsqrt(jnp.mean(xr * xr, axis=-1, keepdims=True) + EPS)
            hn_t = (xr * inv * gain + sh1).astype(BF16)
            hn_ref[r, :] = hn_t
            gates = _dot(hn_t, wg_ref[...]) + gate_bias
            gall_ref[r, :] = jnp.where(is_forget, _log_sigmoid(gates), gates) * LOG2_E
            return carry

        lax.fori_loop(0, rows_total // ROW_TILE, body, 0, unroll=4)

    hp = hp_ref[...]
    scale = HEAD_DIM ** -0.5

    r_i = lax.broadcasted_iota(jnp.int32, (L, L), 0)
    s_i = lax.broadcasted_iota(jnp.int32, (L, L), 1)
    r_f = r_i.astype(F32)
    s_f = s_i.astype(F32)
    lg_f = _log_sigmoid(hp[8:9])
    lg_b = _log_sigmoid(hp[9:10])
    rc_ref[0] = (jnp.where(r_i >= s_i, jnp.exp(lg_f * jnp.where(r_i >= s_i, r_f - s_f, 0.0)), 0.0)
                 + jnp.where(s_i >= r_i, jnp.exp(lg_b * jnp.where(s_i >= r_i, s_f - r_f, 0.0)), 0.0))
    rc_ref[1] = jnp.exp(lg_f * (r_f + 1.0))
    rc_ref[2] = jnp.exp(lg_b * (L - r_f))
    rc_ref[3] = jnp.exp(lg_f * (L - 1.0 - r_f))
    rc_ref[4] = jnp.exp(lg_b * r_f)
    chunk_decay = (jnp.exp(lg_f * float(L)), jnp.exp(lg_b * float(L)))

    tri = (jnp.where(r_i >= s_i, 1.0, 0.0).astype(BF16),
           jnp.where(s_i >= r_i, 1.0, 0.0).astype(BF16))

    for raw_ref in (rawq_ref, rawk_ref):
        raw_ref[0:CONV_HALO, :] = jnp.zeros((CONV_HALO, HEAD_DIM), F32)
        raw_ref[rows_total + CONV_HALO:rows_total + 2 * CONV_HALO, :] = jnp.zeros((CONV_HALO, HEAD_DIM), F32)

    def rows(c):
        return slice(c * L, (c + 1) * L)

    def gate_col(c, lane):
        g = gt_ref[rows(c), :]
        return jnp.broadcast_to(g[:, lane:lane + 1], (L, HEAD_DIM))

    hp_prev = hp_prev_ref[...]

    def finish_rows(r):
        rg = rg_ref[r, :]
        ret_y = _group_norm(rq_ref[r, :], hp_prev[6:7]) * (rg * _sigmoid(rg))
        y_ref[r, 0:128] = ret_y.astype(BF16)
        ml_y = _group_norm(am_ref[r, :], hp_prev[7:8]) * _sigmoid(mog_ref[r, :])
        y_ref[r, 128:256] = ml_y.astype(BF16)

    def project(blk, n_rows):
        hn = hn_ref[blk, :]
        pr = _dot(hn, w_ref[:, 0:256])
        q = pr[:, 0:128]
        k = pr[:, 128:256] * scale
        if latent:
            cos2 = cos_ref[blk, :]
            sin2 = sin_ref[blk, :]
            q = q * cos2 + pltpu.roll(q, HEAD_DIM // 2, axis=1) * sin2
            k = k * cos2 + pltpu.roll(k, HEAD_DIM // 2, axis=1) * sin2
        rq_ref[blk, :] = q
        rk_ref[blk, :] = k

        pv = _dot(hn, w_ref[:, 256:512])
        rv_ref[blk, :] = pv[:, 0:128].astype(BF16)
        rg_ref[blk, :] = pv[:, 128:256]

        pm = _dot(hn, w_ref[:, 512:768])
        halo_blk = slice(blk.start + CONV_HALO, blk.stop + CONV_HALO)
        rawq_ref[halo_blk, :] = pm[:, 0:128]
        rawk_ref[halo_blk, :] = pm[:, 128:256]

        po = _dot(hn, w_ref[:, 768:1024])
        mvx_ref[blk, 0:128] = po[:, 0:128].astype(BF16)
        mvx_ref[blk, 128:256] = jnp.ones((n_rows, HEAD_DIM), BF16)
        mog_ref[blk, :] = po[:, 128:256]

        gt_ref[blk, :] = pltpu.roll(gall_ref[blk, :], N_GATE_COLS - head, axis=1)

    def conv_silu(blk, n_rows):
        sub = lax.broadcasted_iota(jnp.int32, (CONV_HALO, HEAD_DIM), 0)
        starts_seq = blk.start % seq_len == 0
        ends_seq = blk.stop % seq_len == 0
        for raw_ref, dst_ref, taps, post in ((rawq_ref, mq_ref, hp[0:3], 1.0), (rawk_ref, mk_ref, hp[3:6], scale)):
            lo = blk.start + CONV_HALO
            prev = raw_ref[lo - 1:lo - 1 + n_rows, :]
            nxt = raw_ref[lo + 1:lo + 1 + n_rows, :]
            if starts_seq:
                first = jnp.where(sub == 0, 0.0, prev[0:CONV_HALO, :])
                prev = jnp.concatenate([first, prev[CONV_HALO:, :]], axis=0)
            if ends_seq:
                last = jnp.where(sub == CONV_HALO - 1, 0.0, nxt[n_rows - CONV_HALO:, :])
                nxt = jnp.concatenate([nxt[:n_rows - CONV_HALO, :], last], axis=0)
            out = prev * taps[0:1] + raw_ref[lo:lo + n_rows, :] * taps[1:2] + nxt * taps[2:3]
            out = out * _sigmoid(out)
            dst_ref[blk, :] = out if post == 1.0 else out * post

    def project_and_scan():
        slab_rows = min(seq_len, PROJ_SLAB_ROWS)
        assert seq_len % slab_rows == 0
        slabs = [slice(s0, s0 + slab_rows) for s0 in range(0, rows_total, slab_rows)]
        for i, slab in enumerate(slabs):
            finish_rows(slab)
            project(slab, slab_rows)
            if i > 0:
                conv_silu(slabs[i - 1], slab_rows)
        conv_silu(slabs[-1], slab_rows)
        pre_pass(range(n_chunks))
        scan_passes(range(n_chunks), range(n_seq))

    def pre_pass(chunks):
        for c in chunks:
            ar_ref[rows(c), :] = (_dot_nt(rq_ref[rows(c), :].astype(BF16), rk_ref[rows(c), :].astype(BF16))
                                  * rc_ref[0]).astype(BF16)
            am_ref[rows(c), :] = _dot_nt(mq_ref[rows(c), :].astype(BF16), mk_ref[rows(c), :].astype(BF16))
        for c in chunks:
            for d in range(2):
                li = gate_col(c, 8 * d)
                lf = gate_col(c, 8 * d + 4)
                strict = (r_i > s_i) if d == 0 else (r_i < s_i)
                x_mat = jnp.where(strict, lf, jnp.where(r_i == s_i, li, 0.0))
                hi, lo = _split2(x_mat)
                d_mat = _dot(tri[d], hi) + _dot(tri[d], lo)
                e = 0 if d == 0 else L - 1
                bcum = (jnp.broadcast_to(d_mat[:, e:e + 1], (L, HEAD_DIM))
                        + (lf[e:e + 1, :] - li[e:e + 1, :]))
                causal = (s_i <= r_i) if d == 0 else (s_i >= r_i)
                logi_ref[d, rows(c), :] = jnp.where(causal, d_mat, -jnp.inf)
                bc_ref[d, rows(c), :] = bcum
                b_end = bcum[L - 1:L, :] if d == 0 else bcum[0:1, :]
                cs_ref[_B_END + d, c:c + 1, :] = b_end
                cs_ref[_MAX_LWE + d, c:c + 1, :] = jnp.max(b_end - bcum + li, axis=0, keepdims=True)
        for c in chunks:
            kf = rk_ref[rows(c), :]
            vb = rv_ref[rows(c), :]
            for d in range(2):
                skv_ref[d, c] = _dot_tn((kf * rc_ref[3 + d]).astype(BF16), vb)

    def scan_passes(chunks, seqs):
        m_final = {}
        for s in seqs:
            for d in range(2):
                m = sm_in[d:d + 1, :] * LOG2_E if latent else jnp.zeros((1, HEAD_DIM), F32)
                order = range(cps) if d == 0 else range(cps - 1, -1, -1)
                for j in order:
                    c = s * cps + j
                    cs_ref[_M_PREV + d, c:c + 1, :] = m
                    b_end = cs_ref[_B_END + d, c:c + 1, :]
                    m_new = jnp.maximum(b_end + m, cs_ref[_MAX_LWE + d, c:c + 1, :])
                    cs_ref[_M_NEW + d, c:c + 1, :] = m_new
                    cs_ref[_DECAY + d, c:c + 1, :] = jnp.exp2(b_end + m - m_new)
                    m = m_new
                m_final[(s, d)] = m

        for c in chunks:
            kf = mk_ref[rows(c), :]
            vx = mvx_ref[rows(c), :]
            for d in range(2):
                log_w_end = cs_ref[_B_END + d, c:c + 1, :] - bc_ref[d, rows(c), :] + gate_col(c, 8 * d)
                w_end = jnp.exp2(log_w_end - cs_ref[_M_NEW + d, c:c + 1, :])
                ckv_ref[d, c] = _dot_tn((kf * w_end).astype(BF16), vx)

        for s in seqs:
            for d in range(2):
                if latent:
                    s_state = sr_in[d]
                    n_rep = jnp.broadcast_to(sn_in[d], (HEAD_DIM, HEAD_DIM)).T
                    c_state = jnp.concatenate([sc_in[d], n_rep], axis=1)
                else:
                    s_state = jnp.zeros((HEAD_DIM, HEAD_DIM), F32)
                    c_state = jnp.zeros((HEAD_DIM, 2 * HEAD_DIM), F32)
                order = range(cps) if d == 0 else range(cps - 1, -1, -1)
                for j in order:
                    c = s * cps + j
                    inc = skv_ref[d, c]
                    skv_ref[d, c] = s_state
                    s_state = s_state * chunk_decay[d] + inc
                    inc = ckv_ref[d, c]
                    ckv_ref[d, c] = c_state
                    decay = cs_ref[_DECAY + d, c:c + 1, :]
                    c_state = c_state * jnp.concatenate([decay, decay], axis=1) + inc
                if not latent:
                    so_ref[s, d] = s_state
                    co_ref[s, d] = c_state[:, 0:128]
                    no_ref[s, d:d + 1, :] = c_state[:, 128:256].T[0:1, :]
                    mo_ref[s, d:d + 1, :] = m_final[(s, d)] * LN_2

        for c in chunks:
            qf = rq_ref[rows(c), :]
            lhs = jnp.concatenate([ar_ref[rows(c), :], (qf * rc_ref[1]).astype(BF16),
                                   (qf * rc_ref[2]).astype(BF16)], axis=1)
            rhs = jnp.concatenate([rv_ref[rows(c), :], skv_ref[0, c].astype(BF16),
                                   skv_ref[1, c].astype(BF16)], axis=0)
            rq_ref[rows(c), :] = _dot(lhs, rhs)
            qf = mq_ref[rows(c), :]
            a_mat = am_ref[rows(c), :]
            vx = mvx_ref[rows(c), :]
            h_sum = None
            for d in range(2):
                log_intra = logi_ref[d, rows(c), :]
                log_inter = bc_ref[d, rows(c), :] + cs_ref[_M_PREV + d, c:c + 1, :]
                m_t = jnp.maximum(log_inter, jnp.max(log_intra, axis=1, keepdims=True))
                w_inter = jnp.exp2(log_inter - m_t)
                w_intra = jnp.exp2(log_intra - m_t)
                lhs = jnp.concatenate([(a_mat * w_intra).astype(BF16), (qf * w_inter).astype(BF16)], axis=1)
                rhs = jnp.concatenate([vx, ckv_ref[d, c].astype(BF16)], axis=0)
                res = _dot(lhs, rhs)
                h_dir = res[:, 0:128] / jnp.maximum(jnp.abs(res[:, 128:256]), jnp.exp2(-m_t))
                h_sum = h_dir if h_sum is None else h_sum + h_dir
            am_ref[rows(c), :] = h_sum

    pl.when(live)(project_and_scan)

    @pl.when(step == n_steps)
    def _():
        def body(i, carry):
            finish_rows(pl.ds(pl.multiple_of(i * ROW_TILE, ROW_TILE), ROW_TILE))
            return carry

        lax.fori_loop(0, rows_total // ROW_TILE, body, 0)


def _mixer(x2d, mod3, mod_row, norm1_g, w_pair, w_gate, head_params, *, latent, n_seq, cps,
           rope=None, states=None):
    n_tok = x2d.shape[0]
    rows_blk = n_seq * cps * CHUNK
    n_blk = n_tok // rows_blk
    n_chunks = n_seq * cps
    n_steps = n_blk * N_HEADS
    kern = functools.partial(_mixer_kernel, latent=latent, n_seq=n_seq, cps=cps, n_steps=n_steps)

    def cur(f):
        def index_map(j):
            item = jnp.minimum(j, n_steps - 1)
            return f(item // N_HEADS, item % N_HEADS)
        return index_map

    def prev(f):
        def index_map(j):
            item = jnp.maximum(j - 1, 0)
            return f(item // N_HEADS, item % N_HEADS)
        return index_map

    once = pl.Buffered(1)
    in_specs = [
        pl.BlockSpec((rows_blk, D_MODEL), cur(lambda b, h: (b, 0))),
        pl.BlockSpec((None, 6, D_MODEL), cur(lambda b, h: (mod_row(b), 0, 0))),
        pl.BlockSpec((1, D_MODEL), lambda j: (0, 0)),
        pl.BlockSpec((D_MODEL, PAIR_COLS), cur(lambda b, h: (0, h))),
        pl.BlockSpec((D_MODEL, HEAD_DIM), lambda j: (0, 0)),
        pl.BlockSpec((None, 16, HEAD_DIM), cur(lambda b, h: (h, 0, 0))),
        pl.BlockSpec((None, 16, HEAD_DIM), prev(lambda b, h: (h, 0, 0))),
    ]
    args = [x2d, mod3, norm1_g, w_pair, w_gate, head_params, head_params]
    y_shape = jax.ShapeDtypeStruct((n_tok, D_MODEL), BF16)
    y_spec = pl.BlockSpec((rows_blk, 2 * HEAD_DIM), prev(lambda b, h: (b, h)))
    if latent:
        assert n_seq == 1
        cos2, sin2 = rope
        s_ret, s_c, s_n, s_m = states
        in_specs += [
            pl.BlockSpec((rows_blk, HEAD_DIM), lambda j: (0, 0), pipeline_mode=once),
            pl.BlockSpec((rows_blk, HEAD_DIM), lambda j: (0, 0), pipeline_mode=once),
            pl.BlockSpec((None, None, 2, None, HEAD_DIM, HEAD_DIM), cur(lambda b, h: (b, 0, 0, h, 0, 0))),
            pl.BlockSpec((None, None, 2, None, HEAD_DIM, HEAD_DIM), cur(lambda b, h: (b, 0, 0, h, 0, 0))),
            pl.BlockSpec((None, None, 2, 1, HEAD_DIM), cur(lambda b, h: (b, h, 0, 0, 0))),
            pl.BlockSpec((None, None, 2, HEAD_DIM), cur(lambda b, h: (b, h, 0, 0))),
        ]
        args += [cos2, sin2, s_ret, s_c, s_n, s_m]
        out_shape = y_shape
        out_specs = y_spec
    else:
        bsz = n_blk * n_seq
        st = jax.ShapeDtypeStruct((bsz, 1, 2, N_HEADS, HEAD_DIM, HEAD_DIM), F32)
        vec = jax.ShapeDtypeStruct((bsz, N_HEADS, 2, HEAD_DIM), F32)
        st_spec = pl.BlockSpec((n_seq, None, 2, None, HEAD_DIM, HEAD_DIM), cur(lambda b, h: (b, 0, 0, h, 0, 0)))
        vec_spec = pl.BlockSpec((n_seq, None, 2, HEAD_DIM), cur(lambda b, h: (b, h, 0, 0)))
        out_shape = (y_shape, st, st, vec, vec)
        out_specs = (y_spec, st_spec, st_spec, vec_spec, vec_spec)
    col = lambda dt: pltpu.VMEM((rows_blk, HEAD_DIM), dt)
    scratch = [
        pltpu.VMEM((rows_blk, D_MODEL), BF16),
        col(F32), col(F32), col(BF16), col(F32),
        col(F32), col(F32),
        pltpu.VMEM((rows_blk, 2 * HEAD_DIM), BF16),
        col(F32), col(F32),
        col(BF16), col(F32),
        pltpu.VMEM((2, rows_blk, HEAD_DIM), F32),
        pltpu.VMEM((2, rows_blk, HEAD_DIM), F32),
        pltpu.VMEM((2, n_chunks, HEAD_DIM, HEAD_DIM), F32),
        pltpu.VMEM((2, n_chunks, HEAD_DIM, 2 * HEAD_DIM), F32),
        pltpu.VMEM((10, max(n_chunks, 8), HEAD_DIM), F32),
        pltpu.VMEM((5, CHUNK, CHUNK), F32),
        pltpu.VMEM((rows_blk + 2 * CONV_HALO, HEAD_DIM), F32),
        pltpu.VMEM((rows_blk + 2 * CONV_HALO, HEAD_DIM), F32),
        col(F32),
    ]
    return pl.pallas_call(
        kern,
        out_shape=out_shape,
        grid=(n_steps + 1,),
        in_specs=in_specs,
        out_specs=out_specs,
        scratch_shapes=scratch,
        compiler_params=pltpu.CompilerParams(
            dimension_semantics=("arbitrary",), vmem_limit_bytes=VMEM_LIMIT),
        name="mixer_latent" if latent else "mixer_context",
    )(*args)


def _ffn_kernel(x_ref, y_ref, mod_ref, wo_ref, g2_ref, w1_ref, w2_ref, gf_ref, out_ref):
    g1 = mod_ref[2:3, :]
    sh2 = mod_ref[3:4, :]
    sc2 = mod_ref[4:5, :]
    g2 = mod_ref[5:6, :]
    x1 = x_ref[...] + g1 * _dot(y_ref[...], wo_ref[...])
    h2 = (_rms(x1, g2_ref[...]) * (1.0 + sc2) + sh2).astype(BF16)
    f = jnp.zeros_like(x1)
    for j in range(D_FF // D_MODEL):
        cols = slice(j * D_MODEL, (j + 1) * D_MODEL)
        hid = jnp.maximum(_dot(h2, w1_ref[:, cols]), 0.0)
        f = f + _dot((hid * hid).astype(BF16), w2_ref[cols, :])
    out_ref[...] = _rms(x1 + g2 * f, gf_ref[...])


def _ffn(x2d, y2d, mod3, mod_row, w_out, norm2_g, w_ff1, w_ff2, final_g):
    n_tok = x2d.shape[0]
    const = lambda i: (0, 0)
    return pl.pallas_call(
        _ffn_kernel,
        out_shape=jax.ShapeDtypeStruct((n_tok, D_MODEL), F32),
        grid=(n_tok // FFN_ROWS,),
        in_specs=[
            pl.BlockSpec((FFN_ROWS, D_MODEL), lambda i: (i, 0)),
            pl.BlockSpec((FFN_ROWS, D_MODEL), lambda i: (i, 0)),
            pl.BlockSpec((None, 6, D_MODEL), lambda i: (mod_row(i), 0, 0)),
            pl.BlockSpec((D_MODEL, D_MODEL), const, pipeline_mode=pl.Buffered(1)),
            pl.BlockSpec((1, D_MODEL), const),
            pl.BlockSpec((D_MODEL, D_FF), const, pipeline_mode=pl.Buffered(1)),
            pl.BlockSpec((D_FF, D_MODEL), const, pipeline_mode=pl.Buffered(1)),
            pl.BlockSpec((1, D_MODEL), const),
        ],
        out_specs=pl.BlockSpec((FFN_ROWS, D_MODEL), lambda i: (i, 0)),
        compiler_params=pltpu.CompilerParams(
            dimension_semantics=("arbitrary",), vmem_limit_bytes=VMEM_LIMIT),
        name="outproj_mlp",
    )(x2d, y2d, mod3, w_out, norm2_g, w_ff1, w_ff2, final_g)


def _rope_tables(seq):
    pos = np.arange(seq)
    row = (pos // GRID_W).astype(np.float64)
    col = (pos % GRID_W).astype(np.float64)
    nf = HEAD_DIM // 4
    inv = ROPE_BASE ** (-np.arange(nf, dtype=np.float64) / nf)
    ang = np.concatenate([row[:, None] * inv, col[:, None] * inv], -1)
    cos = np.cos(ang)
    sin = np.sin(ang)
    cos2 = np.concatenate([cos, cos], -1).astype(np.float32)
    sin2 = np.concatenate([-sin, sin], -1).astype(np.float32)
    return jnp.asarray(cos2), jnp.asarray(sin2)


def kernel(x_prompt, x_sample, state_ret, state_mlstm_C, state_mlstm_n, state_mlstm_m, c, c_ctx,
           w_ada, b_ada, norm1_g, norm2_g, w_in, conv_w, ret_decay_logit, mlstm_gate_bias,
           ret_gn_g, mlstm_gn_g, w_out, w_ff1, w_ff2, final_g):
    assert w_ada.shape[0] == 1, "single-layer kernel"
    bp, tp, _ = x_prompt.shape
    bs, ts, _ = x_sample.shape
    assert tp % CHUNK == 0 and ts % CHUNK == 0 and bp % CTX_SEQS_PER_STEP == 0

    cond = jnp.concatenate([c_ctx[None, :], c, jnp.zeros((8 - 1 - bs, D_MODEL), F32)], 0)
    mod, w_pair, w_gate, w_out_p, w1, w2 = _prepare(cond, w_ada, b_ada, w_in, w_out, w_ff1, w_ff2)
    mod3 = mod[:1 + bs].reshape(1 + bs, 6, D_MODEL)


    taps = jnp.transpose(conv_w[0].reshape(3, 2 * N_HEADS, HEAD_DIM), (1, 0, 2))
    decay = jnp.broadcast_to(jnp.transpose(ret_decay_logit[0])[:, :, None], (N_HEADS, 2, HEAD_DIM))
    gate_bias = jnp.pad(mlstm_gate_bias[0].reshape(1, 1, N_GATE_COLS), ((0, 0), (0, 0), (HEAD_DIM - N_GATE_COLS, 0)))
    head_params = jnp.concatenate(
        [taps[:N_HEADS], taps[N_HEADS:],
         ret_gn_g[0].reshape(N_HEADS, 1, HEAD_DIM), mlstm_gn_g[0].reshape(N_HEADS, 1, HEAD_DIM),
         decay, jnp.broadcast_to(gate_bias, (N_HEADS, 1, HEAD_DIM)),
         jnp.zeros((N_HEADS, 5, HEAD_DIM), F32)], axis=1).astype(F32)

    g1 = norm1_g[0][None, :]
    g2 = norm2_g[0][None, :]
    gf = final_g[None, :]
    xp2d = x_prompt.reshape(bp * tp, D_MODEL)
    xs2d = x_sample.reshape(bs * ts, D_MODEL)

    y_p, new_ret, new_c, new_n, new_m = _mixer(
        xp2d, mod3, lambda b: 0, g1, w_pair, w_gate, head_params,
        latent=False, n_seq=CTX_SEQS_PER_STEP, cps=tp // CHUNK)
    out_p = _ffn(xp2d, y_p, mod3, lambda i: 0, w_out_p, g2, w1, w2, gf).reshape(bp, tp, D_MODEL)

    s_n = jnp.transpose(state_mlstm_n[:, 0], (0, 2, 1, 3))[:, :, :, None, :]
    s_m = jnp.broadcast_to(jnp.transpose(state_mlstm_m[:, 0], (0, 2, 1))[..., None],
                           (bs, N_HEADS, 2, HEAD_DIM))
    y_s = _mixer(xs2d, mod3, lambda b: 1 + b, g1, w_pair, w_gate, head_params,
                 latent=True, n_seq=1, cps=ts // CHUNK,
                 rope=_rope_tables(ts), states=(state_ret, state_mlstm_C, s_n, s_m))
    tiles_per_seq = ts // FFN_ROWS
    out_s = _ffn(xs2d, y_s, mod3, lambda i: 1 + i // tiles_per_seq,
                 w_out_p, g2, w1, w2, gf).reshape(bs, ts, D_MODEL)

    new_n = jnp.transpose(new_n, (0, 2, 1, 3))[:, None]
    new_m = jnp.transpose(new_m[..., 0], (0, 2, 1))[:, None]
    return out_p, out_s, new_ret, new_c, new_n, new_m
```

```python
import functools

import numpy as np
import jax
import jax.numpy as jnp
from jax import lax
from jax.experimental import pallas as pl
from jax.experimental.pallas import tpu as pltpu

F32 = jnp.float32
BF16 = jnp.bfloat16

D_MODEL = 1024
N_HEADS = 4
HEAD_DIM = 128
CHUNK = 128
GRID_W = 64
D_FF = 4 * D_MODEL
EPS = 1e-6
ROPE_BASE = 10000.0
LOG2_E = 1.4426950408889634
LN_2 = 0.6931471805599453
PAIR_COLS = 8 * HEAD_DIM
N_GATE_COLS = 4 * N_HEADS
ROW_TILE = 256
FFN_ROWS = 512
CTX_SEQS_PER_STEP = 8
PROJ_SLAB_ROWS = 256
CONV_HALO = 8
VMEM_LIMIT = 60 * 1024 * 1024


def _dot(a, b):
    return jnp.dot(a, b, preferred_element_type=F32)


def _dot_nt(a, b):
    return lax.dot_general(a, b, (((1,), (1,)), ((), ())), preferred_element_type=F32)


def _dot_tn(a, b):
    return lax.dot_general(a, b, (((0,), (0,)), ((), ())), preferred_element_type=F32)


def _rms(x, g):
    return x * lax.rsqrt(jnp.mean(x * x, axis=-1, keepdims=True) + EPS) * g


def _group_norm(o, g):
    mu = jnp.mean(o, axis=-1, keepdims=True)
    c = o - mu
    var = jnp.mean(c * c, axis=-1, keepdims=True)
    return c * lax.rsqrt(var + EPS) * g


def _log_sigmoid(x):
    return jnp.minimum(x, 0.0) - jnp.log(1.0 + jnp.exp(-jnp.abs(x)))


def _sigmoid(x):
    return 1.0 / (1.0 + jnp.exp(-x))


def _split2(x):
    hi = x.astype(BF16)
    lo = (x - hi.astype(F32)).astype(BF16)
    return hi, lo


PREP_STEPS = 8
N_MOD = 6
COND_ROWS = 8


def _mod_kernel(cctx_ref, c_ref, w_ref, b_ref, out_ref):
    row = lax.broadcasted_iota(jnp.int32, out_ref.shape, 0)
    c = jnp.where(row == 0, cctx_ref[...], 0.0)
    for b in range(c_ref.shape[0]):
        c = jnp.where(row == 1 + b, c_ref[b:b + 1, :], c)
    s = (c * _sigmoid(c)).astype(BF16)
    out_ref[...] = _dot(s, w_ref[...].astype(BF16)) + b_ref[...]


def _regroup_in_kernel(wt_ref, out_ref, gate_ref):
    for h in range(N_HEADS):
        for g in range(8):
            src = (g * N_HEADS + h) * HEAD_DIM
            dst = (h * 8 + g) * HEAD_DIM
            out_ref[:, dst:dst + HEAD_DIM] = wt_ref[src:src + HEAD_DIM, :].T.astype(BF16)
    n_rows = wt_ref.shape[0]
    n_gate = n_rows - N_HEADS * PAIR_COLS
    tail = wt_ref[n_rows - HEAD_DIM:n_rows, :].T
    lane = lax.broadcasted_iota(jnp.int32, tail.shape, 1)
    gate_ref[...] = jnp.where(lane >= HEAD_DIM - n_gate, tail, 0.0).astype(BF16)


def _regroup_out_kernel(w_ref, out_ref):
    for h in range(N_HEADS):
        for g in range(2):
            src = (g * N_HEADS + h) * HEAD_DIM
            dst = (h * 2 + g) * HEAD_DIM
            out_ref[dst:dst + HEAD_DIM, :] = w_ref[src:src + HEAD_DIM, :].astype(BF16)


def _prepare_kernel(cctx_ref, c_ref, wada_ref, bada_ref, wt_ref, wout_ref, w1_ref, w2_ref,
                    mod_ref, wpair_ref, wgate_ref, woutp_ref, w1o_ref, w2o_ref):
    pl.when(pl.program_id(0) < N_MOD)(
        functools.partial(_mod_kernel, cctx_ref, c_ref, wada_ref, bada_ref, mod_ref))
    _regroup_in_kernel(wt_ref, wpair_ref, wgate_ref)
    _regroup_out_kernel(wout_ref, woutp_ref)
    w1o_ref[...] = w1_ref[...].astype(BF16)
    w2o_ref[...] = w2_ref[...].astype(BF16)


def _prepare(c_ctx, c, w_ada, b_ada, w_in, w_out, w_ff1, w_ff2):
    w_t = jnp.transpose(w_in[0])
    n_cols = w_t.shape[0]
    assert n_cols == N_HEADS * PAIR_COLS + N_GATE_COLS
    assert w_ada.shape[-1] == N_MOD * D_MODEL and N_MOD <= PREP_STEPS and 1 + c.shape[0] <= COND_ROWS
    k_rows = D_MODEL // PREP_STEPS
    mod_step = lambda i: jnp.minimum(i, N_MOD - 1)
    f_cols = D_FF // PREP_STEPS
    bf = lambda shape: jax.ShapeDtypeStruct(shape, BF16)
    return pl.pallas_call(
        _prepare_kernel,
        out_shape=(jax.ShapeDtypeStruct((COND_ROWS, N_MOD * D_MODEL), F32),
                   bf((D_MODEL, N_HEADS * PAIR_COLS)), bf((D_MODEL, HEAD_DIM)),
                   bf((D_MODEL, D_MODEL)), bf((D_MODEL, D_FF)), bf((D_FF, D_MODEL))),
        grid=(PREP_STEPS,),
        in_specs=[pl.BlockSpec((1, D_MODEL), lambda i: (0, 0)),
                  pl.BlockSpec(c.shape, lambda i: (0, 0)),
                  pl.BlockSpec((None, D_MODEL, D_MODEL), lambda i: (0, 0, mod_step(i))),
                  pl.BlockSpec((1, D_MODEL), lambda i: (0, mod_step(i))),
                  pl.BlockSpec((n_cols, k_rows), lambda i: (0, i)),
                  pl.BlockSpec((None, D_MODEL, k_rows), lambda i: (0, 0, i)),
                  pl.BlockSpec((None, D_MODEL, f_cols), lambda i: (0, 0, i)),
                  pl.BlockSpec((None, f_cols, D_MODEL), lambda i: (0, i, 0))],
        out_specs=(pl.BlockSpec((COND_ROWS, D_MODEL), lambda i: (0, mod_step(i))),
                   pl.BlockSpec((k_rows, N_HEADS * PAIR_COLS), lambda i: (i, 0)),
                   pl.BlockSpec((k_rows, HEAD_DIM), lambda i: (i, 0)),
                   pl.BlockSpec((D_MODEL, k_rows), lambda i: (0, i)),
                   pl.BlockSpec((D_MODEL, f_cols), lambda i: (0, i)),
                   pl.BlockSpec((f_cols, D_MODEL), lambda i: (i, 0))),
        compiler_params=pltpu.CompilerParams(
            dimension_semantics=("arbitrary",), vmem_limit_bytes=VMEM_LIMIT),
        name="prepare_weights",
    )(c_ctx[None, :], c, w_ada, b_ada, w_t, w_out, w_ff1, w_ff2)


_B_END, _MAX_LWE, _M_PREV, _M_NEW, _DECAY = 0, 2, 4, 6, 8


def _mixer_kernel(*refs, latent, n_seq, cps, n_steps):
    L = CHUNK
    n_chunks = n_seq * cps
    rows_total = n_chunks * L
    seq_len = cps * L
    assert seq_len & (seq_len - 1) == 0
    if latent:
        (x_ref, mod_ref, g1_ref, w_ref, wg_ref, hp_ref, hp_prev_ref, cos_ref, sin_ref,
         sr_in, sc_in, sn_in, sm_in, y_ref, *scratch) = refs
    else:
        (x_ref, mod_ref, g1_ref, w_ref, wg_ref, hp_ref, hp_prev_ref,
         y_ref, so_ref, co_ref, no_ref, mo_ref, *scratch) = refs
    (hn_ref, rq_ref, rk_ref, rv_ref, rg_ref, mq_ref, mk_ref, mvx_ref, mog_ref, gt_ref,
     ar_ref, am_ref, logi_ref, bc_ref, skv_ref, ckv_ref, cs_ref, rc_ref, rawq_ref, rawk_ref,
     gall_ref) = scratch

    step = pl.program_id(0)
    live = step < n_steps
    head = lax.rem(jnp.minimum(step, n_steps - 1), N_HEADS)

    @pl.when(step == 0)
    def _():
        for ref in (rq_ref, am_ref, rg_ref, mog_ref):
            ref[...] = jnp.zeros_like(ref)

    @pl.when(jnp.logical_and(head == 0, live))
    def _():
        gain = g1_ref[...] * (1.0 + mod_ref[1:2, :])
        sh1 = mod_ref[0:1, :]
        gate_bias = hp_ref[10:11, :]
        lane_t = lax.broadcasted_iota(jnp.int32, (ROW_TILE, HEAD_DIM), 1)
        is_forget = jnp.bitwise_and(lane_t, 4) == 4

        def body(i, carry):
            r = pl.ds(pl.multiple_of(i * ROW_TILE, ROW_TILE), ROW_TILE)
            xr = x_ref[r, :]
            inv = lax.rsqrt(jnp.mean(xr * xr, axis=-1, keepdims=True) + EPS)
            hn_t = (xr * inv * gain + sh1).astype(BF16)
            hn_ref[r, :] = hn_t
            gates = _dot(hn_t, wg_ref[...]) + gate_bias
            gall_ref[r, :] = jnp.where(is_forget, _log_sigmoid(gates), gates) * LOG2_E
            return carry

        lax.fori_loop(0, rows_total // ROW_TILE, body, 0, unroll=4)

    hp = hp_ref[...]
    scale = HEAD_DIM ** -0.5

    r_i = lax.broadcasted_iota(jnp.int32, (L, L), 0)
    s_i = lax.broadcasted_iota(jnp.int32, (L, L), 1)
    r_f = r_i.astype(F32)
    s_f = s_i.astype(F32)
    lg_f = _log_sigmoid(hp[8:9])
    lg_b = _log_sigmoid(hp[9:10])
    rc_ref[0] = (jnp.where(r_i >= s_i, jnp.exp(lg_f * jnp.where(r_i >= s_i, r_f - s_f, 0.0)), 0.0)
                 + jnp.where(s_i >= r_i, jnp.exp(lg_b * jnp.where(s_i >= r_i, s_f - r_f, 0.0)), 0.0))
    rc_ref[1] = jnp.exp(lg_f * (r_f + 1.0))
    rc_ref[2] = jnp.exp(lg_b * (L - r_f))
    rc_ref[3] = jnp.exp(lg_f * (L - 1.0 - r_f))
    rc_ref[4] = jnp.exp(lg_b * r_f)
    chunk_decay = (jnp.exp(lg_f * float(L)), jnp.exp(lg_b * float(L)))

    tri = (jnp.where(r_i >= s_i, 1.0, 0.0).astype(BF16),
           jnp.where(s_i >= r_i, 1.0, 0.0).astype(BF16))

    for raw_ref in (rawq_ref, rawk_ref):
        raw_ref[0:CONV_HALO, :] = jnp.zeros((CONV_HALO, HEAD_DIM), F32)
        raw_ref[rows_total + CONV_HALO:rows_total + 2 * CONV_HALO, :] = jnp.zeros((CONV_HALO, HEAD_DIM), F32)

    def rows(c):
        return slice(c * L, (c + 1) * L)

    def gate_col(c, lane):
        g = gt_ref[rows(c), :]
        return jnp.broadcast_to(g[:, lane:lane + 1], (L, HEAD_DIM))

    hp_prev = hp_prev_ref[...]

    def finish_rows(r):
        rg = rg_ref[r, :]
        ret_y = _group_norm(rq_ref[r, :], hp_prev[6:7]) * (rg * _sigmoid(rg))
        y_ref[r, 0:128] = ret_y.astype(BF16)
        ml_y = _group_norm(am_ref[r, :], hp_prev[7:8]) * _sigmoid(mog_ref[r, :])
        y_ref[r, 128:256] = ml_y.astype(BF16)

    def project(blk, n_rows):
        hn = hn_ref[blk, :]
        pr = _dot(hn, w_ref[:, 0:256])
        q = pr[:, 0:128]
        k = pr[:, 128:256] * scale
        if latent:
            cos2 = cos_ref[blk, :]
            sin2 = sin_ref[blk, :]
            q = q * cos2 + pltpu.roll(q, HEAD_DIM // 2, axis=1) * sin2
            k = k * cos2 + pltpu.roll(k, HEAD_DIM // 2, axis=1) * sin2
        rq_ref[blk, :] = q
        rk_ref[blk, :] = k

        pv = _dot(hn, w_ref[:, 256:512])
        rv_ref[blk, :] = pv[:, 0:128].astype(BF16)
        rg_ref[blk, :] = pv[:, 128:256]

        pm = _dot(hn, w_ref[:, 512:768])
        halo_blk = slice(blk.start + CONV_HALO, blk.stop + CONV_HALO)
        rawq_ref[halo_blk, :] = pm[:, 0:128]
        rawk_ref[halo_blk, :] = pm[:, 128:256]

        po = _dot(hn, w_ref[:, 768:1024])
        mvx_ref[blk, 0:128] = po[:, 0:128].astype(BF16)
        mvx_ref[blk, 128:256] = jnp.ones((n_rows, HEAD_DIM), BF16)
        mog_ref[blk, :] = po[:, 128:256]

        gt_ref[blk, :] = pltpu.roll(gall_ref[blk, :], N_GATE_COLS - head, axis=1)

    def conv_silu(blk, n_rows):
        sub = lax.broadcasted_iota(jnp.int32, (CONV_HALO, HEAD_DIM), 0)
        starts_seq = blk.start % seq_len == 0
        ends_seq = blk.stop % seq_len == 0
        for raw_ref, dst_ref, taps, post in ((rawq_ref, mq_ref, hp[0:3], 1.0), (rawk_ref, mk_ref, hp[3:6], scale)):
            lo = blk.start + CONV_HALO
            prev = raw_ref[lo - 1:lo - 1 + n_rows, :]
            nxt = raw_ref[lo + 1:lo + 1 + n_rows, :]
            if starts_seq:
                first = jnp.where(sub == 0, 0.0, prev[0:CONV_HALO, :])
                prev = jnp.concatenate([first, prev[CONV_HALO:, :]], axis=0)
            if ends_seq:
                last = jnp.where(sub == CONV_HALO - 1, 0.0, nxt[n_rows - CONV_HALO:, :])
                nxt = jnp.concatenate([nxt[:n_rows - CONV_HALO, :], last], axis=0)
            out = prev * taps[0:1] + raw_ref[lo:lo + n_rows, :] * taps[1:2] + nxt * taps[2:3]
            out = out * _sigmoid(out)
            dst_ref[blk, :] = out if post == 1.0 else out * post

    def project_and_scan():
        slab_rows = min(seq_len, PROJ_SLAB_ROWS)
        assert seq_len % slab_rows == 0
        slabs = [slice(s0, s0 + slab_rows) for s0 in range(0, rows_total, slab_rows)]
        for i, slab in enumerate(slabs):
            finish_rows(slab)
            project(slab, slab_rows)
            if i > 0:
                conv_silu(slabs[i - 1], slab_rows)
        conv_silu(slabs[-1], slab_rows)
        pre_pass(range(n_chunks))
        scan_passes(range(n_chunks), range(n_seq))

    def pre_pass(chunks):
        for c in chunks:
            ar_ref[rows(c), :] = (_dot_nt(rq_ref[rows(c), :].astype(BF16), rk_ref[rows(c), :].astype(BF16))
                                  * rc_ref[0]).astype(BF16)
            am_ref[rows(c), :] = _dot_nt(mq_ref[rows(c), :].astype(BF16), mk_ref[rows(c), :].astype(BF16))
        for c in chunks:
            for d in range(2):
                li = gate_col(c, 8 * d)
                lf = gate_col(c, 8 * d + 4)
                strict = (r_i > s_i) if d == 0 else (r_i < s_i)
                x_mat = jnp.where(strict, lf, jnp.where(r_i == s_i, li, 0.0))
                hi, lo = _split2(x_mat)
                d_mat = _dot(tri[d], hi) + _dot(tri[d], lo)
                e = 0 if d == 0 else L - 1
                bcum = (jnp.broadcast_to(d_mat[:, e:e + 1], (L, HEAD_DIM))
                        + (lf[e:e + 1, :] - li[e:e + 1, :]))
                causal = (s_i <= r_i) if d == 0 else (s_i >= r_i)
                logi_ref[d, rows(c), :] = jnp.where(causal, d_mat, -jnp.inf)
                bc_ref[d, rows(c), :] = bcum
                b_end = bcum[L - 1:L, :] if d == 0 else bcum[0:1, :]
                cs_ref[_B_END + d, c:c + 1, :] = b_end
                cs_ref[_MAX_LWE + d, c:c + 1, :] = jnp.max(b_end - bcum + li, axis=0, keepdims=True)
        for c in chunks:
            kf = rk_ref[rows(c), :]
            vb = rv_ref[rows(c), :]
            for d in range(2):
                skv_ref[d, c] = _dot_tn((kf * rc_ref[3 + d]).astype(BF16), vb)

    def scan_passes(chunks, seqs):
        m_final = {}
        for s in seqs:
            for d in range(2):
                m = sm_in[d:d + 1, :] * LOG2_E if latent else jnp.zeros((1, HEAD_DIM), F32)
                order = range(cps) if d == 0 else range(cps - 1, -1, -1)
                for j in order:
                    c = s * cps + j
                    cs_ref[_M_PREV + d, c:c + 1, :] = m
                    b_end = cs_ref[_B_END + d, c:c + 1, :]
                    m_new = jnp.maximum(b_end + m, cs_ref[_MAX_LWE + d, c:c + 1, :])
                    cs_ref[_M_NEW + d, c:c + 1, :] = m_new
                    cs_ref[_DECAY + d, c:c + 1, :] = jnp.exp2(b_end + m - m_new)
                    m = m_new
                m_final[(s, d)] = m

        for c in chunks:
            kf = mk_ref[rows(c), :]
            vx = mvx_ref[rows(c), :]
            for d in range(2):
                log_w_end = cs_ref[_B_END + d, c:c + 1, :] - bc_ref[d, rows(c), :] + gate_col(c, 8 * d)
                w_end = jnp.exp2(log_w_end - cs_ref[_M_NEW + d, c:c + 1, :])
                ckv_ref[d, c] = _dot_tn((kf * w_end).astype(BF16), vx)

        for s in seqs:
            for d in range(2):
                if latent:
                    s_state = sr_in[d]
                    n_rep = jnp.broadcast_to(sn_in[d], (HEAD_DIM, HEAD_DIM)).T
                    c_state = jnp.concatenate([sc_in[d], n_rep], axis=1)
                else:
                    s_state = jnp.zeros((HEAD_DIM, HEAD_DIM), F32)
                    c_state = jnp.zeros((HEAD_DIM, 2 * HEAD_DIM), F32)
                order = range(cps) if d == 0 else range(cps - 1, -1, -1)
                for j in order:
                    c = s * cps + j
                    inc = skv_ref[d, c]
                    skv_ref[d, c] = s_state
                    s_state = s_state * chunk_decay[d] + inc
                    inc = ckv_ref[d, c]
                    ckv_ref[d, c] = c_state
                    decay = cs_ref[_DECAY + d, c:c + 1, :]
                    c_state = c_state * jnp.concatenate([decay, decay], axis=1) + inc
                if not latent:
                    so_ref[s, d] = s_state
                    co_ref[s, d] = c_state[:, 0:128]
                    no_ref[s, d:d + 1, :] = c_state[:, 128:256].T[0:1, :]
                    mo_ref[s, d:d + 1, :] = m_final[(s, d)] * LN_2

        for c in chunks:
            qf = rq_ref[rows(c), :]
            lhs = jnp.concatenate([ar_ref[rows(c), :], (qf * rc_ref[1]).astype(BF16),
                                   (qf * rc_ref[2]).astype(BF16)], axis=1)
            rhs = jnp.concatenate([rv_ref[rows(c), :], skv_ref[0, c].astype(BF16),
                                   skv_ref[1, c].astype(BF16)], axis=0)
            rq_ref[rows(c), :] = _dot(lhs, rhs)
            qf = mq_ref[rows(c), :]
            a_mat = am_ref[rows(c), :]
            vx = mvx_ref[rows(c), :]
            h_sum = None
            for d in range(2):
                log_intra = logi_ref[d, rows(c), :]
                log_inter = bc_ref[d, rows(c), :] + cs_ref[_M_PREV + d, c:c + 1, :]
                m_t = jnp.maximum(log_inter, jnp.max(log_intra, axis=1, keepdims=True))
                w_inter = jnp.exp2(log_inter - m_t)
                w_intra = jnp.exp2(log_intra - m_t)
                lhs = jnp.concatenate([(a_mat * w_intra).astype(BF16), (qf * w_inter).astype(BF16)], axis=1)
                rhs = jnp.concatenate([vx, ckv_ref[d, c].astype(BF16)], axis=0)
                res = _dot(lhs, rhs)
                h_dir = res[:, 0:128] / jnp.maximum(jnp.abs(res[:, 128:256]), jnp.exp2(-m_t))
                h_sum = h_dir if h_sum is None else h_sum + h_dir
            am_ref[rows(c), :] = h_sum

    pl.when(live)(project_and_scan)

    @pl.when(step == n_steps)
    def _():
        def body(i, carry):
            finish_rows(pl.ds(pl.multiple_of(i * ROW_TILE, ROW_TILE), ROW_TILE))
            return carry

        lax.fori_loop(0, rows_total // ROW_TILE, body, 0)


def _mixer(x2d, mod3, mod_row, norm1_g, w_pair, w_gate, head_params, *, latent, n_seq, cps,
           rope=None, states=None):
    n_tok = x2d.shape[0]
    rows_blk = n_seq * cps * CHUNK
    n_blk = n_tok // rows_blk
    n_chunks = n_seq * cps
    n_steps = n_blk * N_HEADS
    kern = functools.partial(_mixer_kernel, latent=latent, n_seq=n_seq, cps=cps, n_steps=n_steps)

    def cur(f):
        def index_map(j):
            item = jnp.minimum(j, n_steps - 1)
            return f(item // N_HEADS, item % N_HEADS)
        return index_map

    def prev(f):
        def index_map(j):
            item = jnp.maximum(j - 1, 0)
            return f(item // N_HEADS, item % N_HEADS)
        return index_map

    once = pl.Buffered(1)
    in_specs = [
        pl.BlockSpec((rows_blk, D_MODEL), cur(lambda b, h: (b, 0))),
        pl.BlockSpec((None, 6, D_MODEL), cur(lambda b, h: (mod_row(b), 0, 0))),
        pl.BlockSpec((1, D_MODEL), lambda j: (0, 0)),
        pl.BlockSpec((D_MODEL, PAIR_COLS), cur(lambda b, h: (0, h))),
        pl.BlockSpec((D_MODEL, HEAD_DIM), lambda j: (0, 0)),
        pl.BlockSpec((None, 16, HEAD_DIM), cur(lambda b, h: (h, 0, 0))),
        pl.BlockSpec((None, 16, HEAD_DIM), prev(lambda b, h: (h, 0, 0))),
    ]
    args = [x2d, mod3, norm1_g, w_pair, w_gate, head_params, head_params]
    y_shape = jax.ShapeDtypeStruct((n_tok, D_MODEL), BF16)
    y_spec = pl.BlockSpec((rows_blk, 2 * HEAD_DIM), prev(lambda b, h: (b, h)))
    if latent:
        assert n_seq == 1
        cos2, sin2 = rope
        s_ret, s_c, s_n, s_m = states
        in_specs += [
            pl.BlockSpec((rows_blk, HEAD_DIM), lambda j: (0, 0), pipeline_mode=once),
            pl.BlockSpec((rows_blk, HEAD_DIM), lambda j: (0, 0), pipeline_mode=once),
            pl.BlockSpec((None, None, 2, None, HEAD_DIM, HEAD_DIM), cur(lambda b, h: (b, 0, 0, h, 0, 0))),
            pl.BlockSpec((None, None, 2, None, HEAD_DIM, HEAD_DIM), cur(lambda b, h: (b, 0, 0, h, 0, 0))),
            pl.BlockSpec((None, None, 2, 1, HEAD_DIM), cur(lambda b, h: (b, h, 0, 0, 0))),
            pl.BlockSpec((None, None, 2, HEAD_DIM), cur(lambda b, h: (b, h, 0, 0))),
        ]
        args += [cos2, sin2, s_ret, s_c, s_n, s_m]
        out_shape = y_shape
        out_specs = y_spec
    else:
        bsz = n_blk * n_seq
        st = jax.ShapeDtypeStruct((bsz, 1, 2, N_HEADS, HEAD_DIM, HEAD_DIM), F32)
        vec = jax.ShapeDtypeStruct((bsz, N_HEADS, 2, HEAD_DIM), F32)
        st_spec = pl.BlockSpec((n_seq, None, 2, None, HEAD_DIM, HEAD_DIM), cur(lambda b, h: (b, 0, 0, h, 0, 0)))
        vec_spec = pl.BlockSpec((n_seq, None, 2, HEAD_DIM), cur(lambda b, h: (b, h, 0, 0)))
        out_shape = (y_shape, st, st, vec, vec)
        out_specs = (y_spec, st_spec, st_spec, vec_spec, vec_spec)
    col = lambda dt: pltpu.VMEM((rows_blk, HEAD_DIM), dt)
    scratch = [
        pltpu.VMEM((rows_blk, D_MODEL), BF16),
        col(F32), col(F32), col(BF16), col(F32),
        col(F32), col(F32),
        pltpu.VMEM((rows_blk, 2 * HEAD_DIM), BF16),
        col(F32), col(F32),
        col(BF16), col(F32),
        pltpu.VMEM((2, rows_blk, HEAD_DIM), F32),
        pltpu.VMEM((2, rows_blk, HEAD_DIM), F32),
        pltpu.VMEM((2, n_chunks, HEAD_DIM, HEAD_DIM), F32),
        pltpu.VMEM((2, n_chunks, HEAD_DIM, 2 * HEAD_DIM), F32),
        pltpu.VMEM((10, max(n_chunks, 8), HEAD_DIM), F32),
        pltpu.VMEM((5, CHUNK, CHUNK), F32),
        pltpu.VMEM((rows_blk + 2 * CONV_HALO, HEAD_DIM), F32),
        pltpu.VMEM((rows_blk + 2 * CONV_HALO, HEAD_DIM), F32),
        col(F32),
    ]
    return pl.pallas_call(
        kern,
        out_shape=out_shape,
        grid=(n_steps + 1,),
        in_specs=in_specs,
        out_specs=out_specs,
        scratch_shapes=scratch,
        compiler_params=pltpu.CompilerParams(
            dimension_semantics=("arbitrary",), vmem_limit_bytes=VMEM_LIMIT),
        name="mixer_latent" if latent else "mixer_context",
    )(*args)


def _ffn_kernel(x_ref, y_ref, mod_ref, wo_ref, g2_ref, w1_ref, w2_ref, gf_ref, out_ref):
    g1 = mod_ref[2:3, :]
    sh2 = mod_ref[3:4, :]
    sc2 = mod_ref[4:5, :]
    g2 = mod_ref[5:6, :]
    x1 = x_ref[...] + g1 * _dot(y_ref[...], wo_ref[...])
    h2 = (_rms(x1, g2_ref[...]) * (1.0 + sc2) + sh2).astype(BF16)
    f = jnp.zeros_like(x1)
    for j in range(D_FF // D_MODEL):
        cols = slice(j * D_MODEL, (j + 1) * D_MODEL)
        hid = jnp.maximum(_dot(h2, w1_ref[:, cols]), 0.0)
        f = f + _dot((hid * hid).astype(BF16), w2_ref[cols, :])
    out_ref[...] = _rms(x1 + g2 * f, gf_ref[...])


def _ffn(x2d, y2d, mod3, mod_row, w_out, norm2_g, w_ff1, w_ff2, final_g):
    n_tok = x2d.shape[0]
    const = lambda i: (0, 0)
    return pl.pallas_call(
        _ffn_kernel,
        out_shape=jax.ShapeDtypeStruct((n_tok, D_MODEL), F32),
        grid=(n_tok // FFN_ROWS,),
        in_specs=[
            pl.BlockSpec((FFN_ROWS, D_MODEL), lambda i: (i, 0)),
            pl.BlockSpec((FFN_ROWS, D_MODEL), lambda i: (i, 0)),
            pl.BlockSpec((None, 6, D_MODEL), lambda i: (mod_row(i), 0, 0)),
            pl.BlockSpec((D_MODEL, D_MODEL), const, pipeline_mode=pl.Buffered(1)),
            pl.BlockSpec((1, D_MODEL), const),
            pl.BlockSpec((D_MODEL, D_FF), const, pipeline_mode=pl.Buffered(1)),
            pl.BlockSpec((D_FF, D_MODEL), const, pipeline_mode=pl.Buffered(1)),
            pl.BlockSpec((1, D_MODEL), const),
        ],
        out_specs=pl.BlockSpec((FFN_ROWS, D_MODEL), lambda i: (i, 0)),
        compiler_params=pltpu.CompilerParams(
            dimension_semantics=("arbitrary",), vmem_limit_bytes=VMEM_LIMIT),
        name="outproj_mlp",
    )(x2d, y2d, mod3, w_out, norm2_g, w_ff1, w_ff2, final_g)


def _rope_tables(seq):
    pos = np.arange(seq)
    row = (pos // GRID_W).astype(np.float64)
    col = (pos % GRID_W).astype(np.float64)
    nf = HEAD_DIM // 4
    inv = ROPE_BASE ** (-np.arange(nf, dtype=np.float64) / nf)
    ang = np.concatenate([row[:, None] * inv, col[:, None] * inv], -1)
    cos = np.cos(ang)
    sin = np.sin(ang)
    cos2 = np.concatenate([cos, cos], -1).astype(np.float32)
    sin2 = np.concatenate([-sin, sin], -1).astype(np.float32)
    return jnp.asarray(cos2), jnp.asarray(sin2)


def kernel(x_prompt, x_sample, state_ret, state_mlstm_C, state_mlstm_n, state_mlstm_m, c, c_ctx,
           w_ada, b_ada, norm1_g, norm2_g, w_in, conv_w, ret_decay_logit, mlstm_gate_bias,
           ret_gn_g, mlstm_gn_g, w_out, w_ff1, w_ff2, final_g):
    assert w_ada.shape[0] == 1, "single-layer kernel"
    bp, tp, _ = x_prompt.shape
    bs, ts, _ = x_sample.shape
    assert tp % CHUNK == 0 and ts % CHUNK == 0 and bp % CTX_SEQS_PER_STEP == 0

    mod, w_pair, w_gate, w_out_p, w1, w2 = _prepare(c_ctx, c, w_ada, b_ada, w_in, w_out, w_ff1, w_ff2)
    mod3 = mod.reshape(COND_ROWS, N_MOD, D_MODEL)


    taps = jnp.transpose(conv_w[0].reshape(3, 2 * N_HEADS, HEAD_DIM), (1, 0, 2))
    decay = jnp.broadcast_to(jnp.transpose(ret_decay_logit[0])[:, :, None], (N_HEADS, 2, HEAD_DIM))
    gate_bias = jnp.pad(mlstm_gate_bias[0].reshape(1, 1, N_GATE_COLS), ((0, 0), (0, 0), (HEAD_DIM - N_GATE_COLS, 0)))
    head_params = jnp.concatenate(
        [taps[:N_HEADS], taps[N_HEADS:],
         ret_gn_g[0].reshape(N_HEADS, 1, HEAD_DIM), mlstm_gn_g[0].reshape(N_HEADS, 1, HEAD_DIM),
         decay, jnp.broadcast_to(gate_bias, (N_HEADS, 1, HEAD_DIM)),
         jnp.zeros((N_HEADS, 5, HEAD_DIM), F32)], axis=1).astype(F32)

    g1 = norm1_g[0][None, :]
    g2 = norm2_g[0][None, :]
    gf = final_g[None, :]
    xp2d = x_prompt.reshape(bp * tp, D_MODEL)
    xs2d = x_sample.reshape(bs * ts, D_MODEL)

    y_p, new_ret, new_c, new_n, new_m = _mixer(
        xp2d, mod3, lambda b: 0, g1, w_pair, w_gate, head_params,
        latent=False, n_seq=CTX_SEQS_PER_STEP, cps=tp // CHUNK)
    out_p = _ffn(xp2d, y_p, mod3, lambda i: 0, w_out_p, g2, w1, w2, gf).reshape(bp, tp, D_MODEL)

    s_n = jnp.transpose(state_mlstm_n[:, 0], (0, 2, 1, 3))[:, :, :, None, :]
    s_m = jnp.broadcast_to(jnp.transpose(state_mlstm_m[:, 0], (0, 2, 1))[..., None],
                           (bs, N_HEADS, 2, HEAD_DIM))
    y_s = _mixer(xs2d, mod3, lambda b: 1 + b, g1, w_pair, w_gate, head_params,
                 latent=True, n_seq=1, cps=ts // CHUNK,
                 rope=_rope_tables(ts), states=(state_ret, state_mlstm_C, s_n, s_m))
    tiles_per_seq = ts // FFN_ROWS
    out_s = _ffn(xs2d, y_s, mod3, lambda i: 1 + i // tiles_per_seq,
                 w_out_p, g2, w1, w2, gf).reshape(bs, ts, D_MODEL)

    new_n = jnp.transpose(new_n, (0, 2, 1, 3))[:, None]
    new_m = jnp.transpose(new_m[..., 0], (0, 2, 1))[:, None]
    return out_p, out_s, new_ret, new_c, new_n, new_m
```

```python
import functools

import numpy as np
import jax
import jax.numpy as jnp
from jax import lax
from jax.experimental import pallas as pl
from jax.experimental.pallas import tpu as pltpu

F32 = jnp.float32
BF16 = jnp.bfloat16

D_MODEL = 1024
N_HEADS = 4
HEAD_DIM = 128
CHUNK = 128
GRID_W = 64
D_FF = 4 * D_MODEL
EPS = 1e-6
ROPE_BASE = 10000.0
LOG2_E = 1.4426950408889634
LN_2 = 0.6931471805599453
PAIR_COLS = 8 * HEAD_DIM
N_GATE_COLS = 4 * N_HEADS
ROW_TILE = 256
FFN_ROWS = 512
CTX_SEQS_PER_STEP = 8
PROJ_SLAB_ROWS = 256
CONV_HALO = 8
VMEM_LIMIT = 60 * 1024 * 1024


def _dot(a, b):
    return jnp.dot(a, b, preferred_element_type=F32)


def _dot_nt(a, b):
    return lax.dot_general(a, b, (((1,), (1,)), ((), ())), preferred_element_type=F32)


def _dot_tn(a, b):
    return lax.dot_general(a, b, (((0,), (0,)), ((), ())), preferred_element_type=F32)


def _rms(x, g):
    return x * lax.rsqrt(jnp.mean(x * x, axis=-1, keepdims=True) + EPS) * g


def _group_norm(o, g):
    mu = jnp.mean(o, axis=-1, keepdims=True)
    c = o - mu
    var = jnp.mean(c * c, axis=-1, keepdims=True)
    return c * lax.rsqrt(var + EPS) * g


def _log_sigmoid(x):
    return jnp.minimum(x, 0.0) - jnp.log(1.0 + jnp.exp(-jnp.abs(x)))


def _sigmoid(x):
    return 1.0 / (1.0 + jnp.exp(-x))


def _split2(x):
    hi = x.astype(BF16)
    lo = (x - hi.astype(F32)).astype(BF16)
    return hi, lo


PREP_STEPS = 8
N_MOD = 6
COND_ROWS = 8


def _mod_kernel(cctx_ref, c_ref, w_ref, b_ref, out_ref):
    row = lax.broadcasted_iota(jnp.int32, out_ref.shape, 0)
    c = jnp.where(row == 0, cctx_ref[...], 0.0)
    for b in range(c_ref.shape[0]):
        c = jnp.where(row == 1 + b, c_ref[b:b + 1, :], c)
    s = (c * _sigmoid(c)).astype(BF16)
    out_ref[...] = _dot(s, w_ref[...].astype(BF16)) + b_ref[...]


def _regroup_in_kernel(wt_ref, out_ref, gate_ref):
    for h in range(N_HEADS):
        for g in range(8):
            src = (g * N_HEADS + h) * HEAD_DIM
            dst = (h * 8 + g) * HEAD_DIM
            out_ref[:, dst:dst + HEAD_DIM] = wt_ref[src:src + HEAD_DIM, :].T.astype(BF16)
    n_rows = wt_ref.shape[0]
    n_gate = n_rows - N_HEADS * PAIR_COLS
    tail = wt_ref[n_rows - HEAD_DIM:n_rows, :].T
    lane = lax.broadcasted_iota(jnp.int32, tail.shape, 1)
    gate_ref[...] = jnp.where(lane >= HEAD_DIM - n_gate, tail, 0.0).astype(BF16)


def _regroup_out_kernel(w_ref, out_ref):
    for h in range(N_HEADS):
        for g in range(2):
            src = (g * N_HEADS + h) * HEAD_DIM
            dst = (h * 2 + g) * HEAD_DIM
            out_ref[dst:dst + HEAD_DIM, :] = w_ref[src:src + HEAD_DIM, :].astype(BF16)


def _prepare_kernel(cctx_ref, c_ref, wada_ref, bada_ref, wt_ref, wout_ref, w1_ref, w2_ref,
                    mod_ref, wpair_ref, wgate_ref, woutp_ref, w1o_ref, w2o_ref):
    pl.when(pl.program_id(0) < N_MOD)(
        functools.partial(_mod_kernel, cctx_ref, c_ref, wada_ref, bada_ref, mod_ref))
    _regroup_in_kernel(wt_ref, wpair_ref, wgate_ref)
    _regroup_out_kernel(wout_ref, woutp_ref)
    w1o_ref[...] = w1_ref[...].astype(BF16)
    w2o_ref[...] = w2_ref[...].astype(BF16)


def _prepare(c_ctx, c, w_ada, b_ada, w_in, w_out, w_ff1, w_ff2):
    w_t = jnp.transpose(w_in[0])
    n_cols = w_t.shape[0]
    assert n_cols == N_HEADS * PAIR_COLS + N_GATE_COLS
    assert w_ada.shape[-1] == N_MOD * D_MODEL and N_MOD <= PREP_STEPS and 1 + c.shape[0] <= COND_ROWS
    k_rows = D_MODEL // PREP_STEPS
    mod_step = lambda i: jnp.minimum(i, N_MOD - 1)
    f_cols = D_FF // PREP_STEPS
    bf = lambda shape: jax.ShapeDtypeStruct(shape, BF16)
    return pl.pallas_call(
        _prepare_kernel,
        out_shape=(jax.ShapeDtypeStruct((N_MOD, COND_ROWS, D_MODEL), F32),
                   bf((D_MODEL, N_HEADS * PAIR_COLS)), bf((D_MODEL, HEAD_DIM)),
                   bf((D_MODEL, D_MODEL)), bf((D_MODEL, D_FF)), bf((D_FF, D_MODEL))),
        grid=(PREP_STEPS,),
        in_specs=[pl.BlockSpec((1, D_MODEL), lambda i: (0, 0)),
                  pl.BlockSpec(c.shape, lambda i: (0, 0)),
                  pl.BlockSpec((None, D_MODEL, D_MODEL), lambda i: (0, 0, mod_step(i))),
                  pl.BlockSpec((1, D_MODEL), lambda i: (0, mod_step(i))),
                  pl.BlockSpec((n_cols, k_rows), lambda i: (0, i)),
                  pl.BlockSpec((None, D_MODEL, k_rows), lambda i: (0, 0, i)),
                  pl.BlockSpec((None, D_MODEL, f_cols), lambda i: (0, 0, i)),
                  pl.BlockSpec((None, f_cols, D_MODEL), lambda i: (0, i, 0))],
        out_specs=(pl.BlockSpec((None, COND_ROWS, D_MODEL), lambda i: (mod_step(i), 0, 0)),
                   pl.BlockSpec((k_rows, N_HEADS * PAIR_COLS), lambda i: (i, 0)),
                   pl.BlockSpec((k_rows, HEAD_DIM), lambda i: (i, 0)),
                   pl.BlockSpec((D_MODEL, k_rows), lambda i: (0, i)),
                   pl.BlockSpec((D_MODEL, f_cols), lambda i: (0, i)),
                   pl.BlockSpec((f_cols, D_MODEL), lambda i: (i, 0))),
        compiler_params=pltpu.CompilerParams(
            dimension_semantics=("arbitrary",), vmem_limit_bytes=VMEM_LIMIT),
        name="prepare_weights",
    )(c_ctx[None, :], c, w_ada, b_ada, w_t, w_out, w_ff1, w_ff2)


_B_END, _MAX_LWE, _M_PREV, _M_NEW, _DECAY = 0, 2, 4, 6, 8


def _mixer_kernel(*refs, latent, n_seq, cps, n_steps, mod_row):
    L = CHUNK
    n_chunks = n_seq * cps
    rows_total = n_chunks * L
    seq_len = cps * L
    assert seq_len & (seq_len - 1) == 0
    if latent:
        (x_ref, mod_ref, g1_ref, w_ref, wg_ref, hp_ref, hp_prev_ref, cos_ref, sin_ref,
         sr_in, sc_in, sn_in, sm_in, y_ref, *scratch) = refs
    else:
        (x_ref, mod_ref, g1_ref, w_ref, wg_ref, hp_ref, hp_prev_ref,
         y_ref, so_ref, co_ref, no_ref, mo_ref, *scratch) = refs
    (hn_ref, rq_ref, rk_ref, rv_ref, rg_ref, mq_ref, mk_ref, mvx_ref, mog_ref, gt_ref,
     ar_ref, am_ref, logi_ref, bc_ref, skv_ref, ckv_ref, cs_ref, rc_ref, rawq_ref, rawk_ref,
     gall_ref) = scratch

    step = pl.program_id(0)
    live = step < n_steps
    head = lax.rem(jnp.minimum(step, n_steps - 1), N_HEADS)

    @pl.when(step == 0)
    def _():
        for ref in (rq_ref, am_ref, rg_ref, mog_ref):
            ref[...] = jnp.zeros_like(ref)

    @pl.when(jnp.logical_and(head == 0, live))
    def _():
        row = pl.ds(mod_row(jnp.minimum(step, n_steps - 1) // N_HEADS), 1)
        gain = g1_ref[...] * (1.0 + mod_ref[1, row, :])
        sh1 = mod_ref[0, row, :]
        gate_bias = hp_ref[10:11, :]
        lane_t = lax.broadcasted_iota(jnp.int32, (ROW_TILE, HEAD_DIM), 1)
        is_forget = jnp.bitwise_and(lane_t, 4) == 4

        def body(i, carry):
            r = pl.ds(pl.multiple_of(i * ROW_TILE, ROW_TILE), ROW_TILE)
            xr = x_ref[r, :]
            inv = lax.rsqrt(jnp.mean(xr * xr, axis=-1, keepdims=True) + EPS)
            hn_t = (xr * inv * gain + sh1).astype(BF16)
            hn_ref[r, :] = hn_t
            gates = _dot(hn_t, wg_ref[...]) + gate_bias
            gall_ref[r, :] = jnp.where(is_forget, _log_sigmoid(gates), gates) * LOG2_E
            return carry

        lax.fori_loop(0, rows_total // ROW_TILE, body, 0, unroll=4)

    hp = hp_ref[...]
    scale = HEAD_DIM ** -0.5

    r_i = lax.broadcasted_iota(jnp.int32, (L, L), 0)
    s_i = lax.broadcasted_iota(jnp.int32, (L, L), 1)
    r_f = r_i.astype(F32)
    s_f = s_i.astype(F32)
    lg_f = _log_sigmoid(hp[8:9])
    lg_b = _log_sigmoid(hp[9:10])
    rc_ref[0] = (jnp.where(r_i >= s_i, jnp.exp(lg_f * jnp.where(r_i >= s_i, r_f - s_f, 0.0)), 0.0)
                 + jnp.where(s_i >= r_i, jnp.exp(lg_b * jnp.where(s_i >= r_i, s_f - r_f, 0.0)), 0.0))
    rc_ref[1] = jnp.exp(lg_f * (r_f + 1.0))
    rc_ref[2] = jnp.exp(lg_b * (L - r_f))
    rc_ref[3] = jnp.exp(lg_f * (L - 1.0 - r_f))
    rc_ref[4] = jnp.exp(lg_b * r_f)
    chunk_decay = (jnp.exp(lg_f * float(L)), jnp.exp(lg_b * float(L)))

    tri = (jnp.where(r_i >= s_i, 1.0, 0.0).astype(BF16),
           jnp.where(s_i >= r_i, 1.0, 0.0).astype(BF16))

    for raw_ref in (rawq_ref, rawk_ref):
        raw_ref[0:CONV_HALO, :] = jnp.zeros((CONV_HALO, HEAD_DIM), F32)
        raw_ref[rows_total + CONV_HALO:rows_total + 2 * CONV_HALO, :] = jnp.zeros((CONV_HALO, HEAD_DIM), F32)

    def rows(c):
        return slice(c * L, (c + 1) * L)

    def gate_col(c, lane):
        g = gt_ref[rows(c), :]
        return jnp.broadcast_to(g[:, lane:lane + 1], (L, HEAD_DIM))

    hp_prev = hp_prev_ref[...]

    def finish_rows(r):
        rg = rg_ref[r, :]
        ret_y = _group_norm(rq_ref[r, :], hp_prev[6:7]) * (rg * _sigmoid(rg))
        y_ref[r, 0:128] = ret_y.astype(BF16)
        ml_y = _group_norm(am_ref[r, :], hp_prev[7:8]) * _sigmoid(mog_ref[r, :])
        y_ref[r, 128:256] = ml_y.astype(BF16)

    def project(blk, n_rows):
        hn = hn_ref[blk, :]
        pr = _dot(hn, w_ref[:, 0:256])
        q = pr[:, 0:128]
        k = pr[:, 128:256] * scale
        if latent:
            cos2 = cos_ref[blk, :]
            sin2 = sin_ref[blk, :]
            q = q * cos2 + pltpu.roll(q, HEAD_DIM // 2, axis=1) * sin2
            k = k * cos2 + pltpu.roll(k, HEAD_DIM // 2, axis=1) * sin2
        rq_ref[blk, :] = q
        rk_ref[blk, :] = k

        pv = _dot(hn, w_ref[:, 256:512])
        rv_ref[blk, :] = pv[:, 0:128].astype(BF16)
        rg_ref[blk, :] = pv[:, 128:256]

        pm = _dot(hn, w_ref[:, 512:768])
        halo_blk = slice(blk.start + CONV_HALO, blk.stop + CONV_HALO)
        rawq_ref[halo_blk, :] = pm[:, 0:128]
        rawk_ref[halo_blk, :] = pm[:, 128:256]

        po = _dot(hn, w_ref[:, 768:1024])
        mvx_ref[blk, 0:128] = po[:, 0:128].astype(BF16)
        mvx_ref[blk, 128:256] = jnp.ones((n_rows, HEAD_DIM), BF16)
        mog_ref[blk, :] = po[:, 128:256]

        gt_ref[blk, :] = pltpu.roll(gall_ref[blk, :], N_GATE_COLS - head, axis=1)

    def conv_silu(blk, n_rows):
        sub = lax.broadcasted_iota(jnp.int32, (CONV_HALO, HEAD_DIM), 0)
        starts_seq = blk.start % seq_len == 0
        ends_seq = blk.stop % seq_len == 0
        for raw_ref, dst_ref, taps, post in ((rawq_ref, mq_ref, hp[0:3], 1.0), (rawk_ref, mk_ref, hp[3:6], scale)):
            lo = blk.start + CONV_HALO
            prev = raw_ref[lo - 1:lo - 1 + n_rows, :]
            nxt = raw_ref[lo + 1:lo + 1 + n_rows, :]
            if starts_seq:
                first = jnp.where(sub == 0, 0.0, prev[0:CONV_HALO, :])
                prev = jnp.concatenate([first, prev[CONV_HALO:, :]], axis=0)
            if ends_seq:
                last = jnp.where(sub == CONV_HALO - 1, 0.0, nxt[n_rows - CONV_HALO:, :])
                nxt = jnp.concatenate([nxt[:n_rows - CONV_HALO, :], last], axis=0)
            out = prev * taps[0:1] + raw_ref[lo:lo + n_rows, :] * taps[1:2] + nxt * taps[2:3]
            out = out * _sigmoid(out)
            dst_ref[blk, :] = out if post == 1.0 else out * post

    def project_and_scan():
        slab_rows = min(seq_len, PROJ_SLAB_ROWS)
        assert seq_len % slab_rows == 0
        slabs = [slice(s0, s0 + slab_rows) for s0 in range(0, rows_total, slab_rows)]
        for i, slab in enumerate(slabs):
            finish_rows(slab)
            project(slab, slab_rows)
            if i > 0:
                conv_silu(slabs[i - 1], slab_rows)
        conv_silu(slabs[-1], slab_rows)
        pre_pass(range(n_chunks))
        scan_passes(range(n_chunks), range(n_seq))

    def pre_pass(chunks):
        for c in chunks:
            ar_ref[rows(c), :] = (_dot_nt(rq_ref[rows(c), :].astype(BF16), rk_ref[rows(c), :].astype(BF16))
                                  * rc_ref[0]).astype(BF16)
            am_ref[rows(c), :] = _dot_nt(mq_ref[rows(c), :].astype(BF16), mk_ref[rows(c), :].astype(BF16))
        for c in chunks:
            for d in range(2):
                li = gate_col(c, 8 * d)
                lf = gate_col(c, 8 * d + 4)
                strict = (r_i > s_i) if d == 0 else (r_i < s_i)
                x_mat = jnp.where(strict, lf, jnp.where(r_i == s_i, li, 0.0))
                hi, lo = _split2(x_mat)
                d_mat = _dot(tri[d], hi) + _dot(tri[d], lo)
                e = 0 if d == 0 else L - 1
                bcum = (jnp.broadcast_to(d_mat[:, e:e + 1], (L, HEAD_DIM))
                        + (lf[e:e + 1, :] - li[e:e + 1, :]))
                causal = (s_i <= r_i) if d == 0 else (s_i >= r_i)
                logi_ref[d, rows(c), :] = jnp.where(causal, d_mat, -jnp.inf)
                bc_ref[d, rows(c), :] = bcum
                b_end = bcum[L - 1:L, :] if d == 0 else bcum[0:1, :]
                cs_ref[_B_END + d, c:c + 1, :] = b_end
                cs_ref[_MAX_LWE + d, c:c + 1, :] = jnp.max(b_end - bcum + li, axis=0, keepdims=True)
        for c in chunks:
            kf = rk_ref[rows(c), :]
            vb = rv_ref[rows(c), :]
            for d in range(2):
                skv_ref[d, c] = _dot_tn((kf * rc_ref[3 + d]).astype(BF16), vb)

    def scan_passes(chunks, seqs):
        m_final = {}
        for s in seqs:
            for d in range(2):
                m = sm_in[d:d + 1, :] * LOG2_E if latent else jnp.zeros((1, HEAD_DIM), F32)
                order = range(cps) if d == 0 else range(cps - 1, -1, -1)
                for j in order:
                    c = s * cps + j
                    cs_ref[_M_PREV + d, c:c + 1, :] = m
                    b_end = cs_ref[_B_END + d, c:c + 1, :]
                    m_new = jnp.maximum(b_end + m, cs_ref[_MAX_LWE + d, c:c + 1, :])
                    cs_ref[_M_NEW + d, c:c + 1, :] = m_new
                    cs_ref[_DECAY + d, c:c + 1, :] = jnp.exp2(b_end + m - m_new)
                    m = m_new
                m_final[(s, d)] = m

        for c in chunks:
            kf = mk_ref[rows(c), :]
            vx = mvx_ref[rows(c), :]
            for d in range(2):
                log_w_end = cs_ref[_B_END + d, c:c + 1, :] - bc_ref[d, rows(c), :] + gate_col(c, 8 * d)
                w_end = jnp.exp2(log_w_end - cs_ref[_M_NEW + d, c:c + 1, :])
                ckv_ref[d, c] = _dot_tn((kf * w_end).astype(BF16), vx)

        for s in seqs:
            for d in range(2):
                if latent:
                    s_state = sr_in[d]
                    n_rep = jnp.broadcast_to(sn_in[d], (HEAD_DIM, HEAD_DIM)).T
                    c_state = jnp.concatenate([sc_in[d], n_rep], axis=1)
                else:
                    s_state = jnp.zeros((HEAD_DIM, HEAD_DIM), F32)
                    c_state = jnp.zeros((HEAD_DIM, 2 * HEAD_DIM), F32)
                order = range(cps) if d == 0 else range(cps - 1, -1, -1)
                for j in order:
                    c = s * cps + j
                    inc = skv_ref[d, c]
                    skv_ref[d, c] = s_state
                    s_state = s_state * chunk_decay[d] + inc
                    inc = ckv_ref[d, c]
                    ckv_ref[d, c] = c_state
                    decay = cs_ref[_DECAY + d, c:c + 1, :]
                    c_state = c_state * jnp.concatenate([decay, decay], axis=1) + inc
                if not latent:
                    so_ref[s, d] = s_state
                    co_ref[s, d] = c_state[:, 0:128]
                    no_ref[s, d:d + 1, :] = c_state[:, 128:256].T[0:1, :]
                    mo_ref[s, d:d + 1, :] = m_final[(s, d)] * LN_2

        for c in chunks:
            qf = rq_ref[rows(c), :]
            lhs = jnp.concatenate([ar_ref[rows(c), :], (qf * rc_ref[1]).astype(BF16),
                                   (qf * rc_ref[2]).astype(BF16)], axis=1)
            rhs = jnp.concatenate([rv_ref[rows(c), :], skv_ref[0, c].astype(BF16),
                                   skv_ref[1, c].astype(BF16)], axis=0)
            rq_ref[rows(c), :] = _dot(lhs, rhs)
            qf = mq_ref[rows(c), :]
            a_mat = am_ref[rows(c), :]
            vx = mvx_ref[rows(c), :]
            h_sum = None
            for d in range(2):
                log_intra = logi_ref[d, rows(c), :]
                log_inter = bc_ref[d, rows(c), :] + cs_ref[_M_PREV + d, c:c + 1, :]
                m_t = jnp.maximum(log_inter, jnp.max(log_intra, axis=1, keepdims=True))
                w_inter = jnp.exp2(log_inter - m_t)
                w_intra = jnp.exp2(log_intra - m_t)
                lhs = jnp.concatenate([(a_mat * w_intra).astype(BF16), (qf * w_inter).astype(BF16)], axis=1)
                rhs = jnp.concatenate([vx, ckv_ref[d, c].astype(BF16)], axis=0)
                res = _dot(lhs, rhs)
                h_dir = res[:, 0:128] / jnp.maximum(jnp.abs(res[:, 128:256]), jnp.exp2(-m_t))
                h_sum = h_dir if h_sum is None else h_sum + h_dir
            am_ref[rows(c), :] = h_sum

    pl.when(live)(project_and_scan)

    @pl.when(step == n_steps)
    def _():
        def body(i, carry):
            finish_rows(pl.ds(pl.multiple_of(i * ROW_TILE, ROW_TILE), ROW_TILE))
            return carry

        lax.fori_loop(0, rows_total // ROW_TILE, body, 0)


def _mixer(x2d, mod3, mod_row, norm1_g, w_pair, w_gate, head_params, *, latent, n_seq, cps,
           rope=None, states=None):
    n_tok = x2d.shape[0]
    rows_blk = n_seq * cps * CHUNK
    n_blk = n_tok // rows_blk
    n_chunks = n_seq * cps
    n_steps = n_blk * N_HEADS
    kern = functools.partial(_mixer_kernel, latent=latent, n_seq=n_seq, cps=cps, n_steps=n_steps,
                             mod_row=mod_row)

    def cur(f):
        def index_map(j):
            item = jnp.minimum(j, n_steps - 1)
            return f(item // N_HEADS, item % N_HEADS)
        return index_map

    def prev(f):
        def index_map(j):
            item = jnp.maximum(j - 1, 0)
            return f(item // N_HEADS, item % N_HEADS)
        return index_map

    once = pl.Buffered(1)
    in_specs = [
        pl.BlockSpec((rows_blk, D_MODEL), cur(lambda b, h: (b, 0))),
        pl.BlockSpec(mod3.shape, lambda j: (0, 0, 0)),
        pl.BlockSpec((1, D_MODEL), lambda j: (0, 0)),
        pl.BlockSpec((D_MODEL, PAIR_COLS), cur(lambda b, h: (0, h))),
        pl.BlockSpec((D_MODEL, HEAD_DIM), lambda j: (0, 0)),
        pl.BlockSpec((None, 16, HEAD_DIM), cur(lambda b, h: (h, 0, 0))),
        pl.BlockSpec((None, 16, HEAD_DIM), prev(lambda b, h: (h, 0, 0))),
    ]
    args = [x2d, mod3, norm1_g, w_pair, w_gate, head_params, head_params]
    y_shape = jax.ShapeDtypeStruct((n_tok, D_MODEL), BF16)
    y_spec = pl.BlockSpec((rows_blk, 2 * HEAD_DIM), prev(lambda b, h: (b, h)))
    if latent:
        assert n_seq == 1
        cos2, sin2 = rope
        s_ret, s_c, s_n, s_m = states
        in_specs += [
            pl.BlockSpec((rows_blk, HEAD_DIM), lambda j: (0, 0), pipeline_mode=once),
            pl.BlockSpec((rows_blk, HEAD_DIM), lambda j: (0, 0), pipeline_mode=once),
            pl.BlockSpec((None, None, 2, None, HEAD_DIM, HEAD_DIM), cur(lambda b, h: (b, 0, 0, h, 0, 0))),
            pl.BlockSpec((None, None, 2, None, HEAD_DIM, HEAD_DIM), cur(lambda b, h: (b, 0, 0, h, 0, 0))),
            pl.BlockSpec((None, None, 2, 1, HEAD_DIM), cur(lambda b, h: (b, h, 0, 0, 0))),
            pl.BlockSpec((None, None, 2, HEAD_DIM), cur(lambda b, h: (b, h, 0, 0))),
        ]
        args += [cos2, sin2, s_ret, s_c, s_n, s_m]
        out_shape = y_shape
        out_specs = y_spec
    else:
        bsz = n_blk * n_seq
        st = jax.ShapeDtypeStruct((bsz, 1, 2, N_HEADS, HEAD_DIM, HEAD_DIM), F32)
        vec = jax.ShapeDtypeStruct((bsz, N_HEADS, 2, HEAD_DIM), F32)
        st_spec = pl.BlockSpec((n_seq, None, 2, None, HEAD_DIM, HEAD_DIM), cur(lambda b, h: (b, 0, 0, h, 0, 0)))
        vec_spec = pl.BlockSpec((n_seq, None, 2, HEAD_DIM), cur(lambda b, h: (b, h, 0, 0)))
        out_shape = (y_shape, st, st, vec, vec)
        out_specs = (y_spec, st_spec, st_spec, vec_spec, vec_spec)
    col = lambda dt: pltpu.VMEM((rows_blk, HEAD_DIM), dt)
    scratch = [
        pltpu.VMEM((rows_blk, D_MODEL), BF16),
        col(F32), col(F32), col(BF16), col(F32),
        col(F32), col(F32),
        pltpu.VMEM((rows_blk, 2 * HEAD_DIM), BF16),
        col(F32), col(F32),
        col(BF16), col(F32),
        pltpu.VMEM((2, rows_blk, HEAD_DIM), F32),
        pltpu.VMEM((2, rows_blk, HEAD_DIM), F32),
        pltpu.VMEM((2, n_chunks, HEAD_DIM, HEAD_DIM), F32),
        pltpu.VMEM((2, n_chunks, HEAD_DIM, 2 * HEAD_DIM), F32),
        pltpu.VMEM((10, max(n_chunks, 8), HEAD_DIM), F32),
        pltpu.VMEM((5, CHUNK, CHUNK), F32),
        pltpu.VMEM((rows_blk + 2 * CONV_HALO, HEAD_DIM), F32),
        pltpu.VMEM((rows_blk + 2 * CONV_HALO, HEAD_DIM), F32),
        col(F32),
    ]
    return pl.pallas_call(
        kern,
        out_shape=out_shape,
        grid=(n_steps + 1,),
        in_specs=in_specs,
        out_specs=out_specs,
        scratch_shapes=scratch,
        compiler_params=pltpu.CompilerParams(
            dimension_semantics=("arbitrary",), vmem_limit_bytes=VMEM_LIMIT),
        name="mixer_latent" if latent else "mixer_context",
    )(*args)


def _ffn_kernel(x_ref, y_ref, mod_ref, wo_ref, g2_ref, w1_ref, w2_ref, gf_ref, out_ref, *, mod_row):
    row = pl.ds(mod_row(pl.program_id(0)), 1)
    g1 = mod_ref[2, row, :]
    sh2 = mod_ref[3, row, :]
    sc2 = mod_ref[4, row, :]
    g2 = mod_ref[5, row, :]
    x1 = x_ref[...] + g1 * _dot(y_ref[...], wo_ref[...])
    h2 = (_rms(x1, g2_ref[...]) * (1.0 + sc2) + sh2).astype(BF16)
    f = jnp.zeros_like(x1)
    for j in range(D_FF // D_MODEL):
        cols = slice(j * D_MODEL, (j + 1) * D_MODEL)
        hid = jnp.maximum(_dot(h2, w1_ref[:, cols]), 0.0)
        f = f + _dot((hid * hid).astype(BF16), w2_ref[cols, :])
    out_ref[...] = _rms(x1 + g2 * f, gf_ref[...])


def _ffn(x2d, y2d, mod3, mod_row, w_out, norm2_g, w_ff1, w_ff2, final_g):
    n_tok = x2d.shape[0]
    const = lambda i: (0, 0)
    return pl.pallas_call(
        functools.partial(_ffn_kernel, mod_row=mod_row),
        out_shape=jax.ShapeDtypeStruct((n_tok, D_MODEL), F32),
        grid=(n_tok // FFN_ROWS,),
        in_specs=[
            pl.BlockSpec((FFN_ROWS, D_MODEL), lambda i: (i, 0)),
            pl.BlockSpec((FFN_ROWS, D_MODEL), lambda i: (i, 0)),
            pl.BlockSpec(mod3.shape, lambda i: (0, 0, 0)),
            pl.BlockSpec((D_MODEL, D_MODEL), const, pipeline_mode=pl.Buffered(1)),
            pl.BlockSpec((1, D_MODEL), const),
            pl.BlockSpec((D_MODEL, D_FF), const, pipeline_mode=pl.Buffered(1)),
            pl.BlockSpec((D_FF, D_MODEL), const, pipeline_mode=pl.Buffered(1)),
            pl.BlockSpec((1, D_MODEL), const),
        ],
        out_specs=pl.BlockSpec((FFN_ROWS, D_MODEL), lambda i: (i, 0)),
        compiler_params=pltpu.CompilerParams(
            dimension_semantics=("arbitrary",), vmem_limit_bytes=VMEM_LIMIT),
        name="outproj_mlp",
    )(x2d, y2d, mod3, w_out, norm2_g, w_ff1, w_ff2, final_g)


def _rope_tables(seq):
    pos = np.arange(seq)
    row = (pos // GRID_W).astype(np.float64)
    col = (pos % GRID_W).astype(np.float64)
    nf = HEAD_DIM // 4
    inv = ROPE_BASE ** (-np.arange(nf, dtype=np.float64) / nf)
    ang = np.concatenate([row[:, None] * inv, col[:, None] * inv], -1)
    cos = np.cos(ang)
    sin = np.sin(ang)
    cos2 = np.concatenate([cos, cos], -1).astype(np.float32)
    sin2 = np.concatenate([-sin, sin], -1).astype(np.float32)
    return jnp.asarray(cos2), jnp.asarray(sin2)


def kernel(x_prompt, x_sample, state_ret, state_mlstm_C, state_mlstm_n, state_mlstm_m, c, c_ctx,
           w_ada, b_ada, norm1_g, norm2_g, w_in, conv_w, ret_decay_logit, mlstm_gate_bias,
           ret_gn_g, mlstm_gn_g, w_out, w_ff1, w_ff2, final_g):
    assert w_ada.shape[0] == 1, "single-layer kernel"
    bp, tp, _ = x_prompt.shape
    bs, ts, _ = x_sample.shape
    assert tp % CHUNK == 0 and ts % CHUNK == 0 and bp % CTX_SEQS_PER_STEP == 0

    mod3, w_pair, w_gate, w_out_p, w1, w2 = _prepare(c_ctx, c, w_ada, b_ada, w_in, w_out, w_ff1, w_ff2)


    taps = jnp.transpose(conv_w[0].reshape(3, 2 * N_HEADS, HEAD_DIM), (1, 0, 2))
    decay = jnp.broadcast_to(jnp.transpose(ret_decay_logit[0])[:, :, None], (N_HEADS, 2, HEAD_DIM))
    gate_bias = jnp.pad(mlstm_gate_bias[0].reshape(1, 1, N_GATE_COLS), ((0, 0), (0, 0), (HEAD_DIM - N_GATE_COLS, 0)))
    head_params = jnp.concatenate(
        [taps[:N_HEADS], taps[N_HEADS:],
         ret_gn_g[0].reshape(N_HEADS, 1, HEAD_DIM), mlstm_gn_g[0].reshape(N_HEADS, 1, HEAD_DIM),
         decay, jnp.broadcast_to(gate_bias, (N_HEADS, 1, HEAD_DIM)),
         jnp.zeros((N_HEADS, 5, HEAD_DIM), F32)], axis=1).astype(F32)

    g1 = norm1_g[0][None, :]
    g2 = norm2_g[0][None, :]
    gf = final_g[None, :]
    xp2d = x_prompt.reshape(bp * tp, D_MODEL)
    xs2d = x_sample.reshape(bs * ts, D_MODEL)

    y_p, new_ret, new_c, new_n, new_m = _mixer(
        xp2d, mod3, lambda b: 0, g1, w_pair, w_gate, head_params,
        latent=False, n_seq=CTX_SEQS_PER_STEP, cps=tp // CHUNK)
    out_p = _ffn(xp2d, y_p, mod3, lambda i: 0, w_out_p, g2, w1, w2, gf).reshape(bp, tp, D_MODEL)

    s_n = jnp.transpose(state_mlstm_n[:, 0], (0, 2, 1, 3))[:, :, :, None, :]
    s_m = jnp.broadcast_to(jnp.transpose(state_mlstm_m[:, 0], (0, 2, 1))[..., None],
                           (bs, N_HEADS, 2, HEAD_DIM))
    y_s = _mixer(xs2d, mod3, lambda b: 1 + b, g1, w_pair, w_gate, head_params,
                 latent=True, n_seq=1, cps=ts // CHUNK,
                 rope=_rope_tables(ts), states=(state_ret, state_mlstm_C, s_n, s_m))
    tiles_per_seq = ts // FFN_ROWS
    out_s = _ffn(xs2d, y_s, mod3, lambda i: 1 + i // tiles_per_seq,
                 w_out_p, g2, w1, w2, gf).reshape(bs, ts, D_MODEL)

    new_n = jnp.transpose(new_n, (0, 2, 1, 3))[:, None]
    new_m = jnp.transpose(new_m[..., 0], (0, 2, 1))[:, None]
    return out_p, out_s, new_ret, new_c, new_n, new_m
```

```python
import functools

import numpy as np
import jax
import jax.numpy as jnp
from jax import lax
from jax.experimental import pallas as pl
from jax.experimental.pallas import tpu as pltpu

F32 = jnp.float32
BF16 = jnp.bfloat16

D_MODEL = 1024
N_HEADS = 4
HEAD_DIM = 128
CHUNK = 128
GRID_W = 64
D_FF = 4 * D_MODEL
EPS = 1e-6
ROPE_BASE = 10000.0
LOG2_E = 1.4426950408889634
LN_2 = 0.6931471805599453
PAIR_COLS = 8 * HEAD_DIM
N_GATE_COLS = 4 * N_HEADS
ROW_TILE = 256
FFN_ROWS = 512
CTX_SEQS_PER_STEP = 8
PROJ_SLAB_ROWS = 256
CONV_HALO = 8
VMEM_LIMIT = 60 * 1024 * 1024


def _dot(a, b):
    return jnp.dot(a, b, preferred_element_type=F32)


def _dot_nt(a, b):
    return lax.dot_general(a, b, (((1,), (1,)), ((), ())), preferred_element_type=F32)


def _dot_tn(a, b):
    return lax.dot_general(a, b, (((0,), (0,)), ((), ())), preferred_element_type=F32)


def _rms(x, g):
    return x * lax.rsqrt(jnp.mean(x * x, axis=-1, keepdims=True) + EPS) * g


def _group_norm(o, g):
    mu = jnp.mean(o, axis=-1, keepdims=True)
    c = o - mu
    var = jnp.mean(c * c, axis=-1, keepdims=True)
    return c * lax.rsqrt(var + EPS) * g


def _log_sigmoid(x):
    return jnp.minimum(x, 0.0) - jnp.log(1.0 + jnp.exp(-jnp.abs(x)))


def _sigmoid(x):
    return 1.0 / (1.0 + jnp.exp(-x))


def _split2(x):
    hi = x.astype(BF16)
    lo = (x - hi.astype(F32)).astype(BF16)
    return hi, lo


PREP_STEPS = 8
N_MOD = 6
COND_ROWS = 8


def _mod_kernel(cctx_ref, c_ref, w_ref, b_ref, out_ref):
    row = lax.broadcasted_iota(jnp.int32, out_ref.shape, 0)
    c = jnp.where(row == 0, cctx_ref[...], 0.0)
    for b in range(c_ref.shape[0]):
        c = jnp.where(row == 1 + b, c_ref[b:b + 1, :], c)
    s = (c * _sigmoid(c)).astype(BF16)
    out_ref[...] = _dot(s, w_ref[...].astype(BF16)) + b_ref[...]


def _regroup_in_kernel(wt_ref, out_ref, gate_ref):
    for h in range(N_HEADS):
        for g in range(8):
            src = (g * N_HEADS + h) * HEAD_DIM
            dst = (h * 8 + g) * HEAD_DIM
            out_ref[:, dst:dst + HEAD_DIM] = wt_ref[src:src + HEAD_DIM, :].T.astype(BF16)
    n_rows = wt_ref.shape[0]
    n_gate = n_rows - N_HEADS * PAIR_COLS
    tail = wt_ref[n_rows - HEAD_DIM:n_rows, :].T
    lane = lax.broadcasted_iota(jnp.int32, tail.shape, 1)
    gate_ref[...] = jnp.where(lane >= HEAD_DIM - n_gate, tail, 0.0).astype(BF16)


def _regroup_out_kernel(w_ref, out_ref):
    for h in range(N_HEADS):
        for g in range(2):
            src = (g * N_HEADS + h) * HEAD_DIM
            dst = (h * 2 + g) * HEAD_DIM
            out_ref[dst:dst + HEAD_DIM, :] = w_ref[src:src + HEAD_DIM, :].astype(BF16)


def _prepare_kernel(cctx_ref, c_ref, wada_ref, bada_ref, wt_ref, wout_ref, w1_ref, w2_ref,
                    mod_ref, wpair_ref, wgate_ref, woutp_ref, w1o_ref, w2o_ref):
    pl.when(pl.program_id(0) < N_MOD)(
        functools.partial(_mod_kernel, cctx_ref, c_ref, wada_ref, bada_ref, mod_ref))
    _regroup_in_kernel(wt_ref, wpair_ref, wgate_ref)
    _regroup_out_kernel(wout_ref, woutp_ref)
    w1o_ref[...] = w1_ref[...].astype(BF16)
    w2o_ref[...] = w2_ref[...].astype(BF16)


def _prepare(c_ctx, c, w_ada, b_ada, w_in, w_out, w_ff1, w_ff2):
    w_t = jnp.transpose(w_in[0])
    n_cols = w_t.shape[0]
    assert n_cols == N_HEADS * PAIR_COLS + N_GATE_COLS
    assert w_ada.shape[-1] == N_MOD * D_MODEL and N_MOD <= PREP_STEPS and 1 + c.shape[0] <= COND_ROWS
    k_rows = D_MODEL // PREP_STEPS
    mod_step = lambda i: jnp.minimum(i, N_MOD - 1)
    f_cols = D_FF // PREP_STEPS
    bf = lambda shape: jax.ShapeDtypeStruct(shape, BF16)
    return pl.pallas_call(
        _prepare_kernel,
        out_shape=(jax.ShapeDtypeStruct((N_MOD, COND_ROWS, D_MODEL), F32),
                   bf((D_MODEL, N_HEADS * PAIR_COLS)), bf((D_MODEL, HEAD_DIM)),
                   bf((D_MODEL, D_MODEL)), bf((D_MODEL, D_FF)), bf((D_FF, D_MODEL))),
        grid=(PREP_STEPS,),
        in_specs=[pl.BlockSpec((1, D_MODEL), lambda i: (0, 0)),
                  pl.BlockSpec(c.shape, lambda i: (0, 0)),
                  pl.BlockSpec((None, D_MODEL, D_MODEL), lambda i: (0, 0, mod_step(i))),
                  pl.BlockSpec((1, D_MODEL), lambda i: (0, mod_step(i))),
                  pl.BlockSpec((n_cols, k_rows), lambda i: (0, i)),
                  pl.BlockSpec((None, D_MODEL, k_rows), lambda i: (0, 0, i)),
                  pl.BlockSpec((None, D_MODEL, f_cols), lambda i: (0, 0, i)),
                  pl.BlockSpec((None, f_cols, D_MODEL), lambda i: (0, i, 0))],
        out_specs=(pl.BlockSpec((None, COND_ROWS, D_MODEL), lambda i: (mod_step(i), 0, 0)),
                   pl.BlockSpec((k_rows, N_HEADS * PAIR_COLS), lambda i: (i, 0)),
                   pl.BlockSpec((k_rows, HEAD_DIM), lambda i: (i, 0)),
                   pl.BlockSpec((D_MODEL, k_rows), lambda i: (0, i)),
                   pl.BlockSpec((D_MODEL, f_cols), lambda i: (0, i)),
                   pl.BlockSpec((f_cols, D_MODEL), lambda i: (i, 0))),
        compiler_params=pltpu.CompilerParams(
            dimension_semantics=("arbitrary",), vmem_limit_bytes=VMEM_LIMIT),
        name="prepare_weights",
    )(c_ctx[None, :], c, w_ada, b_ada, w_t, w_out, w_ff1, w_ff2)


_B_END, _MAX_LWE, _M_PREV, _M_NEW, _DECAY = 0, 2, 4, 6, 8


def _mixer_kernel(*refs, latent, n_seq, cps, n_steps, mod_row):
    L = CHUNK
    n_chunks = n_seq * cps
    rows_total = n_chunks * L
    seq_len = cps * L
    assert seq_len & (seq_len - 1) == 0
    if latent:
        (x_ref, mod_ref, g1_ref, w_ref, wg_ref, hp_ref, hp_prev_ref, cos_ref, sin_ref,
         sr_in, sc_in, sn_in, sm_in, y_ref, *scratch) = refs
    else:
        (x_ref, mod_ref, g1_ref, w_ref, wg_ref, hp_ref, hp_prev_ref,
         y_ref, so_ref, co_ref, no_ref, mo_ref, *scratch) = refs
    (hn_ref, rq_ref, rk_ref, rv_ref, rg_ref, mq_ref, mk_ref, mvx_ref, mog_ref, gt_ref,
     ar_ref, am_ref, logi_ref, bc_ref, skv_ref, ckv_ref, cs_ref, rc_ref, rawq_ref, rawk_ref,
     gall_ref) = scratch

    step = pl.program_id(0)
    live = step < n_steps
    head = lax.rem(jnp.minimum(step, n_steps - 1), N_HEADS)

    @pl.when(step == 0)
    def _():
        for ref in (rq_ref, am_ref, rg_ref, mog_ref):
            ref[...] = jnp.zeros_like(ref)

    @pl.when(jnp.logical_and(head == 0, live))
    def _():
        row = pl.ds(mod_row(jnp.minimum(step, n_steps - 1) // N_HEADS), 1)
        gain = g1_ref[...] * (1.0 + mod_ref[1, row, :])
        sh1 = mod_ref[0, row, :]
        gate_bias = hp_ref[10:11, :]
        lane_t = lax.broadcasted_iota(jnp.int32, (ROW_TILE, HEAD_DIM), 1)
        is_forget = jnp.bitwise_and(lane_t, 4) == 4

        def body(i, carry):
            r = pl.ds(pl.multiple_of(i * ROW_TILE, ROW_TILE), ROW_TILE)
            xr = x_ref[r, :]
            inv = lax.rsqrt(jnp.mean(xr * xr, axis=-1, keepdims=True) + EPS)
            hn_t = (xr * inv * gain + sh1).astype(BF16)
            hn_ref[r, :] = hn_t
            gates = _dot(hn_t, wg_ref[...]) + gate_bias
            gall_ref[r, :] = jnp.where(is_forget, _log_sigmoid(gates), gates) * LOG2_E
            return carry

        lax.fori_loop(0, rows_total // ROW_TILE, body, 0, unroll=4)

    hp = hp_ref[...]
    scale = HEAD_DIM ** -0.5

    r_i = lax.broadcasted_iota(jnp.int32, (L, L), 0)
    s_i = lax.broadcasted_iota(jnp.int32, (L, L), 1)
    r_f = r_i.astype(F32)
    s_f = s_i.astype(F32)
    lg_f = _log_sigmoid(hp[8:9])
    lg_b = _log_sigmoid(hp[9:10])
    rc_ref[0] = (jnp.where(r_i >= s_i, jnp.exp(lg_f * jnp.where(r_i >= s_i, r_f - s_f, 0.0)), 0.0)
                 + jnp.where(s_i >= r_i, jnp.exp(lg_b * jnp.where(s_i >= r_i, s_f - r_f, 0.0)), 0.0))
    rc_ref[1] = jnp.exp(lg_f * (r_f + 1.0))
    rc_ref[2] = jnp.exp(lg_b * (L - r_f))
    rc_ref[3] = jnp.exp(lg_f * (L - 1.0 - r_f))
    rc_ref[4] = jnp.exp(lg_b * r_f)
    chunk_decay = (jnp.exp(lg_f * float(L)), jnp.exp(lg_b * float(L)))

    tri = (jnp.where(r_i >= s_i, 1.0, 0.0).astype(BF16),
           jnp.where(s_i >= r_i, 1.0, 0.0).astype(BF16))

    for raw_ref in (rawq_ref, rawk_ref):
        raw_ref[0:CONV_HALO, :] = jnp.zeros((CONV_HALO, HEAD_DIM), F32)
        raw_ref[rows_total + CONV_HALO:rows_total + 2 * CONV_HALO, :] = jnp.zeros((CONV_HALO, HEAD_DIM), F32)

    def rows(c):
        return slice(c * L, (c + 1) * L)

    def gate_col(c, lane):
        g = gt_ref[rows(c), :]
        return jnp.broadcast_to(g[:, lane:lane + 1], (L, HEAD_DIM))

    hp_prev = hp_prev_ref[...]

    def finish_rows(r):
        rg = rg_ref[r, :]
        ret_y = _group_norm(rq_ref[r, :], hp_prev[6:7]) * (rg * _sigmoid(rg))
        y_ref[r, 0:128] = ret_y.astype(BF16)
        ml_y = _group_norm(am_ref[r, :], hp_prev[7:8]) * _sigmoid(mog_ref[r, :])
        y_ref[r, 128:256] = ml_y.astype(BF16)

    def project(blk, n_rows):
        hn = hn_ref[blk, :]
        pr = _dot(hn, w_ref[:, 0:256])
        q = pr[:, 0:128]
        k = pr[:, 128:256] * scale
        if latent:
            cos2 = cos_ref[blk, :]
            sin2 = sin_ref[blk, :]
            q = q * cos2 + pltpu.roll(q, HEAD_DIM // 2, axis=1) * sin2
            k = k * cos2 + pltpu.roll(k, HEAD_DIM // 2, axis=1) * sin2
        rq_ref[blk, :] = q
        rk_ref[blk, :] = k

        pv = _dot(hn, w_ref[:, 256:512])
        rv_ref[blk, :] = pv[:, 0:128].astype(BF16)
        rg_ref[blk, :] = pv[:, 128:256]

        pm = _dot(hn, w_ref[:, 512:768])
        halo_blk = slice(blk.start + CONV_HALO, blk.stop + CONV_HALO)
        rawq_ref[halo_blk, :] = pm[:, 0:128]
        rawk_ref[halo_blk, :] = pm[:, 128:256]

        po = _dot(hn, w_ref[:, 768:1024])
        mvx_ref[blk, 0:128] = po[:, 0:128].astype(BF16)
        mvx_ref[blk, 128:256] = jnp.ones((n_rows, HEAD_DIM), BF16)
        mog_ref[blk, :] = po[:, 128:256]

        gt_ref[blk, :] = pltpu.roll(gall_ref[blk, :], N_GATE_COLS - head, axis=1)

    def conv_silu(blk, n_rows):
        sub = lax.broadcasted_iota(jnp.int32, (CONV_HALO, HEAD_DIM), 0)
        starts_seq = blk.start % seq_len == 0
        ends_seq = blk.stop % seq_len == 0
        for raw_ref, dst_ref, taps, post in ((rawq_ref, mq_ref, hp[0:3], 1.0), (rawk_ref, mk_ref, hp[3:6], scale)):
            lo = blk.start + CONV_HALO
            prev = raw_ref[lo - 1:lo - 1 + n_rows, :]
            nxt = raw_ref[lo + 1:lo + 1 + n_rows, :]
            if starts_seq:
                first = jnp.where(sub == 0, 0.0, prev[0:CONV_HALO, :])
                prev = jnp.concatenate([first, prev[CONV_HALO:, :]], axis=0)
            if ends_seq:
                last = jnp.where(sub == CONV_HALO - 1, 0.0, nxt[n_rows - CONV_HALO:, :])
                nxt = jnp.concatenate([nxt[:n_rows - CONV_HALO, :], last], axis=0)
            out = prev * taps[0:1] + raw_ref[lo:lo + n_rows, :] * taps[1:2] + nxt * taps[2:3]
            out = out * _sigmoid(out)
            dst_ref[blk, :] = out if post == 1.0 else out * post

    def project_and_scan():
        slab_rows = min(seq_len, PROJ_SLAB_ROWS)
        assert seq_len % slab_rows == 0
        slabs = [slice(s0, s0 + slab_rows) for s0 in range(0, rows_total, slab_rows)]
        for i, slab in enumerate(slabs):
            finish_rows(slab)
            project(slab, slab_rows)
            if i > 0:
                conv_silu(slabs[i - 1], slab_rows)
        conv_silu(slabs[-1], slab_rows)
        pre_pass(range(n_chunks))
        scan_passes(range(n_chunks), range(n_seq))

    def pre_pass(chunks):
        for c in chunks:
            ar_ref[rows(c), :] = (_dot_nt(rq_ref[rows(c), :].astype(BF16), rk_ref[rows(c), :].astype(BF16))
                                  * rc_ref[0]).astype(BF16)
            am_ref[rows(c), :] = _dot_nt(mq_ref[rows(c), :].astype(BF16), mk_ref[rows(c), :].astype(BF16))
        for c in chunks:
            for d in range(2):
                li = gate_col(c, 8 * d)
                lf = gate_col(c, 8 * d + 4)
                strict = (r_i > s_i) if d == 0 else (r_i < s_i)
                x_mat = jnp.where(strict, lf, jnp.where(r_i == s_i, li, 0.0))
                hi, lo = _split2(x_mat)
                d_mat = _dot(tri[d], hi) + _dot(tri[d], lo)
                e = 0 if d == 0 else L - 1
                bcum = (jnp.broadcast_to(d_mat[:, e:e + 1], (L, HEAD_DIM))
                        + (lf[e:e + 1, :] - li[e:e + 1, :]))
                causal = (s_i <= r_i) if d == 0 else (s_i >= r_i)
                logi_ref[d, rows(c), :] = jnp.where(causal, d_mat, -jnp.inf)
                bc_ref[d, rows(c), :] = bcum
                b_end = bcum[L - 1:L, :] if d == 0 else bcum[0:1, :]
                cs_ref[_B_END + d, c:c + 1, :] = b_end
                cs_ref[_MAX_LWE + d, c:c + 1, :] = jnp.max(b_end - bcum + li, axis=0, keepdims=True)
        for c in chunks:
            kf = rk_ref[rows(c), :]
            vb = rv_ref[rows(c), :]
            for d in range(2):
                skv_ref[d, c] = _dot_tn((kf * rc_ref[3 + d]).astype(BF16), vb)

    def scan_passes(chunks, seqs):
        m_final = {}
        for s in seqs:
            for d in range(2):
                m = sm_in[d:d + 1, :] * LOG2_E if latent else jnp.zeros((1, HEAD_DIM), F32)
                order = range(cps) if d == 0 else range(cps - 1, -1, -1)
                for j in order:
                    c = s * cps + j
                    cs_ref[_M_PREV + d, c:c + 1, :] = m
                    b_end = cs_ref[_B_END + d, c:c + 1, :]
                    m_new = jnp.maximum(b_end + m, cs_ref[_MAX_LWE + d, c:c + 1, :])
                    cs_ref[_M_NEW + d, c:c + 1, :] = m_new
                    cs_ref[_DECAY + d, c:c + 1, :] = jnp.exp2(b_end + m - m_new)
                    m = m_new
                m_final[(s, d)] = m

        for c in chunks:
            kf = mk_ref[rows(c), :]
            vx = mvx_ref[rows(c), :]
            for d in range(2):
                log_w_end = cs_ref[_B_END + d, c:c + 1, :] - bc_ref[d, rows(c), :] + gate_col(c, 8 * d)
                w_end = jnp.exp2(log_w_end - cs_ref[_M_NEW + d, c:c + 1, :])
                ckv_ref[d, c] = _dot_tn((kf * w_end).astype(BF16), vx)

        for s in seqs:
            for d in range(2):
                if latent:
                    s_state = sr_in[d]
                    n_rep = jnp.broadcast_to(sn_in[d], (HEAD_DIM, HEAD_DIM)).T
                    c_state = jnp.concatenate([sc_in[d], n_rep], axis=1)
                else:
                    s_state = jnp.zeros((HEAD_DIM, HEAD_DIM), F32)
                    c_state = jnp.zeros((HEAD_DIM, 2 * HEAD_DIM), F32)
                order = range(cps) if d == 0 else range(cps - 1, -1, -1)
                for j in order:
                    c = s * cps + j
                    inc = skv_ref[d, c]
                    skv_ref[d, c] = s_state
                    s_state = s_state * chunk_decay[d] + inc
                    inc = ckv_ref[d, c]
                    ckv_ref[d, c] = c_state
                    decay = cs_ref[_DECAY + d, c:c + 1, :]
                    c_state = c_state * jnp.concatenate([decay, decay], axis=1) + inc
                if not latent:
                    so_ref[s, d] = s_state
                    co_ref[s, d] = c_state[:, 0:128]
                    no_ref[s, d:d + 1, :] = c_state[:, 128:256].T[0:1, :]
                    mo_ref[s, d:d + 1, :] = m_final[(s, d)] * LN_2

        for c in chunks:
            qf = rq_ref[rows(c), :]
            lhs = jnp.concatenate([ar_ref[rows(c), :], (qf * rc_ref[1]).astype(BF16),
                                   (qf * rc_ref[2]).astype(BF16)], axis=1)
            rhs = jnp.concatenate([rv_ref[rows(c), :], skv_ref[0, c].astype(BF16),
                                   skv_ref[1, c].astype(BF16)], axis=0)
            rq_ref[rows(c), :] = _dot(lhs, rhs)
            qf = mq_ref[rows(c), :]
            a_mat = am_ref[rows(c), :]
            vx = mvx_ref[rows(c), :]
            h_sum = None
            for d in range(2):
                log_intra = logi_ref[d, rows(c), :]
                log_inter = bc_ref[d, rows(c), :] + cs_ref[_M_PREV + d, c:c + 1, :]
                m_t = jnp.maximum(log_inter, jnp.max(log_intra, axis=1, keepdims=True))
                w_inter = jnp.exp2(log_inter - m_t)
                w_intra = jnp.exp2(log_intra - m_t)
                lhs = jnp.concatenate([(a_mat * w_intra).astype(BF16), (qf * w_inter).astype(BF16)], axis=1)
                rhs = jnp.concatenate([vx, ckv_ref[d, c].astype(BF16)], axis=0)
                res = _dot(lhs, rhs)
                h_dir = res[:, 0:128] / jnp.maximum(jnp.abs(res[:, 128:256]), jnp.exp2(-m_t))
                h_sum = h_dir if h_sum is None else h_sum + h_dir
            am_ref[rows(c), :] = h_sum

    pl.when(live)(project_and_scan)

    @pl.when(step == n_steps)
    def _():
        def body(i, carry):
            finish_rows(pl.ds(pl.multiple_of(i * ROW_TILE, ROW_TILE), ROW_TILE))
            return carry

        lax.fori_loop(0, rows_total // ROW_TILE, body, 0)


def _mixer(x2d, mod3, mod_row, norm1_g, w_pair, w_gate, head_params, *, latent, n_seq, cps,
           rope=None, states=None):
    n_tok = x2d.shape[0]
    rows_blk = n_seq * cps * CHUNK
    n_blk = n_tok // rows_blk
    n_chunks = n_seq * cps
    n_steps = n_blk * N_HEADS
    kern = functools.partial(_mixer_kernel, latent=latent, n_seq=n_seq, cps=cps, n_steps=n_steps,
                             mod_row=mod_row)

    def cur(f):
        def index_map(j):
            item = jnp.minimum(j, n_steps - 1)
            return f(item // N_HEADS, item % N_HEADS)
        return index_map

    def prev(f):
        def index_map(j):
            item = jnp.maximum(j - 1, 0)
            return f(item // N_HEADS, item % N_HEADS)
        return index_map

    once = pl.Buffered(1)
    in_specs = [
        pl.BlockSpec((rows_blk, D_MODEL), cur(lambda b, h: (b, 0))),
        pl.BlockSpec(mod3.shape, lambda j: (0, 0, 0)),
        pl.BlockSpec((1, D_MODEL), lambda j: (0, 0)),
        pl.BlockSpec((D_MODEL, PAIR_COLS), cur(lambda b, h: (0, h))),
        pl.BlockSpec((D_MODEL, HEAD_DIM), lambda j: (0, 0)),
        pl.BlockSpec((None, 16, HEAD_DIM), cur(lambda b, h: (h, 0, 0))),
        pl.BlockSpec((None, 16, HEAD_DIM), prev(lambda b, h: (h, 0, 0))),
    ]
    args = [x2d, mod3, norm1_g, w_pair, w_gate, head_params, head_params]
    y_shape = jax.ShapeDtypeStruct((n_tok, D_MODEL), BF16)
    y_spec = pl.BlockSpec((rows_blk, 2 * HEAD_DIM), prev(lambda b, h: (b, h)))
    if latent:
        assert n_seq == 1
        cos2, sin2 = rope
        s_ret, s_c, s_n, s_m = states
        in_specs += [
            pl.BlockSpec((rows_blk, HEAD_DIM), lambda j: (0, 0), pipeline_mode=once),
            pl.BlockSpec((rows_blk, HEAD_DIM), lambda j: (0, 0), pipeline_mode=once),
            pl.BlockSpec((None, None, 2, None, HEAD_DIM, HEAD_DIM), cur(lambda b, h: (b, 0, 0, h, 0, 0))),
            pl.BlockSpec((None, None, 2, None, HEAD_DIM, HEAD_DIM), cur(lambda b, h: (b, 0, 0, h, 0, 0))),
            pl.BlockSpec((None, None, 2, 1, HEAD_DIM), cur(lambda b, h: (b, h, 0, 0, 0))),
            pl.BlockSpec((None, None, 2, HEAD_DIM), cur(lambda b, h: (b, h, 0, 0))),
        ]
        args += [cos2, sin2, s_ret, s_c, s_n, s_m]
        out_shape = y_shape
        out_specs = y_spec
    else:
        bsz = n_blk * n_seq
        st = jax.ShapeDtypeStruct((bsz, 1, 2, N_HEADS, HEAD_DIM, HEAD_DIM), F32)
        vec = jax.ShapeDtypeStruct((bsz, N_HEADS, 2, HEAD_DIM), F32)
        st_spec = pl.BlockSpec((n_seq, None, 2, None, HEAD_DIM, HEAD_DIM), cur(lambda b, h: (b, 0, 0, h, 0, 0)))
        vec_spec = pl.BlockSpec((n_seq, None, 2, HEAD_DIM), cur(lambda b, h: (b, h, 0, 0)))
        out_shape = (y_shape, st, st, vec, vec)
        out_specs = (y_spec, st_spec, st_spec, vec_spec, vec_spec)
    col = lambda dt: pltpu.VMEM((rows_blk, HEAD_DIM), dt)
    scratch = [
        pltpu.VMEM((rows_blk, D_MODEL), BF16),
        col(F32), col(F32), col(BF16), col(F32),
        col(F32), col(F32),
        pltpu.VMEM((rows_blk, 2 * HEAD_DIM), BF16),
        col(F32), col(F32),
        col(BF16), col(F32),
        pltpu.VMEM((2, rows_blk, HEAD_DIM), F32),
        pltpu.VMEM((2, rows_blk, HEAD_DIM), F32),
        pltpu.VMEM((2, n_chunks, HEAD_DIM, HEAD_DIM), F32),
        pltpu.VMEM((2, n_chunks, HEAD_DIM, 2 * HEAD_DIM), F32),
        pltpu.VMEM((10, max(n_chunks, 8), HEAD_DIM), F32),
        pltpu.VMEM((5, CHUNK, CHUNK), F32),
        pltpu.VMEM((rows_blk + 2 * CONV_HALO, HEAD_DIM), F32),
        pltpu.VMEM((rows_blk + 2 * CONV_HALO, HEAD_DIM), F32),
        col(F32),
    ]
    return pl.pallas_call(
        kern,
        out_shape=out_shape,
        grid=(n_steps + 1,),
        in_specs=in_specs,
        out_specs=out_specs,
        scratch_shapes=scratch,
        compiler_params=pltpu.CompilerParams(
            dimension_semantics=("arbitrary",), vmem_limit_bytes=VMEM_LIMIT),
        name="mixer_latent" if latent else "mixer_context",
    )(*args)


def _ffn_kernel(x_ref, y_ref, mod_ref, wo_hbm, g2_ref, w1_hbm, w2_hbm, gf_ref, out_ref,
                wo_ref, w1_ref, w2_ref, sem, *, mod_row):
    n_ff_chunks = D_FF // D_MODEL
    ff_cols = lambda j: pl.ds(j * D_MODEL, D_MODEL)
    wo_copy = pltpu.make_async_copy(wo_hbm, wo_ref, sem.at[0])
    w1_copy = [pltpu.make_async_copy(w1_hbm.at[:, ff_cols(j)], w1_ref.at[:, ff_cols(j)], sem.at[1 + 2 * j])
               for j in range(n_ff_chunks)]
    w2_copy = [pltpu.make_async_copy(w2_hbm.at[ff_cols(j), :], w2_ref.at[ff_cols(j), :], sem.at[2 + 2 * j])
               for j in range(n_ff_chunks)]

    def tile(first):
        if first:
            wo_copy.start()
            for j in range(n_ff_chunks):
                w1_copy[j].start()
                w2_copy[j].start()
        row = pl.ds(mod_row(pl.program_id(0)), 1)
        g1 = mod_ref[2, row, :]
        sh2 = mod_ref[3, row, :]
        sc2 = mod_ref[4, row, :]
        g2 = mod_ref[5, row, :]
        if first:
            wo_copy.wait()
        x1 = x_ref[...] + g1 * _dot(y_ref[...], wo_ref[...])
        h2 = (_rms(x1, g2_ref[...]) * (1.0 + sc2) + sh2).astype(BF16)
        f = jnp.zeros_like(x1)
        for j in range(n_ff_chunks):
            cols = slice(j * D_MODEL, (j + 1) * D_MODEL)
            if first:
                w1_copy[j].wait()
                w2_copy[j].wait()
            hid = jnp.maximum(_dot(h2, w1_ref[:, cols]), 0.0)
            f = f + _dot((hid * hid).astype(BF16), w2_ref[cols, :])
        out_ref[...] = _rms(x1 + g2 * f, gf_ref[...])

    is_first = pl.program_id(0) == 0
    pl.when(is_first)(functools.partial(tile, True))
    pl.when(jnp.logical_not(is_first))(functools.partial(tile, False))


def _ffn(x2d, y2d, mod3, mod_row, w_out, norm2_g, w_ff1, w_ff2, final_g):
    n_tok = x2d.shape[0]
    const = lambda i: (0, 0)
    return pl.pallas_call(
        functools.partial(_ffn_kernel, mod_row=mod_row),
        out_shape=jax.ShapeDtypeStruct((n_tok, D_MODEL), F32),
        grid=(n_tok // FFN_ROWS,),
        in_specs=[
            pl.BlockSpec((FFN_ROWS, D_MODEL), lambda i: (i, 0)),
            pl.BlockSpec((FFN_ROWS, D_MODEL), lambda i: (i, 0)),
            pl.BlockSpec(mod3.shape, lambda i: (0, 0, 0)),
            pl.BlockSpec(memory_space=pl.ANY),
            pl.BlockSpec((1, D_MODEL), const),
            pl.BlockSpec(memory_space=pl.ANY),
            pl.BlockSpec(memory_space=pl.ANY),
            pl.BlockSpec((1, D_MODEL), const),
        ],
        out_specs=pl.BlockSpec((FFN_ROWS, D_MODEL), lambda i: (i, 0)),
        scratch_shapes=[pltpu.VMEM(w_out.shape, BF16), pltpu.VMEM(w_ff1.shape, BF16),
                        pltpu.VMEM(w_ff2.shape, BF16),
                        pltpu.SemaphoreType.DMA((1 + 2 * (D_FF // D_MODEL),))],
        compiler_params=pltpu.CompilerParams(
            dimension_semantics=("arbitrary",), vmem_limit_bytes=VMEM_LIMIT),
        name="outproj_mlp",
    )(x2d, y2d, mod3, w_out, norm2_g, w_ff1, w_ff2, final_g)


def _rope_tables(seq):
    pos = np.arange(seq)
    row = (pos // GRID_W).astype(np.float64)
    col = (pos % GRID_W).astype(np.float64)
    nf = HEAD_DIM // 4
    inv = ROPE_BASE ** (-np.arange(nf, dtype=np.float64) / nf)
    ang = np.concatenate([row[:, None] * inv, col[:, None] * inv], -1)
    cos = np.cos(ang)
    sin = np.sin(ang)
    cos2 = np.concatenate([cos, cos], -1).astype(np.float32)
    sin2 = np.concatenate([-sin, sin], -1).astype(np.float32)
    return jnp.asarray(cos2), jnp.asarray(sin2)


def kernel(x_prompt, x_sample, state_ret, state_mlstm_C, state_mlstm_n, state_mlstm_m, c, c_ctx,
           w_ada, b_ada, norm1_g, norm2_g, w_in, conv_w, ret_decay_logit, mlstm_gate_bias,
           ret_gn_g, mlstm_gn_g, w_out, w_ff1, w_ff2, final_g):
    assert w_ada.shape[0] == 1, "single-layer kernel"
    bp, tp, _ = x_prompt.shape
    bs, ts, _ = x_sample.shape
    assert tp % CHUNK == 0 and ts % CHUNK == 0 and bp % CTX_SEQS_PER_STEP == 0

    mod3, w_pair, w_gate, w_out_p, w1, w2 = _prepare(c_ctx, c, w_ada, b_ada, w_in, w_out, w_ff1, w_ff2)


    taps = jnp.transpose(conv_w[0].reshape(3, 2 * N_HEADS, HEAD_DIM), (1, 0, 2))
    decay = jnp.broadcast_to(jnp.transpose(ret_decay_logit[0])[:, :, None], (N_HEADS, 2, HEAD_DIM))
    gate_bias = jnp.pad(mlstm_gate_bias[0].reshape(1, 1, N_GATE_COLS), ((0, 0), (0, 0), (HEAD_DIM - N_GATE_COLS, 0)))
    head_params = jnp.concatenate(
        [taps[:N_HEADS], taps[N_HEADS:],
         ret_gn_g[0].reshape(N_HEADS, 1, HEAD_DIM), mlstm_gn_g[0].reshape(N_HEADS, 1, HEAD_DIM),
         decay, jnp.broadcast_to(gate_bias, (N_HEADS, 1, HEAD_DIM)),
         jnp.zeros((N_HEADS, 5, HEAD_DIM), F32)], axis=1).astype(F32)

    g1 = norm1_g[0][None, :]
    g2 = norm2_g[0][None, :]
    gf = final_g[None, :]
    xp2d = x_prompt.reshape(bp * tp, D_MODEL)
    xs2d = x_sample.reshape(bs * ts, D_MODEL)

    y_p, new_ret, new_c, new_n, new_m = _mixer(
        xp2d, mod3, lambda b: 0, g1, w_pair, w_gate, head_params,
        latent=False, n_seq=CTX_SEQS_PER_STEP, cps=tp // CHUNK)
    out_p = _ffn(xp2d, y_p, mod3, lambda i: 0, w_out_p, g2, w1, w2, gf).reshape(bp, tp, D_MODEL)

    s_n = jnp.transpose(state_mlstm_n[:, 0], (0, 2, 1, 3))[:, :, :, None, :]
    s_m = jnp.broadcast_to(jnp.transpose(state_mlstm_m[:, 0], (0, 2, 1))[..., None],
                           (bs, N_HEADS, 2, HEAD_DIM))
    y_s = _mixer(xs2d, mod3, lambda b: 1 + b, g1, w_pair, w_gate, head_params,
                 latent=True, n_seq=1, cps=ts // CHUNK,
                 rope=_rope_tables(ts), states=(state_ret, state_mlstm_C, s_n, s_m))
    tiles_per_seq = ts // FFN_ROWS
    out_s = _ffn(xs2d, y_s, mod3, lambda i: 1 + i // tiles_per_seq,
                 w_out_p, g2, w1, w2, gf).reshape(bs, ts, D_MODEL)

    new_n = jnp.transpose(new_n, (0, 2, 1, 3))[:, None]
    new_m = jnp.transpose(new_m[..., 0], (0, 2, 1))[:, None]
    return out_p, out_s, new_ret, new_c, new_n, new_m
```
